```python
import jax, jax.numpy as jnp
from jax import lax
import numpy as np

D_MODEL = 2048
BATCH = 8
SEQ = 8192
DEPTH = 1

GRID_W = 64
CTX_LEN = 256
D_FF = 5632
N_MOD = 9
RET_HEADS = 8
RET_DK = 64
RET_DV = 128
RET_CHUNK = 128
RET_ROPE_BASE = 10000.0
MLA_HEADS = 8
MLA_Q_RANK = 512
MLA_KV_RANK = 256
MLA_NOPE = 128
MLA_ROPE = 64
MLA_V = 128
AXIAL_BASE = 10000.0
Q_BLOCK = 128
RMS_EPS = 1e-6
GN_EPS = 1e-5
MIX_OUT = RET_HEADS * RET_DV + MLA_HEADS * MLA_V
SPLITS = (RET_HEADS * RET_DK, RET_HEADS * RET_DK, RET_HEADS * RET_DV, RET_HEADS * RET_DV,
          MLA_Q_RANK, MLA_KV_RANK, MLA_ROPE)
MIX_IN = sum(SPLITS)

kernel_name = "hymba_retention_mla_macaron_dit"


def rms_norm(x, g):
    xf = x.astype(jnp.float32)
    y = xf * lax.rsqrt(jnp.mean(xf * xf, axis=-1, keepdims=True) + RMS_EPS)
    return (y * g.astype(jnp.float32)).astype(x.dtype)


def modulate(h, shift, scale):
    return h * (1.0 + scale) + shift


def swiglu(h, w_in, w_out):
    g, u = jnp.split(h @ w_in, 2, axis=-1)
    return (jax.nn.silu(g) * u) @ w_out


def rope_tables(pos, dim, base):
    inv = base ** (-jnp.arange(0, dim, 2, dtype=jnp.float32) / dim)
    ang = pos.astype(jnp.float32)[:, None] * inv[None, :]
    return jnp.cos(ang)[:, None, :], jnp.sin(ang)[:, None, :]


def rotate(x, cos, sin):
    x1, x2 = jnp.split(x, 2, axis=-1)
    return jnp.concatenate([x1 * cos - x2 * sin, x2 * cos + x1 * sin], axis=-1).astype(x.dtype)


def axial_rope(x, row_tab, col_tab):
    xr, xc = jnp.split(x, 2, axis=-1)
    return jnp.concatenate([rotate(xr, *row_tab), rotate(xc, *col_tab)], axis=-1)


def project(h, w_in, q_norm_g, w_uq, kv_norm_g, w_ukv):
    b, n, _ = h.shape
    idx = [int(i) for i in np.cumsum(SPLITS)[:-1]]
    rq, rk, rv, rg, cq, ckv, kr = jnp.split(h @ w_in, idx, axis=-1)
    rq = rq.reshape(b, n, RET_HEADS, RET_DK)
    rk = rk.reshape(b, n, RET_HEADS, RET_DK) * (RET_DK ** -0.5)
    rv = rv.reshape(b, n, RET_HEADS, RET_DV)
    q = (rms_norm(cq, q_norm_g) @ w_uq).reshape(b, n, MLA_HEADS, MLA_NOPE + MLA_ROPE)
    kv = (rms_norm(ckv, kv_norm_g) @ w_ukv).reshape(b, n, MLA_HEADS, MLA_NOPE + MLA_V)
    q_nope, q_rope = q[..., :MLA_NOPE], q[..., MLA_NOPE:]
    k_nope, v = kv[..., :MLA_NOPE], kv[..., MLA_NOPE:]
    k_rope = kr[:, :, None, :]
    return rq, rk, rv, rg, q_nope, q_rope, k_nope, k_rope, v


def mla_keys(k_nope, k_rope):
    k_rope = jnp.broadcast_to(k_rope, k_nope.shape[:-1] + (MLA_ROPE,))
    return jnp.concatenate([k_nope, k_rope], axis=-1)


def block_attention(q, k, v):
    b, nq, h, dq = q.shape
    nb = nq // Q_BLOCK
    scale = dq ** -0.5
    qb = jnp.moveaxis(q.reshape(b, nb, Q_BLOCK, h, dq), 1, 0)

    def one(qi):
        s = jnp.einsum('bqhd,bkhd->bhqk', qi, k).astype(jnp.float32) * scale
        p = jax.nn.softmax(s, axis=-1)
        return jnp.einsum('bhqk,bkhe->bqhe', p.astype(v.dtype), v)

    out = lax.map(one, qb)
    return jnp.moveaxis(out, 0, 1).reshape(b, nq, h * v.shape[-1])


def retention_chunked(q, k, v, log_gamma, s0, strict):
    b, h, n, dk = q.shape
    dv = v.shape[-1]
    cs = RET_CHUNK
    nc = n // cs
    qc = q.reshape(b, h, nc, cs, dk)
    kc = k.reshape(b, h, nc, cs, dk)
    vc = v.reshape(b, h, nc, cs, dv)
    idx = jnp.arange(cs, dtype=jnp.float32)
    diff = idx[:, None] - idx[None, :]
    mask = diff > 0 if strict else diff >= 0
    lg = log_gamma[:, None, None]
    dmat = jnp.where(mask[None], jnp.exp(lg * jnp.where(mask, diff, 0.0)[None]), 0.0)
    scores = jnp.einsum('bhnid,bhnjd->bhnij', qc, kc) * dmat[None, :, None]
    y_inner = jnp.einsum('bhnij,bhnje->bhnie', scores, vc)
    zeta = jnp.exp(log_gamma[:, None] * (cs - 1 - idx)[None, :])
    xi = jnp.exp(log_gamma[:, None] * (idx + 1.0)[None, :])
    kv = jnp.einsum('bhnjd,bhnje->bhnde', kc * zeta[None, :, None, :, None], vc)
    decay_chunk = jnp.exp(log_gamma * cs)[None, :, None, None]

    def step(s, kv_n):
        return decay_chunk * s + kv_n, s

    s_final, s_prev = lax.scan(step, s0, jnp.moveaxis(kv, 2, 0))
    s_prev = jnp.moveaxis(s_prev, 0, 2)
    y_cross = jnp.einsum('bhnid,bhnde->bhnie', qc * xi[None, :, None, :, None], s_prev)
    return (y_inner + y_cross).reshape(b, h, n, dv), s_final


def bidir_retention(q, k, v, lg_f, lg_b, s0_f, s0_b):
    y_f, s_f = retention_chunked(q, k, v, lg_f, s0_f, False)
    y_b, s_b = retention_chunked(jnp.flip(q, 2), jnp.flip(k, 2), jnp.flip(v, 2), lg_b, s0_b, True)
    return y_f + jnp.flip(y_b, 2), s_f, s_b


def retention_output(y, gate):
    mu = jnp.mean(y, axis=-1, keepdims=True)
    var = jnp.mean(jnp.square(y - mu), axis=-1, keepdims=True)
    y = (y - mu) * lax.rsqrt(var + GN_EPS)
    b, h, n, dv = y.shape
    y = jnp.swapaxes(y, 1, 2).reshape(b, n, h * dv)
    return (jax.nn.silu(gate.astype(jnp.float32)) * y).astype(gate.dtype)


def _fwd_setup_inputs(seed: int = 0) -> dict:
    key = jax.random.key(seed)
    ks = jax.random.split(key, 24)
    f32 = jnp.float32

    def nrm(k, shape, scale):
        return jax.random.normal(k, shape, f32) * scale

    gam = 1.0 - 2.0 ** (-5.0 - jnp.arange(RET_HEADS, dtype=f32))
    decay_logit = jnp.log(gam) - jnp.log1p(-gam)
    return {
        "x": nrm(ks[0], (BATCH, SEQ, D_MODEL), 1.0),
        "c": nrm(ks[1], (BATCH, D_MODEL), 1.0),
        "ctx": nrm(ks[2], (BATCH, CTX_LEN, D_MODEL), 1.0),
        "c_ctx": nrm(ks[3], (D_MODEL,), 1.0),
        "ada_w": nrm(ks[4], (DEPTH, D_MODEL, N_MOD * D_MODEL), 0.5 * D_MODEL ** -0.5),
        "ada_b": nrm(ks[5], (DEPTH, N_MOD * D_MODEL), 0.02),
        "norm1_g": 1.0 + nrm(ks[6], (DEPTH, D_MODEL), 0.02),
        "ffn1_w_in": nrm(ks[7], (DEPTH, D_MODEL, 2 * D_FF), D_MODEL ** -0.5),
        "ffn1_w_out": nrm(ks[8], (DEPTH, D_FF, D_MODEL), D_FF ** -0.5),
        "norm2_g": 1.0 + nrm(ks[9], (DEPTH, D_MODEL), 0.02),
        "mix_w_in": nrm(ks[10], (DEPTH, D_MODEL, MIX_IN), D_MODEL ** -0.5),
        "ret_decay_fwd": decay_logit[None, :] + nrm(ks[11], (DEPTH, RET_HEADS), 0.05),
        "ret_decay_bwd": decay_logit[None, :] + nrm(ks[12], (DEPTH, RET_HEADS), 0.05),
        "mla_q_norm_g": 1.0 + nrm(ks[13], (DEPTH, MLA_Q_RANK), 0.02),
        "mla_w_uq": nrm(ks[14], (DEPTH, MLA_Q_RANK, MLA_HEADS * (MLA_NOPE + MLA_ROPE)), MLA_Q_RANK ** -0.5),
        "mla_kv_norm_g": 1.0 + nrm(ks[15], (DEPTH, MLA_KV_RANK), 0.02),
        "mla_w_ukv": nrm(ks[16], (DEPTH, MLA_KV_RANK, MLA_HEADS * (MLA_NOPE + MLA_V)), MLA_KV_RANK ** -0.5),
        "mix_w_out": nrm(ks[17], (DEPTH, MIX_OUT, D_MODEL), MIX_OUT ** -0.5),
        "norm3_g": 1.0 + nrm(ks[18], (DEPTH, D_MODEL), 0.02),
        "ffn2_w_in": nrm(ks[19], (DEPTH, D_MODEL, 2 * D_FF), D_MODEL ** -0.5),
        "ffn2_w_out": nrm(ks[20], (DEPTH, D_FF, D_MODEL), D_FF ** -0.5),
        "final_norm_g": 1.0 + nrm(ks[21], (D_MODEL,), 0.02),
    }


def _fwd_reference(x, c, ctx, c_ctx, ada_w, ada_b, norm1_g, ffn1_w_in, ffn1_w_out, norm2_g,
              mix_w_in, ret_decay_fwd, ret_decay_bwd, mla_q_norm_g, mla_w_uq, mla_kv_norm_g,
              mla_w_ukv, mix_w_out, norm3_g, ffn2_w_in, ffn2_w_out, final_norm_g):
    b, n_lat, d = x.shape
    n_rows = n_lat // GRID_W
    pos_row = jnp.repeat(jnp.arange(n_rows), GRID_W)
    pos_col = jnp.tile(jnp.arange(GRID_W), n_rows)
    row_tab = rope_tables(pos_row, MLA_ROPE // 2, AXIAL_BASE)
    col_tab = rope_tables(pos_col, MLA_ROPE // 2, AXIAL_BASE)
    ret_tab = rope_tables(jnp.arange(n_lat), RET_DK, RET_ROPE_BASE)
    s_zero = jnp.zeros((b, RET_HEADS, RET_DK, RET_DV), jnp.float32)

    def heads(t):
        return jnp.swapaxes(t, 1, 2).astype(jnp.float32)

    for l in range(DEPTH):
        is_last = l == DEPTH - 1
        m_lat = (jax.nn.silu(c) @ ada_w[l] + ada_b[l]).reshape(b, N_MOD, 1, d)
        m_ctx = (jax.nn.silu(c_ctx)[None, :] @ ada_w[l] + ada_b[l]).reshape(1, N_MOD, 1, d)

        x = x + 0.5 * m_lat[:, 2] * swiglu(modulate(rms_norm(x, norm1_g[l]), m_lat[:, 0], m_lat[:, 1]),
                                           ffn1_w_in[l], ffn1_w_out[l])
        ctx = ctx + 0.5 * m_ctx[:, 2] * swiglu(modulate(rms_norm(ctx, norm1_g[l]), m_ctx[:, 0], m_ctx[:, 1]),
                                               ffn1_w_in[l], ffn1_w_out[l])

        hx = modulate(rms_norm(x, norm2_g[l]), m_lat[:, 3], m_lat[:, 4])
        hc = modulate(rms_norm(ctx, norm2_g[l]), m_ctx[:, 3], m_ctx[:, 4])
        rq, rk, rv, rg, qn, qr, kn, kr, v = project(
            hx, mix_w_in[l], mla_q_norm_g[l], mla_w_uq[l], mla_kv_norm_g[l], mla_w_ukv[l])
        crq, crk, crv, crg, cqn, cqr, ckn, ckr, cv = project(
            hc, mix_w_in[l], mla_q_norm_g[l], mla_w_uq[l], mla_kv_norm_g[l], mla_w_ukv[l])

        lg_f = jax.nn.log_sigmoid(ret_decay_fwd[l].astype(jnp.float32))
        lg_b = jax.nn.log_sigmoid(ret_decay_bwd[l].astype(jnp.float32))
        y_ctx, s_f, s_b = bidir_retention(heads(crq), heads(crk), heads(crv), lg_f, lg_b, s_zero, s_zero)
        rq = rotate(rq, *ret_tab)
        rk = rotate(rk, *ret_tab)
        y_lat, _, _ = bidir_retention(heads(rq), heads(rk), heads(rv), lg_f, lg_b, s_f, s_b)
        ret_out = retention_output(y_lat, rg)

        q_lat = jnp.concatenate([qn, axial_rope(qr, row_tab, col_tab)], axis=-1)
        k_lat = mla_keys(kn, axial_rope(kr, row_tab, col_tab))
        k_ctx = mla_keys(ckn, ckr)
        k_all = jnp.concatenate([k_lat, k_ctx], axis=1)
        v_all = jnp.concatenate([v, cv], axis=1)
        mla_out = block_attention(q_lat, k_all, v_all)

        mix = jnp.concatenate([ret_out.astype(x.dtype), mla_out.astype(x.dtype)], axis=-1) @ mix_w_out[l]
        x = x + m_lat[:, 5] * mix
        if not is_last:
            ctx_ret = retention_output(y_ctx, crg)
            ctx_mla = block_attention(jnp.concatenate([cqn, cqr], axis=-1), k_ctx, cv)
            ctx_mix = jnp.concatenate([ctx_ret.astype(ctx.dtype), ctx_mla.astype(ctx.dtype)], axis=-1) @ mix_w_out[l]
            ctx = ctx + m_ctx[:, 5] * ctx_mix

        x = x + 0.5 * m_lat[:, 8] * swiglu(modulate(rms_norm(x, norm3_g[l]), m_lat[:, 6], m_lat[:, 7]),
                                           ffn2_w_in[l], ffn2_w_out[l])
        if not is_last:
            ctx = ctx + 0.5 * m_ctx[:, 8] * swiglu(modulate(rms_norm(ctx, norm3_g[l]), m_ctx[:, 6], m_ctx[:, 7]),
                                                   ffn2_w_in[l], ffn2_w_out[l])

    return rms_norm(x, final_norm_g)


import jax as _jax
import jax.numpy as _jnp

TWIN_FORMAT = 'train_step'
FWD_PARAMS = ['x', 'c', 'ctx', 'c_ctx', 'ada_w', 'ada_b', 'norm1_g', 'ffn1_w_in', 'ffn1_w_out', 'norm2_g', 'mix_w_in', 'ret_decay_fwd', 'ret_decay_bwd', 'mla_q_norm_g', 'mla_w_uq', 'mla_kv_norm_g', 'mla_w_ukv', 'mix_w_out', 'norm3_g', 'ffn2_w_in', 'ffn2_w_out', 'final_norm_g']
TWIN_WEIGHTS = ['c_ctx', 'ada_w', 'ada_b', 'norm1_g', 'ffn1_w_in', 'ffn1_w_out', 'norm2_g', 'mix_w_in', 'ret_decay_fwd', 'ret_decay_bwd', 'mla_q_norm_g', 'mla_w_uq', 'mla_kv_norm_g', 'mla_w_ukv', 'mix_w_out', 'norm3_g', 'ffn2_w_in', 'ffn2_w_out', 'final_norm_g']
TWIN_DIFF_INPUT = 'x'
TWIN_INPUTS = ['x', 'c', 'ctx', 'c_ctx', 'ada_w', 'ada_b', 'norm1_g', 'ffn1_w_in', 'ffn1_w_out', 'norm2_g', 'mix_w_in', 'ret_decay_fwd', 'ret_decay_bwd', 'mla_q_norm_g', 'mla_w_uq', 'mla_kv_norm_g', 'mla_w_ukv', 'mix_w_out', 'norm3_g', 'ffn2_w_in', 'ffn2_w_out', 'final_norm_g', 'loss_target', 'm_c_ctx', 'm_ada_w', 'm_ada_b', 'm_norm1_g', 'm_ffn1_w_in', 'm_ffn1_w_out', 'm_norm2_g', 'm_mix_w_in', 'm_ret_decay_fwd', 'm_ret_decay_bwd', 'm_mla_q_norm_g', 'm_mla_w_uq', 'm_mla_kv_norm_g', 'm_mla_w_ukv', 'm_mix_w_out', 'm_norm3_g', 'm_ffn2_w_in', 'm_ffn2_w_out', 'm_final_norm_g', 'v_c_ctx', 'v_ada_w', 'v_ada_b', 'v_norm1_g', 'v_ffn1_w_in', 'v_ffn1_w_out', 'v_norm2_g', 'v_mix_w_in', 'v_ret_decay_fwd', 'v_ret_decay_bwd', 'v_mla_q_norm_g', 'v_mla_w_uq', 'v_mla_kv_norm_g', 'v_mla_w_ukv', 'v_mix_w_out', 'v_norm3_g', 'v_ffn2_w_in', 'v_ffn2_w_out', 'v_final_norm_g']
TWIN_OUTPUTS = ['loss', 'grad_x', 'grad_c_ctx', 'grad_ada_w', 'grad_ada_b', 'grad_norm1_g', 'grad_ffn1_w_in', 'grad_ffn1_w_out', 'grad_norm2_g', 'grad_mix_w_in', 'grad_ret_decay_fwd', 'grad_ret_decay_bwd', 'grad_mla_q_norm_g', 'grad_mla_w_uq', 'grad_mla_kv_norm_g', 'grad_mla_w_ukv', 'grad_mix_w_out', 'grad_norm3_g', 'grad_ffn2_w_in', 'grad_ffn2_w_out', 'grad_final_norm_g', 'delta_c_ctx', 'delta_ada_w', 'delta_ada_b', 'delta_norm1_g', 'delta_ffn1_w_in', 'delta_ffn1_w_out', 'delta_norm2_g', 'delta_mix_w_in', 'delta_ret_decay_fwd', 'delta_ret_decay_bwd', 'delta_mla_q_norm_g', 'delta_mla_w_uq', 'delta_mla_kv_norm_g', 'delta_mla_w_ukv', 'delta_mix_w_out', 'delta_norm3_g', 'delta_ffn2_w_in', 'delta_ffn2_w_out', 'delta_final_norm_g', 'new_m_c_ctx', 'new_m_ada_w', 'new_m_ada_b', 'new_m_norm1_g', 'new_m_ffn1_w_in', 'new_m_ffn1_w_out', 'new_m_norm2_g', 'new_m_mix_w_in', 'new_m_ret_decay_fwd', 'new_m_ret_decay_bwd', 'new_m_mla_q_norm_g', 'new_m_mla_w_uq', 'new_m_mla_kv_norm_g', 'new_m_mla_w_ukv', 'new_m_mix_w_out', 'new_m_norm3_g', 'new_m_ffn2_w_in', 'new_m_ffn2_w_out', 'new_m_final_norm_g', 'new_v_c_ctx', 'new_v_ada_w', 'new_v_ada_b', 'new_v_norm1_g', 'new_v_ffn1_w_in', 'new_v_ffn1_w_out', 'new_v_norm2_g', 'new_v_mix_w_in', 'new_v_ret_decay_fwd', 'new_v_ret_decay_bwd', 'new_v_mla_q_norm_g', 'new_v_mla_w_uq', 'new_v_mla_kv_norm_g', 'new_v_mla_w_ukv', 'new_v_mix_w_out', 'new_v_norm3_g', 'new_v_ffn2_w_in', 'new_v_ffn2_w_out', 'new_v_final_norm_g']
TWIN_LEAF_KINDS = {'loss': 'loss', 'grad_x': 'grad_x', 'grad_c_ctx': 'grad_w', 'grad_ada_w': 'grad_w', 'grad_ada_b': 'grad_w', 'grad_norm1_g': 'grad_w', 'grad_ffn1_w_in': 'grad_w', 'grad_ffn1_w_out': 'grad_w', 'grad_norm2_g': 'grad_w', 'grad_mix_w_in': 'grad_w', 'grad_ret_decay_fwd': 'grad_w', 'grad_ret_decay_bwd': 'grad_w', 'grad_mla_q_norm_g': 'grad_w', 'grad_mla_w_uq': 'grad_w', 'grad_mla_kv_norm_g': 'grad_w', 'grad_mla_w_ukv': 'grad_w', 'grad_mix_w_out': 'grad_w', 'grad_norm3_g': 'grad_w', 'grad_ffn2_w_in': 'grad_w', 'grad_ffn2_w_out': 'grad_w', 'grad_final_norm_g': 'grad_w', 'delta_c_ctx': 'delta_w', 'delta_ada_w': 'delta_w', 'delta_ada_b': 'delta_w', 'delta_norm1_g': 'delta_w', 'delta_ffn1_w_in': 'delta_w', 'delta_ffn1_w_out': 'delta_w', 'delta_norm2_g': 'delta_w', 'delta_mix_w_in': 'delta_w', 'delta_ret_decay_fwd': 'delta_w', 'delta_ret_decay_bwd': 'delta_w', 'delta_mla_q_norm_g': 'delta_w', 'delta_mla_w_uq': 'delta_w', 'delta_mla_kv_norm_g': 'delta_w', 'delta_mla_w_ukv': 'delta_w', 'delta_mix_w_out': 'delta_w', 'delta_norm3_g': 'delta_w', 'delta_ffn2_w_in': 'delta_w', 'delta_ffn2_w_out': 'delta_w', 'delta_final_norm_g': 'delta_w', 'new_m_c_ctx': 'new_m', 'new_m_ada_w': 'new_m', 'new_m_ada_b': 'new_m', 'new_m_norm1_g': 'new_m', 'new_m_ffn1_w_in': 'new_m', 'new_m_ffn1_w_out': 'new_m', 'new_m_norm2_g': 'new_m', 'new_m_mix_w_in': 'new_m', 'new_m_ret_decay_fwd': 'new_m', 'new_m_ret_decay_bwd': 'new_m', 'new_m_mla_q_norm_g': 'new_m', 'new_m_mla_w_uq': 'new_m', 'new_m_mla_kv_norm_g': 'new_m', 'new_m_mla_w_ukv': 'new_m', 'new_m_mix_w_out': 'new_m', 'new_m_norm3_g': 'new_m', 'new_m_ffn2_w_in': 'new_m', 'new_m_ffn2_w_out': 'new_m', 'new_m_final_norm_g': 'new_m', 'new_v_c_ctx': 'new_v', 'new_v_ada_w': 'new_v', 'new_v_ada_b': 'new_v', 'new_v_norm1_g': 'new_v', 'new_v_ffn1_w_in': 'new_v', 'new_v_ffn1_w_out': 'new_v', 'new_v_norm2_g': 'new_v', 'new_v_mix_w_in': 'new_v', 'new_v_ret_decay_fwd': 'new_v', 'new_v_ret_decay_bwd': 'new_v', 'new_v_mla_q_norm_g': 'new_v', 'new_v_mla_w_uq': 'new_v', 'new_v_mla_kv_norm_g': 'new_v', 'new_v_mla_w_ukv': 'new_v', 'new_v_mix_w_out': 'new_v', 'new_v_norm3_g': 'new_v', 'new_v_ffn2_w_in': 'new_v', 'new_v_ffn2_w_out': 'new_v', 'new_v_final_norm_g': 'new_v'}


def _forward(args):
    return _fwd_reference(*[args[k] for k in FWD_PARAMS])


def _output_shape():
    def fwd():
        inp = _fwd_setup_inputs(0)
        return _fwd_reference(*[inp[k] for k in FWD_PARAMS])
    out = _jax.eval_shape(fwd)
    return out.shape, out.dtype

N_MICROBATCH = 1
ADAM_LR = 0.001
ADAM_B1 = 0.9
ADAM_B2 = 0.999
ADAM_EPS = 1e-08
ADAM_WD = 0.01
ADAM_STEP = 10
PER_EXAMPLE_BATCH_AXIS = {'x': 0, 'c': 0, 'ctx': 0, 'loss_target': 0}
SHARED_INPUTS = []
_WEIGHT_DTYPES = {'c_ctx': _jnp.float32, 'ada_w': _jnp.float32, 'ada_b': _jnp.float32, 'norm1_g': _jnp.float32, 'ffn1_w_in': _jnp.float32, 'ffn1_w_out': _jnp.float32, 'norm2_g': _jnp.float32, 'mix_w_in': _jnp.float32, 'ret_decay_fwd': _jnp.float32, 'ret_decay_bwd': _jnp.float32, 'mla_q_norm_g': _jnp.float32, 'mla_w_uq': _jnp.float32, 'mla_kv_norm_g': _jnp.float32, 'mla_w_ukv': _jnp.float32, 'mix_w_out': _jnp.float32, 'norm3_g': _jnp.float32, 'ffn2_w_in': _jnp.float32, 'ffn2_w_out': _jnp.float32, 'final_norm_g': _jnp.float32}
MOMENT_SCALE = {'c_ctx': 1.509250e-02, 'ada_w': 2.472812e-02, 'ada_b': 4.003484e-02, 'norm1_g': 1.896246e-02, 'ffn1_w_in': 8.347040e-03, 'ffn1_w_out': 1.363369e-02, 'norm2_g': 3.180722e-02, 'mix_w_in': 2.687333e-02, 'ret_decay_fwd': 1.991759e-01, 'ret_decay_bwd': 1.912457e-01, 'mla_q_norm_g': 4.738683e-03, 'mla_w_uq': 2.864297e-03, 'mla_kv_norm_g': 2.081251e-02, 'mla_w_ukv': 7.802750e-03, 'mix_w_out': 1.857829e-02, 'norm3_g': 1.877023e-02, 'ffn2_w_in': 8.096858e-03, 'ffn2_w_out': 1.322024e-02, 'final_norm_g': 3.198041e+01}


def _to_microbatches(a, axis):
    t = _jnp.moveaxis(a, axis, 0)
    t = t.reshape((N_MICROBATCH, t.shape[0] // N_MICROBATCH) + t.shape[1:])
    return _jnp.moveaxis(t, 1, axis + 1)


def setup_inputs(seed: int = 0) -> dict:
    inp = _fwd_setup_inputs(seed)
    key = _jax.random.fold_in(_jax.random.key(seed), 7919)
    shape, _ = _output_shape()
    out = dict(inp)
    out["loss_target"] = _jax.random.normal(_jax.random.fold_in(key, 0), shape, _jnp.float32)
    for i, name in enumerate(TWIN_WEIGHTS):
        w = inp[name].astype(_jnp.float32)
        if MOMENT_SCALE is None:
            s = _jnp.sqrt(_jnp.mean(_jnp.square(w)) + 1e-30)
        else:
            s = MOMENT_SCALE[name]
        km, kv = _jax.random.split(_jax.random.fold_in(key, i + 1))
        out[name] = w
        out["m_" + name] = s * _jax.random.normal(km, w.shape, _jnp.float32)
        out["v_" + name] = (s * s) * _jax.random.uniform(kv, w.shape, _jnp.float32, 0.5, 1.5)
    if N_MICROBATCH > 1:
        for name, axis in PER_EXAMPLE_BATCH_AXIS.items():
            out[name] = _to_microbatches(out[name], axis)
    return {'x': out['x'], 'c': out['c'], 'ctx': out['ctx'], 'c_ctx': out['c_ctx'], 'ada_w': out['ada_w'], 'ada_b': out['ada_b'], 'norm1_g': out['norm1_g'], 'ffn1_w_in': out['ffn1_w_in'], 'ffn1_w_out': out['ffn1_w_out'], 'norm2_g': out['norm2_g'], 'mix_w_in': out['mix_w_in'], 'ret_decay_fwd': out['ret_decay_fwd'], 'ret_decay_bwd': out['ret_decay_bwd'], 'mla_q_norm_g': out['mla_q_norm_g'], 'mla_w_uq': out['mla_w_uq'], 'mla_kv_norm_g': out['mla_kv_norm_g'], 'mla_w_ukv': out['mla_w_ukv'], 'mix_w_out': out['mix_w_out'], 'norm3_g': out['norm3_g'], 'ffn2_w_in': out['ffn2_w_in'], 'ffn2_w_out': out['ffn2_w_out'], 'final_norm_g': out['final_norm_g'], 'loss_target': out['loss_target'], 'm_c_ctx': out['m_c_ctx'], 'm_ada_w': out['m_ada_w'], 'm_ada_b': out['m_ada_b'], 'm_norm1_g': out['m_norm1_g'], 'm_ffn1_w_in': out['m_ffn1_w_in'], 'm_ffn1_w_out': out['m_ffn1_w_out'], 'm_norm2_g': out['m_norm2_g'], 'm_mix_w_in': out['m_mix_w_in'], 'm_ret_decay_fwd': out['m_ret_decay_fwd'], 'm_ret_decay_bwd': out['m_ret_decay_bwd'], 'm_mla_q_norm_g': out['m_mla_q_norm_g'], 'm_mla_w_uq': out['m_mla_w_uq'], 'm_mla_kv_norm_g': out['m_mla_kv_norm_g'], 'm_mla_w_ukv': out['m_mla_w_ukv'], 'm_mix_w_out': out['m_mix_w_out'], 'm_norm3_g': out['m_norm3_g'], 'm_ffn2_w_in': out['m_ffn2_w_in'], 'm_ffn2_w_out': out['m_ffn2_w_out'], 'm_final_norm_g': out['m_final_norm_g'], 'v_c_ctx': out['v_c_ctx'], 'v_ada_w': out['v_ada_w'], 'v_ada_b': out['v_ada_b'], 'v_norm1_g': out['v_norm1_g'], 'v_ffn1_w_in': out['v_ffn1_w_in'], 'v_ffn1_w_out': out['v_ffn1_w_out'], 'v_norm2_g': out['v_norm2_g'], 'v_mix_w_in': out['v_mix_w_in'], 'v_ret_decay_fwd': out['v_ret_decay_fwd'], 'v_ret_decay_bwd': out['v_ret_decay_bwd'], 'v_mla_q_norm_g': out['v_mla_q_norm_g'], 'v_mla_w_uq': out['v_mla_w_uq'], 'v_mla_kv_norm_g': out['v_mla_kv_norm_g'], 'v_mla_w_ukv': out['v_mla_w_ukv'], 'v_mix_w_out': out['v_mix_w_out'], 'v_norm3_g': out['v_norm3_g'], 'v_ffn2_w_in': out['v_ffn2_w_in'], 'v_ffn2_w_out': out['v_ffn2_w_out'], 'v_final_norm_g': out['v_final_norm_g']}


def _loss(weights, diff, rest, loss_target):
    with _jax.named_scope("forward"):
        args = {**rest, TWIN_DIFF_INPUT: diff, **{k: w.astype(_WEIGHT_DTYPES[k]) for k, w in weights.items()}}
        y = _forward(args)
    with _jax.named_scope("loss_head"):
        err = _jnp.square(y.astype(_jnp.float32) - loss_target)
        return 0.5 * _jnp.sum(_jnp.mean(err, axis=-1)) if err.ndim else 0.5 * err


def _adamw(w, g, m, v):
    m = ADAM_B1 * m + (1.0 - ADAM_B1) * g
    v = ADAM_B2 * v + (1.0 - ADAM_B2) * _jnp.square(g)
    m_hat = m / (1.0 - ADAM_B1 ** ADAM_STEP)
    v_hat = v / (1.0 - ADAM_B2 ** ADAM_STEP)
    delta = -ADAM_LR * (m_hat / (_jnp.sqrt(v_hat) + ADAM_EPS) + ADAM_WD * w)
    return delta, m, v


def reference(x, c, ctx, c_ctx, ada_w, ada_b, norm1_g, ffn1_w_in, ffn1_w_out, norm2_g, mix_w_in, ret_decay_fwd, ret_decay_bwd, mla_q_norm_g, mla_w_uq, mla_kv_norm_g, mla_w_ukv, mix_w_out, norm3_g, ffn2_w_in, ffn2_w_out, final_norm_g, loss_target, m_c_ctx, m_ada_w, m_ada_b, m_norm1_g, m_ffn1_w_in, m_ffn1_w_out, m_norm2_g, m_mix_w_in, m_ret_decay_fwd, m_ret_decay_bwd, m_mla_q_norm_g, m_mla_w_uq, m_mla_kv_norm_g, m_mla_w_ukv, m_mix_w_out, m_norm3_g, m_ffn2_w_in, m_ffn2_w_out, m_final_norm_g, v_c_ctx, v_ada_w, v_ada_b, v_norm1_g, v_ffn1_w_in, v_ffn1_w_out, v_norm2_g, v_mix_w_in, v_ret_decay_fwd, v_ret_decay_bwd, v_mla_q_norm_g, v_mla_w_uq, v_mla_kv_norm_g, v_mla_w_ukv, v_mix_w_out, v_norm3_g, v_ffn2_w_in, v_ffn2_w_out, v_final_norm_g):
    given = dict(x=x, c=c, ctx=ctx, c_ctx=c_ctx, ada_w=ada_w, ada_b=ada_b, norm1_g=norm1_g, ffn1_w_in=ffn1_w_in, ffn1_w_out=ffn1_w_out, norm2_g=norm2_g, mix_w_in=mix_w_in, ret_decay_fwd=ret_decay_fwd, ret_decay_bwd=ret_decay_bwd, mla_q_norm_g=mla_q_norm_g, mla_w_uq=mla_w_uq, mla_kv_norm_g=mla_kv_norm_g, mla_w_ukv=mla_w_ukv, mix_w_out=mix_w_out, norm3_g=norm3_g, ffn2_w_in=ffn2_w_in, ffn2_w_out=ffn2_w_out, final_norm_g=final_norm_g, loss_target=loss_target, m_c_ctx=m_c_ctx, m_ada_w=m_ada_w, m_ada_b=m_ada_b, m_norm1_g=m_norm1_g, m_ffn1_w_in=m_ffn1_w_in, m_ffn1_w_out=m_ffn1_w_out, m_norm2_g=m_norm2_g, m_mix_w_in=m_mix_w_in, m_ret_decay_fwd=m_ret_decay_fwd, m_ret_decay_bwd=m_ret_decay_bwd, m_mla_q_norm_g=m_mla_q_norm_g, m_mla_w_uq=m_mla_w_uq, m_mla_kv_norm_g=m_mla_kv_norm_g, m_mla_w_ukv=m_mla_w_ukv, m_mix_w_out=m_mix_w_out, m_norm3_g=m_norm3_g, m_ffn2_w_in=m_ffn2_w_in, m_ffn2_w_out=m_ffn2_w_out, m_final_norm_g=m_final_norm_g, v_c_ctx=v_c_ctx, v_ada_w=v_ada_w, v_ada_b=v_ada_b, v_norm1_g=v_norm1_g, v_ffn1_w_in=v_ffn1_w_in, v_ffn1_w_out=v_ffn1_w_out, v_norm2_g=v_norm2_g, v_mix_w_in=v_mix_w_in, v_ret_decay_fwd=v_ret_decay_fwd, v_ret_decay_bwd=v_ret_decay_bwd, v_mla_q_norm_g=v_mla_q_norm_g, v_mla_w_uq=v_mla_w_uq, v_mla_kv_norm_g=v_mla_kv_norm_g, v_mla_w_ukv=v_mla_w_ukv, v_mix_w_out=v_mix_w_out, v_norm3_g=v_norm3_g, v_ffn2_w_in=v_ffn2_w_in, v_ffn2_w_out=v_ffn2_w_out, v_final_norm_g=v_final_norm_g)
    weights = {n: given[n] for n in TWIN_WEIGHTS}
    shared = {n: given[n] for n in SHARED_INPUTS}
    per_example = {n: given[n] for n in ['x', 'c', 'ctx']}
    grad_fn = _jax.value_and_grad(_loss, argnums=(0, 1))

    def one_microbatch(ex, loss_target):
        ex = dict(ex)
        diff = ex.pop(TWIN_DIFF_INPUT)
        return grad_fn(weights, diff, {**shared, **ex}, loss_target)

    if N_MICROBATCH == 1:
        loss, (grad_w, grad_x) = one_microbatch(per_example, given["loss_target"])
    else:
        def body(carry, xs):
            loss_sum, grad_sum = carry
            l_k, (gw_k, gx_k) = one_microbatch(xs[0], xs[1])
            with _jax.named_scope("update"):
                return (loss_sum + l_k, _jax.tree.map(_jnp.add, grad_sum, gw_k)), gx_k

        init = (_jnp.zeros((), _jnp.float32), _jax.tree.map(_jnp.zeros_like, weights))
        (loss, grad_w), grad_x = _jax.lax.scan(body, init, (per_example, given["loss_target"]))
    with _jax.named_scope("update"):
        delta_w, new_m, new_v = {}, {}, {}
        for n in TWIN_WEIGHTS:
            delta_w[n], new_m[n], new_v[n] = _adamw(weights[n], grad_w[n], given["m_" + n], given["v_" + n])
    return (loss, grad_x, *[grad_w[n] for n in TWIN_WEIGHTS], *[delta_w[n] for n in TWIN_WEIGHTS],
            *[new_m[n] for n in TWIN_WEIGHTS], *[new_v[n] for n in TWIN_WEIGHTS])
```

```python
import functools

import jax
import jax.numpy as jnp
import numpy as np
from jax import lax
from jax.experimental import pallas as pl
from jax.experimental.pallas import tpu as pltpu

F32 = jnp.float32
BF16 = jnp.bfloat16
MESH = pl.DeviceIdType.MESH

VMEM_LIMIT_BYTES = 52 * 1024 * 1024
LANES = 128
SUBLANES = 8

D_FF_SPLIT = 2
RET_HEADS, RET_DK, RET_DV, RET_CHUNK = 8, 64, 128, 128
MLA_HEADS, MLA_Q_RANK, MLA_KV_RANK, MLA_NOPE, MLA_ROPE, MLA_V = 8, 512, 256, 128, 64, 128
GRID_W = 64
ROPE_BASE = 10000.0
RMS_EPS = 1e-6
GN_EPS = 1e-5
MIX_SPLITS = (RET_HEADS * RET_DK, RET_HEADS * RET_DK, RET_HEADS * RET_DV, RET_HEADS * RET_DV,
              MLA_Q_RANK, MLA_KV_RANK, MLA_ROPE)
MIX_IN = sum(MIX_SPLITS)
MIX_IN_PAD = 4096
ADAM_LR, ADAM_B1, ADAM_B2, ADAM_EPS, ADAM_WD, ADAM_STEP = 0.001, 0.9, 0.999, 1e-08, 0.01, 10


def _tile(n, pref, align):
    best = None
    t = align
    while t <= min(n, pref):
        if n % t == 0:
            best = t
        t += align
    return n if best is None else best


def _params(sem=None):
    return pltpu.CompilerParams(dimension_semantics=sem, vmem_limit_bytes=VMEM_LIMIT_BYTES)


def _matmul(a, b, mode, out_dtype, name, add=None, tm=1024, tn=1024, tk=2048):
    if mode == "nn":
        (m, k), (k2, n) = a.shape, b.shape
        dims = (((1,), (0,)), ((), ()))
    elif mode == "nt":
        (m, k), (n, k2) = a.shape, b.shape
        dims = (((1,), (1,)), ((), ()))
    else:
        (k, m), (k2, n) = a.shape, b.shape
        dims = (((0,), (0,)), ((), ()))
    assert k == k2, (a.shape, b.shape, mode)
    tm = _tile(m, tm, LANES if mode == "tn" else 16)
    tn = _tile(n, tn, LANES)
    tk = _tile(k, tk, LANES if mode != "tn" else 16)
    nk = k // tk
    a_spec = pl.BlockSpec((tk, tm), lambda i, j, kk: (kk, i)) if mode == "tn" else pl.BlockSpec((tm, tk), lambda i, j, kk: (i, kk))
    b_spec = pl.BlockSpec((tn, tk), lambda i, j, kk: (j, kk)) if mode == "nt" else pl.BlockSpec((tk, tn), lambda i, j, kk: (kk, j))
    o_spec = pl.BlockSpec((tm, tn), lambda i, j, kk: (i, j))
    has_add = add is not None

    def body(*refs):
        a_ref, b_ref = refs[0], refs[1]
        add_ref = refs[2] if has_add else None
        o_ref = refs[2 + has_add]
        p = lax.dot_general(a_ref[...].astype(BF16), b_ref[...].astype(BF16), dims, preferred_element_type=F32)
        if nk == 1:
            if has_add:
                p = p + add_ref[...].astype(F32)
            o_ref[...] = p.astype(out_dtype)
        else:
            acc = refs[3 + has_add]
            kk = pl.program_id(2)

            @pl.when(kk == 0)
            def _():
                acc[...] = p + add_ref[...].astype(F32) if has_add else p

            @pl.when(kk > 0)
            def _():
                acc[...] += p

            @pl.when(kk == nk - 1)
            def _():
                o_ref[...] = acc[...].astype(out_dtype)

    return pl.pallas_call(
        body, name=name, grid=(m // tm, n // tn, nk),
        in_specs=[a_spec, b_spec] + ([o_spec] if has_add else []),
        out_specs=o_spec,
        out_shape=jax.ShapeDtypeStruct((m, n), out_dtype),
        scratch_shapes=[pltpu.VMEM((tm, tn), F32)] if nk > 1 else [],
        compiler_params=_params(("parallel", "parallel", "arbitrary")),
    )(*((a, b, add) if has_add else (a, b)))


def _row_tile(t, n_ctx, d):
    pref = max(SUBLANES, min(256, (1 << 19) // d))
    r = _tile(int(np.gcd(t, n_ctx)) if n_ctx else t, pref, SUBLANES)
    return r, (n_ctx // r if n_ctx else 0)


def _seg_map(nct):
    if nct:
        return lambda i: (jnp.minimum(i // nct, 1), 0, 0)
    return lambda i: (0, 0, 0)


def _normmod_fwd(x, g, shift, scale, n_ctx, name):
    t, d = x.shape
    r, nct = _row_tile(t, n_ctx, d)

    def body(x_ref, g_ref, sh_ref, sc_ref, h_ref):
        xv = x_ref[...]
        rstd = lax.rsqrt(jnp.mean(xv * xv, axis=-1, keepdims=True) + RMS_EPS)
        n = xv * rstd * g_ref[...]
        h_ref[...] = (n * (1.0 + sc_ref[0]) + sh_ref[0]).astype(BF16)

    row = pl.BlockSpec((r, d), lambda i: (i, 0))
    seg = pl.BlockSpec((1, 1, d), _seg_map(nct))
    return pl.pallas_call(
        body, name=name, grid=(t // r,),
        in_specs=[row, pl.BlockSpec((1, d), lambda i: (0, 0)), seg, seg],
        out_specs=row, out_shape=jax.ShapeDtypeStruct((t, d), BF16),
        compiler_params=_params(("parallel",)),
    )(x, g, shift, scale)


def _normmod_bwd(dh, x, g, shift, scale, dres, n_ctx, name):
    t, d = x.shape
    r, nct = _row_tile(t, n_ctx, d)
    has_res = dres is not None
    nseg = shift.shape[0]

    def body(*refs):
        dh_ref, x_ref, g_ref, sh_ref, sc_ref = refs[:5]
        dres_ref = refs[5] if has_res else None
        dx_ref, dg_ref, dsh_ref, dsc_ref = refs[5 + has_res:]
        i = pl.program_id(0)
        xv = x_ref[...]
        dhv = dh_ref[...].astype(F32)
        rstd = lax.rsqrt(jnp.mean(xv * xv, axis=-1, keepdims=True) + RMS_EPS)
        y = xv * rstd
        gv = g_ref[...]
        dn = dhv * (1.0 + sc_ref[0])
        dy = dn * gv
        dx = rstd * (dy - y * jnp.mean(dy * y, axis=-1, keepdims=True))
        if has_res:
            dx = dx + dres_ref[...]
        dx_ref[...] = dx

        @pl.when(i == 0)
        def _():
            dg_ref[...] = jnp.zeros_like(dg_ref)

        @pl.when(jnp.logical_or(i == 0, i == nct))
        def _():
            dsh_ref[...] = jnp.zeros_like(dsh_ref)
            dsc_ref[...] = jnp.zeros_like(dsc_ref)

        dg_ref[...] += jnp.sum(dn * y, axis=0, keepdims=True)
        dsh_ref[0] += jnp.sum(dhv, axis=0, keepdims=True)
        dsc_ref[0] += jnp.sum(dhv * (y * gv), axis=0, keepdims=True)

    row = pl.BlockSpec((r, d), lambda i: (i, 0))
    seg = pl.BlockSpec((1, 1, d), _seg_map(nct))
    vec = pl.BlockSpec((1, d), lambda i: (0, 0))
    return pl.pallas_call(
        body, name=name, grid=(t // r,),
        in_specs=[row, row, vec, seg, seg] + ([row] if has_res else []),
        out_specs=[row, vec, seg, seg],
        out_shape=[jax.ShapeDtypeStruct((t, d), F32), jax.ShapeDtypeStruct((1, d), F32),
                   jax.ShapeDtypeStruct((nseg, 1, d), F32), jax.ShapeDtypeStruct((nseg, 1, d), F32)],
        compiler_params=_params(("arbitrary",)),
    )(*((dh, x, g, shift, scale, dres) if has_res else (dh, x, g, shift, scale)))


def _gated_res_fwd(x, y, gate, coef, n_ctx, name):
    t, d = x.shape
    r, nct = _row_tile(t, n_ctx, d)

    def body(x_ref, y_ref, gt_ref, o_ref):
        o_ref[...] = x_ref[...] + (coef * gt_ref[0]) * y_ref[...]

    row = pl.BlockSpec((r, d), lambda i: (i, 0))
    return pl.pallas_call(
        body, name=name, grid=(t // r,),
        in_specs=[row, row, pl.BlockSpec((1, 1, d), _seg_map(nct))],
        out_specs=row, out_shape=jax.ShapeDtypeStruct((t, d), F32),
        compiler_params=_params(("parallel",)),
    )(x, y, gate)


def _gated_res_bwd(dout, y, gate, coef, n_ctx, name):
    t, d = dout.shape
    r, nct = _row_tile(t, n_ctx, d)
    nseg = gate.shape[0]

    def body(do_ref, y_ref, gt_ref, dy_ref, dgt_ref):
        i = pl.program_id(0)
        dov = do_ref[...] * coef
        dy_ref[...] = (dov * gt_ref[0]).astype(BF16)

        @pl.when(jnp.logical_or(i == 0, i == nct))
        def _():
            dgt_ref[...] = jnp.zeros_like(dgt_ref)

        dgt_ref[0] += jnp.sum(dov * y_ref[...], axis=0, keepdims=True)

    row = pl.BlockSpec((r, d), lambda i: (i, 0))
    seg = pl.BlockSpec((1, 1, d), _seg_map(nct))
    return pl.pallas_call(
        body, name=name, grid=(t // r,),
        in_specs=[row, row, seg], out_specs=[row, seg],
        out_shape=[jax.ShapeDtypeStruct((t, d), BF16), jax.ShapeDtypeStruct((nseg, 1, d), F32)],
        compiler_params=_params(("arbitrary",)),
    )(dout, y, gate)


def _swiglu_fwd(gg, uu, name):
    t, f = gg.shape
    r, c = _tile(t, 256, SUBLANES), _tile(f, 1408, LANES)

    def body(g_ref, u_ref, a_ref):
        gv = g_ref[...]
        a_ref[...] = (gv * jax.nn.sigmoid(gv) * u_ref[...]).astype(BF16)

    blk = pl.BlockSpec((r, c), lambda i, j: (i, j))
    return pl.pallas_call(
        body, name=name, grid=(t // r, f // c), in_specs=[blk, blk], out_specs=blk,
        out_shape=jax.ShapeDtypeStruct((t, f), BF16), compiler_params=_params(("parallel", "parallel")),
    )(gg, uu)


def _swiglu_bwd(da, gg, uu, name):
    t, f = gg.shape
    r, c = _tile(t, 256, SUBLANES), _tile(f, 1408, LANES)

    def body(da_ref, g_ref, u_ref, dg_ref, du_ref):
        gv = g_ref[...]
        dav = da_ref[...].astype(F32)
        sg = jax.nn.sigmoid(gv)
        silu = gv * sg
        dg_ref[...] = (dav * u_ref[...] * (sg * (1.0 + gv * (1.0 - sg)))).astype(BF16)
        du_ref[...] = (dav * silu).astype(BF16)

    blk = pl.BlockSpec((r, c), lambda i, j: (i, j))
    return pl.pallas_call(
        body, name=name, grid=(t // r, f // c), in_specs=[blk, blk, blk], out_specs=[blk, blk],
        out_shape=[jax.ShapeDtypeStruct((t, f), BF16)] * 2, compiler_params=_params(("parallel", "parallel")),
    )(da, gg, uu)


def _make_ffn(n_ctx, tag):
    @jax.custom_vjp
    def ffn(x, g, shift, scale, gate, wg, wu, wo):
        return fwd(x, g, shift, scale, gate, wg, wu, wo)[0]

    def fwd(x, g, shift, scale, gate, wg, wu, wo):
        h = _normmod_fwd(x, g, shift, scale, n_ctx, tag + "_norm")
        gg = _matmul(h, wg, "nn", F32, tag + "_mm_g", tm=768)
        uu = _matmul(h, wu, "nn", F32, tag + "_mm_u", tm=768)
        a = _swiglu_fwd(gg, uu, tag + "_act")
        y = _matmul(a, wo, "nn", F32, tag + "_mm_o", tm=768, tk=1408)
        out = _gated_res_fwd(x, y, gate, 0.5, n_ctx, tag + "_res")
        return out, (x, g, shift, scale, gate, wg, wu, wo, h, gg, uu, a, y)

    def bwd(res, dout):
        x, g, shift, scale, gate, wg, wu, wo, h, gg, uu, a, y = res
        dy, dgate = _gated_res_bwd(dout, y, gate, 0.5, n_ctx, tag + "_res_b")
        dwo = _matmul(a, dy, "tn", BF16, tag + "_dwo", tk=768)
        da = _matmul(dy, wo, "nt", BF16, tag + "_da", tm=768, tn=1408)
        dgg, duu = _swiglu_bwd(da, gg, uu, tag + "_act_b")
        dwg = _matmul(h, dgg, "tn", BF16, tag + "_dwg", tk=768)
        dwu = _matmul(h, duu, "tn", BF16, tag + "_dwu", tk=768)
        dh = _matmul(dgg, wg, "nt", F32, tag + "_dh_g", tm=768, tk=1408)
        dh = _matmul(duu, wu, "nt", F32, tag + "_dh_u", add=dh, tm=768, tk=1408)
        dx, dg, dshift, dscale = _normmod_bwd(dh, x, g, shift, scale, dout, n_ctx, tag + "_norm_b")
        return dx, dg, dshift, dscale, dgate, dwg, dwu, dwo

    ffn.defvjp(fwd, bwd)
    return ffn


def _make_normmod_linear(n_ctx, tag):
    @jax.custom_vjp
    def op(x, g, shift, scale, w):
        return fwd(x, g, shift, scale, w)[0]

    def fwd(x, g, shift, scale, w):
        h = _normmod_fwd(x, g, shift, scale, n_ctx, tag + "_norm")
        y = _matmul(h, w, "nn", F32, tag + "_mm", tm=768)
        return y, (x, g, shift, scale, w, h)

    def bwd(res, dy):
        x, g, shift, scale, w, h = res
        dw = _matmul(h, dy, "tn", BF16, tag + "_dw", tk=768)
        dh = _matmul(dy, w, "nt", F32, tag + "_dh", tm=768)
        dx, dg, dshift, dscale = _normmod_bwd(dh, x, g, shift, scale, None, n_ctx, tag + "_norm_b")
        return dx, dg, dshift, dscale, dw

    op.defvjp(fwd, bwd)
    return op


def _make_linear_gated_res(tag):
    @jax.custom_vjp
    def op(x, a, w, gate):
        return fwd(x, a, w, gate)[0]

    def fwd(x, a, w, gate):
        y = _matmul(a, w, "nn", F32, tag + "_mm", tm=512)
        out = _gated_res_fwd(x, y, gate, 1.0, 0, tag + "_res")
        return out, (a, w, gate, y)

    def bwd(res, dout):
        a, w, gate, y = res
        dy, dgate = _gated_res_bwd(dout, y, gate, 1.0, 0, tag + "_res_b")
        dw = _matmul(a, dy, "tn", BF16, tag + "_dw", tk=512)
        da = _matmul(dy, w, "nt", F32, tag + "_da", tm=1024)
        return dout, da, dw, dgate

    op.defvjp(fwd, bwd)
    return op


def _final_loss_call(x, g, target, name):
    t, d = x.shape
    r = _tile(t, 256, SUBLANES)

    def body(x_ref, g_ref, t_ref, loss_ref, dx_ref, dg_ref):
        i = pl.program_id(0)
        xv = x_ref[...]
        gv = g_ref[...]
        rstd = lax.rsqrt(jnp.mean(xv * xv, axis=-1, keepdims=True) + RMS_EPS)
        xh = xv * rstd
        e = xh * gv - t_ref[...]
        dy = e * (1.0 / d)
        dn = dy * gv
        dx_ref[...] = rstd * (dn - xh * jnp.mean(dn * xh, axis=-1, keepdims=True))

        @pl.when(i == 0)
        def _():
            loss_ref[...] = jnp.zeros_like(loss_ref)
            dg_ref[...] = jnp.zeros_like(dg_ref)

        loss_ref[...] += 0.5 * jnp.sum(jnp.mean(e * e, axis=-1, keepdims=True), axis=0, keepdims=True)
        dg_ref[...] += jnp.sum(dy * xh, axis=0, keepdims=True)

    row = pl.BlockSpec((r, d), lambda i: (i, 0))
    vec = pl.BlockSpec((1, d), lambda i: (0, 0))
    return pl.pallas_call(
        body, name=name, grid=(t // r,),
        in_specs=[row, vec, row], out_specs=[pl.BlockSpec((1, 1), lambda i: (0, 0)), row, vec],
        out_shape=[jax.ShapeDtypeStruct((1, 1), F32), jax.ShapeDtypeStruct((t, d), F32), jax.ShapeDtypeStruct((1, d), F32)],
        compiler_params=_params(("arbitrary",)),
    )(x, g, target)


@jax.custom_vjp
def _final_loss(x, g, target):
    return _final_loss_call(x, g, target, "final_loss")[0][0, 0]


def _final_loss_fwd(x, g, target):
    loss, dx, dg = _final_loss_call(x, g, target, "final_loss")
    return loss[0, 0], (dx, dg, target)


def _final_loss_bwd(res, dl):
    dx, dg, target = res
    return dx * dl, dg * dl, jnp.zeros_like(target)


_final_loss.defvjp(_final_loss_fwd, _final_loss_bwd)


_NT = (((1,), (1,)), ((), ()))
_TN = (((0,), (0,)), ((), ()))


def _attn_fwd_call(q, k, v, scale):
    h, nq, dq = q.shape
    nk, dv = v.shape[1], v.shape[2]
    tq = _tile(nq, 256, 16)

    def body(q_ref, k_ref, v_ref, o_ref, lse_ref):
        s = lax.dot_general(q_ref[0], k_ref[0], _NT, preferred_element_type=F32) * scale
        m = jnp.max(s, axis=-1, keepdims=True)
        p = jnp.exp(s - m)
        l = jnp.sum(p, axis=-1, keepdims=True)
        o = jnp.dot(p.astype(BF16), v_ref[0], preferred_element_type=F32)
        o_ref[0] = o / l
        lse_ref[0] = m + jnp.log(l)

    return pl.pallas_call(
        body, name="attn_fwd", grid=(h, nq // tq),
        in_specs=[pl.BlockSpec((1, tq, dq), lambda hh, i: (hh, i, 0)),
                  pl.BlockSpec((1, nk, dq), lambda hh, i: (hh, 0, 0)),
                  pl.BlockSpec((1, nk, dv), lambda hh, i: (hh, 0, 0))],
        out_specs=[pl.BlockSpec((1, tq, dv), lambda hh, i: (hh, i, 0)),
                   pl.BlockSpec((1, tq, 1), lambda hh, i: (hh, i, 0))],
        out_shape=[jax.ShapeDtypeStruct((h, nq, dv), F32), jax.ShapeDtypeStruct((h, nq, 1), F32)],
        compiler_params=_params(("parallel", "arbitrary")),
    )(q, k, v)


def _attn_bwd_call(q, k, v, o, do, lse, scale):
    h, nq, dq = q.shape
    nk, dv = v.shape[1], v.shape[2]
    tq = _tile(nq, 256, 16)
    ck = _tile(nk, 1408, LANES)
    nchunk = nk // ck

    def body(q_ref, k_ref, v_ref, o_ref, do_ref, lse_ref, dq_ref, dk_ref, dv_ref):
        i = pl.program_id(1)

        @pl.when(i == 0)
        def _():
            dk_ref[...] = jnp.zeros_like(dk_ref)
            dv_ref[...] = jnp.zeros_like(dv_ref)

        qv = q_ref[0]
        dov = do_ref[0]
        dob = dov.astype(BF16)
        delta = jnp.sum(dov * o_ref[0], axis=-1, keepdims=True)
        lse_v = lse_ref[0]

        def chunk(c, dq_acc):
            rows = pl.ds(pl.multiple_of(c * ck, ck), ck)
            ks = k_ref[0, rows, :]
            vs = v_ref[0, rows, :]
            s = lax.dot_general(qv, ks, _NT, preferred_element_type=F32) * scale
            p = jnp.exp(s - lse_v)
            dp = lax.dot_general(dob, vs, _NT, preferred_element_type=F32)
            ds = (p * (dp - delta) * scale).astype(BF16)
            dv_ref[0, rows, :] += lax.dot_general(p.astype(BF16), dob, _TN, preferred_element_type=F32)
            dk_ref[0, rows, :] += lax.dot_general(ds, qv, _TN, preferred_element_type=F32)
            return dq_acc + jnp.dot(ds, ks, preferred_element_type=F32)

        dq_ref[0] = lax.fori_loop(0, nchunk, chunk, jnp.zeros((tq, dq), F32))

    qspec = lambda d: pl.BlockSpec((1, tq, d), lambda hh, i: (hh, i, 0))
    kspec = lambda d: pl.BlockSpec((1, nk, d), lambda hh, i: (hh, 0, 0))
    return pl.pallas_call(
        body, name="attn_bwd", grid=(h, nq // tq),
        in_specs=[qspec(dq), kspec(dq), kspec(dv), qspec(dv), qspec(dv), qspec(1)],
        out_specs=[qspec(dq), kspec(dq), kspec(dv)],
        out_shape=[jax.ShapeDtypeStruct((h, nq, dq), F32), jax.ShapeDtypeStruct((h, nk, dq), F32),
                   jax.ShapeDtypeStruct((h, nk, dv), F32)],
        compiler_params=_params(("parallel", "arbitrary")),
    )(q, k, v, o, do, lse)


@jax.custom_vjp
def _attention(q, k, v):
    return _attention_fwd(q, k, v)[0]


def _attention_fwd(q, k, v):
    scale = q.shape[-1] ** -0.5
    qb, kb, vb = q.astype(BF16), k.astype(BF16), v.astype(BF16)
    o, lse = _attn_fwd_call(qb, kb, vb, scale)
    return o, (qb, kb, vb, o, lse)


def _attention_bwd(res, do):
    qb, kb, vb, o, lse = res
    return _attn_bwd_call(qb, kb, vb, o, do, lse, qb.shape[-1] ** -0.5)


_attention.defvjp(_attention_fwd, _attention_bwd)


def _bf(x):
    return x.astype(BF16)


def _dot(a, b, dims=(((1,), (0,)), ((), ()))):
    return lax.dot_general(_bf(a), _bf(b), dims, preferred_element_type=F32)


def _sum_all(x):
    return jnp.sum(jnp.sum(x, axis=1, keepdims=True), axis=0, keepdims=True)


def _ret_consts(lgf_ref, lgb_ref):
    c = RET_CHUNK
    lgf = lgf_ref[0][:, :1]
    lgb = lgb_ref[0][:, :1]
    diff = (lax.broadcasted_iota(jnp.int32, (c, c), 0) - lax.broadcasted_iota(jnp.int32, (c, c), 1)).astype(F32)
    mf = diff >= 0
    dmat = jnp.where(mf, jnp.exp(lgf * jnp.where(mf, diff, 0.0)), jnp.exp(lgb * jnp.where(mf, 0.0, -diff)))
    col = lax.broadcasted_iota(jnp.int32, (c, 1), 0).astype(F32)
    return dict(diff=diff, mf=mf, dmat=dmat, col=col,
                xif=jnp.exp(lgf * (col + 1.0)), zf=jnp.exp(lgf * (c - 1.0 - col)),
                xib=jnp.exp(lgb * (c - col)), zb=jnp.exp(lgb * col),
                gf=jnp.exp(lgf * c), gb=jnp.exp(lgb * c))


def _ret_rows(n):
    return pl.ds(pl.multiple_of(n * RET_CHUNK, RET_CHUNK), RET_CHUNK)


def _ret_fwd_call(q, k, v, lgf, lgb, s0f, s0b):
    h, n_tok, dk = q.shape
    dv = v.shape[-1]
    nc = n_tok // RET_CHUNK

    def body(q_ref, k_ref, v_ref, lgf_ref, lgb_ref, s0f_ref, s0b_ref, y_ref, sff_ref, sbf_ref, sb_scr):
        cs = _ret_consts(lgf_ref, lgb_ref)

        def right_to_left(t, sb):
            n = nc - 1 - t
            sb_scr[n] = sb
            return cs["gb"] * sb + _dot(k_ref[0, _ret_rows(n), :] * cs["zb"], v_ref[0, _ret_rows(n), :], _TN)

        sbf_ref[0] = lax.fori_loop(0, nc, right_to_left, s0b_ref[0])

        def left_to_right(n, sf):
            qc, kc, vc = q_ref[0, _ret_rows(n), :], k_ref[0, _ret_rows(n), :], v_ref[0, _ret_rows(n), :]
            p = _dot(qc, kc, _NT) * cs["dmat"]
            y_ref[0, _ret_rows(n), :] = _dot(p, vc) + _dot(qc * cs["xif"], sf) + _dot(qc * cs["xib"], sb_scr[n])
            return cs["gf"] * sf + _dot(kc * cs["zf"], vc, _TN)

        sff_ref[0] = lax.fori_loop(0, nc, left_to_right, s0f_ref[0])

    tok = lambda d: pl.BlockSpec((1, n_tok, d), lambda hh: (hh, 0, 0), pipeline_mode=pl.Buffered(1))
    lg = pl.BlockSpec((1, 1, LANES), lambda hh: (hh, 0, 0))
    st = pl.BlockSpec((1, dk, dv), lambda hh: (hh, 0, 0))
    return pl.pallas_call(
        body, name="ret_fwd_%d" % n_tok, grid=(h,),
        in_specs=[tok(dk), tok(dk), tok(dv), lg, lg, st, st], out_specs=[tok(dv), st, st],
        out_shape=[jax.ShapeDtypeStruct((h, n_tok, dv), F32)] + [jax.ShapeDtypeStruct((h, dk, dv), F32)] * 2,
        scratch_shapes=[pltpu.VMEM((nc, dk, dv), F32)],
        compiler_params=_params(("parallel",)),
    )(q, k, v, lgf, lgb, s0f, s0b)


def _ret_bwd_call(q, k, v, lgf, lgb, s0f, s0b, dy, dsff, dsbf):
    h, n_tok, dk = q.shape
    dv = v.shape[-1]
    nc = n_tok // RET_CHUNK
    c = float(RET_CHUNK)

    def body(q_ref, k_ref, v_ref, lgf_ref, lgb_ref, s0f_ref, s0b_ref, dy_ref, dsff_ref, dsbf_ref,
             dq_ref, dk_ref, dv_ref, dlgf_ref, dlgb_ref, ds0f_ref, ds0b_ref, sb_scr, gf_scr, st_a, st_b):
        cs = _ret_consts(lgf_ref, lgb_ref)

        st_a[...] = s0b_ref[0]
        st_b[...] = dsff_ref[0]

        @pl.loop(0, nc)
        def _(t):
            n = nc - 1 - t
            sb, gf_next = st_a[...], st_b[...]
            sb_scr[n] = sb
            gf_scr[n] = gf_next
            qc, kc, vc, dyc = (r[0, _ret_rows(n), :] for r in (q_ref, k_ref, v_ref, dy_ref))
            st_a[...] = cs["gb"] * sb + _dot(kc * cs["zb"], vc, _TN)
            st_b[...] = _dot(qc * cs["xif"], dyc, _TN) + cs["gf"] * gf_next

        ds0f_ref[0] = st_b[...]

        st_a[...] = s0f_ref[0]
        st_b[...] = dsbf_ref[0]
        dlgf_ref[...] = jnp.zeros_like(dlgf_ref)
        dlgb_ref[...] = jnp.zeros_like(dlgb_ref)

        @pl.loop(0, nc)
        def _(n):
            sf, gb_prev = st_a[...], st_b[...]
            sb, gf_next = sb_scr[n], gf_scr[n]
            qc, kc, vc, dyc = (r[0, _ret_rows(n), :] for r in (q_ref, k_ref, v_ref, dy_ref))
            a = _dot(qc, kc, _NT)
            dp = _dot(dyc, vc, _NT)
            da = _bf(dp * cs["dmat"])
            dqf = _dot(dyc, sf, _NT)
            dqb = _dot(dyc, sb, _NT)
            dkf = _dot(vc, gf_next, _NT)
            dkb = _dot(vc, gb_prev, _NT)
            dq_ref[0, _ret_rows(n), :] = _dot(da, kc) + dqf * cs["xif"] + dqb * cs["xib"]
            dk_ref[0, _ret_rows(n), :] = _dot(da, qc, _TN) + dkf * cs["zf"] + dkb * cs["zb"]
            dv_ref[0, _ret_rows(n), :] = (_dot(a * cs["dmat"], dyc, _TN) + _dot(kc * cs["zf"], gf_next)
                                         + _dot(kc * cs["zb"], gb_prev))
            w = dp * a * cs["dmat"] * cs["diff"]
            row = lambda x: jnp.sum(x, axis=1, keepdims=True)
            dlgf_ref[0] += (_sum_all(jnp.where(cs["mf"], w, 0.0))
                            + _sum_all((cs["col"] + 1.0) * cs["xif"] * row(dqf * qc) + (c - 1.0 - cs["col"]) * cs["zf"] * row(dkf * kc))
                            + c * cs["gf"] * _sum_all(gf_next * sf))
            dlgb_ref[0] += (_sum_all((c - cs["col"]) * cs["xib"] * row(dqb * qc) + cs["col"] * cs["zb"] * row(dkb * kc))
                            + c * cs["gb"] * _sum_all(gb_prev * sb) - _sum_all(jnp.where(cs["mf"], 0.0, w)))
            st_a[...] = cs["gf"] * sf + _dot(kc * cs["zf"], vc, _TN)
            st_b[...] = _dot(qc * cs["xib"], dyc, _TN) + cs["gb"] * gb_prev

        ds0b_ref[0] = st_b[...]

    tok = lambda d: pl.BlockSpec((1, n_tok, d), lambda hh: (hh, 0, 0), pipeline_mode=pl.Buffered(1))
    lg = pl.BlockSpec((1, 1, LANES), lambda hh: (hh, 0, 0))
    st = pl.BlockSpec((1, dk, dv), lambda hh: (hh, 0, 0))
    return pl.pallas_call(
        body, name="ret_bwd_%d" % n_tok, grid=(h,),
        in_specs=[tok(dk), tok(dk), tok(dv), lg, lg, st, st, tok(dv), st, st],
        out_specs=[tok(dk), tok(dk), tok(dv), lg, lg, st, st],
        out_shape=[jax.ShapeDtypeStruct((h, n_tok, dk), F32)] * 2 + [jax.ShapeDtypeStruct((h, n_tok, dv), F32)]
        + [jax.ShapeDtypeStruct((h, 1, LANES), F32)] * 2 + [jax.ShapeDtypeStruct((h, dk, dv), F32)] * 2,
        scratch_shapes=[pltpu.VMEM((nc, dk, dv), F32), pltpu.VMEM((nc, dk, dv), F32), pltpu.VMEM((dk, dv), F32), pltpu.VMEM((dk, dv), F32)],
        compiler_params=_params(("parallel",)),
    )(q, k, v, lgf, lgb, s0f, s0b, dy, dsff, dsbf)


def _lane_bcast(lg):
    return jnp.broadcast_to(lg[:, None, None], (lg.shape[0], 1, LANES))


@jax.custom_vjp
def _retention(q, k, v, lgf, lgb, s0f, s0b):
    return tuple(_ret_fwd_call(q, k, v, _lane_bcast(lgf), _lane_bcast(lgb), s0f, s0b))


def _retention_fwd(q, k, v, lgf, lgb, s0f, s0b):
    return _retention(q, k, v, lgf, lgb, s0f, s0b), (q, k, v, lgf, lgb, s0f, s0b)


def _retention_bwd(res, cts):
    q, k, v, lgf, lgb, s0f, s0b = res
    dy, dsff, dsbf = cts
    dq, dk, dv, dlgf, dlgb, ds0f, ds0b = _ret_bwd_call(q, k, v, _lane_bcast(lgf), _lane_bcast(lgb), s0f, s0b, dy, dsff, dsbf)
    return dq, dk, dv, dlgf[:, 0, 0], dlgb[:, 0, 0], ds0f, ds0b


_retention.defvjp(_retention_fwd, _retention_bwd)


def _gn_specs(y):
    h, n, dv = y.shape
    r = _tile(n, 512, SUBLANES)
    return (h, n, dv, r, pl.BlockSpec((1, r, dv), lambda i, hh: (hh, i, 0)), pl.BlockSpec((r, dv), lambda i, hh: (i, hh)))


def _gn_norm(yv):
    mu = jnp.mean(yv, axis=-1, keepdims=True)
    yc = yv - mu
    rstd = lax.rsqrt(jnp.mean(yc * yc, axis=-1, keepdims=True) + GN_EPS)
    return yc * rstd, rstd


def _gn_gate_fwd_call(y, gate):
    h, n, dv, r, yspec, gspec = _gn_specs(y)

    def body(y_ref, g_ref, o_ref):
        gv = g_ref[...]
        o_ref[...] = gv * jax.nn.sigmoid(gv) * _gn_norm(y_ref[0])[0]

    return pl.pallas_call(
        body, name="gn_gate", grid=(n // r, h), in_specs=[yspec, gspec], out_specs=gspec,
        out_shape=jax.ShapeDtypeStruct((n, h * dv), F32), compiler_params=_params(("parallel", "parallel")),
    )(y, gate)


def _gn_gate_bwd_call(y, gate, dout):
    h, n, dv, r, yspec, gspec = _gn_specs(y)

    def body(y_ref, g_ref, do_ref, dy_ref, dg_ref):
        gv = g_ref[...]
        dov = do_ref[...]
        yn, rstd = _gn_norm(y_ref[0])
        sg = jax.nn.sigmoid(gv)
        dg_ref[...] = dov * yn * (sg * (1.0 + gv * (1.0 - sg)))
        dyn = dov * (gv * sg)
        dy_ref[0] = rstd * (dyn - jnp.mean(dyn, axis=-1, keepdims=True) - yn * jnp.mean(dyn * yn, axis=-1, keepdims=True))

    return pl.pallas_call(
        body, name="gn_gate_b", grid=(n // r, h), in_specs=[yspec, gspec, gspec], out_specs=[yspec, gspec],
        out_shape=[jax.ShapeDtypeStruct((h, n, dv), F32), jax.ShapeDtypeStruct((n, h * dv), F32)],
        compiler_params=_params(("parallel", "parallel")),
    )(y, gate, dout)


@jax.custom_vjp
def _gn_gate(y, gate):
    return _gn_gate_fwd_call(y, gate)


_gn_gate.defvjp(lambda y, gate: (_gn_gate_fwd_call(y, gate), (y, gate)),
                lambda res, dout: tuple(_gn_gate_bwd_call(res[0], res[1], dout)))


def _rope_tables(pos, dim, base):
    inv = base ** (-jnp.arange(0, dim, 2, dtype=F32) / dim)
    ang = pos.astype(F32)[:, None] * inv[None, :]
    return jnp.cos(ang)[:, None, :], jnp.sin(ang)[:, None, :]


def _rotate(x, cos, sin):
    x1, x2 = jnp.split(x, 2, axis=-1)
    return jnp.concatenate([x1 * cos - x2 * sin, x2 * cos + x1 * sin], axis=-1)


def _axial_rope(x, row_tab, col_tab):
    xr, xc = jnp.split(x, 2, axis=-1)
    return jnp.concatenate([_rotate(xr, *row_tab), _rotate(xc, *col_tab)], axis=-1)


def _heads(t):
    return jnp.swapaxes(t, 0, 1)


def _local_loss(x, mods_lat, mods_ctx, small, big, ctx, target):
    n_lat, d = x.shape
    n_ctx = ctx.shape[0]
    both = lambda i: jnp.stack([mods_ctx[i], mods_lat[i]])[:, None, :]
    lat = lambda i: mods_lat[i][None, None, :]

    xs = jnp.concatenate([ctx, x], axis=0)
    x1 = _make_ffn(n_ctx, "ffn1")(xs, small["norm1_g"], both(0), both(1), both(2),
                                  big["ffn1_wg"], big["ffn1_wu"], big["ffn1_wo"])
    proj = _make_normmod_linear(n_ctx, "mix_in")(x1, small["norm2_g"], both(3), both(4), big["mix_in"])
    offs = np.cumsum((0,) + MIX_SPLITS)
    part = lambda i, rows: proj[rows, offs[i]:offs[i + 1]]
    lat_rows, ctx_rows = slice(n_ctx, None), slice(0, n_ctx)

    zq = jnp.zeros((1, 1, MLA_Q_RANK), F32)
    zkv = jnp.zeros((1, 1, MLA_KV_RANK), F32)
    q = _make_normmod_linear(0, "mla_q")(part(4, lat_rows), small["mla_q_norm_g"], zq, zq, big["w_uq"])
    kv = _make_normmod_linear(0, "mla_kv")(part(5, slice(None)), small["mla_kv_norm_g"], zkv, zkv, big["w_ukv"])

    lgf = jax.nn.log_sigmoid(small["ret_decay_fwd"][0])
    lgb = jax.nn.log_sigmoid(small["ret_decay_bwd"][0])
    ret_tab = _rope_tables(jnp.arange(n_lat), RET_DK, ROPE_BASE)
    hd = lambda t, dd: t.reshape(t.shape[0], RET_HEADS, dd)
    s_zero = jnp.zeros((RET_HEADS, RET_DK, RET_DV), F32)
    _, s_f, s_b = _retention(_heads(hd(part(0, ctx_rows), RET_DK)), _heads(hd(part(1, ctx_rows), RET_DK) * (RET_DK ** -0.5)),
                             _heads(hd(part(2, ctx_rows), RET_DV)), lgf, lgb, s_zero, s_zero)
    rq = _rotate(hd(part(0, lat_rows), RET_DK), *ret_tab)
    rk = _rotate(hd(part(1, lat_rows), RET_DK) * (RET_DK ** -0.5), *ret_tab)
    y_lat, _, _ = _retention(_heads(rq), _heads(rk), _heads(hd(part(2, lat_rows), RET_DV)), lgf, lgb, s_f, s_b)
    ret_out = _gn_gate(y_lat, part(3, lat_rows))

    pos = jnp.arange(n_lat)
    row_tab = _rope_tables(pos // GRID_W, MLA_ROPE // 2, ROPE_BASE)
    col_tab = _rope_tables(pos % GRID_W, MLA_ROPE // 2, ROPE_BASE)
    q = q.reshape(n_lat, MLA_HEADS, MLA_NOPE + MLA_ROPE)
    q_all = jnp.concatenate([q[..., :MLA_NOPE], _axial_rope(q[..., MLA_NOPE:], row_tab, col_tab)], axis=-1)
    kv = kv.reshape(n_ctx + n_lat, MLA_HEADS, MLA_NOPE + MLA_V)
    kr_lat = _axial_rope(part(6, lat_rows)[:, None, :], row_tab, col_tab)
    kr = jnp.concatenate([kr_lat, part(6, ctx_rows)[:, None, :]], axis=0)
    kv_lat_first = jnp.concatenate([kv[n_ctx:], kv[:n_ctx]], axis=0)
    k_all = jnp.concatenate([kv_lat_first[..., :MLA_NOPE], jnp.broadcast_to(kr, (n_ctx + n_lat, MLA_HEADS, MLA_ROPE))], axis=-1)
    mla = _attention(_heads(q_all), _heads(k_all), _heads(kv_lat_first[..., MLA_NOPE:]))
    mla_out = _heads(mla).reshape(n_lat, MLA_HEADS * MLA_V)

    x2 = _make_linear_gated_res("mix_out")(x1[n_ctx:], jnp.concatenate([ret_out, mla_out], axis=-1), big["mix_out"], lat(5))
    x3 = _make_ffn(0, "ffn2")(x2, small["norm3_g"], lat(6), lat(7), lat(8), big["ffn2_wg"], big["ffn2_wu"], big["ffn2_wo"])
    return _final_loss(x3, small["final_norm_g"][None, :], target)


HBM_SPEC = pl.BlockSpec(memory_space=pl.ANY)
VMEM_SPEC = pl.BlockSpec(memory_space=pltpu.VMEM)
ALL_PEERS = (1, 2, 3, 4, 5, 6, 7)
CHIP_PEERS = (4, 2, 6)


def _me():
    return lax.axis_index("x"), lax.axis_index("y"), lax.axis_index("c")


def _flip(pos, mask):
    x, y, c = pos
    return (1 - x if mask & 4 else x, 1 - y if mask & 2 else y, 1 - c if mask & 1 else c)


def _allgather_small(block, masks, chips_only, name):
    r, c = block.shape
    n_slots = 4 if chips_only else 8

    def body(x_ref, out_ref, send_sems, recv_sems, local_sem):
        pos = _me()
        slot = 2 * pos[0] + pos[1] if chips_only else 4 * pos[0] + 2 * pos[1] + pos[2]
        local = pltpu.make_async_copy(x_ref, out_ref.at[slot], local_sem)
        local.start()
        copies = [pltpu.make_async_remote_copy(src_ref=x_ref, dst_ref=out_ref.at[slot], send_sem=send_sems.at[j], recv_sem=recv_sems.at[j],
                                               device_id=_flip(pos, mask), device_id_type=MESH) for j, mask in enumerate(masks)]
        for cp in copies:
            cp.start()
        for cp in copies:
            cp.wait()
        local.wait()

    return pl.pallas_call(
        body, name=name, in_specs=[VMEM_SPEC], out_specs=VMEM_SPEC,
        out_shape=jax.ShapeDtypeStruct((n_slots, r, c), block.dtype),
        scratch_shapes=[pltpu.SemaphoreType.DMA((len(masks),)), pltpu.SemaphoreType.DMA((len(masks),)), pltpu.SemaphoreType.DMA],
        compiler_params=pltpu.CompilerParams(vmem_limit_bytes=VMEM_LIMIT_BYTES),
    )(block)


def _gather_weights(shards):
    n = len(shards)

    def body(*refs):
        ins, outs = refs[:n], refs[n:2 * n]
        send_sems, recv_sems, pass_send, pass_recv, local_sems = refs[2 * n:]
        pos = _me()
        x, y, c = pos
        sibling = _flip(pos, 1)
        local, sends, passes = [], [], []
        for w in range(n):
            half = shards[w].shape[0] // 2
            mine = pl.ds(c * half, half)
            cp = pltpu.make_async_copy(ins[w], outs[w].at[2 * x + y], local_sems.at[w])
            cp.start()
            local.append(cp)
            for j, mask in enumerate(CHIP_PEERS):
                cp = pltpu.make_async_remote_copy(src_ref=ins[w].at[mine], dst_ref=outs[w].at[2 * x + y, mine], send_sem=send_sems.at[3 * w + j],
                                                  recv_sem=recv_sems.at[3 * w + j], device_id=_flip(pos, mask), device_id_type=MESH)
                cp.start()
                sends.append(cp)
        for w in range(n):
            half = shards[w].shape[0] // 2
            mine = pl.ds(c * half, half)
            for j, mask in enumerate(CHIP_PEERS):
                px, py, _ = _flip(pos, mask)
                landed = outs[w].at[2 * px + py, mine]
                pltpu.make_async_remote_copy(src_ref=landed, dst_ref=landed, send_sem=send_sems.at[3 * w + j], recv_sem=recv_sems.at[3 * w + j],
                                             device_id=_flip(pos, mask), device_id_type=MESH).wait_recv()
                cp = pltpu.make_async_remote_copy(src_ref=landed, dst_ref=landed, send_sem=pass_send.at[3 * w + j], recv_sem=pass_recv.at[3 * w + j],
                                                  device_id=sibling, device_id_type=MESH)
                cp.start()
                passes.append(cp)
        for w in range(n):
            half = shards[w].shape[0] // 2
            theirs = pl.ds((1 - c) * half, half)
            for j, mask in enumerate(CHIP_PEERS):
                px, py, _ = _flip(pos, mask)
                slab = outs[w].at[2 * px + py, theirs]
                pltpu.make_async_remote_copy(src_ref=slab, dst_ref=slab, send_sem=pass_send.at[3 * w + j], recv_sem=pass_recv.at[3 * w + j],
                                             device_id=sibling, device_id_type=MESH).wait_recv()
        for cp in sends + passes:
            cp.wait_send()
        for cp in local:
            cp.wait()

    dma = lambda k: pltpu.SemaphoreType.DMA((k,))
    return pl.pallas_call(
        body, name="gather_weights", in_specs=[HBM_SPEC] * n, out_specs=[HBM_SPEC] * n,
        out_shape=[jax.ShapeDtypeStruct((4,) + s.shape, s.dtype) for s in shards],
        scratch_shapes=[dma(3 * n), dma(3 * n), dma(3 * n), dma(3 * n), dma(n)],
    )(*shards)


def _pair_swap_halves(grads):
    n = len(grads)

    def body(*refs):
        ins, outs = refs[:n], refs[n:2 * n]
        send_sems, recv_sems = refs[2 * n:]
        pos = _me()
        copies = []
        for w in range(n):
            half = grads[w].shape[1] // 2
            cp = pltpu.make_async_remote_copy(src_ref=ins[w].at[:, pl.ds((1 - pos[2]) * half, half), :], dst_ref=outs[w], send_sem=send_sems.at[w],
                                              recv_sem=recv_sems.at[w], device_id=_flip(pos, 1), device_id_type=MESH)
            cp.start()
            copies.append(cp)
        for cp in copies:
            cp.wait()

    return pl.pallas_call(
        body, name="pair_swap_halves", in_specs=[HBM_SPEC] * n, out_specs=[HBM_SPEC] * n,
        out_shape=[jax.ShapeDtypeStruct((4, g.shape[1] // 2, g.shape[2]), g.dtype) for g in grads],
        scratch_shapes=[pltpu.SemaphoreType.DMA((n,)), pltpu.SemaphoreType.DMA((n,))],
    )(*grads)


def _chip_scatter(parts):
    n = len(parts)

    def body(*refs):
        ins, outs = refs[:n], refs[n:2 * n]
        send_sems, recv_sems, local_sems = refs[2 * n:]
        pos = _me()
        me = 2 * pos[0] + pos[1]
        copies, local = [], []
        for w in range(n):
            cp = pltpu.make_async_copy(ins[w].at[me], outs[w].at[me], local_sems.at[w])
            cp.start()
            local.append(cp)
            for j, mask in enumerate(CHIP_PEERS):
                px, py, _ = _flip(pos, mask)
                cp = pltpu.make_async_remote_copy(src_ref=ins[w].at[2 * px + py], dst_ref=outs[w].at[me], send_sem=send_sems.at[3 * w + j],
                                                  recv_sem=recv_sems.at[3 * w + j], device_id=_flip(pos, mask), device_id_type=MESH)
                cp.start()
                copies.append(cp)
        for cp in copies:
            cp.wait()
        for cp in local:
            cp.wait()

    dma = lambda k: pltpu.SemaphoreType.DMA((k,))
    return pl.pallas_call(
        body, name="chip_scatter", in_specs=[HBM_SPEC] * n, out_specs=[HBM_SPEC] * n,
        out_shape=[jax.ShapeDtypeStruct(p.shape, p.dtype) for p in parts],
        scratch_shapes=[dma(3 * n), dma(3 * n), dma(n)],
    )(*parts)


def _pair_join_halves(halves):
    n = len(halves)

    def body(*refs):
        ins, outs = refs[:n], refs[n:2 * n]
        send_sems, recv_sems, local_sems = refs[2 * n:]
        pos = _me()
        copies, local = [], []
        for w in range(n):
            h = halves[w].shape[0]
            mine = outs[w].at[pl.ds(pos[2] * h, h)]
            cp = pltpu.make_async_copy(ins[w], mine, local_sems.at[w])
            cp.start()
            local.append(cp)
            cp = pltpu.make_async_remote_copy(src_ref=ins[w], dst_ref=mine, send_sem=send_sems.at[w], recv_sem=recv_sems.at[w],
                                              device_id=_flip(pos, 1), device_id_type=MESH)
            cp.start()
            copies.append(cp)
        for cp in copies:
            cp.wait()
        for cp in local:
            cp.wait()

    dma = lambda k: pltpu.SemaphoreType.DMA((k,))
    return pl.pallas_call(
        body, name="pair_join_halves", in_specs=[HBM_SPEC] * n, out_specs=[HBM_SPEC] * n,
        out_shape=[jax.ShapeDtypeStruct((2 * h.shape[0], h.shape[1]), h.dtype) for h in halves],
        scratch_shapes=[dma(n), dma(n), dma(n)],
    )(*halves)


def _add_pair(mine, theirs, name):
    s, h, c = mine.shape
    r = _tile(h, max(16, (1 << 19) // c), 16)

    def body(a_ref, b_ref, o_ref):
        o_ref[...] = (a_ref[...].astype(F32) + b_ref[...].astype(F32)).astype(BF16)

    blk = pl.BlockSpec((1, r, c), lambda i, j: (i, j, 0))
    return pl.pallas_call(
        body, name=name, grid=(s, h // r), in_specs=[blk, blk], out_specs=blk,
        out_shape=jax.ShapeDtypeStruct(mine.shape, BF16), compiler_params=_params(("parallel", "parallel")),
    )(mine, theirs)


def _sum_slots(parts, name):
    s, h, c = parts.shape
    r = _tile(h, max(16, (1 << 18) // c), 16)

    def body(p_ref, o_ref):
        acc = p_ref[0].astype(F32)
        for k in range(1, s):
            acc = acc + p_ref[k].astype(F32)
        o_ref[...] = acc

    return pl.pallas_call(
        body, name=name, grid=(h // r,), in_specs=[pl.BlockSpec((s, r, c), lambda i: (0, i, 0))],
        out_specs=pl.BlockSpec((r, c), lambda i: (i, 0)),
        out_shape=jax.ShapeDtypeStruct((h, c), F32), compiler_params=_params(("parallel",)),
    )(parts)


def _reduce_scatter_grads(stacked):
    c = lax.axis_index("c")
    theirs = _pair_swap_halves(stacked)
    parts = []
    for w, (g, t) in enumerate(zip(stacked, theirs)):
        half = g.shape[1] // 2
        mine = lax.dynamic_slice_in_dim(g, c * half, half, axis=1)
        parts.append(_add_pair(mine, t, "rs_add_pair_%d" % w))
    landed = _chip_scatter(parts)
    halves = [_sum_slots(p, "rs_sum_slots_%d" % w) for w, p in enumerate(landed)]
    return _pair_join_halves(halves)


def _adamw_math(w, g, m, v):
    m = ADAM_B1 * m + (1.0 - ADAM_B1) * g
    v = ADAM_B2 * v + (1.0 - ADAM_B2) * (g * g)
    m_hat = m / (1.0 - ADAM_B1 ** ADAM_STEP)
    v_hat = v / (1.0 - ADAM_B2 ** ADAM_STEP)
    return -ADAM_LR * (m_hat / (jnp.sqrt(v_hat) + ADAM_EPS) + ADAM_WD * w), m, v


def _adamw(w, g, m, v, name):
    rows, cols = w.shape
    r = _tile(rows, max(SUBLANES, (1 << 18) // cols), SUBLANES)

    def body(w_ref, g_ref, m_ref, v_ref, d_ref, mo_ref, vo_ref):
        d_ref[...], mo_ref[...], vo_ref[...] = _adamw_math(w_ref[...], g_ref[...], m_ref[...], v_ref[...])

    blk = pl.BlockSpec((r, cols), lambda i: (i, 0))
    return pl.pallas_call(
        body, name=name, grid=(rows // r,), in_specs=[blk] * 4, out_specs=[blk] * 3,
        out_shape=[jax.ShapeDtypeStruct(w.shape, F32)] * 3, compiler_params=_params(("parallel",)),
    )(w, g, m, v)


def _adamw_reduced(parts, w, m, v, name):
    def body(p_ref, w_ref, m_ref, v_ref, g_ref, d_ref, mo_ref, vo_ref):
        g = p_ref[0]
        for k in range(1, parts.shape[0]):
            g = g + p_ref[k]
        g_ref[...] = g
        d_ref[...], mo_ref[...], vo_ref[...] = _adamw_math(w_ref[...], g, m_ref[...], v_ref[...])

    return pl.pallas_call(
        body, name=name, in_specs=[VMEM_SPEC] * 4, out_specs=[VMEM_SPEC] * 4,
        out_shape=[jax.ShapeDtypeStruct(w.shape, F32)] * 4,
        compiler_params=pltpu.CompilerParams(vmem_limit_bytes=VMEM_LIMIT_BYTES),
    )(parts, w, m, v)


WEIGHTS = ("c_ctx", "ada_w", "ada_b", "norm1_g", "ffn1_w_in", "ffn1_w_out", "norm2_g", "mix_w_in", "ret_decay_fwd", "ret_decay_bwd",
           "mla_q_norm_g", "mla_w_uq", "mla_kv_norm_g", "mla_w_ukv", "mix_w_out", "norm3_g", "ffn2_w_in", "ffn2_w_out", "final_norm_g")
SMALL = ("c_ctx", "ada_b", "norm1_g", "norm2_g", "ret_decay_fwd", "ret_decay_bwd", "mla_q_norm_g", "mla_kv_norm_g", "norm3_g", "final_norm_g")
BIG = (("ffn1_w_in", 1), ("ffn1_w_out", 0), ("mix_w_in", 1), ("mla_w_uq", 1), ("mla_w_ukv", 1), ("mix_w_out", 0), ("ffn2_w_in", 1), ("ffn2_w_out", 0))


def _pack(vectors):
    flat = jnp.concatenate([v.reshape(-1) for v in vectors])
    return jnp.pad(flat, (0, -flat.shape[0] % (SUBLANES * LANES))).reshape(SUBLANES, -1)


def _rows8(a):
    return a.reshape(a.shape[0] * SUBLANES, a.shape[1] // SUBLANES)


def _unpack(packed, like):
    packed = packed.reshape(-1)
    out, off = [], 0
    for ref in like:
        out.append(packed[off:off + ref.size].reshape(ref.shape))
        off += ref.size
    return out


def kernel(x, c, ctx, c_ctx, ada_w, ada_b, norm1_g, ffn1_w_in, ffn1_w_out, norm2_g, mix_w_in, ret_decay_fwd, ret_decay_bwd, mla_q_norm_g, mla_w_uq, mla_kv_norm_g, mla_w_ukv, mix_w_out, norm3_g, ffn2_w_in, ffn2_w_out, final_norm_g, loss_target, m_c_ctx, m_ada_w, m_ada_b, m_norm1_g, m_ffn1_w_in, m_ffn1_w_out, m_norm2_g, m_mix_w_in, m_ret_decay_fwd, m_ret_decay_bwd, m_mla_q_norm_g, m_mla_w_uq, m_mla_kv_norm_g, m_mla_w_ukv, m_mix_w_out, m_norm3_g, m_ffn2_w_in, m_ffn2_w_out, m_final_norm_g, v_c_ctx, v_ada_w, v_ada_b, v_norm1_g, v_ffn1_w_in, v_ffn1_w_out, v_norm2_g, v_mix_w_in, v_ret_decay_fwd, v_ret_decay_bwd, v_mla_q_norm_g, v_mla_w_uq, v_mla_kv_norm_g, v_mla_w_ukv, v_mix_w_out, v_norm3_g, v_ffn2_w_in, v_ffn2_w_out, v_final_norm_g):
    w = dict(c_ctx=c_ctx, ada_w=ada_w, ada_b=ada_b, norm1_g=norm1_g, ffn1_w_in=ffn1_w_in, ffn1_w_out=ffn1_w_out, norm2_g=norm2_g,
             mix_w_in=mix_w_in, ret_decay_fwd=ret_decay_fwd, ret_decay_bwd=ret_decay_bwd, mla_q_norm_g=mla_q_norm_g, mla_w_uq=mla_w_uq,
             mla_kv_norm_g=mla_kv_norm_g, mla_w_ukv=mla_w_ukv, mix_w_out=mix_w_out, norm3_g=norm3_g, ffn2_w_in=ffn2_w_in,
             ffn2_w_out=ffn2_w_out, final_norm_g=final_norm_g)
    mom_m = dict(zip(WEIGHTS, (m_c_ctx, m_ada_w, m_ada_b, m_norm1_g, m_ffn1_w_in, m_ffn1_w_out, m_norm2_g, m_mix_w_in, m_ret_decay_fwd,
                               m_ret_decay_bwd, m_mla_q_norm_g, m_mla_w_uq, m_mla_kv_norm_g, m_mla_w_ukv, m_mix_w_out, m_norm3_g,
                               m_ffn2_w_in, m_ffn2_w_out, m_final_norm_g)))
    mom_v = dict(zip(WEIGHTS, (v_c_ctx, v_ada_w, v_ada_b, v_norm1_g, v_ffn1_w_in, v_ffn1_w_out, v_norm2_g, v_mix_w_in, v_ret_decay_fwd,
                               v_ret_decay_bwd, v_mla_q_norm_g, v_mla_w_uq, v_mla_kv_norm_g, v_mla_w_ukv, v_mix_w_out, v_norm3_g,
                               v_ffn2_w_in, v_ffn2_w_out, v_final_norm_g)))
    xi, yi, ci = _me()
    chip = 2 * xi + yi
    example = 2 * chip + ci
    d = x.shape[-1]
    n_mod = ada_b.shape[-1] // d

    c_all = _allgather_small(_rows8(c), ALL_PEERS, False, "gather_c").reshape(8, d)
    cond = jnp.concatenate([c_all, jnp.broadcast_to(c_ctx[None, :], (8, d))], axis=0)
    cond_act = jax.nn.silu(cond)
    n_cols = ada_w.shape[-1]
    bias = lax.dynamic_slice_in_dim(ada_b, chip * n_cols, n_cols, axis=1)
    mods_cols = _matmul(cond_act, ada_w[0], "nn", F32, "ada_fwd", add=jnp.broadcast_to(bias, (16, n_cols)))
    mods = jnp.swapaxes(_allgather_small(mods_cols, CHIP_PEERS, True, "gather_mods"), 0, 1).reshape(16, 4 * n_cols)
    mods_lat = lax.dynamic_slice_in_dim(mods, example, 1, axis=0).reshape(n_mod, d)
    mods_ctx = mods[8].reshape(n_mod, d)

    stacked = _gather_weights([w[name][0].astype(BF16) for name, _ in BIG])
    full = {}
    for (name, axis), st in zip(BIG, stacked):
        full[name] = st.reshape(-1, st.shape[-1]) if axis == 0 else jnp.swapaxes(st, 0, 1).reshape(st.shape[1], -1)
    d_ff = full["ffn1_w_out"].shape[0]
    big = dict(ffn1_wg=full["ffn1_w_in"][:, :d_ff], ffn1_wu=full["ffn1_w_in"][:, d_ff:], ffn1_wo=full["ffn1_w_out"],
               mix_in=jnp.pad(full["mix_w_in"], ((0, 0), (0, MIX_IN_PAD - MIX_IN))), w_uq=full["mla_w_uq"], w_ukv=full["mla_w_ukv"],
               mix_out=full["mix_w_out"], ffn2_wg=full["ffn2_w_in"][:, :d_ff], ffn2_wu=full["ffn2_w_in"][:, d_ff:], ffn2_wo=full["ffn2_w_out"])
    small = {k: w[k] for k in ("norm1_g", "norm2_g", "norm3_g", "final_norm_g", "mla_q_norm_g", "mla_kv_norm_g", "ret_decay_fwd", "ret_decay_bwd")}

    loss_mine, (dx, dmods_lat, dmods_ctx, dsmall, dbig) = jax.value_and_grad(_local_loss, argnums=(0, 1, 2, 3, 4))(
        x[0], mods_lat, mods_ctx, small, big, ctx[0], loss_target[0])
    loss = lax.psum(loss_mine, ("x", "y", "c"))

    dmods = _allgather_small(_rows8(jnp.stack([dmods_lat.reshape(-1), dmods_ctx.reshape(-1)])), ALL_PEERS, False, "gather_dmods")
    dmods = dmods.reshape(8, 2, n_mod * d)
    dmods_rows = jnp.concatenate([dmods[:, 0, :], dmods[:, 1, :]], axis=0)
    dmods_cols = lax.dynamic_slice_in_dim(dmods_rows, chip * n_cols, n_cols, axis=1)
    g_ada_w = _matmul(cond_act, dmods_cols, "tn", F32, "ada_dw")
    dcond_act = _matmul(dmods_cols, ada_w[0], "nt", F32, "ada_dcond")
    sig = jax.nn.sigmoid(c_ctx)
    dc_ctx = jnp.sum(dcond_act[8:], axis=0) * (sig * (1.0 + c_ctx * (1.0 - sig)))
    share = dict(dsmall)
    share["c_ctx"] = jnp.where(ci == 0, dc_ctx, jnp.zeros_like(dc_ctx))
    share["ada_b"] = (dmods_lat + dmods_ctx).reshape(1, -1)
    parts = _allgather_small(_pack([share[k] for k in SMALL]), ALL_PEERS, False, "gather_small_grads")
    packed = _adamw_reduced(parts, _pack([w[k] for k in SMALL]), _pack([mom_m[k] for k in SMALL]), _pack([mom_v[k] for k in SMALL]),
                            "adamw_small")
    grads, deltas, new_m, new_v = ({k: a for k, a in zip(SMALL, _unpack(p, [w[k] for k in SMALL]))} for p in packed)

    dfull = {"ffn1_w_in": jnp.concatenate([dbig["ffn1_wg"], dbig["ffn1_wu"]], axis=1), "ffn1_w_out": dbig["ffn1_wo"],
             "mix_w_in": dbig["mix_in"][:, :MIX_IN], "mla_w_uq": dbig["w_uq"], "mla_w_ukv": dbig["w_ukv"], "mix_w_out": dbig["mix_out"],
             "ffn2_w_in": jnp.concatenate([dbig["ffn2_wg"], dbig["ffn2_wu"]], axis=1), "ffn2_w_out": dbig["ffn2_wo"]}
    dstacked = []
    for name, axis in BIG:
        g = dfull[name]
        dstacked.append(g.reshape(4, g.shape[0] // 4, g.shape[1]) if axis == 0 else jnp.swapaxes(g.reshape(g.shape[0], 4, g.shape[1] // 4), 0, 1))
    for (name, _), g in zip(BIG, _reduce_scatter_grads(dstacked)):
        grads[name] = g[None]
    grads["ada_w"] = g_ada_w[None]
    for name in ("ada_w",) + tuple(n for n, _ in BIG):
        dl, mo, vo = _adamw(w[name][0], grads[name][0], mom_m[name][0], mom_v[name][0], "adamw_" + name)
        deltas[name], new_m[name], new_v[name] = dl[None], mo[None], vo[None]

    return (loss, dx[None], *[grads[k] for k in WEIGHTS], *[deltas[k] for k in WEIGHTS], *[new_m[k] for k in WEIGHTS],
            *[new_v[k] for k in WEIGHTS])
```

```python
import functools

import jax
import jax.numpy as jnp
import numpy as np
from jax import lax
from jax.experimental import pallas as pl
from jax.experimental.pallas import tpu as pltpu

F32 = jnp.float32
BF16 = jnp.bfloat16
MESH = pl.DeviceIdType.MESH

VMEM_LIMIT_BYTES = 52 * 1024 * 1024
LANES = 128
SUBLANES = 8

D_FF_SPLIT = 2
RET_HEADS, RET_DK, RET_DV, RET_CHUNK = 8, 64, 128, 128
MLA_HEADS, MLA_Q_RANK, MLA_KV_RANK, MLA_NOPE, MLA_ROPE, MLA_V = 8, 512, 256, 128, 64, 128
GRID_W = 64
ROPE_BASE = 10000.0
RMS_EPS = 1e-6
GN_EPS = 1e-5
MIX_SPLITS = (RET_HEADS * RET_DK, RET_HEADS * RET_DK, RET_HEADS * RET_DV, RET_HEADS * RET_DV,
              MLA_Q_RANK, MLA_KV_RANK, MLA_ROPE)
MIX_IN = sum(MIX_SPLITS)
MIX_IN_PAD = 4096
ADAM_LR, ADAM_B1, ADAM_B2, ADAM_EPS, ADAM_WD, ADAM_STEP = 0.001, 0.9, 0.999, 1e-08, 0.01, 10


def _tile(n, pref, align):
    best = None
    t = align
    while t <= min(n, pref):
        if n % t == 0:
            best = t
        t += align
    return n if best is None else best


def _params(sem=None):
    return pltpu.CompilerParams(dimension_semantics=sem, vmem_limit_bytes=VMEM_LIMIT_BYTES)


def _matmul(a, b, mode, out_dtype, name, add=None, tm=1024, tn=1024, tk=2048):
    if mode == "nn":
        (m, k), (k2, n) = a.shape, b.shape
        dims = (((1,), (0,)), ((), ()))
    elif mode == "nt":
        (m, k), (n, k2) = a.shape, b.shape
        dims = (((1,), (1,)), ((), ()))
    else:
        (k, m), (k2, n) = a.shape, b.shape
        dims = (((0,), (0,)), ((), ()))
    assert k == k2, (a.shape, b.shape, mode)
    tm = _tile(m, tm, LANES if mode == "tn" else 16)
    tn = _tile(n, tn, LANES)
    tk = _tile(k, tk, LANES if mode != "tn" else 16)
    nk = k // tk
    a_spec = pl.BlockSpec((tk, tm), lambda i, j, kk: (kk, i)) if mode == "tn" else pl.BlockSpec((tm, tk), lambda i, j, kk: (i, kk))
    b_spec = pl.BlockSpec((tn, tk), lambda i, j, kk: (j, kk)) if mode == "nt" else pl.BlockSpec((tk, tn), lambda i, j, kk: (kk, j))
    o_spec = pl.BlockSpec((tm, tn), lambda i, j, kk: (i, j))
    has_add = add is not None

    def body(*refs):
        a_ref, b_ref = refs[0], refs[1]
        add_ref = refs[2] if has_add else None
        o_ref = refs[2 + has_add]
        p = lax.dot_general(a_ref[...].astype(BF16), b_ref[...].astype(BF16), dims, preferred_element_type=F32)
        if nk == 1:
            if has_add:
                p = p + add_ref[...].astype(F32)
            o_ref[...] = p.astype(out_dtype)
        else:
            acc = refs[3 + has_add]
            kk = pl.program_id(2)

            @pl.when(kk == 0)
            def _():
                acc[...] = p + add_ref[...].astype(F32) if has_add else p

            @pl.when(kk > 0)
            def _():
                acc[...] += p

            @pl.when(kk == nk - 1)
            def _():
                o_ref[...] = acc[...].astype(out_dtype)

    return pl.pallas_call(
        body, name=name, grid=(m // tm, n // tn, nk),
        in_specs=[a_spec, b_spec] + ([o_spec] if has_add else []),
        out_specs=o_spec,
        out_shape=jax.ShapeDtypeStruct((m, n), out_dtype),
        scratch_shapes=[pltpu.VMEM((tm, tn), F32)] if nk > 1 else [],
        compiler_params=_params(("parallel", "parallel", "arbitrary")),
    )(*((a, b, add) if has_add else (a, b)))


def _row_tile(t, n_ctx, d):
    pref = max(SUBLANES, min(256, (1 << 19) // d))
    r = _tile(int(np.gcd(t, n_ctx)) if n_ctx else t, pref, SUBLANES)
    return r, (n_ctx // r if n_ctx else 0)


def _seg_map(nct):
    if nct:
        return lambda i: (jnp.minimum(i // nct, 1), 0, 0)
    return lambda i: (0, 0, 0)


def _normmod_fwd(x, g, shift, scale, n_ctx, name):
    t, d = x.shape
    r, nct = _row_tile(t, n_ctx, d)

    def body(x_ref, g_ref, sh_ref, sc_ref, h_ref):
        xv = x_ref[...]
        rstd = lax.rsqrt(jnp.mean(xv * xv, axis=-1, keepdims=True) + RMS_EPS)
        n = xv * rstd * g_ref[...]
        h_ref[...] = (n * (1.0 + sc_ref[0]) + sh_ref[0]).astype(BF16)

    row = pl.BlockSpec((r, d), lambda i: (i, 0))
    seg = pl.BlockSpec((1, 1, d), _seg_map(nct))
    return pl.pallas_call(
        body, name=name, grid=(t // r,),
        in_specs=[row, pl.BlockSpec((1, d), lambda i: (0, 0)), seg, seg],
        out_specs=row, out_shape=jax.ShapeDtypeStruct((t, d), BF16),
        compiler_params=_params(("parallel",)),
    )(x, g, shift, scale)


def _normmod_bwd(dh, x, g, shift, scale, dres, n_ctx, name):
    t, d = x.shape
    r, nct = _row_tile(t, n_ctx, d)
    has_res = dres is not None
    nseg = shift.shape[0]

    def body(*refs):
        dh_ref, x_ref, g_ref, sh_ref, sc_ref = refs[:5]
        dres_ref = refs[5] if has_res else None
        dx_ref, dg_ref, dsh_ref, dsc_ref = refs[5 + has_res:]
        i = pl.program_id(0)
        xv = x_ref[...]
        dhv = dh_ref[...].astype(F32)
        rstd = lax.rsqrt(jnp.mean(xv * xv, axis=-1, keepdims=True) + RMS_EPS)
        y = xv * rstd
        gv = g_ref[...]
        dn = dhv * (1.0 + sc_ref[0])
        dy = dn * gv
        dx = rstd * (dy - y * jnp.mean(dy * y, axis=-1, keepdims=True))
        if has_res:
            dx = dx + dres_ref[...]
        dx_ref[...] = dx

        @pl.when(i == 0)
        def _():
            dg_ref[...] = jnp.zeros_like(dg_ref)

        @pl.when(jnp.logical_or(i == 0, i == nct))
        def _():
            dsh_ref[...] = jnp.zeros_like(dsh_ref)
            dsc_ref[...] = jnp.zeros_like(dsc_ref)

        dg_ref[...] += jnp.sum(dn * y, axis=0, keepdims=True)
        dsh_ref[0] += jnp.sum(dhv, axis=0, keepdims=True)
        dsc_ref[0] += jnp.sum(dhv * (y * gv), axis=0, keepdims=True)

    row = pl.BlockSpec((r, d), lambda i: (i, 0))
    seg = pl.BlockSpec((1, 1, d), _seg_map(nct))
    vec = pl.BlockSpec((1, d), lambda i: (0, 0))
    return pl.pallas_call(
        body, name=name, grid=(t // r,),
        in_specs=[row, row, vec, seg, seg] + ([row] if has_res else []),
        out_specs=[row, vec, seg, seg],
        out_shape=[jax.ShapeDtypeStruct((t, d), F32), jax.ShapeDtypeStruct((1, d), F32),
                   jax.ShapeDtypeStruct((nseg, 1, d), F32), jax.ShapeDtypeStruct((nseg, 1, d), F32)],
        compiler_params=_params(("arbitrary",)),
    )(*((dh, x, g, shift, scale, dres) if has_res else (dh, x, g, shift, scale)))


def _gated_res_fwd(x, y, gate, coef, n_ctx, name):
    t, d = x.shape
    r, nct = _row_tile(t, n_ctx, d)

    def body(x_ref, y_ref, gt_ref, o_ref):
        o_ref[...] = x_ref[...] + (coef * gt_ref[0]) * y_ref[...]

    row = pl.BlockSpec((r, d), lambda i: (i, 0))
    return pl.pallas_call(
        body, name=name, grid=(t // r,),
        in_specs=[row, row, pl.BlockSpec((1, 1, d), _seg_map(nct))],
        out_specs=row, out_shape=jax.ShapeDtypeStruct((t, d), F32),
        compiler_params=_params(("parallel",)),
    )(x, y, gate)


def _gated_res_bwd(dout, y, gate, coef, n_ctx, name):
    t, d = dout.shape
    r, nct = _row_tile(t, n_ctx, d)
    nseg = gate.shape[0]

    def body(do_ref, y_ref, gt_ref, dy_ref, dgt_ref):
        i = pl.program_id(0)
        dov = do_ref[...] * coef
        dy_ref[...] = (dov * gt_ref[0]).astype(BF16)

        @pl.when(jnp.logical_or(i == 0, i == nct))
        def _():
            dgt_ref[...] = jnp.zeros_like(dgt_ref)

        dgt_ref[0] += jnp.sum(dov * y_ref[...], axis=0, keepdims=True)

    row = pl.BlockSpec((r, d), lambda i: (i, 0))
    seg = pl.BlockSpec((1, 1, d), _seg_map(nct))
    return pl.pallas_call(
        body, name=name, grid=(t // r,),
        in_specs=[row, row, seg], out_specs=[row, seg],
        out_shape=[jax.ShapeDtypeStruct((t, d), BF16), jax.ShapeDtypeStruct((nseg, 1, d), F32)],
        compiler_params=_params(("arbitrary",)),
    )(dout, y, gate)


def _swiglu_fwd(gg, uu, name):
    t, f = gg.shape
    r, c = _tile(t, 256, SUBLANES), _tile(f, 1408, LANES)

    def body(g_ref, u_ref, a_ref):
        gv = g_ref[...]
        a_ref[...] = (gv * jax.nn.sigmoid(gv) * u_ref[...]).astype(BF16)

    blk = pl.BlockSpec((r, c), lambda i, j: (i, j))
    return pl.pallas_call(
        body, name=name, grid=(t // r, f // c), in_specs=[blk, blk], out_specs=blk,
        out_shape=jax.ShapeDtypeStruct((t, f), BF16), compiler_params=_params(("parallel", "parallel")),
    )(gg, uu)


def _swiglu_bwd(da, gg, uu, name):
    t, f = gg.shape
    r, c = _tile(t, 256, SUBLANES), _tile(f, 1408, LANES)

    def body(da_ref, g_ref, u_ref, dg_ref, du_ref):
        gv = g_ref[...]
        dav = da_ref[...].astype(F32)
        sg = jax.nn.sigmoid(gv)
        silu = gv * sg
        dg_ref[...] = (dav * u_ref[...] * (sg * (1.0 + gv * (1.0 - sg)))).astype(BF16)
        du_ref[...] = (dav * silu).astype(BF16)

    blk = pl.BlockSpec((r, c), lambda i, j: (i, j))
    return pl.pallas_call(
        body, name=name, grid=(t // r, f // c), in_specs=[blk, blk, blk], out_specs=[blk, blk],
        out_shape=[jax.ShapeDtypeStruct((t, f), BF16)] * 2, compiler_params=_params(("parallel", "parallel")),
    )(da, gg, uu)


def _tok_tile(t):
    return 1024 if t % 1024 == 0 else 768 if t % 768 == 0 else _tile(t, 1024, 16)


FF_TILE = 1408


def _make_ffn(n_ctx, tag):
    @jax.custom_vjp
    def ffn(x, g, shift, scale, gate, wg, wu, wo):
        return fwd(x, g, shift, scale, gate, wg, wu, wo)[0]

    def fwd(x, g, shift, scale, gate, wg, wu, wo):
        tt = _tok_tile(x.shape[0])
        h = _normmod_fwd(x, g, shift, scale, n_ctx, tag + "_norm")
        gg = _matmul(h, wg, "nn", F32, tag + "_mm_g", tm=tt, tn=FF_TILE)
        uu = _matmul(h, wu, "nn", F32, tag + "_mm_u", tm=tt, tn=FF_TILE)
        a = _swiglu_fwd(gg, uu, tag + "_act")
        y = _matmul(a, wo, "nn", F32, tag + "_mm_o", tm=tt, tk=FF_TILE)
        out = _gated_res_fwd(x, y, gate, 0.5, n_ctx, tag + "_res")
        return out, (x, g, shift, scale, gate, wg, wu, wo, h, gg, uu, a, y)

    def bwd(res, dout):
        x, g, shift, scale, gate, wg, wu, wo, h, gg, uu, a, y = res
        tt = _tok_tile(x.shape[0])
        dy, dgate = _gated_res_bwd(dout, y, gate, 0.5, n_ctx, tag + "_res_b")
        dwo = _matmul(a.T, dy, "nn", BF16, tag + "_dwo", tm=FF_TILE, tk=tt)
        da = _matmul(dy, wo, "nt", BF16, tag + "_da", tm=tt, tn=FF_TILE)
        dgg, duu = _swiglu_bwd(da, gg, uu, tag + "_act_b")
        ht = h.T
        dwg = _matmul(ht, dgg, "nn", BF16, tag + "_dwg", tn=FF_TILE, tk=tt)
        dwu = _matmul(ht, duu, "nn", BF16, tag + "_dwu", tn=FF_TILE, tk=tt)
        dh = _matmul(dgg, wg, "nt", F32, tag + "_dh_g", tm=tt, tk=FF_TILE)
        dh = _matmul(duu, wu, "nt", F32, tag + "_dh_u", add=dh, tm=tt, tk=FF_TILE)
        dx, dg, dshift, dscale = _normmod_bwd(dh, x, g, shift, scale, dout, n_ctx, tag + "_norm_b")
        return dx, dg, dshift, dscale, dgate, dwg, dwu, dwo

    ffn.defvjp(fwd, bwd)
    return ffn


def _make_normmod_linear(n_ctx, tag):
    @jax.custom_vjp
    def op(x, g, shift, scale, w):
        return fwd(x, g, shift, scale, w)[0]

    def fwd(x, g, shift, scale, w):
        h = _normmod_fwd(x, g, shift, scale, n_ctx, tag + "_norm")
        y = _matmul(h, w, "nn", F32, tag + "_mm", tm=_tok_tile(x.shape[0]))
        return y, (x, g, shift, scale, w, h)

    def bwd(res, dy):
        x, g, shift, scale, w, h = res
        tt = _tok_tile(x.shape[0])
        dw = _matmul(h.T, dy, "nn", BF16, tag + "_dw", tk=tt)
        dh = _matmul(dy, w, "nt", F32, tag + "_dh", tm=tt)
        dx, dg, dshift, dscale = _normmod_bwd(dh, x, g, shift, scale, None, n_ctx, tag + "_norm_b")
        return dx, dg, dshift, dscale, dw

    op.defvjp(fwd, bwd)
    return op


def _make_linear_gated_res(tag):
    @jax.custom_vjp
    def op(x, a, w, gate):
        return fwd(x, a, w, gate)[0]

    def fwd(x, a, w, gate):
        ab = a.astype(BF16)
        y = _matmul(ab, w, "nn", F32, tag + "_mm", tm=_tok_tile(x.shape[0]))
        out = _gated_res_fwd(x, y, gate, 1.0, 0, tag + "_res")
        return out, (ab, w, gate, y)

    def bwd(res, dout):
        ab, w, gate, y = res
        tt = _tok_tile(dout.shape[0])
        dy, dgate = _gated_res_bwd(dout, y, gate, 1.0, 0, tag + "_res_b")
        dw = _matmul(ab.T, dy, "nn", BF16, tag + "_dw", tk=tt)
        da = _matmul(dy, w, "nt", F32, tag + "_da", tm=tt)
        return dout, da, dw, dgate

    op.defvjp(fwd, bwd)
    return op


def _final_loss_call(x, g, target, name):
    t, d = x.shape
    r = _tile(t, 256, SUBLANES)

    def body(x_ref, g_ref, t_ref, loss_ref, dx_ref, dg_ref):
        i = pl.program_id(0)
        xv = x_ref[...]
        gv = g_ref[...]
        rstd = lax.rsqrt(jnp.mean(xv * xv, axis=-1, keepdims=True) + RMS_EPS)
        xh = xv * rstd
        e = xh * gv - t_ref[...]
        dy = e * (1.0 / d)
        dn = dy * gv
        dx_ref[...] = rstd * (dn - xh * jnp.mean(dn * xh, axis=-1, keepdims=True))

        @pl.when(i == 0)
        def _():
            loss_ref[...] = jnp.zeros_like(loss_ref)
            dg_ref[...] = jnp.zeros_like(dg_ref)

        loss_ref[...] += 0.5 * jnp.sum(jnp.mean(e * e, axis=-1, keepdims=True), axis=0, keepdims=True)
        dg_ref[...] += jnp.sum(dy * xh, axis=0, keepdims=True)

    row = pl.BlockSpec((r, d), lambda i: (i, 0))
    vec = pl.BlockSpec((1, d), lambda i: (0, 0))
    return pl.pallas_call(
        body, name=name, grid=(t // r,),
        in_specs=[row, vec, row], out_specs=[pl.BlockSpec((1, 1), lambda i: (0, 0)), row, vec],
        out_shape=[jax.ShapeDtypeStruct((1, 1), F32), jax.ShapeDtypeStruct((t, d), F32), jax.ShapeDtypeStruct((1, d), F32)],
        compiler_params=_params(("arbitrary",)),
    )(x, g, target)


@jax.custom_vjp
def _final_loss(x, g, target):
    return _final_loss_call(x, g, target, "final_loss")[0][0, 0]


def _final_loss_fwd(x, g, target):
    loss, dx, dg = _final_loss_call(x, g, target, "final_loss")
    return loss[0, 0], (dx, dg, target)


def _final_loss_bwd(res, dl):
    dx, dg, target = res
    return dx * dl, dg * dl, jnp.zeros_like(target)


_final_loss.defvjp(_final_loss_fwd, _final_loss_bwd)


_NT = (((1,), (1,)), ((), ()))
_TN = (((0,), (0,)), ((), ()))
ATTN_Q_TILE = 512
ATTN_K_CHUNK = 1408


def _attn_fwd_call(q, k, v, scale):
    h, nq, dq = q.shape
    nk, dv = v.shape[1], v.shape[2]
    tq = _tile(nq, ATTN_Q_TILE, 16)
    ck = _tile(nk, ATTN_K_CHUNK, LANES)
    nchunk = nk // ck

    def body(q_ref, k_ref, v_ref, o_ref, lse_ref, m_scr, l_scr, acc_scr):
        qv = q_ref[0]
        m_scr[...] = jnp.full_like(m_scr, -jnp.inf)
        l_scr[...] = jnp.zeros_like(l_scr)
        acc_scr[...] = jnp.zeros_like(acc_scr)

        @pl.loop(0, nchunk)
        def _(c):
            rows = pl.ds(pl.multiple_of(c * ck, ck), ck)
            s = lax.dot_general(qv, k_ref[0, rows, :], _NT, preferred_element_type=F32) * scale
            m_old = m_scr[...]
            m_new = jnp.maximum(m_old, jnp.max(s, axis=-1, keepdims=True))
            alpha = jnp.exp(m_old - m_new)
            p = jnp.exp(s - m_new)
            l_scr[...] = alpha * l_scr[...] + jnp.sum(p, axis=-1, keepdims=True)
            acc_scr[...] = alpha * acc_scr[...] + jnp.dot(p.astype(BF16), v_ref[0, rows, :], preferred_element_type=F32)
            m_scr[...] = m_new

        o_ref[0] = acc_scr[...] / l_scr[...]
        lse_ref[0] = m_scr[...] + jnp.log(l_scr[...])

    keys = lambda d: pl.BlockSpec((1, nk, d), lambda hh, i: (hh, 0, 0))
    return pl.pallas_call(
        body, name="attn_fwd", grid=(h, nq // tq),
        in_specs=[pl.BlockSpec((1, tq, dq), lambda hh, i: (hh, i, 0)), keys(dq), keys(dv)],
        out_specs=[pl.BlockSpec((1, tq, dv), lambda hh, i: (hh, i, 0)),
                   pl.BlockSpec((1, tq, 1), lambda hh, i: (hh, i, 0))],
        out_shape=[jax.ShapeDtypeStruct((h, nq, dv), F32), jax.ShapeDtypeStruct((h, nq, 1), F32)],
        scratch_shapes=[pltpu.VMEM((tq, 1), F32), pltpu.VMEM((tq, 1), F32), pltpu.VMEM((tq, dv), F32)],
        compiler_params=_params(("parallel", "arbitrary")),
    )(q, k, v)


def _attn_bwd_call(q, k, v, o, do, lse, scale):
    h, nq, dq = q.shape
    nk, dv = v.shape[1], v.shape[2]
    tq = _tile(nq, ATTN_Q_TILE, 16)
    ck = _tile(nk, ATTN_K_CHUNK, LANES)
    nchunk = nk // ck

    def body(q_ref, k_ref, v_ref, o_ref, do_ref, lse_ref, dq_ref, dk_ref, dv_ref):
        i = pl.program_id(1)

        @pl.when(i == 0)
        def _():
            dk_ref[...] = jnp.zeros_like(dk_ref)
            dv_ref[...] = jnp.zeros_like(dv_ref)

        qv = q_ref[0]
        dov = do_ref[0]
        dob = dov.astype(BF16)
        delta = jnp.sum(dov * o_ref[0], axis=-1, keepdims=True)
        lse_v = lse_ref[0]

        def chunk(c, dq_acc):
            rows = pl.ds(pl.multiple_of(c * ck, ck), ck)
            ks = k_ref[0, rows, :]
            vs = v_ref[0, rows, :]
            s = lax.dot_general(qv, ks, _NT, preferred_element_type=F32) * scale
            p = jnp.exp(s - lse_v)
            dp = lax.dot_general(dob, vs, _NT, preferred_element_type=F32)
            ds = (p * (dp - delta) * scale).astype(BF16)
            dv_ref[0, rows, :] += lax.dot_general(p.astype(BF16), dob, _TN, preferred_element_type=F32)
            dk_ref[0, rows, :] += lax.dot_general(ds, qv, _TN, preferred_element_type=F32)
            return dq_acc + jnp.dot(ds, ks, preferred_element_type=F32)

        dq_ref[0] = lax.fori_loop(0, nchunk, chunk, jnp.zeros((tq, dq), F32))

    qspec = lambda d: pl.BlockSpec((1, tq, d), lambda hh, i: (hh, i, 0))
    kspec = lambda d: pl.BlockSpec((1, nk, d), lambda hh, i: (hh, 0, 0), pipeline_mode=pl.Buffered(1))
    return pl.pallas_call(
        body, name="attn_bwd", grid=(h, nq // tq),
        in_specs=[qspec(dq), kspec(dq), kspec(dv), qspec(dv), qspec(dv), qspec(1)],
        out_specs=[qspec(dq), kspec(dq), kspec(dv)],
        out_shape=[jax.ShapeDtypeStruct((h, nq, dq), F32), jax.ShapeDtypeStruct((h, nk, dq), F32),
                   jax.ShapeDtypeStruct((h, nk, dv), F32)],
        compiler_params=_params(("parallel", "arbitrary")),
    )(q, k, v, o, do, lse)


@jax.custom_vjp
def _attention(q, k, v):
    return _attention_fwd(q, k, v)[0]


def _attention_fwd(q, k, v):
    scale = q.shape[-1] ** -0.5
    qb, kb, vb = q.astype(BF16), k.astype(BF16), v.astype(BF16)
    o, lse = _attn_fwd_call(qb, kb, vb, scale)
    return o, (qb, kb, vb, o, lse)


def _attention_bwd(res, do):
    qb, kb, vb, o, lse = res
    return _attn_bwd_call(qb, kb, vb, o, do, lse, qb.shape[-1] ** -0.5)


_attention.defvjp(_attention_fwd, _attention_bwd)


def _bf(x):
    return x.astype(BF16)


def _dot(a, b, dims=(((1,), (0,)), ((), ()))):
    return lax.dot_general(_bf(a), _bf(b), dims, preferred_element_type=F32)


def _sum_all(x):
    return jnp.sum(jnp.sum(x, axis=1, keepdims=True), axis=0, keepdims=True)


def _ret_consts(lgf_ref, lgb_ref):
    c = RET_CHUNK
    lgf = lgf_ref[0][:, :1]
    lgb = lgb_ref[0][:, :1]
    diff = (lax.broadcasted_iota(jnp.int32, (c, c), 0) - lax.broadcasted_iota(jnp.int32, (c, c), 1)).astype(F32)
    mf = diff >= 0
    dmat = jnp.where(mf, jnp.exp(lgf * jnp.where(mf, diff, 0.0)), jnp.exp(lgb * jnp.where(mf, 0.0, -diff)))
    col = lax.broadcasted_iota(jnp.int32, (c, 1), 0).astype(F32)
    return dict(diff=diff, mf=mf, dmat=dmat, col=col,
                xif=jnp.exp(lgf * (col + 1.0)), zf=jnp.exp(lgf * (c - 1.0 - col)),
                xib=jnp.exp(lgb * (c - col)), zb=jnp.exp(lgb * col),
                gf=jnp.exp(lgf * c), gb=jnp.exp(lgb * c))


def _ret_rows(n):
    return pl.ds(pl.multiple_of(n * RET_CHUNK, RET_CHUNK), RET_CHUNK)


def _ret_fwd_call(q, k, v, lgf, lgb, s0f, s0b):
    h, n_tok, dk = q.shape
    dv = v.shape[-1]
    nc = n_tok // RET_CHUNK

    def body(q_ref, k_ref, v_ref, lgf_ref, lgb_ref, s0f_ref, s0b_ref, y_ref, sff_ref, sbf_ref, sb_scr):
        cs = _ret_consts(lgf_ref, lgb_ref)

        def right_to_left(t, sb):
            n = nc - 1 - t
            sb_scr[n] = sb
            return cs["gb"] * sb + _dot(k_ref[0, _ret_rows(n), :] * cs["zb"], v_ref[0, _ret_rows(n), :], _TN)

        sbf_ref[0] = lax.fori_loop(0, nc, right_to_left, s0b_ref[0])

        def left_to_right(n, sf):
            qc, kc, vc = q_ref[0, _ret_rows(n), :], k_ref[0, _ret_rows(n), :], v_ref[0, _ret_rows(n), :]
            p = _dot(qc, kc, _NT) * cs["dmat"]
            y_ref[0, _ret_rows(n), :] = _dot(p, vc) + _dot(qc * cs["xif"], sf) + _dot(qc * cs["xib"], sb_scr[n])
            return cs["gf"] * sf + _dot(kc * cs["zf"], vc, _TN)

        sff_ref[0] = lax.fori_loop(0, nc, left_to_right, s0f_ref[0])

    tok = lambda d: pl.BlockSpec((1, n_tok, d), lambda hh: (hh, 0, 0), pipeline_mode=pl.Buffered(1))
    lg = pl.BlockSpec((1, 1, LANES), lambda hh: (hh, 0, 0))
    st = pl.BlockSpec((1, dk, dv), lambda hh: (hh, 0, 0))
    return pl.pallas_call(
        body, name="ret_fwd_%d" % n_tok, grid=(h,),
        in_specs=[tok(dk), tok(dk), tok(dv), lg, lg, st, st], out_specs=[tok(dv), st, st],
        out_shape=[jax.ShapeDtypeStruct((h, n_tok, dv), F32)] + [jax.ShapeDtypeStruct((h, dk, dv), F32)] * 2,
        scratch_shapes=[pltpu.VMEM((nc, dk, dv), F32)],
        compiler_params=_params(("parallel",)),
    )(q, k, v, lgf, lgb, s0f, s0b)


def _ret_bwd_call(q, k, v, lgf, lgb, s0f, s0b, dy, dsff, dsbf):
    h, n_tok, dk = q.shape
    dv = v.shape[-1]
    nc = n_tok // RET_CHUNK
    c = float(RET_CHUNK)

    def body(q_ref, k_ref, v_ref, lgf_ref, lgb_ref, s0f_ref, s0b_ref, dy_ref, dsff_ref, dsbf_ref,
             dq_ref, dk_ref, dv_ref, dlgf_ref, dlgb_ref, ds0f_ref, ds0b_ref, sb_scr, gf_scr, st_a, st_b):
        cs = _ret_consts(lgf_ref, lgb_ref)

        st_a[...] = s0b_ref[0]
        st_b[...] = dsff_ref[0]

        @pl.loop(0, nc)
        def _(t):
            n = nc - 1 - t
            sb, gf_next = st_a[...], st_b[...]
            sb_scr[n] = sb
            gf_scr[n] = gf_next
            qc, kc, vc, dyc = (r[0, _ret_rows(n), :] for r in (q_ref, k_ref, v_ref, dy_ref))
            st_a[...] = cs["gb"] * sb + _dot(kc * cs["zb"], vc, _TN)
            st_b[...] = _dot(qc * cs["xif"], dyc, _TN) + cs["gf"] * gf_next

        ds0f_ref[0] = st_b[...]

        st_a[...] = s0f_ref[0]
        st_b[...] = dsbf_ref[0]
        dlgf_ref[...] = jnp.zeros_like(dlgf_ref)
        dlgb_ref[...] = jnp.zeros_like(dlgb_ref)

        @pl.loop(0, nc)
        def _(n):
            sf, gb_prev = st_a[...], st_b[...]
            sb, gf_next = sb_scr[n], gf_scr[n]
            qc, kc, vc, dyc = (r[0, _ret_rows(n), :] for r in (q_ref, k_ref, v_ref, dy_ref))
            a = _dot(qc, kc, _NT)
            dp = _dot(dyc, vc, _NT)
            da = _bf(dp * cs["dmat"])
            dqf = _dot(dyc, sf, _NT)
            dqb = _dot(dyc, sb, _NT)
            dkf = _dot(vc, gf_next, _NT)
            dkb = _dot(vc, gb_prev, _NT)
            dq_ref[0, _ret_rows(n), :] = _dot(da, kc) + dqf * cs["xif"] + dqb * cs["xib"]
            dk_ref[0, _ret_rows(n), :] = _dot(da, qc, _TN) + dkf * cs["zf"] + dkb * cs["zb"]
            dv_ref[0, _ret_rows(n), :] = (_dot(a * cs["dmat"], dyc, _TN) + _dot(kc * cs["zf"], gf_next)
                                         + _dot(kc * cs["zb"], gb_prev))
            w = dp * a * cs["dmat"] * cs["diff"]
            row = lambda x: jnp.sum(x, axis=1, keepdims=True)
            dlgf_ref[0] += (_sum_all(jnp.where(cs["mf"], w, 0.0))
                            + _sum_all((cs["col"] + 1.0) * cs["xif"] * row(dqf * qc) + (c - 1.0 - cs["col"]) * cs["zf"] * row(dkf * kc))
                            + c * cs["gf"] * _sum_all(gf_next * sf))
            dlgb_ref[0] += (_sum_all((c - cs["col"]) * cs["xib"] * row(dqb * qc) + cs["col"] * cs["zb"] * row(dkb * kc))
                            + c * cs["gb"] * _sum_all(gb_prev * sb) - _sum_all(jnp.where(cs["mf"], 0.0, w)))
            st_a[...] = cs["gf"] * sf + _dot(kc * cs["zf"], vc, _TN)
            st_b[...] = _dot(qc * cs["xib"], dyc, _TN) + cs["gb"] * gb_prev

        ds0b_ref[0] = st_b[...]

    tok = lambda d: pl.BlockSpec((1, n_tok, d), lambda hh: (hh, 0, 0), pipeline_mode=pl.Buffered(1))
    lg = pl.BlockSpec((1, 1, LANES), lambda hh: (hh, 0, 0))
    st = pl.BlockSpec((1, dk, dv), lambda hh: (hh, 0, 0))
    return pl.pallas_call(
        body, name="ret_bwd_%d" % n_tok, grid=(h,),
        in_specs=[tok(dk), tok(dk), tok(dv), lg, lg, st, st, tok(dv), st, st],
        out_specs=[tok(dk), tok(dk), tok(dv), lg, lg, st, st],
        out_shape=[jax.ShapeDtypeStruct((h, n_tok, dk), F32)] * 2 + [jax.ShapeDtypeStruct((h, n_tok, dv), F32)]
        + [jax.ShapeDtypeStruct((h, 1, LANES), F32)] * 2 + [jax.ShapeDtypeStruct((h, dk, dv), F32)] * 2,
        scratch_shapes=[pltpu.VMEM((nc, dk, dv), F32), pltpu.VMEM((nc, dk, dv), F32), pltpu.VMEM((dk, dv), F32), pltpu.VMEM((dk, dv), F32)],
        compiler_params=_params(("parallel",)),
    )(q, k, v, lgf, lgb, s0f, s0b, dy, dsff, dsbf)


def _lane_bcast(lg):
    return jnp.broadcast_to(lg[:, None, None], (lg.shape[0], 1, LANES))


@jax.custom_vjp
def _retention(q, k, v, lgf, lgb, s0f, s0b):
    return tuple(_ret_fwd_call(q, k, v, _lane_bcast(lgf), _lane_bcast(lgb), s0f, s0b))


def _retention_fwd(q, k, v, lgf, lgb, s0f, s0b):
    return _retention(q, k, v, lgf, lgb, s0f, s0b), (q, k, v, lgf, lgb, s0f, s0b)


def _retention_bwd(res, cts):
    q, k, v, lgf, lgb, s0f, s0b = res
    dy, dsff, dsbf = cts
    dq, dk, dv, dlgf, dlgb, ds0f, ds0b = _ret_bwd_call(q, k, v, _lane_bcast(lgf), _lane_bcast(lgb), s0f, s0b, dy, dsff, dsbf)
    return dq, dk, dv, dlgf[:, 0, 0], dlgb[:, 0, 0], ds0f, ds0b


_retention.defvjp(_retention_fwd, _retention_bwd)


def _gn_specs(y):
    h, n, dv = y.shape
    r = _tile(n, 512, SUBLANES)
    return (h, n, dv, r, pl.BlockSpec((1, r, dv), lambda i, hh: (hh, i, 0)), pl.BlockSpec((r, dv), lambda i, hh: (i, hh)))


def _gn_norm(yv):
    mu = jnp.mean(yv, axis=-1, keepdims=True)
    yc = yv - mu
    rstd = lax.rsqrt(jnp.mean(yc * yc, axis=-1, keepdims=True) + GN_EPS)
    return yc * rstd, rstd


def _gn_gate_fwd_call(y, gate):
    h, n, dv, r, yspec, gspec = _gn_specs(y)

    def body(y_ref, g_ref, o_ref):
        gv = g_ref[...]
        o_ref[...] = gv * jax.nn.sigmoid(gv) * _gn_norm(y_ref[0])[0]

    return pl.pallas_call(
        body, name="gn_gate", grid=(n // r, h), in_specs=[yspec, gspec], out_specs=gspec,
        out_shape=jax.ShapeDtypeStruct((n, h * dv), F32), compiler_params=_params(("parallel", "parallel")),
    )(y, gate)


def _gn_gate_bwd_call(y, gate, dout):
    h, n, dv, r, yspec, gspec = _gn_specs(y)

    def body(y_ref, g_ref, do_ref, dy_ref, dg_ref):
        gv = g_ref[...]
        dov = do_ref[...]
        yn, rstd = _gn_norm(y_ref[0])
        sg = jax.nn.sigmoid(gv)
        dg_ref[...] = dov * yn * (sg * (1.0 + gv * (1.0 - sg)))
        dyn = dov * (gv * sg)
        dy_ref[0] = rstd * (dyn - jnp.mean(dyn, axis=-1, keepdims=True) - yn * jnp.mean(dyn * yn, axis=-1, keepdims=True))

    return pl.pallas_call(
        body, name="gn_gate_b", grid=(n // r, h), in_specs=[yspec, gspec, gspec], out_specs=[yspec, gspec],
        out_shape=[jax.ShapeDtypeStruct((h, n, dv), F32), jax.ShapeDtypeStruct((n, h * dv), F32)],
        compiler_params=_params(("parallel", "parallel")),
    )(y, gate, dout)


@jax.custom_vjp
def _gn_gate(y, gate):
    return _gn_gate_fwd_call(y, gate)


_gn_gate.defvjp(lambda y, gate: (_gn_gate_fwd_call(y, gate), (y, gate)),
                lambda res, dout: tuple(_gn_gate_bwd_call(res[0], res[1], dout)))


def _rope_tables(pos, dim, base):
    inv = base ** (-jnp.arange(0, dim, 2, dtype=F32) / dim)
    ang = pos.astype(F32)[:, None] * inv[None, :]
    return jnp.cos(ang)[:, None, :], jnp.sin(ang)[:, None, :]


def _rotate(x, cos, sin):
    x1, x2 = jnp.split(x, 2, axis=-1)
    return jnp.concatenate([x1 * cos - x2 * sin, x2 * cos + x1 * sin], axis=-1)


def _axial_rope(x, row_tab, col_tab):
    xr, xc = jnp.split(x, 2, axis=-1)
    return jnp.concatenate([_rotate(xr, *row_tab), _rotate(xc, *col_tab)], axis=-1)


def _heads(t):
    return jnp.swapaxes(t, 0, 1)


def _local_loss(x, mods_lat, mods_ctx, small, big, ctx, target):
    n_lat, d = x.shape
    n_ctx = ctx.shape[0]
    both = lambda i: jnp.stack([mods_ctx[i], mods_lat[i]])[:, None, :]
    lat = lambda i: mods_lat[i][None, None, :]

    xs = jnp.concatenate([ctx, x], axis=0)
    x1 = _make_ffn(n_ctx, "ffn1")(xs, small["norm1_g"], both(0), both(1), both(2),
                                  big["ffn1_wg"], big["ffn1_wu"], big["ffn1_wo"])
    proj = _make_normmod_linear(n_ctx, "mix_in")(x1, small["norm2_g"], both(3), both(4), big["mix_in"])
    offs = np.cumsum((0,) + MIX_SPLITS)
    part = lambda i, rows: proj[rows, offs[i]:offs[i + 1]]
    lat_rows, ctx_rows = slice(n_ctx, None), slice(0, n_ctx)

    zq = jnp.zeros((1, 1, MLA_Q_RANK), F32)
    zkv = jnp.zeros((1, 1, MLA_KV_RANK), F32)
    q = _make_normmod_linear(0, "mla_q")(part(4, lat_rows), small["mla_q_norm_g"], zq, zq, big["w_uq"])
    kv = _make_normmod_linear(0, "mla_kv")(part(5, slice(None)), small["mla_kv_norm_g"], zkv, zkv, big["w_ukv"])

    lgf = jax.nn.log_sigmoid(small["ret_decay_fwd"][0])
    lgb = jax.nn.log_sigmoid(small["ret_decay_bwd"][0])
    ret_tab = _rope_tables(jnp.arange(n_lat), RET_DK, ROPE_BASE)
    hd = lambda t, dd: t.reshape(t.shape[0], RET_HEADS, dd)
    s_zero = jnp.zeros((RET_HEADS, RET_DK, RET_DV), F32)
    _, s_f, s_b = _retention(_heads(hd(part(0, ctx_rows), RET_DK)), _heads(hd(part(1, ctx_rows), RET_DK) * (RET_DK ** -0.5)),
                             _heads(hd(part(2, ctx_rows), RET_DV)), lgf, lgb, s_zero, s_zero)
    rq = _rotate(hd(part(0, lat_rows), RET_DK), *ret_tab)
    rk = _rotate(hd(part(1, lat_rows), RET_DK) * (RET_DK ** -0.5), *ret_tab)
    y_lat, _, _ = _retention(_heads(rq), _heads(rk), _heads(hd(part(2, lat_rows), RET_DV)), lgf, lgb, s_f, s_b)
    ret_out = _gn_gate(y_lat, part(3, lat_rows))

    pos = jnp.arange(n_lat)
    row_tab = _rope_tables(pos // GRID_W, MLA_ROPE // 2, ROPE_BASE)
    col_tab = _rope_tables(pos % GRID_W, MLA_ROPE // 2, ROPE_BASE)
    q = q.reshape(n_lat, MLA_HEADS, MLA_NOPE + MLA_ROPE)
    q_all = jnp.concatenate([q[..., :MLA_NOPE], _axial_rope(q[..., MLA_NOPE:], row_tab, col_tab)], axis=-1)
    kv = kv.reshape(n_ctx + n_lat, MLA_HEADS, MLA_NOPE + MLA_V)
    kr_lat = _axial_rope(part(6, lat_rows)[:, None, :], row_tab, col_tab)
    kr = jnp.concatenate([kr_lat, part(6, ctx_rows)[:, None, :]], axis=0)
    kv_lat_first = jnp.concatenate([kv[n_ctx:], kv[:n_ctx]], axis=0)
    k_all = jnp.concatenate([kv_lat_first[..., :MLA_NOPE], jnp.broadcast_to(kr, (n_ctx + n_lat, MLA_HEADS, MLA_ROPE))], axis=-1)
    mla = _attention(_heads(q_all), _heads(k_all), _heads(kv_lat_first[..., MLA_NOPE:]))
    mla_out = _heads(mla).reshape(n_lat, MLA_HEADS * MLA_V)

    x2 = _make_linear_gated_res("mix_out")(x1[n_ctx:], jnp.concatenate([ret_out, mla_out], axis=-1), big["mix_out"], lat(5))
    x3 = _make_ffn(0, "ffn2")(x2, small["norm3_g"], lat(6), lat(7), lat(8), big["ffn2_wg"], big["ffn2_wu"], big["ffn2_wo"])
    return _final_loss(x3, small["final_norm_g"][None, :], target)


HBM_SPEC = pl.BlockSpec(memory_space=pl.ANY)
VMEM_SPEC = pl.BlockSpec(memory_space=pltpu.VMEM)
ALL_PEERS = (1, 2, 3, 4, 5, 6, 7)
CHIP_PEERS = (4, 2, 6)


def _me():
    return lax.axis_index("x"), lax.axis_index("y"), lax.axis_index("c")


def _flip(pos, mask):
    x, y, c = pos
    return (1 - x if mask & 4 else x, 1 - y if mask & 2 else y, 1 - c if mask & 1 else c)


def _allgather_small(block, masks, chips_only, name):
    r, c = block.shape
    n_slots = 4 if chips_only else 8

    def body(x_ref, out_ref, send_sems, recv_sems, local_sem):
        pos = _me()
        slot = 2 * pos[0] + pos[1] if chips_only else 4 * pos[0] + 2 * pos[1] + pos[2]
        local = pltpu.make_async_copy(x_ref, out_ref.at[slot], local_sem)
        local.start()
        copies = [pltpu.make_async_remote_copy(src_ref=x_ref, dst_ref=out_ref.at[slot], send_sem=send_sems.at[j], recv_sem=recv_sems.at[j],
                                               device_id=_flip(pos, mask), device_id_type=MESH) for j, mask in enumerate(masks)]
        for cp in copies:
            cp.start()
        for cp in copies:
            cp.wait()
        local.wait()

    return pl.pallas_call(
        body, name=name, in_specs=[VMEM_SPEC], out_specs=VMEM_SPEC,
        out_shape=jax.ShapeDtypeStruct((n_slots, r, c), block.dtype),
        scratch_shapes=[pltpu.SemaphoreType.DMA((len(masks),)), pltpu.SemaphoreType.DMA((len(masks),)), pltpu.SemaphoreType.DMA],
        compiler_params=pltpu.CompilerParams(vmem_limit_bytes=VMEM_LIMIT_BYTES),
    )(block)


def _gather_weights(shards):
    n = len(shards)

    def body(*refs):
        ins, outs = refs[:n], refs[n:2 * n]
        send_sems, recv_sems, pass_send, pass_recv = refs[2 * n:]
        pos = _me()
        x, y, c = pos
        sibling = _flip(pos, 1)
        sends, passes = [], []
        for w in range(n):
            half = shards[w].shape[0] // 2
            mine = pl.ds(c * half, half)
            for j, mask in enumerate(CHIP_PEERS):
                cp = pltpu.make_async_remote_copy(src_ref=ins[w].at[mine], dst_ref=outs[w].at[2 * x + y, mine], send_sem=send_sems.at[3 * w + j],
                                                  recv_sem=recv_sems.at[3 * w + j], device_id=_flip(pos, mask), device_id_type=MESH)
                cp.start()
                sends.append(cp)
        for w in range(n):
            half = shards[w].shape[0] // 2
            mine = pl.ds(c * half, half)
            for j, mask in enumerate(CHIP_PEERS):
                px, py, _ = _flip(pos, mask)
                landed = outs[w].at[2 * px + py, mine]
                pltpu.make_async_remote_copy(src_ref=landed, dst_ref=landed, send_sem=send_sems.at[3 * w + j], recv_sem=recv_sems.at[3 * w + j],
                                             device_id=_flip(pos, mask), device_id_type=MESH).wait_recv()
                cp = pltpu.make_async_remote_copy(src_ref=landed, dst_ref=landed, send_sem=pass_send.at[3 * w + j], recv_sem=pass_recv.at[3 * w + j],
                                                  device_id=sibling, device_id_type=MESH)
                cp.start()
                passes.append(cp)
        for w in range(n):
            half = shards[w].shape[0] // 2
            theirs = pl.ds((1 - c) * half, half)
            for j, mask in enumerate(CHIP_PEERS):
                px, py, _ = _flip(pos, mask)
                slab = outs[w].at[2 * px + py, theirs]
                pltpu.make_async_remote_copy(src_ref=slab, dst_ref=slab, send_sem=pass_send.at[3 * w + j], recv_sem=pass_recv.at[3 * w + j],
                                             device_id=sibling, device_id_type=MESH).wait_recv()
        for cp in sends + passes:
            cp.wait_send()

    dma = lambda k: pltpu.SemaphoreType.DMA((k,))
    stacked = pl.pallas_call(
        body, name="gather_weights", in_specs=[HBM_SPEC] * n, out_specs=[HBM_SPEC] * n,
        out_shape=[jax.ShapeDtypeStruct((4,) + s.shape, s.dtype) for s in shards],
        scratch_shapes=[dma(3 * n), dma(3 * n), dma(3 * n), dma(3 * n)],
    )(*shards)
    chip = 2 * lax.axis_index("x") + lax.axis_index("y")
    return [lax.dynamic_update_slice_in_dim(st, sh[None], chip, axis=0) for st, sh in zip(stacked, shards)]


def _pair_swap_halves(grads):
    n = len(grads)

    def body(*refs):
        ins, outs = refs[:n], refs[n:2 * n]
        send_sems, recv_sems = refs[2 * n:]
        pos = _me()
        copies = []
        for w in range(n):
            half = grads[w].shape[1] // 2
            cp = pltpu.make_async_remote_copy(src_ref=ins[w].at[:, pl.ds((1 - pos[2]) * half, half), :], dst_ref=outs[w], send_sem=send_sems.at[w],
                                              recv_sem=recv_sems.at[w], device_id=_flip(pos, 1), device_id_type=MESH)
            cp.start()
            copies.append(cp)
        for cp in copies:
            cp.wait()

    return pl.pallas_call(
        body, name="pair_swap_halves", in_specs=[HBM_SPEC] * n, out_specs=[HBM_SPEC] * n,
        out_shape=[jax.ShapeDtypeStruct((4, g.shape[1] // 2, g.shape[2]), g.dtype) for g in grads],
        scratch_shapes=[pltpu.SemaphoreType.DMA((n,)), pltpu.SemaphoreType.DMA((n,))],
    )(*grads)


def _chip_scatter(parts):
    n = len(parts)

    def body(*refs):
        ins, outs = refs[:n], refs[n:2 * n]
        send_sems, recv_sems, local_sems = refs[2 * n:]
        pos = _me()
        me = 2 * pos[0] + pos[1]
        copies, local = [], []
        for w in range(n):
            cp = pltpu.make_async_copy(ins[w].at[me], outs[w].at[me], local_sems.at[w])
            cp.start()
            local.append(cp)
            for j, mask in enumerate(CHIP_PEERS):
                px, py, _ = _flip(pos, mask)
                cp = pltpu.make_async_remote_copy(src_ref=ins[w].at[2 * px + py], dst_ref=outs[w].at[me], send_sem=send_sems.at[3 * w + j],
                                                  recv_sem=recv_sems.at[3 * w + j], device_id=_flip(pos, mask), device_id_type=MESH)
                cp.start()
                copies.append(cp)
        for cp in copies:
            cp.wait()
        for cp in local:
            cp.wait()

    dma = lambda k: pltpu.SemaphoreType.DMA((k,))
    return pl.pallas_call(
        body, name="chip_scatter", in_specs=[HBM_SPEC] * n, out_specs=[HBM_SPEC] * n,
        out_shape=[jax.ShapeDtypeStruct(p.shape, p.dtype) for p in parts],
        scratch_shapes=[dma(3 * n), dma(3 * n), dma(n)],
    )(*parts)


def _pair_swap_reduced(halves):
    n = len(halves)

    def body(*refs):
        ins, outs = refs[:n], refs[n:2 * n]
        send_sems, recv_sems = refs[2 * n:]
        pos = _me()
        copies = []
        for w in range(n):
            cp = pltpu.make_async_remote_copy(src_ref=ins[w], dst_ref=outs[w], send_sem=send_sems.at[w], recv_sem=recv_sems.at[w],
                                              device_id=_flip(pos, 1), device_id_type=MESH)
            cp.start()
            copies.append(cp)
        for cp in copies:
            cp.wait()

    dma = lambda k: pltpu.SemaphoreType.DMA((k,))
    return pl.pallas_call(
        body, name="pair_swap_reduced", in_specs=[HBM_SPEC] * n, out_specs=[HBM_SPEC] * n,
        out_shape=[jax.ShapeDtypeStruct(h.shape, h.dtype) for h in halves],
        scratch_shapes=[dma(n), dma(n)],
    )(*halves)


def _add_pair(mine, theirs, name):
    s, h, c = mine.shape
    r = _tile(h, max(16, (1 << 19) // c), 16)

    def body(a_ref, b_ref, o_ref):
        o_ref[...] = (a_ref[...].astype(F32) + b_ref[...].astype(F32)).astype(BF16)

    blk = pl.BlockSpec((1, r, c), lambda i, j: (i, j, 0))
    return pl.pallas_call(
        body, name=name, grid=(s, h // r), in_specs=[blk, blk], out_specs=blk,
        out_shape=jax.ShapeDtypeStruct(mine.shape, BF16), compiler_params=_params(("parallel", "parallel")),
    )(mine, theirs)


def _sum_slots(parts, name):
    s, h, c = parts.shape
    r = _tile(h, max(16, (1 << 18) // c), 16)

    def body(p_ref, o_ref):
        acc = p_ref[0].astype(F32)
        for k in range(1, s):
            acc = acc + p_ref[k].astype(F32)
        o_ref[...] = acc

    return pl.pallas_call(
        body, name=name, grid=(h // r,), in_specs=[pl.BlockSpec((s, r, c), lambda i: (0, i, 0))],
        out_specs=pl.BlockSpec((r, c), lambda i: (i, 0)),
        out_shape=jax.ShapeDtypeStruct((h, c), F32), compiler_params=_params(("parallel",)),
    )(parts)


def _reduce_scatter_grads(stacked):
    c = lax.axis_index("c")
    theirs = _pair_swap_halves(stacked)
    parts = []
    for w, (g, t) in enumerate(zip(stacked, theirs)):
        half = g.shape[1] // 2
        mine = lax.dynamic_slice_in_dim(g, c * half, half, axis=1)
        parts.append(_add_pair(mine, t, "rs_add_pair_%d" % w))
    landed = _chip_scatter(parts)
    halves = [_sum_slots(p, "rs_sum_slots_%d" % w) for w, p in enumerate(landed)]
    return list(zip(halves, _pair_swap_reduced(halves)))


def _adamw_math(w, g, m, v):
    m = ADAM_B1 * m + (1.0 - ADAM_B1) * g
    v = ADAM_B2 * v + (1.0 - ADAM_B2) * (g * g)
    m_hat = m / (1.0 - ADAM_B1 ** ADAM_STEP)
    v_hat = v / (1.0 - ADAM_B2 ** ADAM_STEP)
    return -ADAM_LR * (m_hat / (jnp.sqrt(v_hat) + ADAM_EPS) + ADAM_WD * w), m, v


def _adamw(w, g, m, v, name):
    rows, cols = w.shape
    r = _tile(rows, max(SUBLANES, (1 << 18) // cols), SUBLANES)

    def body(w_ref, g_ref, m_ref, v_ref, d_ref, mo_ref, vo_ref):
        d_ref[...], mo_ref[...], vo_ref[...] = _adamw_math(w_ref[...], g_ref[...], m_ref[...], v_ref[...])

    blk = pl.BlockSpec((r, cols), lambda i: (i, 0))
    return pl.pallas_call(
        body, name=name, grid=(rows // r,), in_specs=[blk] * 4, out_specs=[blk] * 3,
        out_shape=[jax.ShapeDtypeStruct(w.shape, F32)] * 3, compiler_params=_params(("parallel",)),
    )(w, g, m, v)


def _adamw_halves(w, g_mine, g_theirs, m, v, core, name):
    rows, cols = w.shape
    half = rows // 2
    r = _tile(half, max(SUBLANES, (1 << 18) // cols), SUBLANES)
    nbh = half // r

    def body(core_ref, w_ref, gm_ref, gt_ref, m_ref, v_ref, g_ref, d_ref, mo_ref, vo_ref):
        is_mine = (pl.program_id(0) // nbh) == core_ref[0]

        @pl.when(is_mine)
        def _():
            g_ref[...] = gm_ref[...]

        @pl.when(jnp.logical_not(is_mine))
        def _():
            g_ref[...] = gt_ref[...]

        g = g_ref[...]
        d_ref[...], mo_ref[...], vo_ref[...] = _adamw_math(w_ref[...], g, m_ref[...], v_ref[...])

    full = pl.BlockSpec((r, cols), lambda i, core_ref: (i, 0))
    part = pl.BlockSpec((r, cols), lambda i, core_ref: (i % nbh, 0))
    return pl.pallas_call(
        body, name=name,
        grid_spec=pltpu.PrefetchScalarGridSpec(num_scalar_prefetch=1, grid=(rows // r,), in_specs=[full, part, part, full, full],
                                               out_specs=[full] * 4),
        out_shape=[jax.ShapeDtypeStruct(w.shape, F32)] * 4, compiler_params=_params(("parallel",)),
    )(core, w, g_mine, g_theirs, m, v)


def _adamw_reduced(parts, w, m, v, name):
    def body(p_ref, w_ref, m_ref, v_ref, g_ref, d_ref, mo_ref, vo_ref):
        g = p_ref[0]
        for k in range(1, parts.shape[0]):
            g = g + p_ref[k]
        g_ref[...] = g
        d_ref[...], mo_ref[...], vo_ref[...] = _adamw_math(w_ref[...], g, m_ref[...], v_ref[...])

    return pl.pallas_call(
        body, name=name, in_specs=[VMEM_SPEC] * 4, out_specs=[VMEM_SPEC] * 4,
        out_shape=[jax.ShapeDtypeStruct(w.shape, F32)] * 4,
        compiler_params=pltpu.CompilerParams(vmem_limit_bytes=VMEM_LIMIT_BYTES),
    )(parts, w, m, v)


WEIGHTS = ("c_ctx", "ada_w", "ada_b", "norm1_g", "ffn1_w_in", "ffn1_w_out", "norm2_g", "mix_w_in", "ret_decay_fwd", "ret_decay_bwd",
           "mla_q_norm_g", "mla_w_uq", "mla_kv_norm_g", "mla_w_ukv", "mix_w_out", "norm3_g", "ffn2_w_in", "ffn2_w_out", "final_norm_g")
SMALL = ("c_ctx", "ada_b", "norm1_g", "norm2_g", "ret_decay_fwd", "ret_decay_bwd", "mla_q_norm_g", "mla_kv_norm_g", "norm3_g", "final_norm_g")
BIG = (("ffn1_w_in", 1), ("ffn1_w_out", 0), ("mix_w_in", 1), ("mla_w_uq", 1), ("mla_w_ukv", 1), ("mix_w_out", 0), ("ffn2_w_in", 1), ("ffn2_w_out", 0))


def _pack(vectors):
    flat = jnp.concatenate([v.reshape(-1) for v in vectors])
    return jnp.pad(flat, (0, -flat.shape[0] % (SUBLANES * LANES))).reshape(SUBLANES, -1)


def _rows8(a):
    return a.reshape(a.shape[0] * SUBLANES, a.shape[1] // SUBLANES)


def _unpack(packed, like):
    packed = packed.reshape(-1)
    out, off = [], 0
    for ref in like:
        out.append(packed[off:off + ref.size].reshape(ref.shape))
        off += ref.size
    return out


def kernel(x, c, ctx, c_ctx, ada_w, ada_b, norm1_g, ffn1_w_in, ffn1_w_out, norm2_g, mix_w_in, ret_decay_fwd, ret_decay_bwd, mla_q_norm_g, mla_w_uq, mla_kv_norm_g, mla_w_ukv, mix_w_out, norm3_g, ffn2_w_in, ffn2_w_out, final_norm_g, loss_target, m_c_ctx, m_ada_w, m_ada_b, m_norm1_g, m_ffn1_w_in, m_ffn1_w_out, m_norm2_g, m_mix_w_in, m_ret_decay_fwd, m_ret_decay_bwd, m_mla_q_norm_g, m_mla_w_uq, m_mla_kv_norm_g, m_mla_w_ukv, m_mix_w_out, m_norm3_g, m_ffn2_w_in, m_ffn2_w_out, m_final_norm_g, v_c_ctx, v_ada_w, v_ada_b, v_norm1_g, v_ffn1_w_in, v_ffn1_w_out, v_norm2_g, v_mix_w_in, v_ret_decay_fwd, v_ret_decay_bwd, v_mla_q_norm_g, v_mla_w_uq, v_mla_kv_norm_g, v_mla_w_ukv, v_mix_w_out, v_norm3_g, v_ffn2_w_in, v_ffn2_w_out, v_final_norm_g):
    w = dict(c_ctx=c_ctx, ada_w=ada_w, ada_b=ada_b, norm1_g=norm1_g, ffn1_w_in=ffn1_w_in, ffn1_w_out=ffn1_w_out, norm2_g=norm2_g,
             mix_w_in=mix_w_in, ret_decay_fwd=ret_decay_fwd, ret_decay_bwd=ret_decay_bwd, mla_q_norm_g=mla_q_norm_g, mla_w_uq=mla_w_uq,
             mla_kv_norm_g=mla_kv_norm_g, mla_w_ukv=mla_w_ukv, mix_w_out=mix_w_out, norm3_g=norm3_g, ffn2_w_in=ffn2_w_in,
             ffn2_w_out=ffn2_w_out, final_norm_g=final_norm_g)
    mom_m = dict(zip(WEIGHTS, (m_c_ctx, m_ada_w, m_ada_b, m_norm1_g, m_ffn1_w_in, m_ffn1_w_out, m_norm2_g, m_mix_w_in, m_ret_decay_fwd,
                               m_ret_decay_bwd, m_mla_q_norm_g, m_mla_w_uq, m_mla_kv_norm_g, m_mla_w_ukv, m_mix_w_out, m_norm3_g,
                               m_ffn2_w_in, m_ffn2_w_out, m_final_norm_g)))
    mom_v = dict(zip(WEIGHTS, (v_c_ctx, v_ada_w, v_ada_b, v_norm1_g, v_ffn1_w_in, v_ffn1_w_out, v_norm2_g, v_mix_w_in, v_ret_decay_fwd,
                               v_ret_decay_bwd, v_mla_q_norm_g, v_mla_w_uq, v_mla_kv_norm_g, v_mla_w_ukv, v_mix_w_out, v_norm3_g,
                               v_ffn2_w_in, v_ffn2_w_out, v_final_norm_g)))
    xi, yi, ci = _me()
    chip = 2 * xi + yi
    example = 2 * chip + ci
    d = x.shape[-1]
    n_mod = ada_b.shape[-1] // d

    c_all = _allgather_small(_rows8(c), ALL_PEERS, False, "gather_c").reshape(8, d)
    cond = jnp.concatenate([c_all, jnp.broadcast_to(c_ctx[None, :], (8, d))], axis=0)
    cond_act = jax.nn.silu(cond)
    n_cols = ada_w.shape[-1]
    bias = lax.dynamic_slice_in_dim(ada_b, chip * n_cols, n_cols, axis=1)
    mods_cols = _matmul(cond_act, ada_w[0], "nn", F32, "ada_fwd", add=jnp.broadcast_to(bias, (16, n_cols)))
    mods = jnp.swapaxes(_allgather_small(mods_cols, CHIP_PEERS, True, "gather_mods"), 0, 1).reshape(16, 4 * n_cols)
    mods_lat = lax.dynamic_slice_in_dim(mods, example, 1, axis=0).reshape(n_mod, d)
    mods_ctx = mods[8].reshape(n_mod, d)

    stacked = _gather_weights([w[name][0].astype(BF16) for name, _ in BIG])
    full = {}
    for (name, axis), st in zip(BIG, stacked):
        full[name] = st.reshape(-1, st.shape[-1]) if axis == 0 else jnp.swapaxes(st, 0, 1).reshape(st.shape[1], -1)
    d_ff = full["ffn1_w_out"].shape[0]
    big = dict(ffn1_wg=full["ffn1_w_in"][:, :d_ff], ffn1_wu=full["ffn1_w_in"][:, d_ff:], ffn1_wo=full["ffn1_w_out"],
               mix_in=jnp.pad(full["mix_w_in"], ((0, 0), (0, MIX_IN_PAD - MIX_IN))), w_uq=full["mla_w_uq"], w_ukv=full["mla_w_ukv"],
               mix_out=full["mix_w_out"], ffn2_wg=full["ffn2_w_in"][:, :d_ff], ffn2_wu=full["ffn2_w_in"][:, d_ff:], ffn2_wo=full["ffn2_w_out"])
    small = {k: w[k] for k in ("norm1_g", "norm2_g", "norm3_g", "final_norm_g", "mla_q_norm_g", "mla_kv_norm_g", "ret_decay_fwd", "ret_decay_bwd")}

    loss_mine, (dx, dmods_lat, dmods_ctx, dsmall, dbig) = jax.value_and_grad(_local_loss, argnums=(0, 1, 2, 3, 4))(
        x[0], mods_lat, mods_ctx, small, big, ctx[0], loss_target[0])
    loss = lax.psum(loss_mine, ("x", "y", "c"))

    dmods = _allgather_small(_rows8(jnp.stack([dmods_lat.reshape(-1), dmods_ctx.reshape(-1)])), ALL_PEERS, False, "gather_dmods")
    dmods = dmods.reshape(8, 2, n_mod * d)
    dmods_rows = jnp.concatenate([dmods[:, 0, :], dmods[:, 1, :]], axis=0)
    dmods_cols = lax.dynamic_slice_in_dim(dmods_rows, chip * n_cols, n_cols, axis=1)
    g_ada_w = _matmul(cond_act, dmods_cols, "tn", F32, "ada_dw")
    dcond_act = _matmul(dmods_cols, ada_w[0], "nt", F32, "ada_dcond")
    sig = jax.nn.sigmoid(c_ctx)
    dc_ctx = jnp.sum(dcond_act[8:], axis=0) * (sig * (1.0 + c_ctx * (1.0 - sig)))
    share = dict(dsmall)
    share["c_ctx"] = jnp.where(ci == 0, dc_ctx, jnp.zeros_like(dc_ctx))
    share["ada_b"] = (dmods_lat + dmods_ctx).reshape(1, -1)
    parts = _allgather_small(_pack([share[k] for k in SMALL]), ALL_PEERS, False, "gather_small_grads")
    packed = _adamw_reduced(parts, _pack([w[k] for k in SMALL]), _pack([mom_m[k] for k in SMALL]), _pack([mom_v[k] for k in SMALL]),
                            "adamw_small")
    grads, deltas, new_m, new_v = ({k: a for k, a in zip(SMALL, _unpack(p, [w[k] for k in SMALL]))} for p in packed)

    dfull = {"ffn1_w_in": jnp.concatenate([dbig["ffn1_wg"], dbig["ffn1_wu"]], axis=1), "ffn1_w_out": dbig["ffn1_wo"],
             "mix_w_in": dbig["mix_in"][:, :MIX_IN], "mla_w_uq": dbig["w_uq"], "mla_w_ukv": dbig["w_ukv"], "mix_w_out": dbig["mix_out"],
             "ffn2_w_in": jnp.concatenate([dbig["ffn2_wg"], dbig["ffn2_wu"]], axis=1), "ffn2_w_out": dbig["ffn2_wo"]}
    dstacked = []
    for name, axis in BIG:
        g = dfull[name]
        dstacked.append(g.reshape(4, g.shape[0] // 4, g.shape[1]) if axis == 0 else jnp.swapaxes(g.reshape(g.shape[0], 4, g.shape[1] // 4), 0, 1))
    core = ci.astype(jnp.int32).reshape(1)
    for (name, _), (g_mine, g_theirs) in zip(BIG, _reduce_scatter_grads(dstacked)):
        g, dl, mo, vo = _adamw_halves(w[name][0], g_mine, g_theirs, mom_m[name][0], mom_v[name][0], core, "adamw_" + name)
        grads[name], deltas[name], new_m[name], new_v[name] = g[None], dl[None], mo[None], vo[None]
    dl, mo, vo = _adamw(ada_w[0], g_ada_w, m_ada_w[0], v_ada_w[0], "adamw_ada_w")
    grads["ada_w"], deltas["ada_w"], new_m["ada_w"], new_v["ada_w"] = g_ada_w[None], dl[None], mo[None], vo[None]

    return (loss, dx[None], *[grads[k] for k in WEIGHTS], *[deltas[k] for k in WEIGHTS], *[new_m[k] for k in WEIGHTS],
            *[new_v[k] for k in WEIGHTS])
```

```python
import functools

import jax
import jax.numpy as jnp
import numpy as np
from jax import lax
from jax.experimental import pallas as pl
from jax.experimental.pallas import tpu as pltpu

F32 = jnp.float32
BF16 = jnp.bfloat16
MESH = pl.DeviceIdType.MESH

VMEM_LIMIT_BYTES = 52 * 1024 * 1024
LANES = 128
SUBLANES = 8

D_FF_SPLIT = 2
RET_HEADS, RET_DK, RET_DV, RET_CHUNK = 8, 64, 128, 128
MLA_HEADS, MLA_Q_RANK, MLA_KV_RANK, MLA_NOPE, MLA_ROPE, MLA_V = 8, 512, 256, 128, 64, 128
GRID_W = 64
ROPE_BASE = 10000.0
RMS_EPS = 1e-6
GN_EPS = 1e-5
MIX_SPLITS = (RET_HEADS * RET_DK, RET_HEADS * RET_DK, RET_HEADS * RET_DV, RET_HEADS * RET_DV,
              MLA_Q_RANK, MLA_KV_RANK, MLA_ROPE)
MIX_IN = sum(MIX_SPLITS)
MIX_IN_PAD = 4096
ADAM_LR, ADAM_B1, ADAM_B2, ADAM_EPS, ADAM_WD, ADAM_STEP = 0.001, 0.9, 0.999, 1e-08, 0.01, 10


def _tile(n, pref, align):
    best = None
    t = align
    while t <= min(n, pref):
        if n % t == 0:
            best = t
        t += align
    return n if best is None else best


def _params(sem=None):
    return pltpu.CompilerParams(dimension_semantics=sem, vmem_limit_bytes=VMEM_LIMIT_BYTES)


def _matmul(a, b, mode, out_dtype, name, add=None, tm=1024, tn=1024, tk=2048):
    if mode == "nn":
        (m, k), (k2, n) = a.shape, b.shape
        dims = (((1,), (0,)), ((), ()))
    elif mode == "nt":
        (m, k), (n, k2) = a.shape, b.shape
        dims = (((1,), (1,)), ((), ()))
    else:
        (k, m), (k2, n) = a.shape, b.shape
        dims = (((0,), (0,)), ((), ()))
    assert k == k2, (a.shape, b.shape, mode)
    tm = _tile(m, tm, LANES if mode == "tn" else 16)
    tn = _tile(n, tn, LANES)
    tk = _tile(k, tk, LANES if mode != "tn" else 16)
    nk = k // tk
    a_spec = pl.BlockSpec((tk, tm), lambda i, j, kk: (kk, i)) if mode == "tn" else pl.BlockSpec((tm, tk), lambda i, j, kk: (i, kk))
    b_spec = pl.BlockSpec((tn, tk), lambda i, j, kk: (j, kk)) if mode == "nt" else pl.BlockSpec((tk, tn), lambda i, j, kk: (kk, j))
    o_spec = pl.BlockSpec((tm, tn), lambda i, j, kk: (i, j))
    has_add = add is not None

    def body(*refs):
        a_ref, b_ref = refs[0], refs[1]
        add_ref = refs[2] if has_add else None
        o_ref = refs[2 + has_add]
        p = lax.dot_general(a_ref[...].astype(BF16), b_ref[...].astype(BF16), dims, preferred_element_type=F32)
        if nk == 1:
            if has_add:
                p = p + add_ref[...].astype(F32)
            o_ref[...] = p.astype(out_dtype)
        else:
            acc = refs[3 + has_add]
            kk = pl.program_id(2)

            @pl.when(kk == 0)
            def _():
                acc[...] = p + add_ref[...].astype(F32) if has_add else p

            @pl.when(kk > 0)
            def _():
                acc[...] += p

            @pl.when(kk == nk - 1)
            def _():
                o_ref[...] = acc[...].astype(out_dtype)

    return pl.pallas_call(
        body, name=name, grid=(m // tm, n // tn, nk),
        in_specs=[a_spec, b_spec] + ([o_spec] if has_add else []),
        out_specs=o_spec,
        out_shape=jax.ShapeDtypeStruct((m, n), out_dtype),
        scratch_shapes=[pltpu.VMEM((tm, tn), F32)] if nk > 1 else [],
        compiler_params=_params(("parallel", "parallel", "arbitrary")),
    )(*((a, b, add) if has_add else (a, b)))


def _row_tile(t, n_ctx, d):
    pref = max(SUBLANES, min(256, (1 << 19) // d))
    r = _tile(int(np.gcd(t, n_ctx)) if n_ctx else t, pref, SUBLANES)
    return r, (n_ctx // r if n_ctx else 0)


def _seg_map(nct):
    if nct:
        return lambda i: (jnp.minimum(i // nct, 1), 0, 0)
    return lambda i: (0, 0, 0)


def _normmod_fwd(x, g, shift, scale, n_ctx, name):
    t, d = x.shape
    r, nct = _row_tile(t, n_ctx, d)

    def body(x_ref, g_ref, sh_ref, sc_ref, h_ref, ht_ref):
        xv = x_ref[...]
        rstd = lax.rsqrt(jnp.mean(xv * xv, axis=-1, keepdims=True) + RMS_EPS)
        n = xv * rstd * g_ref[...]
        h = n * (1.0 + sc_ref[0]) + sh_ref[0]
        h_ref[...] = h.astype(BF16)
        ht_ref[...] = h.T.astype(BF16)

    row = pl.BlockSpec((r, d), lambda i: (i, 0))
    seg = pl.BlockSpec((1, 1, d), _seg_map(nct))
    return pl.pallas_call(
        body, name=name, grid=(t // r,),
        in_specs=[row, pl.BlockSpec((1, d), lambda i: (0, 0)), seg, seg],
        out_specs=[row, pl.BlockSpec((d, r), lambda i: (0, i))],
        out_shape=[jax.ShapeDtypeStruct((t, d), BF16), jax.ShapeDtypeStruct((d, t), BF16)],
        compiler_params=_params(("parallel",)),
    )(x, g, shift, scale)


def _normmod_bwd(dh, x, g, shift, scale, dres, n_ctx, name):
    t, d = x.shape
    r, nct = _row_tile(t, n_ctx, d)
    has_res = dres is not None
    nseg = shift.shape[0]

    def body(*refs):
        dh_ref, x_ref, g_ref, sh_ref, sc_ref = refs[:5]
        dres_ref = refs[5] if has_res else None
        dx_ref, dg_ref, dsh_ref, dsc_ref = refs[5 + has_res:]
        i = pl.program_id(0)
        xv = x_ref[...]
        dhv = dh_ref[...].astype(F32)
        rstd = lax.rsqrt(jnp.mean(xv * xv, axis=-1, keepdims=True) + RMS_EPS)
        y = xv * rstd
        gv = g_ref[...]
        dn = dhv * (1.0 + sc_ref[0])
        dy = dn * gv
        dx = rstd * (dy - y * jnp.mean(dy * y, axis=-1, keepdims=True))
        if has_res:
            dx = dx + dres_ref[...]
        dx_ref[...] = dx

        @pl.when(i == 0)
        def _():
            dg_ref[...] = jnp.zeros_like(dg_ref)

        @pl.when(jnp.logical_or(i == 0, i == nct))
        def _():
            dsh_ref[...] = jnp.zeros_like(dsh_ref)
            dsc_ref[...] = jnp.zeros_like(dsc_ref)

        dg_ref[...] += jnp.sum(dn * y, axis=0, keepdims=True)
        dsh_ref[0] += jnp.sum(dhv, axis=0, keepdims=True)
        dsc_ref[0] += jnp.sum(dhv * (y * gv), axis=0, keepdims=True)

    row = pl.BlockSpec((r, d), lambda i: (i, 0))
    seg = pl.BlockSpec((1, 1, d), _seg_map(nct))
    vec = pl.BlockSpec((1, d), lambda i: (0, 0))
    return pl.pallas_call(
        body, name=name, grid=(t // r,),
        in_specs=[row, row, vec, seg, seg] + ([row] if has_res else []),
        out_specs=[row, vec, seg, seg],
        out_shape=[jax.ShapeDtypeStruct((t, d), F32), jax.ShapeDtypeStruct((1, d), F32),
                   jax.ShapeDtypeStruct((nseg, 1, d), F32), jax.ShapeDtypeStruct((nseg, 1, d), F32)],
        compiler_params=_params(("arbitrary",)),
    )(*((dh, x, g, shift, scale, dres) if has_res else (dh, x, g, shift, scale)))


def _gated_res_fwd(x, y, gate, coef, n_ctx, name):
    t, d = x.shape
    r, nct = _row_tile(t, n_ctx, d)

    def body(x_ref, y_ref, gt_ref, o_ref):
        o_ref[...] = x_ref[...] + (coef * gt_ref[0]) * y_ref[...]

    row = pl.BlockSpec((r, d), lambda i: (i, 0))
    return pl.pallas_call(
        body, name=name, grid=(t // r,),
        in_specs=[row, row, pl.BlockSpec((1, 1, d), _seg_map(nct))],
        out_specs=row, out_shape=jax.ShapeDtypeStruct((t, d), F32),
        compiler_params=_params(("parallel",)),
    )(x, y, gate)


def _gated_res_bwd(dout, y, gate, coef, n_ctx, name):
    t, d = dout.shape
    r, nct = _row_tile(t, n_ctx, d)
    nseg = gate.shape[0]

    def body(do_ref, y_ref, gt_ref, dy_ref, dgt_ref):
        i = pl.program_id(0)
        dov = do_ref[...] * coef
        dy_ref[...] = (dov * gt_ref[0]).astype(BF16)

        @pl.when(jnp.logical_or(i == 0, i == nct))
        def _():
            dgt_ref[...] = jnp.zeros_like(dgt_ref)

        dgt_ref[0] += jnp.sum(dov * y_ref[...], axis=0, keepdims=True)

    row = pl.BlockSpec((r, d), lambda i: (i, 0))
    seg = pl.BlockSpec((1, 1, d), _seg_map(nct))
    return pl.pallas_call(
        body, name=name, grid=(t // r,),
        in_specs=[row, row, seg], out_specs=[row, seg],
        out_shape=[jax.ShapeDtypeStruct((t, d), BF16), jax.ShapeDtypeStruct((nseg, 1, d), F32)],
        compiler_params=_params(("arbitrary",)),
    )(dout, y, gate)


def _swiglu_matmul(h, wg, wu, name, tm):
    t, k = h.shape
    f = wg.shape[1]
    tm = _tile(t, tm, LANES)
    tn = _tile(f, 512, LANES)

    def body(h_ref, wg_ref, wu_ref, g_ref, u_ref, a_ref, at_ref):
        hv = h_ref[...]
        g = jnp.dot(hv, wg_ref[...], preferred_element_type=F32)
        u = jnp.dot(hv, wu_ref[...], preferred_element_type=F32)
        a = g * jax.nn.sigmoid(g) * u
        g_ref[...] = g.astype(BF16)
        u_ref[...] = u.astype(BF16)
        a_ref[...] = a.astype(BF16)
        at_ref[...] = a.T.astype(BF16)

    w_spec = pl.BlockSpec((k, tn), lambda i, j: (0, j))
    o_spec = pl.BlockSpec((tm, tn), lambda i, j: (i, j))
    return pl.pallas_call(
        body, name=name, grid=(t // tm, f // tn),
        in_specs=[pl.BlockSpec((tm, k), lambda i, j: (i, 0)), w_spec, w_spec],
        out_specs=[o_spec, o_spec, o_spec, pl.BlockSpec((tn, tm), lambda i, j: (j, i))],
        out_shape=[jax.ShapeDtypeStruct((t, f), BF16)] * 3 + [jax.ShapeDtypeStruct((f, t), BF16)],
        compiler_params=_params(("parallel", "parallel")),
    )(h, wg, wu)


def _swiglu_bwd(da, gg, uu, name):
    t, f = gg.shape
    r, c = _tile(t, 256, SUBLANES), _tile(f, 1408, LANES)

    def body(da_ref, g_ref, u_ref, dg_ref, du_ref):
        gv = g_ref[...]
        dav = da_ref[...].astype(F32)
        sg = jax.nn.sigmoid(gv)
        silu = gv * sg
        dg_ref[...] = (dav * u_ref[...] * (sg * (1.0 + gv * (1.0 - sg)))).astype(BF16)
        du_ref[...] = (dav * silu).astype(BF16)

    blk = pl.BlockSpec((r, c), lambda i, j: (i, j))
    return pl.pallas_call(
        body, name=name, grid=(t // r, f // c), in_specs=[blk, blk, blk], out_specs=[blk, blk],
        out_shape=[jax.ShapeDtypeStruct((t, f), BF16)] * 2, compiler_params=_params(("parallel", "parallel")),
    )(da, gg, uu)


def _tok_tile(t):
    return 1024 if t % 1024 == 0 else 768 if t % 768 == 0 else _tile(t, 1024, 16)


FF_TILE = 1408


def _make_ffn(n_ctx, tag):
    @jax.custom_vjp
    def ffn(x, g, shift, scale, gate, wg, wu, wo):
        return fwd(x, g, shift, scale, gate, wg, wu, wo)[0]

    def fwd(x, g, shift, scale, gate, wg, wu, wo):
        tt = _tok_tile(x.shape[0])
        h, ht = _normmod_fwd(x, g, shift, scale, n_ctx, tag + "_norm")
        gg, uu, a, at = _swiglu_matmul(h, wg, wu, tag + "_mm_gu", tt)
        y = _matmul(a, wo, "nn", F32, tag + "_mm_o", tm=tt, tk=FF_TILE)
        out = _gated_res_fwd(x, y, gate, 0.5, n_ctx, tag + "_res")
        return out, (x, g, shift, scale, gate, wg, wu, wo, ht, gg, uu, at, y)

    def bwd(res, dout):
        x, g, shift, scale, gate, wg, wu, wo, ht, gg, uu, at, y = res
        tt = _tok_tile(x.shape[0])
        dy, dgate = _gated_res_bwd(dout, y, gate, 0.5, n_ctx, tag + "_res_b")
        dwo = _matmul(at, dy, "nn", BF16, tag + "_dwo", tm=FF_TILE, tk=tt)
        da = _matmul(dy, wo, "nt", BF16, tag + "_da", tm=tt, tn=FF_TILE)
        dgg, duu = _swiglu_bwd(da, gg, uu, tag + "_act_b")
        dwg = _matmul(ht, dgg, "nn", BF16, tag + "_dwg", tn=FF_TILE, tk=tt)
        dwu = _matmul(ht, duu, "nn", BF16, tag + "_dwu", tn=FF_TILE, tk=tt)
        dh = _matmul(dgg, wg, "nt", F32, tag + "_dh_g", tm=tt, tk=FF_TILE)
        dh = _matmul(duu, wu, "nt", F32, tag + "_dh_u", add=dh, tm=tt, tk=FF_TILE)
        dx, dg, dshift, dscale = _normmod_bwd(dh, x, g, shift, scale, dout, n_ctx, tag + "_norm_b")
        return dx, dg, dshift, dscale, dgate, dwg, dwu, dwo

    ffn.defvjp(fwd, bwd)
    return ffn


def _make_normmod_linear(n_ctx, tag):
    @jax.custom_vjp
    def op(x, g, shift, scale, w):
        return fwd(x, g, shift, scale, w)[0]

    def fwd(x, g, shift, scale, w):
        h, ht = _normmod_fwd(x, g, shift, scale, n_ctx, tag + "_norm")
        y = _matmul(h, w, "nn", F32, tag + "_mm", tm=_tok_tile(x.shape[0]))
        return y, (x, g, shift, scale, w, ht)

    def bwd(res, dy):
        x, g, shift, scale, w, ht = res
        tt = _tok_tile(x.shape[0])
        dw = _matmul(ht, dy, "nn", BF16, tag + "_dw", tk=tt)
        dh = _matmul(dy, w, "nt", F32, tag + "_dh", tm=tt)
        dx, dg, dshift, dscale = _normmod_bwd(dh, x, g, shift, scale, None, n_ctx, tag + "_norm_b")
        return dx, dg, dshift, dscale, dw

    op.defvjp(fwd, bwd)
    return op


def _make_linear_gated_res(tag):
    @jax.custom_vjp
    def op(x, a, w, gate):
        return fwd(x, a, w, gate)[0]

    def fwd(x, a, w, gate):
        ab = a.astype(BF16)
        y = _matmul(ab, w, "nn", F32, tag + "_mm", tm=_tok_tile(x.shape[0]))
        out = _gated_res_fwd(x, y, gate, 1.0, 0, tag + "_res")
        return out, (ab, w, gate, y)

    def bwd(res, dout):
        ab, w, gate, y = res
        tt = _tok_tile(dout.shape[0])
        dy, dgate = _gated_res_bwd(dout, y, gate, 1.0, 0, tag + "_res_b")
        dw = _matmul(ab.T, dy, "nn", BF16, tag + "_dw", tk=tt)
        da = _matmul(dy, w, "nt", F32, tag + "_da", tm=tt)
        return dout, da, dw, dgate

    op.defvjp(fwd, bwd)
    return op


def _final_loss_call(x, g, target, name):
    t, d = x.shape
    r = _tile(t, 256, SUBLANES)

    def body(x_ref, g_ref, t_ref, loss_ref, dx_ref, dg_ref):
        i = pl.program_id(0)
        xv = x_ref[...]
        gv = g_ref[...]
        rstd = lax.rsqrt(jnp.mean(xv * xv, axis=-1, keepdims=True) + RMS_EPS)
        xh = xv * rstd
        e = xh * gv - t_ref[...]
        dy = e * (1.0 / d)
        dn = dy * gv
        dx_ref[...] = rstd * (dn - xh * jnp.mean(dn * xh, axis=-1, keepdims=True))

        @pl.when(i == 0)
        def _():
            loss_ref[...] = jnp.zeros_like(loss_ref)
            dg_ref[...] = jnp.zeros_like(dg_ref)

        loss_ref[...] += 0.5 * jnp.sum(jnp.mean(e * e, axis=-1, keepdims=True), axis=0, keepdims=True)
        dg_ref[...] += jnp.sum(dy * xh, axis=0, keepdims=True)

    row = pl.BlockSpec((r, d), lambda i: (i, 0))
    vec = pl.BlockSpec((1, d), lambda i: (0, 0))
    return pl.pallas_call(
        body, name=name, grid=(t // r,),
        in_specs=[row, vec, row], out_specs=[pl.BlockSpec((1, 1), lambda i: (0, 0)), row, vec],
        out_shape=[jax.ShapeDtypeStruct((1, 1), F32), jax.ShapeDtypeStruct((t, d), F32), jax.ShapeDtypeStruct((1, d), F32)],
        compiler_params=_params(("arbitrary",)),
    )(x, g, target)


@jax.custom_vjp
def _final_loss(x, g, target):
    return _final_loss_call(x, g, target, "final_loss")[0][0, 0]


def _final_loss_fwd(x, g, target):
    loss, dx, dg = _final_loss_call(x, g, target, "final_loss")
    return loss[0, 0], (dx, dg, target)


def _final_loss_bwd(res, dl):
    dx, dg, target = res
    return dx * dl, dg * dl, jnp.zeros_like(target)


_final_loss.defvjp(_final_loss_fwd, _final_loss_bwd)


_NT = (((1,), (1,)), ((), ()))
_TN = (((0,), (0,)), ((), ()))
ATTN_Q_TILE = 512
ATTN_K_CHUNK = 1408


def _attn_fwd_call(q, k, v, scale):
    h, nq, dq = q.shape
    nk, dv = v.shape[1], v.shape[2]
    tq = _tile(nq, ATTN_Q_TILE, 16)
    ck = _tile(nk, ATTN_K_CHUNK, LANES)
    nchunk = nk // ck

    exp2_scale = scale * float(np.log2(np.e))

    def body(q_ref, k_ref, v_ref, o_ref, lse_ref, m_scr, l_scr, acc_scr):
        qv = q_ref[0]
        m_scr[...] = jnp.full_like(m_scr, -jnp.inf)
        l_scr[...] = jnp.zeros_like(l_scr)
        acc_scr[...] = jnp.zeros_like(acc_scr)

        scores = lambda c: lax.dot_general(qv, k_ref[0, c * ck:(c + 1) * ck, :], _NT, preferred_element_type=F32)
        s_next = scores(0)
        for c in range(nchunk):
            s = s_next
            if c + 1 < nchunk:
                s_next = scores(c + 1)
            m_old = m_scr[...]
            m_new = jnp.maximum(m_old, jnp.max(s, axis=-1, keepdims=True))
            alpha = jnp.exp2((m_old - m_new) * exp2_scale)
            p = jnp.exp2((s - m_new) * exp2_scale)
            l_scr[...] = alpha * l_scr[...] + jnp.sum(p, axis=-1, keepdims=True)
            acc_scr[...] = alpha * acc_scr[...] + jnp.dot(p.astype(BF16), v_ref[0, c * ck:(c + 1) * ck, :], preferred_element_type=F32)
            m_scr[...] = m_new

        o_ref[0] = acc_scr[...] / l_scr[...]
        lse_ref[0] = m_scr[...] * scale + jnp.log(l_scr[...])

    keys = lambda d: pl.BlockSpec((1, nk, d), lambda hh, i: (hh, 0, 0))
    return pl.pallas_call(
        body, name="attn_fwd", grid=(h, nq // tq),
        in_specs=[pl.BlockSpec((1, tq, dq), lambda hh, i: (hh, i, 0)), keys(dq), keys(dv)],
        out_specs=[pl.BlockSpec((1, tq, dv), lambda hh, i: (hh, i, 0)),
                   pl.BlockSpec((1, tq, 1), lambda hh, i: (hh, i, 0))],
        out_shape=[jax.ShapeDtypeStruct((h, nq, dv), F32), jax.ShapeDtypeStruct((h, nq, 1), F32)],
        scratch_shapes=[pltpu.VMEM((tq, 1), F32), pltpu.VMEM((tq, 1), F32), pltpu.VMEM((tq, dv), F32)],
        compiler_params=_params(("parallel", "arbitrary")),
    )(q, k, v)


def _attn_bwd_call(q, k, v, o, do, lse, scale):
    h, nq, dq = q.shape
    nk, dv = v.shape[1], v.shape[2]
    tq = _tile(nq, ATTN_Q_TILE, 16)
    ck = _tile(nk, ATTN_K_CHUNK, LANES)
    nchunk = nk // ck

    log2e = float(np.log2(np.e))
    exp2_scale = scale * log2e

    def body(q_ref, k_ref, v_ref, o_ref, do_ref, lse_ref, dq_ref, dk_ref, dv_ref):
        i = pl.program_id(1)

        @pl.when(i == 0)
        def _():
            dk_ref[...] = jnp.zeros_like(dk_ref)
            dv_ref[...] = jnp.zeros_like(dv_ref)

        qv = q_ref[0]
        dov = do_ref[0]
        dob = dov.astype(BF16)
        delta = jnp.sum(dov * o_ref[0], axis=-1, keepdims=True)
        lse2 = lse_ref[0] * log2e

        def scores(c):
            rows = slice(c * ck, (c + 1) * ck)
            return (lax.dot_general(qv, k_ref[0, rows, :], _NT, preferred_element_type=F32),
                    lax.dot_general(dob, v_ref[0, rows, :], _NT, preferred_element_type=F32))

        nxt = scores(0)
        dq_acc = None
        for c in range(nchunk):
            rows = slice(c * ck, (c + 1) * ck)
            s, dp = nxt
            if c + 1 < nchunk:
                nxt = scores(c + 1)
            p = jnp.exp2(s * exp2_scale - lse2)
            ds = (p * (dp - delta) * scale).astype(BF16)
            dv_ref[0, rows, :] += lax.dot_general(p.astype(BF16), dob, _TN, preferred_element_type=F32)
            dk_ref[0, rows, :] += lax.dot_general(ds, qv, _TN, preferred_element_type=F32)
            part = jnp.dot(ds, k_ref[0, rows, :], preferred_element_type=F32)
            dq_acc = part if dq_acc is None else dq_acc + part
        dq_ref[0] = dq_acc

    qspec = lambda d: pl.BlockSpec((1, tq, d), lambda hh, i: (hh, i, 0))
    kspec = lambda d: pl.BlockSpec((1, nk, d), lambda hh, i: (hh, 0, 0), pipeline_mode=pl.Buffered(1))
    return pl.pallas_call(
        body, name="attn_bwd", grid=(h, nq // tq),
        in_specs=[qspec(dq), kspec(dq), kspec(dv), qspec(dv), qspec(dv), qspec(1)],
        out_specs=[qspec(dq), kspec(dq), kspec(dv)],
        out_shape=[jax.ShapeDtypeStruct((h, nq, dq), F32), jax.ShapeDtypeStruct((h, nk, dq), F32),
                   jax.ShapeDtypeStruct((h, nk, dv), F32)],
        compiler_params=_params(("parallel", "arbitrary")),
    )(q, k, v, o, do, lse)


@jax.custom_vjp
def _attention(q, k, v):
    return _attention_fwd(q, k, v)[0]


def _attention_fwd(q, k, v):
    scale = q.shape[-1] ** -0.5
    qb, kb, vb = q.astype(BF16), k.astype(BF16), v.astype(BF16)
    o, lse = _attn_fwd_call(qb, kb, vb, scale)
    return o, (qb, kb, vb, o, lse)


def _attention_bwd(res, do):
    qb, kb, vb, o, lse = res
    return _attn_bwd_call(qb, kb, vb, o, do, lse, qb.shape[-1] ** -0.5)


_attention.defvjp(_attention_fwd, _attention_bwd)


RET_UNROLL = 2


def _bf(x):
    return x.astype(BF16)


def _dot(a, b, dims=(((1,), (0,)), ((), ()))):
    return lax.dot_general(_bf(a), _bf(b), dims, preferred_element_type=F32)


def _sum_all(x):
    return jnp.sum(jnp.sum(x, axis=1, keepdims=True), axis=0, keepdims=True)


def _ret_consts(lgf_ref, lgb_ref):
    c = RET_CHUNK
    lgf = lgf_ref[0][:, :1]
    lgb = lgb_ref[0][:, :1]
    diff = (lax.broadcasted_iota(jnp.int32, (c, c), 0) - lax.broadcasted_iota(jnp.int32, (c, c), 1)).astype(F32)
    mf = diff >= 0
    dmat = jnp.where(mf, jnp.exp(lgf * jnp.where(mf, diff, 0.0)), jnp.exp(lgb * jnp.where(mf, 0.0, -diff)))
    col = lax.broadcasted_iota(jnp.int32, (c, 1), 0).astype(F32)
    return dict(diff=diff, mf=mf, dmat=dmat, col=col,
                xif=jnp.exp(lgf * (col + 1.0)), zf=jnp.exp(lgf * (c - 1.0 - col)),
                xib=jnp.exp(lgb * (c - col)), zb=jnp.exp(lgb * col),
                gf=jnp.exp(lgf * c), gb=jnp.exp(lgb * c))


def _ret_rows(n):
    return pl.ds(pl.multiple_of(n * RET_CHUNK, RET_CHUNK), RET_CHUNK)


def _ret_fwd_call(q, k, v, lgf, lgb, s0f, s0b):
    h, n_tok, dk = q.shape
    dv = v.shape[-1]
    nc = n_tok // RET_CHUNK

    def body(q_ref, k_ref, v_ref, lgf_ref, lgb_ref, s0f_ref, s0b_ref, y_ref, sff_ref, sbf_ref, sb_scr):
        cs = _ret_consts(lgf_ref, lgb_ref)

        def right_to_left(t, sb):
            n = nc - 1 - t
            sb_scr[n] = sb
            return cs["gb"] * sb + _dot(k_ref[0, _ret_rows(n), :] * cs["zb"], v_ref[0, _ret_rows(n), :], _TN)

        sbf_ref[0] = lax.fori_loop(0, nc, right_to_left, s0b_ref[0], unroll=RET_UNROLL)

        def left_to_right(n, sf):
            qc, kc, vc = q_ref[0, _ret_rows(n), :], k_ref[0, _ret_rows(n), :], v_ref[0, _ret_rows(n), :]
            p = _dot(qc, kc, _NT) * cs["dmat"]
            y_ref[0, _ret_rows(n), :] = _dot(p, vc) + _dot(qc * cs["xif"], sf) + _dot(qc * cs["xib"], sb_scr[n])
            return cs["gf"] * sf + _dot(kc * cs["zf"], vc, _TN)

        sff_ref[0] = lax.fori_loop(0, nc, left_to_right, s0f_ref[0], unroll=RET_UNROLL)

    tok = lambda d: pl.BlockSpec((1, n_tok, d), lambda hh: (hh, 0, 0), pipeline_mode=pl.Buffered(1))
    lg = pl.BlockSpec((1, 1, LANES), lambda hh: (hh, 0, 0))
    st = pl.BlockSpec((1, dk, dv), lambda hh: (hh, 0, 0))
    return pl.pallas_call(
        body, name="ret_fwd_%d" % n_tok, grid=(h,),
        in_specs=[tok(dk), tok(dk), tok(dv), lg, lg, st, st], out_specs=[tok(dv), st, st],
        out_shape=[jax.ShapeDtypeStruct((h, n_tok, dv), F32)] + [jax.ShapeDtypeStruct((h, dk, dv), F32)] * 2,
        scratch_shapes=[pltpu.VMEM((nc, dk, dv), F32)],
        compiler_params=_params(("parallel",)),
    )(q, k, v, lgf, lgb, s0f, s0b)


def _ret_bwd_call(q, k, v, lgf, lgb, s0f, s0b, dy, dsff, dsbf):
    h, n_tok, dk = q.shape
    dv = v.shape[-1]
    nc = n_tok // RET_CHUNK
    c = float(RET_CHUNK)

    def body(q_ref, k_ref, v_ref, lgf_ref, lgb_ref, s0f_ref, s0b_ref, dy_ref, dsff_ref, dsbf_ref,
             dq_ref, dk_ref, dv_ref, dlgf_ref, dlgb_ref, ds0f_ref, ds0b_ref, sb_scr, gf_scr, st_a, st_b):
        cs = _ret_consts(lgf_ref, lgb_ref)

        st_a[...] = s0b_ref[0]
        st_b[...] = dsff_ref[0]

        @pl.loop(0, nc, unroll=RET_UNROLL)
        def _(t):
            n = nc - 1 - t
            sb, gf_next = st_a[...], st_b[...]
            sb_scr[n] = sb
            gf_scr[n] = gf_next
            qc, kc, vc, dyc = (r[0, _ret_rows(n), :] for r in (q_ref, k_ref, v_ref, dy_ref))
            st_a[...] = cs["gb"] * sb + _dot(kc * cs["zb"], vc, _TN)
            st_b[...] = _dot(qc * cs["xif"], dyc, _TN) + cs["gf"] * gf_next

        ds0f_ref[0] = st_b[...]

        st_a[...] = s0f_ref[0]
        st_b[...] = dsbf_ref[0]
        dlgf_ref[...] = jnp.zeros_like(dlgf_ref)
        dlgb_ref[...] = jnp.zeros_like(dlgb_ref)

        @pl.loop(0, nc, unroll=RET_UNROLL)
        def _(n):
            sf, gb_prev = st_a[...], st_b[...]
            sb, gf_next = sb_scr[n], gf_scr[n]
            qc, kc, vc, dyc = (r[0, _ret_rows(n), :] for r in (q_ref, k_ref, v_ref, dy_ref))
            a = _dot(qc, kc, _NT)
            dp = _dot(dyc, vc, _NT)
            da = _bf(dp * cs["dmat"])
            dqf = _dot(dyc, sf, _NT)
            dqb = _dot(dyc, sb, _NT)
            dkf = _dot(vc, gf_next, _NT)
            dkb = _dot(vc, gb_prev, _NT)
            dq_ref[0, _ret_rows(n), :] = _dot(da, kc) + dqf * cs["xif"] + dqb * cs["xib"]
            dk_ref[0, _ret_rows(n), :] = _dot(da, qc, _TN) + dkf * cs["zf"] + dkb * cs["zb"]
            dv_ref[0, _ret_rows(n), :] = (_dot(a * cs["dmat"], dyc, _TN) + _dot(kc * cs["zf"], gf_next)
                                         + _dot(kc * cs["zb"], gb_prev))
            w = dp * a * cs["dmat"] * cs["diff"]
            row = lambda x: jnp.sum(x, axis=1, keepdims=True)
            dlgf_ref[0] += (_sum_all(jnp.where(cs["mf"], w, 0.0))
                            + _sum_all((cs["col"] + 1.0) * cs["xif"] * row(dqf * qc) + (c - 1.0 - cs["col"]) * cs["zf"] * row(dkf * kc))
                            + c * cs["gf"] * _sum_all(gf_next * sf))
            dlgb_ref[0] += (_sum_all((c - cs["col"]) * cs["xib"] * row(dqb * qc) + cs["col"] * cs["zb"] * row(dkb * kc))
                            + c * cs["gb"] * _sum_all(gb_prev * sb) - _sum_all(jnp.where(cs["mf"], 0.0, w)))
            st_a[...] = cs["gf"] * sf + _dot(kc * cs["zf"], vc, _TN)
            st_b[...] = _dot(qc * cs["xib"], dyc, _TN) + cs["gb"] * gb_prev

        ds0b_ref[0] = st_b[...]

    tok = lambda d: pl.BlockSpec((1, n_tok, d), lambda hh: (hh, 0, 0), pipeline_mode=pl.Buffered(1))
    lg = pl.BlockSpec((1, 1, LANES), lambda hh: (hh, 0, 0))
    st = pl.BlockSpec((1, dk, dv), lambda hh: (hh, 0, 0))
    return pl.pallas_call(
        body, name="ret_bwd_%d" % n_tok, grid=(h,),
        in_specs=[tok(dk), tok(dk), tok(dv), lg, lg, st, st, tok(dv), st, st],
        out_specs=[tok(dk), tok(dk), tok(dv), lg, lg, st, st],
        out_shape=[jax.ShapeDtypeStruct((h, n_tok, dk), F32)] * 2 + [jax.ShapeDtypeStruct((h, n_tok, dv), F32)]
        + [jax.ShapeDtypeStruct((h, 1, LANES), F32)] * 2 + [jax.ShapeDtypeStruct((h, dk, dv), F32)] * 2,
        scratch_shapes=[pltpu.VMEM((nc, dk, dv), F32), pltpu.VMEM((nc, dk, dv), F32), pltpu.VMEM((dk, dv), F32), pltpu.VMEM((dk, dv), F32)],
        compiler_params=_params(("parallel",)),
    )(q, k, v, lgf, lgb, s0f, s0b, dy, dsff, dsbf)


def _lane_bcast(lg):
    return jnp.broadcast_to(lg[:, None, None], (lg.shape[0], 1, LANES))


@jax.custom_vjp
def _retention(q, k, v, lgf, lgb, s0f, s0b):
    return tuple(_ret_fwd_call(q, k, v, _lane_bcast(lgf), _lane_bcast(lgb), s0f, s0b))


def _retention_fwd(q, k, v, lgf, lgb, s0f, s0b):
    return _retention(q, k, v, lgf, lgb, s0f, s0b), (q, k, v, lgf, lgb, s0f, s0b)


def _retention_bwd(res, cts):
    q, k, v, lgf, lgb, s0f, s0b = res
    dy, dsff, dsbf = cts
    dq, dk, dv, dlgf, dlgb, ds0f, ds0b = _ret_bwd_call(q, k, v, _lane_bcast(lgf), _lane_bcast(lgb), s0f, s0b, dy, dsff, dsbf)
    return dq, dk, dv, dlgf[:, 0, 0], dlgb[:, 0, 0], ds0f, ds0b


_retention.defvjp(_retention_fwd, _retention_bwd)


def _gn_specs(y):
    h, n, dv = y.shape
    r = _tile(n, 512, SUBLANES)
    return (h, n, dv, r, pl.BlockSpec((1, r, dv), lambda i, hh: (hh, i, 0)), pl.BlockSpec((r, dv), lambda i, hh: (i, hh)))


def _gn_norm(yv):
    mu = jnp.mean(yv, axis=-1, keepdims=True)
    yc = yv - mu
    rstd = lax.rsqrt(jnp.mean(yc * yc, axis=-1, keepdims=True) + GN_EPS)
    return yc * rstd, rstd


def _gn_gate_fwd_call(y, gate):
    h, n, dv, r, yspec, gspec = _gn_specs(y)

    def body(y_ref, g_ref, o_ref):
        gv = g_ref[...]
        o_ref[...] = gv * jax.nn.sigmoid(gv) * _gn_norm(y_ref[0])[0]

    return pl.pallas_call(
        body, name="gn_gate", grid=(n // r, h), in_specs=[yspec, gspec], out_specs=gspec,
        out_shape=jax.ShapeDtypeStruct((n, h * dv), F32), compiler_params=_params(("parallel", "parallel")),
    )(y, gate)


def _gn_gate_bwd_call(y, gate, dout):
    h, n, dv, r, yspec, gspec = _gn_specs(y)

    def body(y_ref, g_ref, do_ref, dy_ref, dg_ref):
        gv = g_ref[...]
        dov = do_ref[...]
        yn, rstd = _gn_norm(y_ref[0])
        sg = jax.nn.sigmoid(gv)
        dg_ref[...] = dov * yn * (sg * (1.0 + gv * (1.0 - sg)))
        dyn = dov * (gv * sg)
        dy_ref[0] = rstd * (dyn - jnp.mean(dyn, axis=-1, keepdims=True) - yn * jnp.mean(dyn * yn, axis=-1, keepdims=True))

    return pl.pallas_call(
        body, name="gn_gate_b", grid=(n // r, h), in_specs=[yspec, gspec, gspec], out_specs=[yspec, gspec],
        out_shape=[jax.ShapeDtypeStruct((h, n, dv), F32), jax.ShapeDtypeStruct((n, h * dv), F32)],
        compiler_params=_params(("parallel", "parallel")),
    )(y, gate, dout)


@jax.custom_vjp
def _gn_gate(y, gate):
    return _gn_gate_fwd_call(y, gate)


_gn_gate.defvjp(lambda y, gate: (_gn_gate_fwd_call(y, gate), (y, gate)),
                lambda res, dout: tuple(_gn_gate_bwd_call(res[0], res[1], dout)))


def _rope_tables(pos, dim, base):
    inv = base ** (-jnp.arange(0, dim, 2, dtype=F32) / dim)
    ang = pos.astype(F32)[:, None] * inv[None, :]
    return jnp.cos(ang)[:, None, :], jnp.sin(ang)[:, None, :]


def _rotate(x, cos, sin):
    x1, x2 = jnp.split(x, 2, axis=-1)
    return jnp.concatenate([x1 * cos - x2 * sin, x2 * cos + x1 * sin], axis=-1)


def _axial_rope(x, row_tab, col_tab):
    xr, xc = jnp.split(x, 2, axis=-1)
    return jnp.concatenate([_rotate(xr, *row_tab), _rotate(xc, *col_tab)], axis=-1)


def _heads(t):
    return jnp.swapaxes(t, 0, 1)


def _local_loss(x, mods_lat, mods_ctx, small, big, ctx, target):
    n_lat, d = x.shape
    n_ctx = ctx.shape[0]
    both = lambda i: jnp.stack([mods_ctx[i], mods_lat[i]])[:, None, :]
    lat = lambda i: mods_lat[i][None, None, :]

    xs = jnp.concatenate([ctx, x], axis=0)
    x1 = _make_ffn(n_ctx, "ffn1")(xs, small["norm1_g"], both(0), both(1), both(2),
                                  big["ffn1_wg"], big["ffn1_wu"], big["ffn1_wo"])
    proj = _make_normmod_linear(n_ctx, "mix_in")(x1, small["norm2_g"], both(3), both(4), big["mix_in"])
    offs = np.cumsum((0,) + MIX_SPLITS)
    part = lambda i, rows: proj[rows, offs[i]:offs[i + 1]]
    lat_rows, ctx_rows = slice(n_ctx, None), slice(0, n_ctx)

    zq = jnp.zeros((1, 1, MLA_Q_RANK), F32)
    zkv = jnp.zeros((1, 1, MLA_KV_RANK), F32)
    q = _make_normmod_linear(0, "mla_q")(part(4, lat_rows), small["mla_q_norm_g"], zq, zq, big["w_uq"])
    kv = _make_normmod_linear(0, "mla_kv")(part(5, slice(None)), small["mla_kv_norm_g"], zkv, zkv, big["w_ukv"])

    lgf = jax.nn.log_sigmoid(small["ret_decay_fwd"][0])
    lgb = jax.nn.log_sigmoid(small["ret_decay_bwd"][0])
    ret_tab = _rope_tables(jnp.arange(n_lat), RET_DK, ROPE_BASE)
    hd = lambda t, dd: t.reshape(t.shape[0], RET_HEADS, dd)
    s_zero = jnp.zeros((RET_HEADS, RET_DK, RET_DV), F32)
    _, s_f, s_b = _retention(_heads(hd(part(0, ctx_rows), RET_DK)), _heads(hd(part(1, ctx_rows), RET_DK) * (RET_DK ** -0.5)),
                             _heads(hd(part(2, ctx_rows), RET_DV)), lgf, lgb, s_zero, s_zero)
    rq = _rotate(hd(part(0, lat_rows), RET_DK), *ret_tab)
    rk = _rotate(hd(part(1, lat_rows), RET_DK) * (RET_DK ** -0.5), *ret_tab)
    y_lat, _, _ = _retention(_heads(rq), _heads(rk), _heads(hd(part(2, lat_rows), RET_DV)), lgf, lgb, s_f, s_b)
    ret_out = _gn_gate(y_lat, part(3, lat_rows))

    pos = jnp.arange(n_lat)
    row_tab = _rope_tables(pos // GRID_W, MLA_ROPE // 2, ROPE_BASE)
    col_tab = _rope_tables(pos % GRID_W, MLA_ROPE // 2, ROPE_BASE)
    q = q.reshape(n_lat, MLA_HEADS, MLA_NOPE + MLA_ROPE)
    q_all = jnp.concatenate([q[..., :MLA_NOPE], _axial_rope(q[..., MLA_NOPE:], row_tab, col_tab)], axis=-1)
    kv = kv.reshape(n_ctx + n_lat, MLA_HEADS, MLA_NOPE + MLA_V)
    kr_lat = _axial_rope(part(6, lat_rows)[:, None, :], row_tab, col_tab)
    kr = jnp.concatenate([kr_lat, part(6, ctx_rows)[:, None, :]], axis=0)
    kv_lat_first = jnp.concatenate([kv[n_ctx:], kv[:n_ctx]], axis=0)
    k_all = jnp.concatenate([kv_lat_first[..., :MLA_NOPE], jnp.broadcast_to(kr, (n_ctx + n_lat, MLA_HEADS, MLA_ROPE))], axis=-1)
    mla = _attention(_heads(q_all), _heads(k_all), _heads(kv_lat_first[..., MLA_NOPE:]))
    mla_out = _heads(mla).reshape(n_lat, MLA_HEADS * MLA_V)

    x2 = _make_linear_gated_res("mix_out")(x1[n_ctx:], jnp.concatenate([ret_out, mla_out], axis=-1), big["mix_out"], lat(5))
    x3 = _make_ffn(0, "ffn2")(x2, small["norm3_g"], lat(6), lat(7), lat(8), big["ffn2_wg"], big["ffn2_wu"], big["ffn2_wo"])
    return _final_loss(x3, small["final_norm_g"][None, :], target)


HBM_SPEC = pl.BlockSpec(memory_space=pl.ANY)
VMEM_SPEC = pl.BlockSpec(memory_space=pltpu.VMEM)
ALL_PEERS = (1, 2, 3, 4, 5, 6, 7)
CHIP_PEERS = (4, 2, 6)


def _me():
    return lax.axis_index("x"), lax.axis_index("y"), lax.axis_index("c")


def _flip(pos, mask):
    x, y, c = pos
    return (1 - x if mask & 4 else x, 1 - y if mask & 2 else y, 1 - c if mask & 1 else c)


def _allgather_small(block, masks, chips_only, name):
    r, c = block.shape
    n_slots = 4 if chips_only else 8

    def body(x_ref, out_ref, send_sems, recv_sems, local_sem):
        pos = _me()
        slot = 2 * pos[0] + pos[1] if chips_only else 4 * pos[0] + 2 * pos[1] + pos[2]
        local = pltpu.make_async_copy(x_ref, out_ref.at[slot], local_sem)
        local.start()
        copies = [pltpu.make_async_remote_copy(src_ref=x_ref, dst_ref=out_ref.at[slot], send_sem=send_sems.at[j], recv_sem=recv_sems.at[j],
                                               device_id=_flip(pos, mask), device_id_type=MESH) for j, mask in enumerate(masks)]
        for cp in copies:
            cp.start()
        for cp in copies:
            cp.wait()
        local.wait()

    return pl.pallas_call(
        body, name=name, in_specs=[VMEM_SPEC], out_specs=VMEM_SPEC,
        out_shape=jax.ShapeDtypeStruct((n_slots, r, c), block.dtype),
        scratch_shapes=[pltpu.SemaphoreType.DMA((len(masks),)), pltpu.SemaphoreType.DMA((len(masks),)), pltpu.SemaphoreType.DMA],
        compiler_params=pltpu.CompilerParams(vmem_limit_bytes=VMEM_LIMIT_BYTES),
    )(block)


def _gather_weights(shards):
    n = len(shards)

    def body(*refs):
        ins, outs = refs[:n], refs[n:2 * n]
        send_sems, recv_sems, pass_send, pass_recv = refs[2 * n:]
        pos = _me()
        x, y, c = pos
        sibling = _flip(pos, 1)
        sends, passes = [], []
        for w in range(n):
            half = shards[w].shape[0] // 2
            mine = pl.ds(c * half, half)
            for j, mask in enumerate(CHIP_PEERS):
                cp = pltpu.make_async_remote_copy(src_ref=ins[w].at[mine], dst_ref=outs[w].at[2 * x + y, mine], send_sem=send_sems.at[3 * w + j],
                                                  recv_sem=recv_sems.at[3 * w + j], device_id=_flip(pos, mask), device_id_type=MESH)
                cp.start()
                sends.append(cp)
        for w in range(n):
            half = shards[w].shape[0] // 2
            mine = pl.ds(c * half, half)
            for j, mask in enumerate(CHIP_PEERS):
                px, py, _ = _flip(pos, mask)
                landed = outs[w].at[2 * px + py, mine]
                pltpu.make_async_remote_copy(src_ref=landed, dst_ref=landed, send_sem=send_sems.at[3 * w + j], recv_sem=recv_sems.at[3 * w + j],
                                             device_id=_flip(pos, mask), device_id_type=MESH).wait_recv()
                cp = pltpu.make_async_remote_copy(src_ref=landed, dst_ref=landed, send_sem=pass_send.at[3 * w + j], recv_sem=pass_recv.at[3 * w + j],
                                                  device_id=sibling, device_id_type=MESH)
                cp.start()
                passes.append(cp)
        for w in range(n):
            half = shards[w].shape[0] // 2
            theirs = pl.ds((1 - c) * half, half)
            for j, mask in enumerate(CHIP_PEERS):
                px, py, _ = _flip(pos, mask)
                slab = outs[w].at[2 * px + py, theirs]
                pltpu.make_async_remote_copy(src_ref=slab, dst_ref=slab, send_sem=pass_send.at[3 * w + j], recv_sem=pass_recv.at[3 * w + j],
                                             device_id=sibling, device_id_type=MESH).wait_recv()
        for cp in sends + passes:
            cp.wait_send()

    dma = lambda k: pltpu.SemaphoreType.DMA((k,))
    stacked = pl.pallas_call(
        body, name="gather_weights", in_specs=[HBM_SPEC] * n, out_specs=[HBM_SPEC] * n,
        out_shape=[jax.ShapeDtypeStruct((4,) + s.shape, s.dtype) for s in shards],
        scratch_shapes=[dma(3 * n), dma(3 * n), dma(3 * n), dma(3 * n)],
    )(*shards)
    chip = 2 * lax.axis_index("x") + lax.axis_index("y")
    return [lax.dynamic_update_slice_in_dim(st, sh[None], chip, axis=0) for st, sh in zip(stacked, shards)]


def _pair_swap_halves(grads):
    n = len(grads)

    def body(*refs):
        ins, outs = refs[:n], refs[n:2 * n]
        send_sems, recv_sems = refs[2 * n:]
        pos = _me()
        copies = []
        for w in range(n):
            half = grads[w].shape[1] // 2
            cp = pltpu.make_async_remote_copy(src_ref=ins[w].at[:, pl.ds((1 - pos[2]) * half, half), :], dst_ref=outs[w], send_sem=send_sems.at[w],
                                              recv_sem=recv_sems.at[w], device_id=_flip(pos, 1), device_id_type=MESH)
            cp.start()
            copies.append(cp)
        for cp in copies:
            cp.wait()

    return pl.pallas_call(
        body, name="pair_swap_halves", in_specs=[HBM_SPEC] * n, out_specs=[HBM_SPEC] * n,
        out_shape=[jax.ShapeDtypeStruct((4, g.shape[1] // 2, g.shape[2]), g.dtype) for g in grads],
        scratch_shapes=[pltpu.SemaphoreType.DMA((n,)), pltpu.SemaphoreType.DMA((n,))],
    )(*grads)


def _chip_scatter(parts):
    n = len(parts)

    def body(*refs):
        ins, outs = refs[:n], refs[n:2 * n]
        send_sems, recv_sems, local_sems = refs[2 * n:]
        pos = _me()
        me = 2 * pos[0] + pos[1]
        copies, local = [], []
        for w in range(n):
            cp = pltpu.make_async_copy(ins[w].at[me], outs[w].at[me], local_sems.at[w])
            cp.start()
            local.append(cp)
            for j, mask in enumerate(CHIP_PEERS):
                px, py, _ = _flip(pos, mask)
                cp = pltpu.make_async_remote_copy(src_ref=ins[w].at[2 * px + py], dst_ref=outs[w].at[me], send_sem=send_sems.at[3 * w + j],
                                                  recv_sem=recv_sems.at[3 * w + j], device_id=_flip(pos, mask), device_id_type=MESH)
                cp.start()
                copies.append(cp)
        for cp in copies:
            cp.wait()
        for cp in local:
            cp.wait()

    dma = lambda k: pltpu.SemaphoreType.DMA((k,))
    return pl.pallas_call(
        body, name="chip_scatter", in_specs=[HBM_SPEC] * n, out_specs=[HBM_SPEC] * n,
        out_shape=[jax.ShapeDtypeStruct(p.shape, p.dtype) for p in parts],
        scratch_shapes=[dma(3 * n), dma(3 * n), dma(n)],
    )(*parts)


def _pair_swap_reduced(halves):
    n = len(halves)

    def body(*refs):
        ins, outs = refs[:n], refs[n:2 * n]
        send_sems, recv_sems = refs[2 * n:]
        pos = _me()
        copies = []
        for w in range(n):
            cp = pltpu.make_async_remote_copy(src_ref=ins[w], dst_ref=outs[w], send_sem=send_sems.at[w], recv_sem=recv_sems.at[w],
                                              device_id=_flip(pos, 1), device_id_type=MESH)
            cp.start()
            copies.append(cp)
        for cp in copies:
            cp.wait()

    dma = lambda k: pltpu.SemaphoreType.DMA((k,))
    return pl.pallas_call(
        body, name="pair_swap_reduced", in_specs=[HBM_SPEC] * n, out_specs=[HBM_SPEC] * n,
        out_shape=[jax.ShapeDtypeStruct(h.shape, h.dtype) for h in halves],
        scratch_shapes=[dma(n), dma(n)],
    )(*halves)


def _add_pair(mine, theirs, name):
    s, h, c = mine.shape
    r = _tile(h, max(16, (1 << 19) // c), 16)

    def body(a_ref, b_ref, o_ref):
        o_ref[...] = (a_ref[...].astype(F32) + b_ref[...].astype(F32)).astype(BF16)

    blk = pl.BlockSpec((1, r, c), lambda i, j: (i, j, 0))
    return pl.pallas_call(
        body, name=name, grid=(s, h // r), in_specs=[blk, blk], out_specs=blk,
        out_shape=jax.ShapeDtypeStruct(mine.shape, BF16), compiler_params=_params(("parallel", "parallel")),
    )(mine, theirs)


def _sum_slots(parts, name):
    s, h, c = parts.shape
    r = _tile(h, max(16, (1 << 18) // c), 16)

    def body(p_ref, o_ref):
        acc = p_ref[0].astype(F32)
        for k in range(1, s):
            acc = acc + p_ref[k].astype(F32)
        o_ref[...] = acc

    return pl.pallas_call(
        body, name=name, grid=(h // r,), in_specs=[pl.BlockSpec((s, r, c), lambda i: (0, i, 0))],
        out_specs=pl.BlockSpec((r, c), lambda i: (i, 0)),
        out_shape=jax.ShapeDtypeStruct((h, c), F32), compiler_params=_params(("parallel",)),
    )(parts)


def _reduce_scatter_grads(stacked):
    c = lax.axis_index("c")
    theirs = _pair_swap_halves(stacked)
    parts = []
    for w, (g, t) in enumerate(zip(stacked, theirs)):
        half = g.shape[1] // 2
        mine = lax.dynamic_slice_in_dim(g, c * half, half, axis=1)
        parts.append(_add_pair(mine, t, "rs_add_pair_%d" % w))
    landed = _chip_scatter(parts)
    halves = [_sum_slots(p, "rs_sum_slots_%d" % w) for w, p in enumerate(landed)]
    return list(zip(halves, _pair_swap_reduced(halves)))


def _adamw_math(w, g, m, v):
    m = ADAM_B1 * m + (1.0 - ADAM_B1) * g
    v = ADAM_B2 * v + (1.0 - ADAM_B2) * (g * g)
    m_hat = m / (1.0 - ADAM_B1 ** ADAM_STEP)
    v_hat = v / (1.0 - ADAM_B2 ** ADAM_STEP)
    return -ADAM_LR * (m_hat / (jnp.sqrt(v_hat) + ADAM_EPS) + ADAM_WD * w), m, v


def _adamw(w, g, m, v, name):
    rows, cols = w.shape
    r = _tile(rows, max(SUBLANES, (1 << 18) // cols), SUBLANES)

    def body(w_ref, g_ref, m_ref, v_ref, d_ref, mo_ref, vo_ref):
        d_ref[...], mo_ref[...], vo_ref[...] = _adamw_math(w_ref[...], g_ref[...], m_ref[...], v_ref[...])

    blk = pl.BlockSpec((r, cols), lambda i: (i, 0))
    return pl.pallas_call(
        body, name=name, grid=(rows // r,), in_specs=[blk] * 4, out_specs=[blk] * 3,
        out_shape=[jax.ShapeDtypeStruct(w.shape, F32)] * 3, compiler_params=_params(("parallel",)),
    )(w, g, m, v)


def _adamw_halves(w, g_mine, g_theirs, m, v, core, name):
    rows, cols = w.shape
    half = rows // 2
    r = _tile(half, max(SUBLANES, (1 << 18) // cols), SUBLANES)
    nbh = half // r

    def body(core_ref, w_ref, gm_ref, gt_ref, m_ref, v_ref, g_ref, d_ref, mo_ref, vo_ref):
        is_mine = (pl.program_id(0) // nbh) == core_ref[0]

        @pl.when(is_mine)
        def _():
            g_ref[...] = gm_ref[...]

        @pl.when(jnp.logical_not(is_mine))
        def _():
            g_ref[...] = gt_ref[...]

        g = g_ref[...]
        d_ref[...], mo_ref[...], vo_ref[...] = _adamw_math(w_ref[...], g, m_ref[...], v_ref[...])

    full = pl.BlockSpec((r, cols), lambda i, core_ref: (i, 0))
    part = pl.BlockSpec((r, cols), lambda i, core_ref: (i % nbh, 0))
    return pl.pallas_call(
        body, name=name,
        grid_spec=pltpu.PrefetchScalarGridSpec(num_scalar_prefetch=1, grid=(rows // r,), in_specs=[full, part, part, full, full],
                                               out_specs=[full] * 4),
        out_shape=[jax.ShapeDtypeStruct(w.shape, F32)] * 4, compiler_params=_params(("parallel",)),
    )(core, w, g_mine, g_theirs, m, v)


def _adamw_reduced(parts, w, m, v, name):
    def body(p_ref, w_ref, m_ref, v_ref, g_ref, d_ref, mo_ref, vo_ref):
        g = p_ref[0]
        for k in range(1, parts.shape[0]):
            g = g + p_ref[k]
        g_ref[...] = g
        d_ref[...], mo_ref[...], vo_ref[...] = _adamw_math(w_ref[...], g, m_ref[...], v_ref[...])

    return pl.pallas_call(
        body, name=name, in_specs=[VMEM_SPEC] * 4, out_specs=[VMEM_SPEC] * 4,
        out_shape=[jax.ShapeDtypeStruct(w.shape, F32)] * 4,
        compiler_params=pltpu.CompilerParams(vmem_limit_bytes=VMEM_LIMIT_BYTES),
    )(parts, w, m, v)


WEIGHTS = ("c_ctx", "ada_w", "ada_b", "norm1_g", "ffn1_w_in", "ffn1_w_out", "norm2_g", "mix_w_in", "ret_decay_fwd", "ret_decay_bwd",
           "mla_q_norm_g", "mla_w_uq", "mla_kv_norm_g", "mla_w_ukv", "mix_w_out", "norm3_g", "ffn2_w_in", "ffn2_w_out", "final_norm_g")
SMALL = ("c_ctx", "ada_b", "norm1_g", "norm2_g", "ret_decay_fwd", "ret_decay_bwd", "mla_q_norm_g", "mla_kv_norm_g", "norm3_g", "final_norm_g")
BIG = (("ffn1_w_in", 1), ("ffn1_w_out", 0), ("mix_w_in", 1), ("mla_w_uq", 1), ("mla_w_ukv", 1), ("mix_w_out", 0), ("ffn2_w_in", 1), ("ffn2_w_out", 0))


def _pack(vectors):
    flat = jnp.concatenate([v.reshape(-1) for v in vectors])
    return jnp.pad(flat, (0, -flat.shape[0] % (SUBLANES * LANES))).reshape(SUBLANES, -1)


def _rows8(a):
    return a.reshape(a.shape[0] * SUBLANES, a.shape[1] // SUBLANES)


def _unpack(packed, like):
    packed = packed.reshape(-1)
    out, off = [], 0
    for ref in like:
        out.append(packed[off:off + ref.size].reshape(ref.shape))
        off += ref.size
    return out


def kernel(x, c, ctx, c_ctx, ada_w, ada_b, norm1_g, ffn1_w_in, ffn1_w_out, norm2_g, mix_w_in, ret_decay_fwd, ret_decay_bwd, mla_q_norm_g, mla_w_uq, mla_kv_norm_g, mla_w_ukv, mix_w_out, norm3_g, ffn2_w_in, ffn2_w_out, final_norm_g, loss_target, m_c_ctx, m_ada_w, m_ada_b, m_norm1_g, m_ffn1_w_in, m_ffn1_w_out, m_norm2_g, m_mix_w_in, m_ret_decay_fwd, m_ret_decay_bwd, m_mla_q_norm_g, m_mla_w_uq, m_mla_kv_norm_g, m_mla_w_ukv, m_mix_w_out, m_norm3_g, m_ffn2_w_in, m_ffn2_w_out, m_final_norm_g, v_c_ctx, v_ada_w, v_ada_b, v_norm1_g, v_ffn1_w_in, v_ffn1_w_out, v_norm2_g, v_mix_w_in, v_ret_decay_fwd, v_ret_decay_bwd, v_mla_q_norm_g, v_mla_w_uq, v_mla_kv_norm_g, v_mla_w_ukv, v_mix_w_out, v_norm3_g, v_ffn2_w_in, v_ffn2_w_out, v_final_norm_g):
    w = dict(c_ctx=c_ctx, ada_w=ada_w, ada_b=ada_b, norm1_g=norm1_g, ffn1_w_in=ffn1_w_in, ffn1_w_out=ffn1_w_out, norm2_g=norm2_g,
             mix_w_in=mix_w_in, ret_decay_fwd=ret_decay_fwd, ret_decay_bwd=ret_decay_bwd, mla_q_norm_g=mla_q_norm_g, mla_w_uq=mla_w_uq,
             mla_kv_norm_g=mla_kv_norm_g, mla_w_ukv=mla_w_ukv, mix_w_out=mix_w_out, norm3_g=norm3_g, ffn2_w_in=ffn2_w_in,
             ffn2_w_out=ffn2_w_out, final_norm_g=final_norm_g)
    mom_m = dict(zip(WEIGHTS, (m_c_ctx, m_ada_w, m_ada_b, m_norm1_g, m_ffn1_w_in, m_ffn1_w_out, m_norm2_g, m_mix_w_in, m_ret_decay_fwd,
                               m_ret_decay_bwd, m_mla_q_norm_g, m_mla_w_uq, m_mla_kv_norm_g, m_mla_w_ukv, m_mix_w_out, m_norm3_g,
                               m_ffn2_w_in, m_ffn2_w_out, m_final_norm_g)))
    mom_v = dict(zip(WEIGHTS, (v_c_ctx, v_ada_w, v_ada_b, v_norm1_g, v_ffn1_w_in, v_ffn1_w_out, v_norm2_g, v_mix_w_in, v_ret_decay_fwd,
                               v_ret_decay_bwd, v_mla_q_norm_g, v_mla_w_uq, v_mla_kv_norm_g, v_mla_w_ukv, v_mix_w_out, v_norm3_g,
                               v_ffn2_w_in, v_ffn2_w_out, v_final_norm_g)))
    xi, yi, ci = _me()
    chip = 2 * xi + yi
    example = 2 * chip + ci
    d = x.shape[-1]
    n_mod = ada_b.shape[-1] // d

    c_all = _allgather_small(_rows8(c), ALL_PEERS, False, "gather_c").reshape(8, d)
    cond = jnp.concatenate([c_all, jnp.broadcast_to(c_ctx[None, :], (8, d))], axis=0)
    cond_act = jax.nn.silu(cond)
    n_cols = ada_w.shape[-1]
    bias = lax.dynamic_slice_in_dim(ada_b, chip * n_cols, n_cols, axis=1)
    mods_cols = _matmul(cond_act, ada_w[0], "nn", F32, "ada_fwd", add=jnp.broadcast_to(bias, (16, n_cols)))
    mods = jnp.swapaxes(_allgather_small(mods_cols, CHIP_PEERS, True, "gather_mods"), 0, 1).reshape(16, 4 * n_cols)
    mods_lat = lax.dynamic_slice_in_dim(mods, example, 1, axis=0).reshape(n_mod, d)
    mods_ctx = mods[8].reshape(n_mod, d)

    stacked = _gather_weights([w[name][0].astype(BF16) for name, _ in BIG])
    full = {}
    for (name, axis), st in zip(BIG, stacked):
        full[name] = st.reshape(-1, st.shape[-1]) if axis == 0 else jnp.swapaxes(st, 0, 1).reshape(st.shape[1], -1)
    d_ff = full["ffn1_w_out"].shape[0]
    big = dict(ffn1_wg=full["ffn1_w_in"][:, :d_ff], ffn1_wu=full["ffn1_w_in"][:, d_ff:], ffn1_wo=full["ffn1_w_out"],
               mix_in=jnp.pad(full["mix_w_in"], ((0, 0), (0, MIX_IN_PAD - MIX_IN))), w_uq=full["mla_w_uq"], w_ukv=full["mla_w_ukv"],
               mix_out=full["mix_w_out"], ffn2_wg=full["ffn2_w_in"][:, :d_ff], ffn2_wu=full["ffn2_w_in"][:, d_ff:], ffn2_wo=full["ffn2_w_out"])
    small = {k: w[k] for k in ("norm1_g", "norm2_g", "norm3_g", "final_norm_g", "mla_q_norm_g", "mla_kv_norm_g", "ret_decay_fwd", "ret_decay_bwd")}

    loss_mine, (dx, dmods_lat, dmods_ctx, dsmall, dbig) = jax.value_and_grad(_local_loss, argnums=(0, 1, 2, 3, 4))(
        x[0], mods_lat, mods_ctx, small, big, ctx[0], loss_target[0])

    dmods = _allgather_small(_rows8(jnp.stack([dmods_lat.reshape(-1), dmods_ctx.reshape(-1)])), ALL_PEERS, False, "gather_dmods")
    dmods = dmods.reshape(8, 2, n_mod * d)
    dmods_rows = jnp.concatenate([dmods[:, 0, :], dmods[:, 1, :]], axis=0)
    dmods_cols = lax.dynamic_slice_in_dim(dmods_rows, chip * n_cols, n_cols, axis=1)
    g_ada_w = _matmul(cond_act, dmods_cols, "tn", F32, "ada_dw")
    dcond_act = _matmul(dmods_cols, ada_w[0], "nt", F32, "ada_dcond")
    sig = jax.nn.sigmoid(c_ctx)
    dc_ctx = jnp.sum(dcond_act[8:], axis=0) * (sig * (1.0 + c_ctx * (1.0 - sig)))
    share = dict(dsmall)
    share["c_ctx"] = jnp.where(ci == 0, dc_ctx, jnp.zeros_like(dc_ctx))
    share["ada_b"] = (dmods_lat + dmods_ctx).reshape(1, -1)
    zero = jnp.zeros((1,), F32)
    parts = _allgather_small(_pack([share[k] for k in SMALL] + [loss_mine.reshape(1)]), ALL_PEERS, False, "gather_small_grads")
    packed = _adamw_reduced(parts, _pack([w[k] for k in SMALL] + [zero]), _pack([mom_m[k] for k in SMALL] + [zero]),
                            _pack([mom_v[k] for k in SMALL] + [zero]), "adamw_small")
    like = [w[k] for k in SMALL] + [zero]
    grads, deltas, new_m, new_v = ({k: a for k, a in zip(SMALL + ("loss",), _unpack(p, like))} for p in packed)
    loss = grads.pop("loss").reshape(())

    dfull = {"ffn1_w_in": jnp.concatenate([dbig["ffn1_wg"], dbig["ffn1_wu"]], axis=1), "ffn1_w_out": dbig["ffn1_wo"],
             "mix_w_in": dbig["mix_in"][:, :MIX_IN], "mla_w_uq": dbig["w_uq"], "mla_w_ukv": dbig["w_ukv"], "mix_w_out": dbig["mix_out"],
             "ffn2_w_in": jnp.concatenate([dbig["ffn2_wg"], dbig["ffn2_wu"]], axis=1), "ffn2_w_out": dbig["ffn2_wo"]}
    dstacked = []
    for name, axis in BIG:
        g = dfull[name]
        dstacked.append(g.reshape(4, g.shape[0] // 4, g.shape[1]) if axis == 0 else jnp.swapaxes(g.reshape(g.shape[0], 4, g.shape[1] // 4), 0, 1))
    core = ci.astype(jnp.int32).reshape(1)
    for (name, _), (g_mine, g_theirs) in zip(BIG, _reduce_scatter_grads(dstacked)):
        g, dl, mo, vo = _adamw_halves(w[name][0], g_mine, g_theirs, mom_m[name][0], mom_v[name][0], core, "adamw_" + name)
        grads[name], deltas[name], new_m[name], new_v[name] = g[None], dl[None], mo[None], vo[None]
    dl, mo, vo = _adamw(ada_w[0], g_ada_w, m_ada_w[0], v_ada_w[0], "adamw_ada_w")
    grads["ada_w"], deltas["ada_w"], new_m["ada_w"], new_v["ada_w"] = g_ada_w[None], dl[None], mo[None], vo[None]

    return (loss, dx[None], *[grads[k] for k in WEIGHTS], *[deltas[k] for k in WEIGHTS], *[new_m[k] for k in WEIGHTS],
            *[new_v[k] for k in WEIGHTS])
```

```python
import functools

import jax
import jax.numpy as jnp
import numpy as np
from jax import lax
from jax.experimental import pallas as pl
from jax.experimental.pallas import tpu as pltpu

F32 = jnp.float32
BF16 = jnp.bfloat16
MESH = pl.DeviceIdType.MESH

VMEM_LIMIT_BYTES = 52 * 1024 * 1024
LANES = 128
SUBLANES = 8

D_FF_SPLIT = 2
RET_HEADS, RET_DK, RET_DV, RET_CHUNK = 8, 64, 128, 128
MLA_HEADS, MLA_Q_RANK, MLA_KV_RANK, MLA_NOPE, MLA_ROPE, MLA_V = 8, 512, 256, 128, 64, 128
GRID_W = 64
ROPE_BASE = 10000.0
RMS_EPS = 1e-6
GN_EPS = 1e-5
MIX_SPLITS = (RET_HEADS * RET_DK, RET_HEADS * RET_DK, RET_HEADS * RET_DV, RET_HEADS * RET_DV,
              MLA_Q_RANK, MLA_KV_RANK, MLA_ROPE)
MIX_IN = sum(MIX_SPLITS)
MIX_IN_PAD = 4096
ADAM_LR, ADAM_B1, ADAM_B2, ADAM_EPS, ADAM_WD, ADAM_STEP = 0.001, 0.9, 0.999, 1e-08, 0.01, 10


def _tile(n, pref, align):
    best = None
    t = align
    while t <= min(n, pref):
        if n % t == 0:
            best = t
        t += align
    return n if best is None else best


def _params(sem=None):
    return pltpu.CompilerParams(dimension_semantics=sem, vmem_limit_bytes=VMEM_LIMIT_BYTES)


def _matmul(a, b, mode, out_dtype, name, add=None, tm=1024, tn=1024, tk=2048):
    if mode == "nn":
        (m, k), (k2, n) = a.shape, b.shape
        dims = (((1,), (0,)), ((), ()))
    elif mode == "nt":
        (m, k), (n, k2) = a.shape, b.shape
        dims = (((1,), (1,)), ((), ()))
    else:
        (k, m), (k2, n) = a.shape, b.shape
        dims = (((0,), (0,)), ((), ()))
    assert k == k2, (a.shape, b.shape, mode)
    tm = _tile(m, tm, LANES if mode == "tn" else 16)
    tn = _tile(n, tn, LANES)
    tk = _tile(k, tk, LANES if mode != "tn" else 16)
    nk = k // tk
    a_spec = pl.BlockSpec((tk, tm), lambda i, j, kk: (kk, i)) if mode == "tn" else pl.BlockSpec((tm, tk), lambda i, j, kk: (i, kk))
    b_spec = pl.BlockSpec((tn, tk), lambda i, j, kk: (j, kk)) if mode == "nt" else pl.BlockSpec((tk, tn), lambda i, j, kk: (kk, j))
    o_spec = pl.BlockSpec((tm, tn), lambda i, j, kk: (i, j))
    has_add = add is not None

    def body(*refs):
        a_ref, b_ref = refs[0], refs[1]
        add_ref = refs[2] if has_add else None
        o_ref = refs[2 + has_add]
        p = lax.dot_general(a_ref[...].astype(BF16), b_ref[...].astype(BF16), dims, preferred_element_type=F32)
        if nk == 1:
            if has_add:
                p = p + add_ref[...].astype(F32)
            o_ref[...] = p.astype(out_dtype)
        else:
            acc = refs[3 + has_add]
            kk = pl.program_id(2)

            @pl.when(kk == 0)
            def _():
                acc[...] = p + add_ref[...].astype(F32) if has_add else p

            @pl.when(kk > 0)
            def _():
                acc[...] += p

            @pl.when(kk == nk - 1)
            def _():
                o_ref[...] = acc[...].astype(out_dtype)

    return pl.pallas_call(
        body, name=name, grid=(m // tm, n // tn, nk),
        in_specs=[a_spec, b_spec] + ([o_spec] if has_add else []),
        out_specs=o_spec,
        out_shape=jax.ShapeDtypeStruct((m, n), out_dtype),
        scratch_shapes=[pltpu.VMEM((tm, tn), F32)] if nk > 1 else [],
        compiler_params=_params(("parallel", "parallel", "arbitrary")),
    )(*((a, b, add) if has_add else (a, b)))


def _row_tile(t, n_ctx, d):
    pref = max(SUBLANES, min(256, (1 << 19) // d))
    r = _tile(int(np.gcd(t, n_ctx)) if n_ctx else t, pref, SUBLANES)
    return r, (n_ctx // r if n_ctx else 0)


def _seg_map(nct):
    if nct:
        return lambda i: (jnp.minimum(i // nct, 1), 0, 0)
    return lambda i: (0, 0, 0)


def _normmod_fwd(x, g, shift, scale, n_ctx, name):
    t, d = x.shape
    r, nct = _row_tile(t, n_ctx, d)

    def body(x_ref, g_ref, sh_ref, sc_ref, h_ref, ht_ref):
        xv = x_ref[...]
        rstd = lax.rsqrt(jnp.mean(xv * xv, axis=-1, keepdims=True) + RMS_EPS)
        n = xv * rstd * g_ref[...]
        h = n * (1.0 + sc_ref[0]) + sh_ref[0]
        h_ref[...] = h.astype(BF16)
        ht_ref[...] = h.T.astype(BF16)

    row = pl.BlockSpec((r, d), lambda i: (i, 0))
    seg = pl.BlockSpec((1, 1, d), _seg_map(nct))
    return pl.pallas_call(
        body, name=name, grid=(t // r,),
        in_specs=[row, pl.BlockSpec((1, d), lambda i: (0, 0)), seg, seg],
        out_specs=[row, pl.BlockSpec((d, r), lambda i: (0, i))],
        out_shape=[jax.ShapeDtypeStruct((t, d), BF16), jax.ShapeDtypeStruct((d, t), BF16)],
        compiler_params=_params(("parallel",)),
    )(x, g, shift, scale)


def _normmod_bwd(dh, x, g, shift, scale, dres, n_ctx, name):
    t, d = x.shape
    r, nct = _row_tile(t, n_ctx, d)
    has_res = dres is not None
    nseg = shift.shape[0]

    def body(*refs):
        dh_ref, x_ref, g_ref, sh_ref, sc_ref = refs[:5]
        dres_ref = refs[5] if has_res else None
        dx_ref, dg_ref, dsh_ref, dsc_ref = refs[5 + has_res:]
        i = pl.program_id(0)
        xv = x_ref[...]
        dhv = dh_ref[...].astype(F32)
        rstd = lax.rsqrt(jnp.mean(xv * xv, axis=-1, keepdims=True) + RMS_EPS)
        y = xv * rstd
        gv = g_ref[...]
        dn = dhv * (1.0 + sc_ref[0])
        dy = dn * gv
        dx = rstd * (dy - y * jnp.mean(dy * y, axis=-1, keepdims=True))
        if has_res:
            dx = dx + dres_ref[...]
        dx_ref[...] = dx

        @pl.when(i == 0)
        def _():
            dg_ref[...] = jnp.zeros_like(dg_ref)

        @pl.when(jnp.logical_or(i == 0, i == nct))
        def _():
            dsh_ref[...] = jnp.zeros_like(dsh_ref)
            dsc_ref[...] = jnp.zeros_like(dsc_ref)

        dg_ref[...] += jnp.sum(dn * y, axis=0, keepdims=True)
        dsh_ref[0] += jnp.sum(dhv, axis=0, keepdims=True)
        dsc_ref[0] += jnp.sum(dhv * (y * gv), axis=0, keepdims=True)

    row = pl.BlockSpec((r, d), lambda i: (i, 0))
    seg = pl.BlockSpec((1, 1, d), _seg_map(nct))
    vec = pl.BlockSpec((1, d), lambda i: (0, 0))
    return pl.pallas_call(
        body, name=name, grid=(t // r,),
        in_specs=[row, row, vec, seg, seg] + ([row] if has_res else []),
        out_specs=[row, vec, seg, seg],
        out_shape=[jax.ShapeDtypeStruct((t, d), F32), jax.ShapeDtypeStruct((1, d), F32),
                   jax.ShapeDtypeStruct((nseg, 1, d), F32), jax.ShapeDtypeStruct((nseg, 1, d), F32)],
        compiler_params=_params(("arbitrary",)),
    )(*((dh, x, g, shift, scale, dres) if has_res else (dh, x, g, shift, scale)))


def _gated_res_fwd(x, y, gate, coef, n_ctx, name):
    t, d = x.shape
    r, nct = _row_tile(t, n_ctx, d)

    def body(x_ref, y_ref, gt_ref, o_ref):
        o_ref[...] = x_ref[...] + (coef * gt_ref[0]) * y_ref[...]

    row = pl.BlockSpec((r, d), lambda i: (i, 0))
    return pl.pallas_call(
        body, name=name, grid=(t // r,),
        in_specs=[row, row, pl.BlockSpec((1, 1, d), _seg_map(nct))],
        out_specs=row, out_shape=jax.ShapeDtypeStruct((t, d), F32),
        compiler_params=_params(("parallel",)),
    )(x, y, gate)


def _matmul_gated_res(a, w, x, gate, coef, name, tm):
    t, k = a.shape
    d = w.shape[1]
    tm = _tile(t, tm, 16)
    tn = _tile(d, 512, LANES)

    def body(a_ref, w_ref, x_ref, gt_ref, o_ref, y_ref):
        y = jnp.dot(a_ref[...], w_ref[...], preferred_element_type=F32)
        o_ref[...] = x_ref[...] + (coef * gt_ref[0]) * y
        y_ref[...] = y.astype(BF16)

    blk = pl.BlockSpec((tm, tn), lambda i, j: (i, j))
    return pl.pallas_call(
        body, name=name, grid=(t // tm, d // tn),
        in_specs=[pl.BlockSpec((tm, k), lambda i, j: (i, 0)), pl.BlockSpec((k, tn), lambda i, j: (0, j)), blk,
                  pl.BlockSpec((1, 1, tn), lambda i, j: (0, 0, j))],
        out_specs=[blk, blk], out_shape=[jax.ShapeDtypeStruct((t, d), F32), jax.ShapeDtypeStruct((t, d), BF16)],
        compiler_params=_params(("parallel", "parallel")),
    )(a, w, x, gate)


def _gated_res_bwd(dout, y, gate, coef, n_ctx, name):
    t, d = dout.shape
    r, nct = _row_tile(t, n_ctx, d)
    nseg = gate.shape[0]

    def body(do_ref, y_ref, gt_ref, dy_ref, dgt_ref):
        i = pl.program_id(0)
        dov = do_ref[...] * coef
        dy_ref[...] = (dov * gt_ref[0]).astype(BF16)

        @pl.when(jnp.logical_or(i == 0, i == nct))
        def _():
            dgt_ref[...] = jnp.zeros_like(dgt_ref)

        dgt_ref[0] += jnp.sum(dov * y_ref[...], axis=0, keepdims=True)

    row = pl.BlockSpec((r, d), lambda i: (i, 0))
    seg = pl.BlockSpec((1, 1, d), _seg_map(nct))
    return pl.pallas_call(
        body, name=name, grid=(t // r,),
        in_specs=[row, row, seg], out_specs=[row, seg],
        out_shape=[jax.ShapeDtypeStruct((t, d), BF16), jax.ShapeDtypeStruct((nseg, 1, d), F32)],
        compiler_params=_params(("arbitrary",)),
    )(dout, y, gate)


def _swiglu_matmul(h, wg, wu, name, tm):
    t, k = h.shape
    f = wg.shape[1]
    tm = _tile(t, tm, LANES)
    tn = _tile(f, 512, LANES)

    def body(h_ref, wg_ref, wu_ref, g_ref, u_ref, a_ref, at_ref):
        hv = h_ref[...]
        g = jnp.dot(hv, wg_ref[...], preferred_element_type=F32)
        u = jnp.dot(hv, wu_ref[...], preferred_element_type=F32)
        a = g * jax.nn.sigmoid(g) * u
        g_ref[...] = g.astype(BF16)
        u_ref[...] = u.astype(BF16)
        a_ref[...] = a.astype(BF16)
        at_ref[...] = a.T.astype(BF16)

    w_spec = pl.BlockSpec((k, tn), lambda i, j: (0, j))
    o_spec = pl.BlockSpec((tm, tn), lambda i, j: (i, j))
    return pl.pallas_call(
        body, name=name, grid=(t // tm, f // tn),
        in_specs=[pl.BlockSpec((tm, k), lambda i, j: (i, 0)), w_spec, w_spec],
        out_specs=[o_spec, o_spec, o_spec, pl.BlockSpec((tn, tm), lambda i, j: (j, i))],
        out_shape=[jax.ShapeDtypeStruct((t, f), BF16)] * 3 + [jax.ShapeDtypeStruct((f, t), BF16)],
        compiler_params=_params(("parallel", "parallel")),
    )(h, wg, wu)


def _swiglu_bwd_matmul(dy, wo, gg, uu, name, tm):
    t, d = dy.shape
    f = wo.shape[0]
    tm = _tile(t, tm, 16)
    tn = _tile(f, 512, LANES)

    def body(dy_ref, wo_ref, g_ref, u_ref, dg_ref, du_ref):
        da = lax.dot_general(dy_ref[...], wo_ref[...], _NT, preferred_element_type=F32)
        gv = g_ref[...].astype(F32)
        sg = jax.nn.sigmoid(gv)
        dg_ref[...] = (da * u_ref[...].astype(F32) * (sg * (1.0 + gv * (1.0 - sg)))).astype(BF16)
        du_ref[...] = (da * (gv * sg)).astype(BF16)

    blk = pl.BlockSpec((tm, tn), lambda i, j: (i, j))
    return pl.pallas_call(
        body, name=name, grid=(t // tm, f // tn),
        in_specs=[pl.BlockSpec((tm, d), lambda i, j: (i, 0)), pl.BlockSpec((tn, d), lambda i, j: (j, 0)), blk, blk],
        out_specs=[blk, blk], out_shape=[jax.ShapeDtypeStruct((t, f), BF16)] * 2,
        compiler_params=_params(("parallel", "parallel")),
    )(dy, wo, gg, uu)


def _tok_tile(t):
    return 1024 if t % 1024 == 0 else 768 if t % 768 == 0 else _tile(t, 1024, 16)


FF_TILE = 1408
FULL_K = 1 << 30


def _make_ffn(n_ctx, tag):
    @jax.custom_vjp
    def ffn(x, g, shift, scale, gate, wg, wu, wo):
        return fwd(x, g, shift, scale, gate, wg, wu, wo)[0]

    def fwd(x, g, shift, scale, gate, wg, wu, wo):
        tt = _tok_tile(x.shape[0])
        h, ht = _normmod_fwd(x, g, shift, scale, n_ctx, tag + "_norm")
        gg, uu, a, at = _swiglu_matmul(h, wg, wu, tag + "_mm_gu", tt)
        if n_ctx:
            y = _matmul(a, wo, "nn", F32, tag + "_mm_o", tm=tt, tn=512, tk=FULL_K)
            out = _gated_res_fwd(x, y, gate, 0.5, n_ctx, tag + "_res")
        else:
            out, y = _matmul_gated_res(a, wo, x, gate, 0.5, tag + "_mm_o_res", tt)
        return out, (x, g, shift, scale, gate, wg, wu, wo, ht, gg, uu, at, y)

    def bwd(res, dout):
        x, g, shift, scale, gate, wg, wu, wo, ht, gg, uu, at, y = res
        tt = _tok_tile(x.shape[0])
        dy, dgate = _gated_res_bwd(dout, y, gate, 0.5, n_ctx, tag + "_res_b")
        dwo = _matmul(at, dy, "nn", BF16, tag + "_dwo", tm=512, tn=512, tk=FULL_K)
        dgg, duu = _swiglu_bwd_matmul(dy, wo, gg, uu, tag + "_da_act", tt)
        dwg = _matmul(ht, dgg, "nn", BF16, tag + "_dwg", tm=512, tn=512, tk=FULL_K)
        dwu = _matmul(ht, duu, "nn", BF16, tag + "_dwu", tm=512, tn=512, tk=FULL_K)
        dh = _matmul(dgg, wg, "nt", F32, tag + "_dh_g", tm=tt, tn=512, tk=FULL_K)
        dh = _matmul(duu, wu, "nt", F32, tag + "_dh_u", add=dh, tm=tt, tn=512, tk=FULL_K)
        dx, dg, dshift, dscale = _normmod_bwd(dh, x, g, shift, scale, dout, n_ctx, tag + "_norm_b")
        return dx, dg, dshift, dscale, dgate, dwg, dwu, dwo

    ffn.defvjp(fwd, bwd)
    return ffn


def _make_normmod_linear(n_ctx, tag):
    @jax.custom_vjp
    def op(x, g, shift, scale, w):
        return fwd(x, g, shift, scale, w)[0]

    def fwd(x, g, shift, scale, w):
        h, ht = _normmod_fwd(x, g, shift, scale, n_ctx, tag + "_norm")
        y = _matmul(h, w, "nn", F32, tag + "_mm", tm=_tok_tile(x.shape[0]))
        return y, (x, g, shift, scale, w, ht)

    def bwd(res, dy):
        x, g, shift, scale, w, ht = res
        tt = _tok_tile(x.shape[0])
        dw = _matmul(ht, dy, "nn", BF16, tag + "_dw", tk=tt)
        dh = _matmul(dy, w, "nt", F32, tag + "_dh", tm=tt)
        dx, dg, dshift, dscale = _normmod_bwd(dh, x, g, shift, scale, None, n_ctx, tag + "_norm_b")
        return dx, dg, dshift, dscale, dw

    op.defvjp(fwd, bwd)
    return op


def _make_linear_gated_res(tag):
    @jax.custom_vjp
    def op(x, a, w, gate):
        return fwd(x, a, w, gate)[0]

    def fwd(x, a, w, gate):
        ab = a.astype(BF16)
        out, y = _matmul_gated_res(ab, w, x, gate, 1.0, tag + "_mm_res", _tok_tile(x.shape[0]))
        return out, (ab, w, gate, y)

    def bwd(res, dout):
        ab, w, gate, y = res
        tt = _tok_tile(dout.shape[0])
        dy, dgate = _gated_res_bwd(dout, y, gate, 1.0, 0, tag + "_res_b")
        dw = _matmul(ab.T, dy, "nn", BF16, tag + "_dw", tm=512, tn=512, tk=FULL_K)
        da = _matmul(dy, w, "nt", F32, tag + "_da", tm=tt)
        return dout, da, dw, dgate

    op.defvjp(fwd, bwd)
    return op


def _final_loss_call(x, g, target, name):
    t, d = x.shape
    r = _tile(t, 256, SUBLANES)

    def body(x_ref, g_ref, t_ref, loss_ref, dx_ref, dg_ref):
        i = pl.program_id(0)
        xv = x_ref[...]
        gv = g_ref[...]
        rstd = lax.rsqrt(jnp.mean(xv * xv, axis=-1, keepdims=True) + RMS_EPS)
        xh = xv * rstd
        e = xh * gv - t_ref[...]
        dy = e * (1.0 / d)
        dn = dy * gv
        dx_ref[...] = rstd * (dn - xh * jnp.mean(dn * xh, axis=-1, keepdims=True))

        @pl.when(i == 0)
        def _():
            loss_ref[...] = jnp.zeros_like(loss_ref)
            dg_ref[...] = jnp.zeros_like(dg_ref)

        loss_ref[...] += 0.5 * jnp.sum(jnp.mean(e * e, axis=-1, keepdims=True), axis=0, keepdims=True)
        dg_ref[...] += jnp.sum(dy * xh, axis=0, keepdims=True)

    row = pl.BlockSpec((r, d), lambda i: (i, 0))
    vec = pl.BlockSpec((1, d), lambda i: (0, 0))
    return pl.pallas_call(
        body, name=name, grid=(t // r,),
        in_specs=[row, vec, row], out_specs=[pl.BlockSpec((1, 1), lambda i: (0, 0)), row, vec],
        out_shape=[jax.ShapeDtypeStruct((1, 1), F32), jax.ShapeDtypeStruct((t, d), F32), jax.ShapeDtypeStruct((1, d), F32)],
        compiler_params=_params(("arbitrary",)),
    )(x, g, target)


@jax.custom_vjp
def _final_loss(x, g, target):
    return _final_loss_call(x, g, target, "final_loss")[0][0, 0]


def _final_loss_fwd(x, g, target):
    loss, dx, dg = _final_loss_call(x, g, target, "final_loss")
    return loss[0, 0], (dx, dg, target)


def _final_loss_bwd(res, dl):
    dx, dg, target = res
    return dx * dl, dg * dl, jnp.zeros_like(target)


_final_loss.defvjp(_final_loss_fwd, _final_loss_bwd)


_NT = (((1,), (1,)), ((), ()))
_TN = (((0,), (0,)), ((), ()))
ATTN_Q_TILE = 512
ATTN_K_CHUNK = 1408


def _attn_fwd_call(q, k, v, scale):
    h, nq, dq = q.shape
    nk, dv = v.shape[1], v.shape[2]
    tq = _tile(nq, ATTN_Q_TILE, 16)
    ck = _tile(nk, ATTN_K_CHUNK, LANES)
    nchunk = nk // ck

    exp2_scale = scale * float(np.log2(np.e))

    def body(q_ref, k_ref, v_ref, o_ref, lse_ref, m_scr, l_scr, acc_scr):
        qv = q_ref[0]
        m_scr[...] = jnp.full_like(m_scr, -jnp.inf)
        l_scr[...] = jnp.zeros_like(l_scr)
        acc_scr[...] = jnp.zeros_like(acc_scr)

        scores = lambda c: lax.dot_general(qv, k_ref[0, c * ck:(c + 1) * ck, :], _NT, preferred_element_type=F32)
        s_next = scores(0)
        for c in range(nchunk):
            s = s_next
            if c + 1 < nchunk:
                s_next = scores(c + 1)
            m_old = m_scr[...]
            m_new = jnp.maximum(m_old, jnp.max(s, axis=-1, keepdims=True))
            alpha = jnp.exp2((m_old - m_new) * exp2_scale)
            p = jnp.exp2((s - m_new) * exp2_scale)
            l_scr[...] = alpha * l_scr[...] + jnp.sum(p, axis=-1, keepdims=True)
            acc_scr[...] = alpha * acc_scr[...] + jnp.dot(p.astype(BF16), v_ref[0, c * ck:(c + 1) * ck, :], preferred_element_type=F32)
            m_scr[...] = m_new

        o_ref[0] = acc_scr[...] / l_scr[...]
        lse_ref[0] = m_scr[...] * scale + jnp.log(l_scr[...])

    keys = lambda d: pl.BlockSpec((1, nk, d), lambda hh, i: (hh, 0, 0))
    return pl.pallas_call(
        body, name="attn_fwd", grid=(h, nq // tq),
        in_specs=[pl.BlockSpec((1, tq, dq), lambda hh, i: (hh, i, 0)), keys(dq), keys(dv)],
        out_specs=[pl.BlockSpec((1, tq, dv), lambda hh, i: (hh, i, 0)),
                   pl.BlockSpec((1, tq, 1), lambda hh, i: (hh, i, 0))],
        out_shape=[jax.ShapeDtypeStruct((h, nq, dv), F32), jax.ShapeDtypeStruct((h, nq, 1), F32)],
        scratch_shapes=[pltpu.VMEM((tq, 1), F32), pltpu.VMEM((tq, 1), F32), pltpu.VMEM((tq, dv), F32)],
        compiler_params=_params(("parallel", "arbitrary")),
    )(q, k, v)


def _attn_bwd_call(q, k, v, o, do, lse, scale):
    h, nq, dq = q.shape
    nk, dv = v.shape[1], v.shape[2]
    tq = _tile(nq, ATTN_Q_TILE, 16)
    ck = _tile(nk, ATTN_K_CHUNK, LANES)
    nchunk = nk // ck

    log2e = float(np.log2(np.e))
    exp2_scale = scale * log2e

    def body(q_ref, k_ref, v_ref, o_ref, do_ref, lse_ref, dq_ref, dk_ref, dv_ref):
        i = pl.program_id(1)

        @pl.when(i == 0)
        def _():
            dk_ref[...] = jnp.zeros_like(dk_ref)
            dv_ref[...] = jnp.zeros_like(dv_ref)

        qv = q_ref[0]
        dov = do_ref[0]
        dob = dov.astype(BF16)
        delta = jnp.sum(dov * o_ref[0], axis=-1, keepdims=True)
        lse2 = lse_ref[0] * log2e

        def scores(c):
            rows = slice(c * ck, (c + 1) * ck)
            return (lax.dot_general(qv, k_ref[0, rows, :], _NT, preferred_element_type=F32),
                    lax.dot_general(dob, v_ref[0, rows, :], _NT, preferred_element_type=F32))

        nxt = scores(0)
        dq_acc = None
        for c in range(nchunk):
            rows = slice(c * ck, (c + 1) * ck)
            s, dp = nxt
            if c + 1 < nchunk:
                nxt = scores(c + 1)
            p = jnp.exp2(s * exp2_scale - lse2)
            ds = (p * (dp - delta) * scale).astype(BF16)
            dv_ref[0, rows, :] += lax.dot_general(p.astype(BF16), dob, _TN, preferred_element_type=F32)
            dk_ref[0, rows, :] += lax.dot_general(ds, qv, _TN, preferred_element_type=F32)
            part = jnp.dot(ds, k_ref[0, rows, :], preferred_element_type=F32)
            dq_acc = part if dq_acc is None else dq_acc + part
        dq_ref[0] = dq_acc

    qspec = lambda d: pl.BlockSpec((1, tq, d), lambda hh, i: (hh, i, 0))
    kspec = lambda d: pl.BlockSpec((1, nk, d), lambda hh, i: (hh, 0, 0), pipeline_mode=pl.Buffered(1))
    return pl.pallas_call(
        body, name="attn_bwd", grid=(h, nq // tq),
        in_specs=[qspec(dq), kspec(dq), kspec(dv), qspec(dv), qspec(dv), qspec(1)],
        out_specs=[qspec(dq), kspec(dq), kspec(dv)],
        out_shape=[jax.ShapeDtypeStruct((h, nq, dq), F32), jax.ShapeDtypeStruct((h, nk, dq), F32),
                   jax.ShapeDtypeStruct((h, nk, dv), F32)],
        compiler_params=_params(("parallel", "arbitrary")),
    )(q, k, v, o, do, lse)


@jax.custom_vjp
def _attention(q, k, v):
    return _attention_fwd(q, k, v)[0]


def _attention_fwd(q, k, v):
    scale = q.shape[-1] ** -0.5
    qb, kb, vb = q.astype(BF16), k.astype(BF16), v.astype(BF16)
    o, lse = _attn_fwd_call(qb, kb, vb, scale)
    return o, (qb, kb, vb, o, lse)


def _attention_bwd(res, do):
    qb, kb, vb, o, lse = res
    return _attn_bwd_call(qb, kb, vb, o, do, lse, qb.shape[-1] ** -0.5)


_attention.defvjp(_attention_fwd, _attention_bwd)


RET_UNROLL = 4


def _bf(x):
    return x.astype(BF16)


def _dot(a, b, dims=(((1,), (0,)), ((), ()))):
    return lax.dot_general(_bf(a), _bf(b), dims, preferred_element_type=F32)


def _sum_all(x):
    return jnp.sum(jnp.sum(x, axis=1, keepdims=True), axis=0, keepdims=True)


def _ret_consts(lgf_ref, lgb_ref):
    c = RET_CHUNK
    lgf = lgf_ref[0][:, :1]
    lgb = lgb_ref[0][:, :1]
    diff = (lax.broadcasted_iota(jnp.int32, (c, c), 0) - lax.broadcasted_iota(jnp.int32, (c, c), 1)).astype(F32)
    mf = diff >= 0
    dmat = jnp.where(mf, jnp.exp(lgf * jnp.where(mf, diff, 0.0)), jnp.exp(lgb * jnp.where(mf, 0.0, -diff)))
    col = lax.broadcasted_iota(jnp.int32, (c, 1), 0).astype(F32)
    return dict(diff=diff, mf=mf, dmat=dmat, col=col,
                xif=jnp.exp(lgf * (col + 1.0)), zf=jnp.exp(lgf * (c - 1.0 - col)),
                xib=jnp.exp(lgb * (c - col)), zb=jnp.exp(lgb * col),
                gf=jnp.exp(lgf * c), gb=jnp.exp(lgb * c))


def _ret_rows(n):
    return pl.ds(pl.multiple_of(n * RET_CHUNK, RET_CHUNK), RET_CHUNK)


def _ret_fwd_call(q, k, v, lgf, lgb, s0f, s0b):
    h, n_tok, dk = q.shape
    dv = v.shape[-1]
    nc = n_tok // RET_CHUNK

    def body(q_ref, k_ref, v_ref, lgf_ref, lgb_ref, s0f_ref, s0b_ref, y_ref, sff_ref, sbf_ref, sb_scr):
        cs = _ret_consts(lgf_ref, lgb_ref)

        sbf_ref[0] = s0b_ref[0]

        @pl.loop(0, nc, unroll=RET_UNROLL)
        def _(t):
            n = nc - 1 - t
            sb = sbf_ref[0]
            sb_scr[n] = sb
            sbf_ref[0] = cs["gb"] * sb + _dot(k_ref[0, _ret_rows(n), :] * cs["zb"], v_ref[0, _ret_rows(n), :], _TN)

        sff_ref[0] = s0f_ref[0]

        @pl.loop(0, nc, unroll=RET_UNROLL)
        def _(n):
            sf = sff_ref[0]
            qc, kc, vc = q_ref[0, _ret_rows(n), :], k_ref[0, _ret_rows(n), :], v_ref[0, _ret_rows(n), :]
            p = _dot(qc, kc, _NT) * cs["dmat"]
            y_ref[0, _ret_rows(n), :] = _dot(p, vc) + _dot(qc * cs["xif"], sf) + _dot(qc * cs["xib"], sb_scr[n])
            sff_ref[0] = cs["gf"] * sf + _dot(kc * cs["zf"], vc, _TN)

    tok = lambda d: pl.BlockSpec((1, n_tok, d), lambda hh: (hh, 0, 0), pipeline_mode=pl.Buffered(1))
    lg = pl.BlockSpec((1, 1, LANES), lambda hh: (hh, 0, 0))
    st = pl.BlockSpec((1, dk, dv), lambda hh: (hh, 0, 0))
    return pl.pallas_call(
        body, name="ret_fwd_%d" % n_tok, grid=(h,),
        in_specs=[tok(dk), tok(dk), tok(dv), lg, lg, st, st], out_specs=[tok(dv), st, st],
        out_shape=[jax.ShapeDtypeStruct((h, n_tok, dv), F32)] + [jax.ShapeDtypeStruct((h, dk, dv), F32)] * 2,
        scratch_shapes=[pltpu.VMEM((nc, dk, dv), F32)],
        compiler_params=_params(("parallel",)),
    )(q, k, v, lgf, lgb, s0f, s0b)


def _ret_bwd_call(q, k, v, lgf, lgb, s0f, s0b, dy, dsff, dsbf):
    h, n_tok, dk = q.shape
    dv = v.shape[-1]
    nc = n_tok // RET_CHUNK
    c = float(RET_CHUNK)

    def body(q_ref, k_ref, v_ref, lgf_ref, lgb_ref, s0f_ref, s0b_ref, dy_ref, dsff_ref, dsbf_ref,
             dq_ref, dk_ref, dv_ref, dlgf_ref, dlgb_ref, ds0f_ref, ds0b_ref, sb_scr, gf_scr, st_a, st_b):
        cs = _ret_consts(lgf_ref, lgb_ref)

        st_a[...] = s0b_ref[0]
        st_b[...] = dsff_ref[0]

        @pl.loop(0, nc, unroll=RET_UNROLL)
        def _(t):
            n = nc - 1 - t
            sb, gf_next = st_a[...], st_b[...]
            sb_scr[n] = sb
            gf_scr[n] = gf_next
            qc, kc, vc, dyc = (r[0, _ret_rows(n), :] for r in (q_ref, k_ref, v_ref, dy_ref))
            st_a[...] = cs["gb"] * sb + _dot(kc * cs["zb"], vc, _TN)
            st_b[...] = _dot(qc * cs["xif"], dyc, _TN) + cs["gf"] * gf_next

        ds0f_ref[0] = st_b[...]

        st_a[...] = s0f_ref[0]
        st_b[...] = dsbf_ref[0]
        dlgf_ref[...] = jnp.zeros_like(dlgf_ref)
        dlgb_ref[...] = jnp.zeros_like(dlgb_ref)

        @pl.loop(0, nc, unroll=RET_UNROLL)
        def _(n):
            sf, gb_prev = st_a[...], st_b[...]
            sb, gf_next = sb_scr[n], gf_scr[n]
            qc, kc, vc, dyc = (r[0, _ret_rows(n), :] for r in (q_ref, k_ref, v_ref, dy_ref))
            a = _dot(qc, kc, _NT)
            dp = _dot(dyc, vc, _NT)
            da = _bf(dp * cs["dmat"])
            dqf = _dot(dyc, sf, _NT)
            dqb = _dot(dyc, sb, _NT)
            dkf = _dot(vc, gf_next, _NT)
            dkb = _dot(vc, gb_prev, _NT)
            dq_ref[0, _ret_rows(n), :] = _dot(da, kc) + dqf * cs["xif"] + dqb * cs["xib"]
            dk_ref[0, _ret_rows(n), :] = _dot(da, qc, _TN) + dkf * cs["zf"] + dkb * cs["zb"]
            dv_ref[0, _ret_rows(n), :] = (_dot(a * cs["dmat"], dyc, _TN) + _dot(kc * cs["zf"], gf_next)
                                         + _dot(kc * cs["zb"], gb_prev))
            w = dp * a * cs["dmat"] * cs["diff"]
            row = lambda x: jnp.sum(x, axis=1, keepdims=True)
            dlgf_ref[0] += (_sum_all(jnp.where(cs["mf"], w, 0.0))
                            + _sum_all((cs["col"] + 1.0) * cs["xif"] * row(dqf * qc) + (c - 1.0 - cs["col"]) * cs["zf"] * row(dkf * kc))
                            + c * cs["gf"] * _sum_all(gf_next * sf))
            dlgb_ref[0] += (_sum_all((c - cs["col"]) * cs["xib"] * row(dqb * qc) + cs["col"] * cs["zb"] * row(dkb * kc))
                            + c * cs["gb"] * _sum_all(gb_prev * sb) - _sum_all(jnp.where(cs["mf"], 0.0, w)))
            st_a[...] = cs["gf"] * sf + _dot(kc * cs["zf"], vc, _TN)
            st_b[...] = _dot(qc * cs["xib"], dyc, _TN) + cs["gb"] * gb_prev

        ds0b_ref[0] = st_b[...]

    tok = lambda d: pl.BlockSpec((1, n_tok, d), lambda hh: (hh, 0, 0), pipeline_mode=pl.Buffered(1))
    lg = pl.BlockSpec((1, 1, LANES), lambda hh: (hh, 0, 0))
    st = pl.BlockSpec((1, dk, dv), lambda hh: (hh, 0, 0))
    return pl.pallas_call(
        body, name="ret_bwd_%d" % n_tok, grid=(h,),
        in_specs=[tok(dk), tok(dk), tok(dv), lg, lg, st, st, tok(dv), st, st],
        out_specs=[tok(dk), tok(dk), tok(dv), lg, lg, st, st],
        out_shape=[jax.ShapeDtypeStruct((h, n_tok, dk), F32)] * 2 + [jax.ShapeDtypeStruct((h, n_tok, dv), F32)]
        + [jax.ShapeDtypeStruct((h, 1, LANES), F32)] * 2 + [jax.ShapeDtypeStruct((h, dk, dv), F32)] * 2,
        scratch_shapes=[pltpu.VMEM((nc, dk, dv), F32), pltpu.VMEM((nc, dk, dv), F32), pltpu.VMEM((dk, dv), F32), pltpu.VMEM((dk, dv), F32)],
        compiler_params=_params(("parallel",)),
    )(q, k, v, lgf, lgb, s0f, s0b, dy, dsff, dsbf)


def _lane_bcast(lg):
    return jnp.broadcast_to(lg[:, None, None], (lg.shape[0], 1, LANES))


@jax.custom_vjp
def _retention(q, k, v, lgf, lgb, s0f, s0b):
    return tuple(_ret_fwd_call(q, k, v, _lane_bcast(lgf), _lane_bcast(lgb), s0f, s0b))


def _retention_fwd(q, k, v, lgf, lgb, s0f, s0b):
    return _retention(q, k, v, lgf, lgb, s0f, s0b), (q, k, v, lgf, lgb, s0f, s0b)


def _retention_bwd(res, cts):
    q, k, v, lgf, lgb, s0f, s0b = res
    dy, dsff, dsbf = cts
    dq, dk, dv, dlgf, dlgb, ds0f, ds0b = _ret_bwd_call(q, k, v, _lane_bcast(lgf), _lane_bcast(lgb), s0f, s0b, dy, dsff, dsbf)
    return dq, dk, dv, dlgf[:, 0, 0], dlgb[:, 0, 0], ds0f, ds0b


_retention.defvjp(_retention_fwd, _retention_bwd)


def _gn_specs(y):
    h, n, dv = y.shape
    r = _tile(n, 512, SUBLANES)
    return (h, n, dv, r, pl.BlockSpec((1, r, dv), lambda i, hh: (hh, i, 0)), pl.BlockSpec((r, dv), lambda i, hh: (i, hh)))


def _gn_norm(yv):
    mu = jnp.mean(yv, axis=-1, keepdims=True)
    yc = yv - mu
    rstd = lax.rsqrt(jnp.mean(yc * yc, axis=-1, keepdims=True) + GN_EPS)
    return yc * rstd, rstd


def _gn_gate_fwd_call(y, gate):
    h, n, dv, r, yspec, gspec = _gn_specs(y)

    def body(y_ref, g_ref, o_ref):
        gv = g_ref[...]
        o_ref[...] = gv * jax.nn.sigmoid(gv) * _gn_norm(y_ref[0])[0]

    return pl.pallas_call(
        body, name="gn_gate", grid=(n // r, h), in_specs=[yspec, gspec], out_specs=gspec,
        out_shape=jax.ShapeDtypeStruct((n, h * dv), F32), compiler_params=_params(("parallel", "parallel")),
    )(y, gate)


def _gn_gate_bwd_call(y, gate, dout):
    h, n, dv, r, yspec, gspec = _gn_specs(y)

    def body(y_ref, g_ref, do_ref, dy_ref, dg_ref):
        gv = g_ref[...]
        dov = do_ref[...]
        yn, rstd = _gn_norm(y_ref[0])
        sg = jax.nn.sigmoid(gv)
        dg_ref[...] = dov * yn * (sg * (1.0 + gv * (1.0 - sg)))
        dyn = dov * (gv * sg)
        dy_ref[0] = rstd * (dyn - jnp.mean(dyn, axis=-1, keepdims=True) - yn * jnp.mean(dyn * yn, axis=-1, keepdims=True))

    return pl.pallas_call(
        body, name="gn_gate_b", grid=(n // r, h), in_specs=[yspec, gspec, gspec], out_specs=[yspec, gspec],
        out_shape=[jax.ShapeDtypeStruct((h, n, dv), F32), jax.ShapeDtypeStruct((n, h * dv), F32)],
        compiler_params=_params(("parallel", "parallel")),
    )(y, gate, dout)


@jax.custom_vjp
def _gn_gate(y, gate):
    return _gn_gate_fwd_call(y, gate)


_gn_gate.defvjp(lambda y, gate: (_gn_gate_fwd_call(y, gate), (y, gate)),
                lambda res, dout: tuple(_gn_gate_bwd_call(res[0], res[1], dout)))


def _rope_tables(pos, dim, base):
    inv = base ** (-jnp.arange(0, dim, 2, dtype=F32) / dim)
    ang = pos.astype(F32)[:, None] * inv[None, :]
    return jnp.cos(ang)[:, None, :], jnp.sin(ang)[:, None, :]


def _rotate(x, cos, sin):
    x1, x2 = jnp.split(x, 2, axis=-1)
    return jnp.concatenate([x1 * cos - x2 * sin, x2 * cos + x1 * sin], axis=-1)


def _axial_rope(x, row_tab, col_tab):
    xr, xc = jnp.split(x, 2, axis=-1)
    return jnp.concatenate([_rotate(xr, *row_tab), _rotate(xc, *col_tab)], axis=-1)


def _heads(t):
    return jnp.swapaxes(t, 0, 1)


def _local_loss(x, mods_lat, mods_ctx, small, big, ctx, target):
    n_lat, d = x.shape
    n_ctx = ctx.shape[0]
    both = lambda i: jnp.stack([mods_ctx[i], mods_lat[i]])[:, None, :]
    lat = lambda i: mods_lat[i][None, None, :]

    xs = jnp.concatenate([ctx, x], axis=0)
    x1 = _make_ffn(n_ctx, "ffn1")(xs, small["norm1_g"], both(0), both(1), both(2),
                                  big["ffn1_wg"], big["ffn1_wu"], big["ffn1_wo"])
    proj = _make_normmod_linear(n_ctx, "mix_in")(x1, small["norm2_g"], both(3), both(4), big["mix_in"])
    offs = np.cumsum((0,) + MIX_SPLITS)
    part = lambda i, rows: proj[rows, offs[i]:offs[i + 1]]
    lat_rows, ctx_rows = slice(n_ctx, None), slice(0, n_ctx)

    zq = jnp.zeros((1, 1, MLA_Q_RANK), F32)
    zkv = jnp.zeros((1, 1, MLA_KV_RANK), F32)
    q = _make_normmod_linear(0, "mla_q")(part(4, lat_rows), small["mla_q_norm_g"], zq, zq, big["w_uq"])
    kv = _make_normmod_linear(0, "mla_kv")(part(5, slice(None)), small["mla_kv_norm_g"], zkv, zkv, big["w_ukv"])

    lgf = jax.nn.log_sigmoid(small["ret_decay_fwd"][0])
    lgb = jax.nn.log_sigmoid(small["ret_decay_bwd"][0])
    ret_tab = _rope_tables(jnp.arange(n_lat), RET_DK, ROPE_BASE)
    hd = lambda t, dd: t.reshape(t.shape[0], RET_HEADS, dd)
    s_zero = jnp.zeros((RET_HEADS, RET_DK, RET_DV), F32)
    _, s_f, s_b = _retention(_heads(hd(part(0, ctx_rows), RET_DK)), _heads(hd(part(1, ctx_rows), RET_DK) * (RET_DK ** -0.5)),
                             _heads(hd(part(2, ctx_rows), RET_DV)), lgf, lgb, s_zero, s_zero)
    rq = _rotate(hd(part(0, lat_rows), RET_DK), *ret_tab)
    rk = _rotate(hd(part(1, lat_rows), RET_DK) * (RET_DK ** -0.5), *ret_tab)
    y_lat, _, _ = _retention(_heads(rq), _heads(rk), _heads(hd(part(2, lat_rows), RET_DV)), lgf, lgb, s_f, s_b)
    ret_out = _gn_gate(y_lat, part(3, lat_rows))

    pos = jnp.arange(n_lat)
    row_tab = _rope_tables(pos // GRID_W, MLA_ROPE // 2, ROPE_BASE)
    col_tab = _rope_tables(pos % GRID_W, MLA_ROPE // 2, ROPE_BASE)
    q = q.reshape(n_lat, MLA_HEADS, MLA_NOPE + MLA_ROPE)
    q_all = jnp.concatenate([q[..., :MLA_NOPE], _axial_rope(q[..., MLA_NOPE:], row_tab, col_tab)], axis=-1)
    kv = kv.reshape(n_ctx + n_lat, MLA_HEADS, MLA_NOPE + MLA_V)
    kr_lat = _axial_rope(part(6, lat_rows)[:, None, :], row_tab, col_tab)
    kr = jnp.concatenate([kr_lat, part(6, ctx_rows)[:, None, :]], axis=0)
    kv_lat_first = jnp.concatenate([kv[n_ctx:], kv[:n_ctx]], axis=0)
    k_all = jnp.concatenate([kv_lat_first[..., :MLA_NOPE], jnp.broadcast_to(kr, (n_ctx + n_lat, MLA_HEADS, MLA_ROPE))], axis=-1)
    mla = _attention(_heads(q_all), _heads(k_all), _heads(kv_lat_first[..., MLA_NOPE:]))
    mla_out = _heads(mla).reshape(n_lat, MLA_HEADS * MLA_V)

    x2 = _make_linear_gated_res("mix_out")(x1[n_ctx:], jnp.concatenate([ret_out, mla_out], axis=-1), big["mix_out"], lat(5))
    x3 = _make_ffn(0, "ffn2")(x2, small["norm3_g"], lat(6), lat(7), lat(8), big["ffn2_wg"], big["ffn2_wu"], big["ffn2_wo"])
    return _final_loss(x3, small["final_norm_g"][None, :], target)


HBM_SPEC = pl.BlockSpec(memory_space=pl.ANY)
VMEM_SPEC = pl.BlockSpec(memory_space=pltpu.VMEM)
ALL_PEERS = (1, 2, 3, 4, 5, 6, 7)
CHIP_PEERS = (4, 2, 6)


def _me():
    return lax.axis_index("x"), lax.axis_index("y"), lax.axis_index("c")


def _flip(pos, mask):
    x, y, c = pos
    return (1 - x if mask & 4 else x, 1 - y if mask & 2 else y, 1 - c if mask & 1 else c)


def _allgather_small(block, masks, chips_only, name):
    r, c = block.shape
    n_slots = 4 if chips_only else 8

    def body(x_ref, out_ref, send_sems, recv_sems, local_sem):
        pos = _me()
        slot = 2 * pos[0] + pos[1] if chips_only else 4 * pos[0] + 2 * pos[1] + pos[2]
        local = pltpu.make_async_copy(x_ref, out_ref.at[slot], local_sem)
        local.start()
        copies = [pltpu.make_async_remote_copy(src_ref=x_ref, dst_ref=out_ref.at[slot], send_sem=send_sems.at[j], recv_sem=recv_sems.at[j],
                                               device_id=_flip(pos, mask), device_id_type=MESH) for j, mask in enumerate(masks)]
        for cp in copies:
            cp.start()
        for cp in copies:
            cp.wait()
        local.wait()

    return pl.pallas_call(
        body, name=name, in_specs=[VMEM_SPEC], out_specs=VMEM_SPEC,
        out_shape=jax.ShapeDtypeStruct((n_slots, r, c), block.dtype),
        scratch_shapes=[pltpu.SemaphoreType.DMA((len(masks),)), pltpu.SemaphoreType.DMA((len(masks),)), pltpu.SemaphoreType.DMA],
        compiler_params=pltpu.CompilerParams(vmem_limit_bytes=VMEM_LIMIT_BYTES),
    )(block)


def _gather_weights(shards):
    n = len(shards)

    def body(*refs):
        ins, outs = refs[:n], refs[n:2 * n]
        send_sems, recv_sems, pass_send, pass_recv = refs[2 * n:]
        pos = _me()
        x, y, c = pos
        sibling = _flip(pos, 1)
        sends, passes = [], []
        for w in range(n):
            half = shards[w].shape[0] // 2
            mine = pl.ds(c * half, half)
            for j, mask in enumerate(CHIP_PEERS):
                cp = pltpu.make_async_remote_copy(src_ref=ins[w].at[mine], dst_ref=outs[w].at[2 * x + y, mine], send_sem=send_sems.at[3 * w + j],
                                                  recv_sem=recv_sems.at[3 * w + j], device_id=_flip(pos, mask), device_id_type=MESH)
                cp.start()
                sends.append(cp)
        for w in range(n):
            half = shards[w].shape[0] // 2
            mine = pl.ds(c * half, half)
            for j, mask in enumerate(CHIP_PEERS):
                px, py, _ = _flip(pos, mask)
                landed = outs[w].at[2 * px + py, mine]
                pltpu.make_async_remote_copy(src_ref=landed, dst_ref=landed, send_sem=send_sems.at[3 * w + j], recv_sem=recv_sems.at[3 * w + j],
                                             device_id=_flip(pos, mask), device_id_type=MESH).wait_recv()
                cp = pltpu.make_async_remote_copy(src_ref=landed, dst_ref=landed, send_sem=pass_send.at[3 * w + j], recv_sem=pass_recv.at[3 * w + j],
                                                  device_id=sibling, device_id_type=MESH)
                cp.start()
                passes.append(cp)
        for w in range(n):
            half = shards[w].shape[0] // 2
            theirs = pl.ds((1 - c) * half, half)
            for j, mask in enumerate(CHIP_PEERS):
                px, py, _ = _flip(pos, mask)
                slab = outs[w].at[2 * px + py, theirs]
                pltpu.make_async_remote_copy(src_ref=slab, dst_ref=slab, send_sem=pass_send.at[3 * w + j], recv_sem=pass_recv.at[3 * w + j],
                                             device_id=sibling, device_id_type=MESH).wait_recv()
        for cp in sends + passes:
            cp.wait_send()

    dma = lambda k: pltpu.SemaphoreType.DMA((k,))
    stacked = pl.pallas_call(
        body, name="gather_weights", in_specs=[HBM_SPEC] * n, out_specs=[HBM_SPEC] * n,
        out_shape=[jax.ShapeDtypeStruct((4,) + s.shape, s.dtype) for s in shards],
        scratch_shapes=[dma(3 * n), dma(3 * n), dma(3 * n), dma(3 * n)],
    )(*shards)
    chip = 2 * lax.axis_index("x") + lax.axis_index("y")
    return [lax.dynamic_update_slice_in_dim(st, sh[None], chip, axis=0) for st, sh in zip(stacked, shards)]


def _pair_swap_halves(grads):
    n = len(grads)

    def body(*refs):
        ins, outs = refs[:n], refs[n:2 * n]
        send_sems, recv_sems = refs[2 * n:]
        pos = _me()
        copies = []
        for w in range(n):
            half = grads[w].shape[1] // 2
            cp = pltpu.make_async_remote_copy(src_ref=ins[w].at[:, pl.ds((1 - pos[2]) * half, half), :], dst_ref=outs[w], send_sem=send_sems.at[w],
                                              recv_sem=recv_sems.at[w], device_id=_flip(pos, 1), device_id_type=MESH)
            cp.start()
            copies.append(cp)
        for cp in copies:
            cp.wait()

    return pl.pallas_call(
        body, name="pair_swap_halves", in_specs=[HBM_SPEC] * n, out_specs=[HBM_SPEC] * n,
        out_shape=[jax.ShapeDtypeStruct((4, g.shape[1] // 2, g.shape[2]), g.dtype) for g in grads],
        scratch_shapes=[pltpu.SemaphoreType.DMA((n,)), pltpu.SemaphoreType.DMA((n,))],
    )(*grads)


def _chip_scatter(parts):
    n = len(parts)

    def body(*refs):
        ins, outs = refs[:n], refs[n:2 * n]
        send_sems, recv_sems, local_sems = refs[2 * n:]
        pos = _me()
        me = 2 * pos[0] + pos[1]
        copies, local = [], []
        for w in range(n):
            cp = pltpu.make_async_copy(ins[w].at[me], outs[w].at[me], local_sems.at[w])
            cp.start()
            local.append(cp)
            for j, mask in enumerate(CHIP_PEERS):
                px, py, _ = _flip(pos, mask)
                cp = pltpu.make_async_remote_copy(src_ref=ins[w].at[2 * px + py], dst_ref=outs[w].at[me], send_sem=send_sems.at[3 * w + j],
                                                  recv_sem=recv_sems.at[3 * w + j], device_id=_flip(pos, mask), device_id_type=MESH)
                cp.start()
                copies.append(cp)
        for cp in copies:
            cp.wait()
        for cp in local:
            cp.wait()

    dma = lambda k: pltpu.SemaphoreType.DMA((k,))
    return pl.pallas_call(
        body, name="chip_scatter", in_specs=[HBM_SPEC] * n, out_specs=[HBM_SPEC] * n,
        out_shape=[jax.ShapeDtypeStruct(p.shape, p.dtype) for p in parts],
        scratch_shapes=[dma(3 * n), dma(3 * n), dma(n)],
    )(*parts)


def _pair_swap_reduced(halves):
    n = len(halves)

    def body(*refs):
        ins, outs = refs[:n], refs[n:2 * n]
        send_sems, recv_sems = refs[2 * n:]
        pos = _me()
        copies = []
        for w in range(n):
            cp = pltpu.make_async_remote_copy(src_ref=ins[w], dst_ref=outs[w], send_sem=send_sems.at[w], recv_sem=recv_sems.at[w],
                                              device_id=_flip(pos, 1), device_id_type=MESH)
            cp.start()
            copies.append(cp)
        for cp in copies:
            cp.wait()

    dma = lambda k: pltpu.SemaphoreType.DMA((k,))
    return pl.pallas_call(
        body, name="pair_swap_reduced", in_specs=[HBM_SPEC] * n, out_specs=[HBM_SPEC] * n,
        out_shape=[jax.ShapeDtypeStruct(h.shape, h.dtype) for h in halves],
        scratch_shapes=[dma(n), dma(n)],
    )(*halves)


def _add_pair(mine, theirs, name):
    s, h, c = mine.shape
    r = _tile(h, max(16, (1 << 19) // c), 16)

    def body(a_ref, b_ref, o_ref):
        o_ref[...] = (a_ref[...].astype(F32) + b_ref[...].astype(F32)).astype(BF16)

    blk = pl.BlockSpec((1, r, c), lambda i, j: (i, j, 0))
    return pl.pallas_call(
        body, name=name, grid=(s, h // r), in_specs=[blk, blk], out_specs=blk,
        out_shape=jax.ShapeDtypeStruct(mine.shape, BF16), compiler_params=_params(("parallel", "parallel")),
    )(mine, theirs)


def _sum_slots(parts, name):
    s, h, c = parts.shape
    r = _tile(h, max(16, (1 << 18) // c), 16)

    def body(p_ref, o_ref):
        acc = p_ref[0].astype(F32)
        for k in range(1, s):
            acc = acc + p_ref[k].astype(F32)
        o_ref[...] = acc

    return pl.pallas_call(
        body, name=name, grid=(h // r,), in_specs=[pl.BlockSpec((s, r, c), lambda i: (0, i, 0))],
        out_specs=pl.BlockSpec((r, c), lambda i: (i, 0)),
        out_shape=jax.ShapeDtypeStruct((h, c), F32), compiler_params=_params(("parallel",)),
    )(parts)


def _reduce_scatter_grads(stacked):
    c = lax.axis_index("c")
    theirs = _pair_swap_halves(stacked)
    parts = []
    for w, (g, t) in enumerate(zip(stacked, theirs)):
        half = g.shape[1] // 2
        mine = lax.dynamic_slice_in_dim(g, c * half, half, axis=1)
        parts.append(_add_pair(mine, t, "rs_add_pair_%d" % w))
    landed = _chip_scatter(parts)
    halves = [_sum_slots(p, "rs_sum_slots_%d" % w) for w, p in enumerate(landed)]
    return list(zip(halves, _pair_swap_reduced(halves)))


def _adamw_math(w, g, m, v):
    m = ADAM_B1 * m + (1.0 - ADAM_B1) * g
    v = ADAM_B2 * v + (1.0 - ADAM_B2) * (g * g)
    m_hat = m / (1.0 - ADAM_B1 ** ADAM_STEP)
    v_hat = v / (1.0 - ADAM_B2 ** ADAM_STEP)
    return -ADAM_LR * (m_hat / (jnp.sqrt(v_hat) + ADAM_EPS) + ADAM_WD * w), m, v


def _adamw(w, g, m, v, name):
    rows, cols = w.shape
    r = _tile(rows, max(SUBLANES, (1 << 18) // cols), SUBLANES)

    def body(w_ref, g_ref, m_ref, v_ref, d_ref, mo_ref, vo_ref):
        d_ref[...], mo_ref[...], vo_ref[...] = _adamw_math(w_ref[...], g_ref[...], m_ref[...], v_ref[...])

    blk = pl.BlockSpec((r, cols), lambda i: (i, 0))
    return pl.pallas_call(
        body, name=name, grid=(rows // r,), in_specs=[blk] * 4, out_specs=[blk] * 3,
        out_shape=[jax.ShapeDtypeStruct(w.shape, F32)] * 3, compiler_params=_params(("parallel",)),
    )(w, g, m, v)


def _adamw_halves(w, g_mine, g_theirs, m, v, core, name):
    rows, cols = w.shape
    half = rows // 2
    r = _tile(half, max(SUBLANES, (1 << 18) // cols), SUBLANES)
    nbh = half // r

    def body(core_ref, w_ref, gm_ref, gt_ref, m_ref, v_ref, g_ref, d_ref, mo_ref, vo_ref):
        is_mine = (pl.program_id(0) // nbh) == core_ref[0]

        @pl.when(is_mine)
        def _():
            g_ref[...] = gm_ref[...]

        @pl.when(jnp.logical_not(is_mine))
        def _():
            g_ref[...] = gt_ref[...]

        g = g_ref[...]
        d_ref[...], mo_ref[...], vo_ref[...] = _adamw_math(w_ref[...], g, m_ref[...], v_ref[...])

    full = pl.BlockSpec((r, cols), lambda i, core_ref: (i, 0))
    part = pl.BlockSpec((r, cols), lambda i, core_ref: (i % nbh, 0))
    return pl.pallas_call(
        body, name=name,
        grid_spec=pltpu.PrefetchScalarGridSpec(num_scalar_prefetch=1, grid=(rows // r,), in_specs=[full, part, part, full, full],
                                               out_specs=[full] * 4),
        out_shape=[jax.ShapeDtypeStruct(w.shape, F32)] * 4, compiler_params=_params(("parallel",)),
    )(core, w, g_mine, g_theirs, m, v)


def _adamw_reduced(parts, w, m, v, name):
    def body(p_ref, w_ref, m_ref, v_ref, g_ref, d_ref, mo_ref, vo_ref):
        g = p_ref[0]
        for k in range(1, parts.shape[0]):
            g = g + p_ref[k]
        g_ref[...] = g
        d_ref[...], mo_ref[...], vo_ref[...] = _adamw_math(w_ref[...], g, m_ref[...], v_ref[...])

    return pl.pallas_call(
        body, name=name, in_specs=[VMEM_SPEC] * 4, out_specs=[VMEM_SPEC] * 4,
        out_shape=[jax.ShapeDtypeStruct(w.shape, F32)] * 4,
        compiler_params=pltpu.CompilerParams(vmem_limit_bytes=VMEM_LIMIT_BYTES),
    )(parts, w, m, v)


WEIGHTS = ("c_ctx", "ada_w", "ada_b", "norm1_g", "ffn1_w_in", "ffn1_w_out", "norm2_g", "mix_w_in", "ret_decay_fwd", "ret_decay_bwd",
           "mla_q_norm_g", "mla_w_uq", "mla_kv_norm_g", "mla_w_ukv", "mix_w_out", "norm3_g", "ffn2_w_in", "ffn2_w_out", "final_norm_g")
SMALL = ("c_ctx", "ada_b", "norm1_g", "norm2_g", "ret_decay_fwd", "ret_decay_bwd", "mla_q_norm_g", "mla_kv_norm_g", "norm3_g", "final_norm_g")
BIG = (("ffn1_w_in", 1), ("ffn1_w_out", 0), ("mix_w_in", 1), ("mla_w_uq", 1), ("mla_w_ukv", 1), ("mix_w_out", 0), ("ffn2_w_in", 1), ("ffn2_w_out", 0))


def _pack(vectors):
    flat = jnp.concatenate([v.reshape(-1) for v in vectors])
    return jnp.pad(flat, (0, -flat.shape[0] % (SUBLANES * LANES))).reshape(SUBLANES, -1)


def _rows8(a):
    return a.reshape(a.shape[0] * SUBLANES, a.shape[1] // SUBLANES)


def _unpack(packed, like):
    packed = packed.reshape(-1)
    out, off = [], 0
    for ref in like:
        out.append(packed[off:off + ref.size].reshape(ref.shape))
        off += ref.size
    return out


def kernel(x, c, ctx, c_ctx, ada_w, ada_b, norm1_g, ffn1_w_in, ffn1_w_out, norm2_g, mix_w_in, ret_decay_fwd, ret_decay_bwd, mla_q_norm_g, mla_w_uq, mla_kv_norm_g, mla_w_ukv, mix_w_out, norm3_g, ffn2_w_in, ffn2_w_out, final_norm_g, loss_target, m_c_ctx, m_ada_w, m_ada_b, m_norm1_g, m_ffn1_w_in, m_ffn1_w_out, m_norm2_g, m_mix_w_in, m_ret_decay_fwd, m_ret_decay_bwd, m_mla_q_norm_g, m_mla_w_uq, m_mla_kv_norm_g, m_mla_w_ukv, m_mix_w_out, m_norm3_g, m_ffn2_w_in, m_ffn2_w_out, m_final_norm_g, v_c_ctx, v_ada_w, v_ada_b, v_norm1_g, v_ffn1_w_in, v_ffn1_w_out, v_norm2_g, v_mix_w_in, v_ret_decay_fwd, v_ret_decay_bwd, v_mla_q_norm_g, v_mla_w_uq, v_mla_kv_norm_g, v_mla_w_ukv, v_mix_w_out, v_norm3_g, v_ffn2_w_in, v_ffn2_w_out, v_final_norm_g):
    w = dict(c_ctx=c_ctx, ada_w=ada_w, ada_b=ada_b, norm1_g=norm1_g, ffn1_w_in=ffn1_w_in, ffn1_w_out=ffn1_w_out, norm2_g=norm2_g,
             mix_w_in=mix_w_in, ret_decay_fwd=ret_decay_fwd, ret_decay_bwd=ret_decay_bwd, mla_q_norm_g=mla_q_norm_g, mla_w_uq=mla_w_uq,
             mla_kv_norm_g=mla_kv_norm_g, mla_w_ukv=mla_w_ukv, mix_w_out=mix_w_out, norm3_g=norm3_g, ffn2_w_in=ffn2_w_in,
             ffn2_w_out=ffn2_w_out, final_norm_g=final_norm_g)
    mom_m = dict(zip(WEIGHTS, (m_c_ctx, m_ada_w, m_ada_b, m_norm1_g, m_ffn1_w_in, m_ffn1_w_out, m_norm2_g, m_mix_w_in, m_ret_decay_fwd,
                               m_ret_decay_bwd, m_mla_q_norm_g, m_mla_w_uq, m_mla_kv_norm_g, m_mla_w_ukv, m_mix_w_out, m_norm3_g,
                               m_ffn2_w_in, m_ffn2_w_out, m_final_norm_g)))
    mom_v = dict(zip(WEIGHTS, (v_c_ctx, v_ada_w, v_ada_b, v_norm1_g, v_ffn1_w_in, v_ffn1_w_out, v_norm2_g, v_mix_w_in, v_ret_decay_fwd,
                               v_ret_decay_bwd, v_mla_q_norm_g, v_mla_w_uq, v_mla_kv_norm_g, v_mla_w_ukv, v_mix_w_out, v_norm3_g,
                               v_ffn2_w_in, v_ffn2_w_out, v_final_norm_g)))
    xi, yi, ci = _me()
    chip = 2 * xi + yi
    example = 2 * chip + ci
    d = x.shape[-1]
    n_mod = ada_b.shape[-1] // d

    c_all = _allgather_small(_rows8(c), ALL_PEERS, False, "gather_c").reshape(8, d)
    cond = jnp.concatenate([c_all, jnp.broadcast_to(c_ctx[None, :], (8, d))], axis=0)
    cond_act = jax.nn.silu(cond)
    n_cols = ada_w.shape[-1]
    bias = lax.dynamic_slice_in_dim(ada_b, chip * n_cols, n_cols, axis=1)
    mods_cols = _matmul(cond_act, ada_w[0], "nn", F32, "ada_fwd", add=jnp.broadcast_to(bias, (16, n_cols)))
    mods = jnp.swapaxes(_allgather_small(mods_cols, CHIP_PEERS, True, "gather_mods"), 0, 1).reshape(16, 4 * n_cols)
    mods_lat = lax.dynamic_slice_in_dim(mods, example, 1, axis=0).reshape(n_mod, d)
    mods_ctx = mods[8].reshape(n_mod, d)

    stacked = _gather_weights([w[name][0].astype(BF16) for name, _ in BIG])
    st = dict(zip([name for name, _ in BIG], stacked))
    rows = lambda s: s.reshape(-1, s.shape[-1])
    cols = lambda s, slots: jnp.concatenate([s[k] for k in slots], axis=1)
    big = dict(ffn1_wg=cols(st["ffn1_w_in"], (0, 1)), ffn1_wu=cols(st["ffn1_w_in"], (2, 3)), ffn1_wo=rows(st["ffn1_w_out"]),
               mix_in=jnp.pad(cols(st["mix_w_in"], (0, 1, 2, 3)), ((0, 0), (0, MIX_IN_PAD - MIX_IN))),
               w_uq=cols(st["mla_w_uq"], (0, 1, 2, 3)), w_ukv=cols(st["mla_w_ukv"], (0, 1, 2, 3)), mix_out=rows(st["mix_w_out"]),
               ffn2_wg=cols(st["ffn2_w_in"], (0, 1)), ffn2_wu=cols(st["ffn2_w_in"], (2, 3)), ffn2_wo=rows(st["ffn2_w_out"]))
    small = {k: w[k] for k in ("norm1_g", "norm2_g", "norm3_g", "final_norm_g", "mla_q_norm_g", "mla_kv_norm_g", "ret_decay_fwd", "ret_decay_bwd")}

    loss_mine, (dx, dmods_lat, dmods_ctx, dsmall, dbig) = jax.value_and_grad(_local_loss, argnums=(0, 1, 2, 3, 4))(
        x[0], mods_lat, mods_ctx, small, big, ctx[0], loss_target[0])

    dmods = _allgather_small(_rows8(jnp.stack([dmods_lat.reshape(-1), dmods_ctx.reshape(-1)])), ALL_PEERS, False, "gather_dmods")
    dmods = dmods.reshape(8, 2, n_mod * d)
    dmods_rows = jnp.concatenate([dmods[:, 0, :], dmods[:, 1, :]], axis=0)
    dmods_cols = lax.dynamic_slice_in_dim(dmods_rows, chip * n_cols, n_cols, axis=1)
    g_ada_w = _matmul(cond_act, dmods_cols, "tn", F32, "ada_dw")
    dcond_act = _matmul(dmods_cols, ada_w[0], "nt", F32, "ada_dcond")
    sig = jax.nn.sigmoid(c_ctx)
    dc_ctx = jnp.sum(dcond_act[8:], axis=0) * (sig * (1.0 + c_ctx * (1.0 - sig)))
    share = dict(dsmall)
    share["c_ctx"] = jnp.where(ci == 0, dc_ctx, jnp.zeros_like(dc_ctx))
    share["ada_b"] = (dmods_lat + dmods_ctx).reshape(1, -1)
    zero = jnp.zeros((1,), F32)
    parts = _allgather_small(_pack([share[k] for k in SMALL] + [loss_mine.reshape(1)]), ALL_PEERS, False, "gather_small_grads")
    packed = _adamw_reduced(parts, _pack([w[k] for k in SMALL] + [zero]), _pack([mom_m[k] for k in SMALL] + [zero]),
                            _pack([mom_v[k] for k in SMALL] + [zero]), "adamw_small")
    like = [w[k] for k in SMALL] + [zero]
    grads, deltas, new_m, new_v = ({k: a for k, a in zip(SMALL + ("loss",), _unpack(p, like))} for p in packed)
    loss = grads.pop("loss").reshape(())

    def col_slots(parts, n):
        return jnp.stack([p[:, k * (p.shape[1] // n):(k + 1) * (p.shape[1] // n)] for p in parts for k in range(n)])

    row_slots = lambda g: g.reshape(4, g.shape[0] // 4, g.shape[1])
    dstacked = [col_slots([dbig["ffn1_wg"], dbig["ffn1_wu"]], 2), row_slots(dbig["ffn1_wo"]), col_slots([dbig["mix_in"][:, :MIX_IN]], 4),
                col_slots([dbig["w_uq"]], 4), col_slots([dbig["w_ukv"]], 4), row_slots(dbig["mix_out"]),
                col_slots([dbig["ffn2_wg"], dbig["ffn2_wu"]], 2), row_slots(dbig["ffn2_wo"])]
    core = ci.astype(jnp.int32).reshape(1)
    for (name, _), (g_mine, g_theirs) in zip(BIG, _reduce_scatter_grads(dstacked)):
        g, dl, mo, vo = _adamw_halves(w[name][0], g_mine, g_theirs, mom_m[name][0], mom_v[name][0], core, "adamw_" + name)
        grads[name], deltas[name], new_m[name], new_v[name] = g[None], dl[None], mo[None], vo[None]
    dl, mo, vo = _adamw(ada_w[0], g_ada_w, m_ada_w[0], v_ada_w[0], "adamw_ada_w")
    grads["ada_w"], deltas["ada_w"], new_m["ada_w"], new_v["ada_w"] = g_ada_w[None], dl[None], mo[None], vo[None]

    return (loss, dx[None], *[grads[k] for k in WEIGHTS], *[deltas[k] for k in WEIGHTS], *[new_m[k] for k in WEIGHTS],
            *[new_v[k] for k in WEIGHTS])
```

```python
import functools

import jax
import jax.numpy as jnp
import numpy as np
from jax import lax
from jax.experimental import pallas as pl
from jax.experimental.pallas import tpu as pltpu

F32 = jnp.float32
BF16 = jnp.bfloat16
MESH = pl.DeviceIdType.MESH

VMEM_LIMIT_BYTES = 52 * 1024 * 1024
LANES = 128
SUBLANES = 8

D_FF_SPLIT = 2
RET_HEADS, RET_DK, RET_DV, RET_CHUNK = 8, 64, 128, 128
MLA_HEADS, MLA_Q_RANK, MLA_KV_RANK, MLA_NOPE, MLA_ROPE, MLA_V = 8, 512, 256, 128, 64, 128
GRID_W = 64
ROPE_BASE = 10000.0
RMS_EPS = 1e-6
GN_EPS = 1e-5
MIX_SPLITS = (RET_HEADS * RET_DK, RET_HEADS * RET_DK, RET_HEADS * RET_DV, RET_HEADS * RET_DV,
              MLA_Q_RANK, MLA_KV_RANK, MLA_ROPE)
MIX_IN = sum(MIX_SPLITS)
MIX_IN_PAD = 4096
ADAM_LR, ADAM_B1, ADAM_B2, ADAM_EPS, ADAM_WD, ADAM_STEP = 0.001, 0.9, 0.999, 1e-08, 0.01, 10


def _tile(n, pref, align):
    best = None
    t = align
    while t <= min(n, pref):
        if n % t == 0:
            best = t
        t += align
    return n if best is None else best


def _params(sem=None):
    return pltpu.CompilerParams(dimension_semantics=sem, vmem_limit_bytes=VMEM_LIMIT_BYTES)


def _matmul(a, b, mode, out_dtype, name, add=None, tm=1024, tn=1024, tk=2048):
    if mode == "nn":
        (m, k), (k2, n) = a.shape, b.shape
        dims = (((1,), (0,)), ((), ()))
    elif mode == "nt":
        (m, k), (n, k2) = a.shape, b.shape
        dims = (((1,), (1,)), ((), ()))
    else:
        (k, m), (k2, n) = a.shape, b.shape
        dims = (((0,), (0,)), ((), ()))
    assert k == k2, (a.shape, b.shape, mode)
    tm = _tile(m, tm, LANES if mode == "tn" else 16)
    tn = _tile(n, tn, LANES)
    tk = _tile(k, tk, LANES if mode != "tn" else 16)
    nk = k // tk
    a_spec = pl.BlockSpec((tk, tm), lambda i, j, kk: (kk, i)) if mode == "tn" else pl.BlockSpec((tm, tk), lambda i, j, kk: (i, kk))
    b_spec = pl.BlockSpec((tn, tk), lambda i, j, kk: (j, kk)) if mode == "nt" else pl.BlockSpec((tk, tn), lambda i, j, kk: (kk, j))
    o_spec = pl.BlockSpec((tm, tn), lambda i, j, kk: (i, j))
    has_add = add is not None

    def body(*refs):
        a_ref, b_ref = refs[0], refs[1]
        add_ref = refs[2] if has_add else None
        o_ref = refs[2 + has_add]
        p = lax.dot_general(a_ref[...].astype(BF16), b_ref[...].astype(BF16), dims, preferred_element_type=F32)
        if nk == 1:
            if has_add:
                p = p + add_ref[...].astype(F32)
            o_ref[...] = p.astype(out_dtype)
        else:
            acc = refs[3 + has_add]
            kk = pl.program_id(2)

            @pl.when(kk == 0)
            def _():
                acc[...] = p + add_ref[...].astype(F32) if has_add else p

            @pl.when(kk > 0)
            def _():
                acc[...] += p

            @pl.when(kk == nk - 1)
            def _():
                o_ref[...] = acc[...].astype(out_dtype)

    return pl.pallas_call(
        body, name=name, grid=(m // tm, n // tn, nk),
        in_specs=[a_spec, b_spec] + ([o_spec] if has_add else []),
        out_specs=o_spec,
        out_shape=jax.ShapeDtypeStruct((m, n), out_dtype),
        scratch_shapes=[pltpu.VMEM((tm, tn), F32)] if nk > 1 else [],
        compiler_params=_params(("parallel", "parallel", "arbitrary")),
    )(*((a, b, add) if has_add else (a, b)))


def _row_tile(t, n_ctx, d):
    pref = max(SUBLANES, min(256, (1 << 19) // d))
    r = _tile(int(np.gcd(t, n_ctx)) if n_ctx else t, pref, SUBLANES)
    return r, (n_ctx // r if n_ctx else 0)


def _seg_map(nct):
    if nct:
        return lambda i: (jnp.minimum(i // nct, 1), 0, 0)
    return lambda i: (0, 0, 0)


def _normmod_fwd(x, g, shift, scale, n_ctx, name):
    t, d = x.shape
    r, nct = _row_tile(t, n_ctx, d)

    def body(x_ref, g_ref, sh_ref, sc_ref, h_ref, ht_ref):
        xv = x_ref[...]
        rstd = lax.rsqrt(jnp.mean(xv * xv, axis=-1, keepdims=True) + RMS_EPS)
        n = xv * rstd * g_ref[...]
        h = n * (1.0 + sc_ref[0]) + sh_ref[0]
        h_ref[...] = h.astype(BF16)
        ht_ref[...] = h.T.astype(BF16)

    row = pl.BlockSpec((r, d), lambda i: (i, 0))
    seg = pl.BlockSpec((1, 1, d), _seg_map(nct))
    return pl.pallas_call(
        body, name=name, grid=(t // r,),
        in_specs=[row, pl.BlockSpec((1, d), lambda i: (0, 0)), seg, seg],
        out_specs=[row, pl.BlockSpec((d, r), lambda i: (0, i))],
        out_shape=[jax.ShapeDtypeStruct((t, d), BF16), jax.ShapeDtypeStruct((d, t), BF16)],
        compiler_params=_params(("parallel",)),
    )(x, g, shift, scale)


def _normmod_bwd(dh, x, g, shift, scale, dres, n_ctx, name):
    t, d = x.shape
    r, nct = _row_tile(t, n_ctx, d)
    has_res = dres is not None
    nseg = shift.shape[0]

    def body(*refs):
        dh_ref, x_ref, g_ref, sh_ref, sc_ref = refs[:5]
        dres_ref = refs[5] if has_res else None
        dx_ref, dg_ref, dsh_ref, dsc_ref = refs[5 + has_res:]
        i = pl.program_id(0)
        xv = x_ref[...]
        dhv = dh_ref[...].astype(F32)
        rstd = lax.rsqrt(jnp.mean(xv * xv, axis=-1, keepdims=True) + RMS_EPS)
        y = xv * rstd
        gv = g_ref[...]
        dn = dhv * (1.0 + sc_ref[0])
        dy = dn * gv
        dx = rstd * (dy - y * jnp.mean(dy * y, axis=-1, keepdims=True))
        if has_res:
            dx = dx + dres_ref[...]
        dx_ref[...] = dx

        @pl.when(i == 0)
        def _():
            dg_ref[...] = jnp.zeros_like(dg_ref)

        @pl.when(jnp.logical_or(i == 0, i == nct))
        def _():
            dsh_ref[...] = jnp.zeros_like(dsh_ref)
            dsc_ref[...] = jnp.zeros_like(dsc_ref)

        dg_ref[...] += jnp.sum(dn * y, axis=0, keepdims=True)
        dsh_ref[0] += jnp.sum(dhv, axis=0, keepdims=True)
        dsc_ref[0] += jnp.sum(dhv * (y * gv), axis=0, keepdims=True)

    row = pl.BlockSpec((r, d), lambda i: (i, 0))
    seg = pl.BlockSpec((1, 1, d), _seg_map(nct))
    vec = pl.BlockSpec((1, d), lambda i: (0, 0))
    return pl.pallas_call(
        body, name=name, grid=(t // r,),
        in_specs=[row, row, vec, seg, seg] + ([row] if has_res else []),
        out_specs=[row, vec, seg, seg],
        out_shape=[jax.ShapeDtypeStruct((t, d), F32), jax.ShapeDtypeStruct((1, d), F32),
                   jax.ShapeDtypeStruct((nseg, 1, d), F32), jax.ShapeDtypeStruct((nseg, 1, d), F32)],
        compiler_params=_params(("arbitrary",)),
    )(*((dh, x, g, shift, scale, dres) if has_res else (dh, x, g, shift, scale)))


def _gated_res_fwd(x, y, gate, coef, n_ctx, name):
    t, d = x.shape
    r, nct = _row_tile(t, n_ctx, d)

    def body(x_ref, y_ref, gt_ref, o_ref):
        o_ref[...] = x_ref[...] + (coef * gt_ref[0]) * y_ref[...]

    row = pl.BlockSpec((r, d), lambda i: (i, 0))
    return pl.pallas_call(
        body, name=name, grid=(t // r,),
        in_specs=[row, row, pl.BlockSpec((1, 1, d), _seg_map(nct))],
        out_specs=row, out_shape=jax.ShapeDtypeStruct((t, d), F32),
        compiler_params=_params(("parallel",)),
    )(x, y, gate)


def _matmul_gated_res(a, w, x, gate, coef, name, tm):
    t, k = a.shape
    d = w.shape[1]
    tm = _tile(t, tm, 16)
    tn = _tile(d, 512, LANES)

    def body(a_ref, w_ref, x_ref, gt_ref, o_ref, y_ref):
        y = jnp.dot(a_ref[...], w_ref[...], preferred_element_type=F32)
        o_ref[...] = x_ref[...] + (coef * gt_ref[0]) * y
        y_ref[...] = y.astype(BF16)

    blk = pl.BlockSpec((tm, tn), lambda i, j: (i, j))
    return pl.pallas_call(
        body, name=name, grid=(t // tm, d // tn),
        in_specs=[pl.BlockSpec((tm, k), lambda i, j: (i, 0)), pl.BlockSpec((k, tn), lambda i, j: (0, j)), blk,
                  pl.BlockSpec((1, 1, tn), lambda i, j: (0, 0, j))],
        out_specs=[blk, blk], out_shape=[jax.ShapeDtypeStruct((t, d), F32), jax.ShapeDtypeStruct((t, d), BF16)],
        compiler_params=_params(("parallel", "parallel")),
    )(a, w, x, gate)


def _gated_res_bwd(dout, y, gate, coef, n_ctx, name):
    t, d = dout.shape
    r, nct = _row_tile(t, n_ctx, d)
    nseg = gate.shape[0]

    def body(do_ref, y_ref, gt_ref, dy_ref, dgt_ref):
        i = pl.program_id(0)
        dov = do_ref[...] * coef
        dy_ref[...] = (dov * gt_ref[0]).astype(BF16)

        @pl.when(jnp.logical_or(i == 0, i == nct))
        def _():
            dgt_ref[...] = jnp.zeros_like(dgt_ref)

        dgt_ref[0] += jnp.sum(dov * y_ref[...], axis=0, keepdims=True)

    row = pl.BlockSpec((r, d), lambda i: (i, 0))
    seg = pl.BlockSpec((1, 1, d), _seg_map(nct))
    return pl.pallas_call(
        body, name=name, grid=(t // r,),
        in_specs=[row, row, seg], out_specs=[row, seg],
        out_shape=[jax.ShapeDtypeStruct((t, d), BF16), jax.ShapeDtypeStruct((nseg, 1, d), F32)],
        compiler_params=_params(("arbitrary",)),
    )(dout, y, gate)


def _swiglu_matmul(h, wg, wu, name, tm):
    t, k = h.shape
    f = wg.shape[1]
    tm = _tile(t, tm, LANES)
    tn = _tile(f, 512, LANES)

    def body(h_ref, wg_ref, wu_ref, g_ref, u_ref, a_ref, at_ref):
        hv = h_ref[...]
        g = jnp.dot(hv, wg_ref[...], preferred_element_type=F32)
        u = jnp.dot(hv, wu_ref[...], preferred_element_type=F32)
        a = g * jax.nn.sigmoid(g) * u
        g_ref[...] = g.astype(BF16)
        u_ref[...] = u.astype(BF16)
        a_ref[...] = a.astype(BF16)
        at_ref[...] = a.T.astype(BF16)

    w_spec = pl.BlockSpec((k, tn), lambda i, j: (0, j))
    o_spec = pl.BlockSpec((tm, tn), lambda i, j: (i, j))
    return pl.pallas_call(
        body, name=name, grid=(t // tm, f // tn),
        in_specs=[pl.BlockSpec((tm, k), lambda i, j: (i, 0)), w_spec, w_spec],
        out_specs=[o_spec, o_spec, o_spec, pl.BlockSpec((tn, tm), lambda i, j: (j, i))],
        out_shape=[jax.ShapeDtypeStruct((t, f), BF16)] * 3 + [jax.ShapeDtypeStruct((f, t), BF16)],
        compiler_params=_params(("parallel", "parallel")),
    )(h, wg, wu)


def _swiglu_bwd_matmul(dy, wo, gg, uu, name, tm):
    t, d = dy.shape
    f = wo.shape[0]
    tm = _tile(t, tm, 16)
    tn = _tile(f, 512, LANES)

    def body(dy_ref, wo_ref, g_ref, u_ref, dg_ref, du_ref):
        da = lax.dot_general(dy_ref[...], wo_ref[...], _NT, preferred_element_type=F32)
        gv = g_ref[...].astype(F32)
        sg = jax.nn.sigmoid(gv)
        dg_ref[...] = (da * u_ref[...].astype(F32) * (sg * (1.0 + gv * (1.0 - sg)))).astype(BF16)
        du_ref[...] = (da * (gv * sg)).astype(BF16)

    blk = pl.BlockSpec((tm, tn), lambda i, j: (i, j))
    return pl.pallas_call(
        body, name=name, grid=(t // tm, f // tn),
        in_specs=[pl.BlockSpec((tm, d), lambda i, j: (i, 0)), pl.BlockSpec((tn, d), lambda i, j: (j, 0)), blk, blk],
        out_specs=[blk, blk], out_shape=[jax.ShapeDtypeStruct((t, f), BF16)] * 2,
        compiler_params=_params(("parallel", "parallel")),
    )(dy, wo, gg, uu)


def _tok_tile(t):
    return 1024 if t % 1024 == 0 else 768 if t % 768 == 0 else _tile(t, 1024, 16)


FF_TILE = 1408
FULL_K = 1 << 30


def _make_ffn(n_ctx, tag):
    @jax.custom_vjp
    def ffn(x, g, shift, scale, gate, wg, wu, wo):
        return fwd(x, g, shift, scale, gate, wg, wu, wo)[0]

    def fwd(x, g, shift, scale, gate, wg, wu, wo):
        tt = _tok_tile(x.shape[0])
        h, ht = _normmod_fwd(x, g, shift, scale, n_ctx, tag + "_norm")
        gg, uu, a, at = _swiglu_matmul(h, wg, wu, tag + "_mm_gu", tt)
        if n_ctx:
            y = _matmul(a, wo, "nn", F32, tag + "_mm_o", tm=tt, tn=512, tk=FULL_K)
            out = _gated_res_fwd(x, y, gate, 0.5, n_ctx, tag + "_res")
        else:
            out, y = _matmul_gated_res(a, wo, x, gate, 0.5, tag + "_mm_o_res", tt)
        return out, (x, g, shift, scale, gate, wg, wu, wo, ht, gg, uu, at, y)

    def bwd(res, dout):
        x, g, shift, scale, gate, wg, wu, wo, ht, gg, uu, at, y = res
        tt = _tok_tile(x.shape[0])
        dy, dgate = _gated_res_bwd(dout, y, gate, 0.5, n_ctx, tag + "_res_b")
        dwo = _matmul(at, dy, "nn", BF16, tag + "_dwo", tm=512, tn=512, tk=FULL_K)
        dgg, duu = _swiglu_bwd_matmul(dy, wo, gg, uu, tag + "_da_act", tt)
        dwg = _matmul(ht, dgg, "nn", BF16, tag + "_dwg", tm=512, tn=512, tk=FULL_K)
        dwu = _matmul(ht, duu, "nn", BF16, tag + "_dwu", tm=512, tn=512, tk=FULL_K)
        dh = _matmul(dgg, wg, "nt", F32, tag + "_dh_g", tm=tt, tn=512, tk=FULL_K)
        dh = _matmul(duu, wu, "nt", F32, tag + "_dh_u", add=dh, tm=tt, tn=512, tk=FULL_K)
        dx, dg, dshift, dscale = _normmod_bwd(dh, x, g, shift, scale, dout, n_ctx, tag + "_norm_b")
        return dx, dg, dshift, dscale, dgate, dwg, dwu, dwo

    ffn.defvjp(fwd, bwd)
    return ffn


def _make_normmod_linear(n_ctx, tag):
    @jax.custom_vjp
    def op(x, g, shift, scale, w):
        return fwd(x, g, shift, scale, w)[0]

    def fwd(x, g, shift, scale, w):
        h, ht = _normmod_fwd(x, g, shift, scale, n_ctx, tag + "_norm")
        y = _matmul(h, w, "nn", F32, tag + "_mm", tm=_tok_tile(x.shape[0]))
        return y, (x, g, shift, scale, w, ht)

    def bwd(res, dy):
        x, g, shift, scale, w, ht = res
        tt = _tok_tile(x.shape[0])
        dyb = dy.astype(BF16)
        dw = _matmul(ht, dyb, "nn", BF16, tag + "_dw", tm=512, tn=512, tk=FULL_K)
        dh = _matmul(dyb, w, "nt", F32, tag + "_dh", tm=tt, tn=512, tk=FULL_K)
        dx, dg, dshift, dscale = _normmod_bwd(dh, x, g, shift, scale, None, n_ctx, tag + "_norm_b")
        return dx, dg, dshift, dscale, dw

    op.defvjp(fwd, bwd)
    return op


def _make_linear_gated_res(tag):
    @jax.custom_vjp
    def op(x, a, w, gate):
        return fwd(x, a, w, gate)[0]

    def fwd(x, a, w, gate):
        ab = a.astype(BF16)
        out, y = _matmul_gated_res(ab, w, x, gate, 1.0, tag + "_mm_res", _tok_tile(x.shape[0]))
        return out, (ab, w, gate, y)

    def bwd(res, dout):
        ab, w, gate, y = res
        tt = _tok_tile(dout.shape[0])
        dy, dgate = _gated_res_bwd(dout, y, gate, 1.0, 0, tag + "_res_b")
        dw = _matmul(ab.T, dy, "nn", BF16, tag + "_dw", tm=512, tn=512, tk=FULL_K)
        da = _matmul(dy, w, "nt", F32, tag + "_da", tm=tt)
        return dout, da, dw, dgate

    op.defvjp(fwd, bwd)
    return op


def _final_loss_call(x, g, target, name):
    t, d = x.shape
    r = _tile(t, 256, SUBLANES)

    def body(x_ref, g_ref, t_ref, loss_ref, dx_ref, dg_ref):
        i = pl.program_id(0)
        xv = x_ref[...]
        gv = g_ref[...]
        rstd = lax.rsqrt(jnp.mean(xv * xv, axis=-1, keepdims=True) + RMS_EPS)
        xh = xv * rstd
        e = xh * gv - t_ref[...]
        dy = e * (1.0 / d)
        dn = dy * gv
        dx_ref[...] = rstd * (dn - xh * jnp.mean(dn * xh, axis=-1, keepdims=True))

        @pl.when(i == 0)
        def _():
            loss_ref[...] = jnp.zeros_like(loss_ref)
            dg_ref[...] = jnp.zeros_like(dg_ref)

        loss_ref[...] += 0.5 * jnp.sum(jnp.mean(e * e, axis=-1, keepdims=True), axis=0, keepdims=True)
        dg_ref[...] += jnp.sum(dy * xh, axis=0, keepdims=True)

    row = pl.BlockSpec((r, d), lambda i: (i, 0))
    vec = pl.BlockSpec((1, d), lambda i: (0, 0))
    return pl.pallas_call(
        body, name=name, grid=(t // r,),
        in_specs=[row, vec, row], out_specs=[pl.BlockSpec((1, 1), lambda i: (0, 0)), row, vec],
        out_shape=[jax.ShapeDtypeStruct((1, 1), F32), jax.ShapeDtypeStruct((t, d), F32), jax.ShapeDtypeStruct((1, d), F32)],
        compiler_params=_params(("arbitrary",)),
    )(x, g, target)


@jax.custom_vjp
def _final_loss(x, g, target):
    return _final_loss_call(x, g, target, "final_loss")[0][0, 0]


def _final_loss_fwd(x, g, target):
    loss, dx, dg = _final_loss_call(x, g, target, "final_loss")
    return loss[0, 0], (dx, dg, target)


def _final_loss_bwd(res, dl):
    dx, dg, target = res
    return dx * dl, dg * dl, jnp.zeros_like(target)


_final_loss.defvjp(_final_loss_fwd, _final_loss_bwd)


_NT = (((1,), (1,)), ((), ()))
_TN = (((0,), (0,)), ((), ()))
ATTN_Q_TILE = 512
ATTN_K_CHUNK = 1408


def _first_last_step(grid):
    ids = [pl.program_id(a) for a in range(len(grid))]
    first = functools.reduce(jnp.logical_and, [i == 0 for i in ids])
    last = functools.reduce(jnp.logical_and, [i == g - 1 for i, g in zip(ids, grid)])
    return first, last


def _attn_fwd_call(q, k, v, scale, shards=()):
    h, nq, dq = q.shape
    nk, dv = v.shape[1], v.shape[2]
    tq = _tile(nq, ATTN_Q_TILE, 16)
    ck = _tile(nk, ATTN_K_CHUNK, LANES)
    nchunk = nk // ck
    n = len(shards)
    grid = (h, nq // tq)

    exp2_scale = scale * float(np.log2(np.e))

    def body(q_ref, k_ref, v_ref, *rest):
        ins, (o_ref, lse_ref), outs = rest[:n], rest[n:n + 2], rest[n + 2:2 * n + 2]
        m_scr, l_scr, acc_scr = rest[2 * n + 2:2 * n + 5]
        sems = rest[2 * n + 5:]
        if n:
            first, last = _first_last_step(grid)
            pl.when(first)(lambda: _gather_start(ins, outs, sems))
        qv = q_ref[0]
        m_scr[...] = jnp.full_like(m_scr, -jnp.inf)
        l_scr[...] = jnp.zeros_like(l_scr)
        acc_scr[...] = jnp.zeros_like(acc_scr)

        scores = lambda c: lax.dot_general(qv, k_ref[0, c * ck:(c + 1) * ck, :], _NT, preferred_element_type=F32)
        s_next = scores(0)
        for c in range(nchunk):
            s = s_next
            if c + 1 < nchunk:
                s_next = scores(c + 1)
            m_old = m_scr[...]
            m_new = jnp.maximum(m_old, jnp.max(s, axis=-1, keepdims=True))
            alpha = jnp.exp2((m_old - m_new) * exp2_scale)
            p = jnp.exp2((s - m_new) * exp2_scale)
            l_scr[...] = alpha * l_scr[...] + jnp.sum(p, axis=-1, keepdims=True)
            acc_scr[...] = alpha * acc_scr[...] + jnp.dot(p.astype(BF16), v_ref[0, c * ck:(c + 1) * ck, :], preferred_element_type=F32)
            m_scr[...] = m_new

        o_ref[0] = acc_scr[...] / l_scr[...]
        lse_ref[0] = m_scr[...] * scale + jnp.log(l_scr[...])
        if n:
            pl.when(last)(lambda: _gather_finish(ins, outs, sems))

    keys = lambda d: pl.BlockSpec((1, nk, d), lambda hh, i: (hh, 0, 0))
    return pl.pallas_call(
        body, name="attn_fwd", grid=grid,
        in_specs=[pl.BlockSpec((1, tq, dq), lambda hh, i: (hh, i, 0)), keys(dq), keys(dv)] + [HBM_SPEC] * n,
        out_specs=[pl.BlockSpec((1, tq, dv), lambda hh, i: (hh, i, 0)),
                   pl.BlockSpec((1, tq, 1), lambda hh, i: (hh, i, 0))] + [HBM_SPEC] * n,
        out_shape=[jax.ShapeDtypeStruct((h, nq, dv), F32), jax.ShapeDtypeStruct((h, nq, 1), F32)] + _gather_out_shapes(shards),
        scratch_shapes=[pltpu.VMEM((tq, 1), F32), pltpu.VMEM((tq, 1), F32), pltpu.VMEM((tq, dv), F32)] + (_gather_sems(n) if n else []),
        compiler_params=_params(("arbitrary", "arbitrary") if n else ("parallel", "arbitrary")),
    )(q, k, v, *shards)


def _attn_bwd_call(q, k, v, o, do, lse, scale, parts=()):
    h, nq, dq = q.shape
    nk, dv = v.shape[1], v.shape[2]
    tq = _tile(nq, ATTN_Q_TILE, 16)
    ck = _tile(nk, ATTN_K_CHUNK, LANES)
    nchunk = nk // ck
    n = len(parts)
    grid = (h, nq // tq)

    log2e = float(np.log2(np.e))
    exp2_scale = scale * log2e

    def body(q_ref, k_ref, v_ref, o_ref, do_ref, lse_ref, *rest):
        ins, (dq_ref, dk_ref, dv_ref), outs, sems = rest[:n], rest[n:n + 3], rest[n + 3:2 * n + 3], rest[2 * n + 3:]
        if n:
            first, last = _first_last_step(grid)
            pl.when(first)(lambda: _scatter_start(ins, outs, sems))
        i = pl.program_id(1)

        @pl.when(i == 0)
        def _():
            dk_ref[...] = jnp.zeros_like(dk_ref)
            dv_ref[...] = jnp.zeros_like(dv_ref)

        qv = q_ref[0]
        dov = do_ref[0]
        dob = dov.astype(BF16)
        delta = jnp.sum(dov * o_ref[0], axis=-1, keepdims=True)
        lse2 = lse_ref[0] * log2e

        def scores(c):
            rows = slice(c * ck, (c + 1) * ck)
            return (lax.dot_general(qv, k_ref[0, rows, :], _NT, preferred_element_type=F32),
                    lax.dot_general(dob, v_ref[0, rows, :], _NT, preferred_element_type=F32))

        nxt = scores(0)
        dq_acc = None
        for c in range(nchunk):
            rows = slice(c * ck, (c + 1) * ck)
            s, dp = nxt
            if c + 1 < nchunk:
                nxt = scores(c + 1)
            p = jnp.exp2(s * exp2_scale - lse2)
            ds = (p * (dp - delta) * scale).astype(BF16)
            dv_ref[0, rows, :] += lax.dot_general(p.astype(BF16), dob, _TN, preferred_element_type=F32)
            dk_ref[0, rows, :] += lax.dot_general(ds, qv, _TN, preferred_element_type=F32)
            part = jnp.dot(ds, k_ref[0, rows, :], preferred_element_type=F32)
            dq_acc = part if dq_acc is None else dq_acc + part
        dq_ref[0] = dq_acc
        if n:
            pl.when(last)(lambda: _scatter_finish(ins, outs, sems))

    qspec = lambda d: pl.BlockSpec((1, tq, d), lambda hh, i: (hh, i, 0))
    kspec = lambda d: pl.BlockSpec((1, nk, d), lambda hh, i: (hh, 0, 0), pipeline_mode=pl.Buffered(1))
    return pl.pallas_call(
        body, name="attn_bwd", grid=grid,
        in_specs=[qspec(dq), kspec(dq), kspec(dv), qspec(dv), qspec(dv), qspec(1)] + [HBM_SPEC] * n,
        out_specs=[qspec(dq), kspec(dq), kspec(dv)] + [HBM_SPEC] * n,
        out_shape=[jax.ShapeDtypeStruct((h, nq, dq), F32), jax.ShapeDtypeStruct((h, nk, dq), F32),
                   jax.ShapeDtypeStruct((h, nk, dv), F32)] + [jax.ShapeDtypeStruct(p.shape, p.dtype) for p in parts],
        scratch_shapes=_scatter_sems(n) if n else [],
        compiler_params=_params(("arbitrary", "arbitrary") if n else ("parallel", "arbitrary")),
    )(q, k, v, o, do, lse, *parts)


@jax.custom_vjp
def _attention_gather(q, k, v, shards):
    return _attention_gather_fwd(q, k, v, shards)[0]


def _attention_gather_fwd(q, k, v, shards):
    scale = q.shape[-1] ** -0.5
    qb, kb, vb = q.astype(BF16), k.astype(BF16), v.astype(BF16)
    blocks = [s.astype(BF16) for s in shards]
    o, lse, *stacked = _attn_fwd_call(qb, kb, vb, scale, blocks)
    return (o, tuple(_fill_own_slot(stacked, blocks))), (qb, kb, vb, o, lse)


def _attention_gather_bwd(res, cts):
    qb, kb, vb, o, lse = res
    do, dstacked = cts
    scale = qb.shape[-1] ** -0.5
    if not dstacked:
        return (*_attn_bwd_call(qb, kb, vb, o, do, lse, scale), ())
    own = {}

    def scatter(parts):
        own["dq"], own["dk"], own["dv"], *landed = _attn_bwd_call(qb, kb, vb, o, do, lse, scale, parts)
        return landed

    top = lax.axis_index("c") == 0
    grads = tuple(jnp.concatenate([jnp.where(top, mine, theirs), jnp.where(top, theirs, mine)], axis=0)
                  for mine, theirs in _reduce_scatter_grads(list(dstacked), "attn", scatter))
    return own["dq"], own["dk"], own["dv"], grads


_attention_gather.defvjp(_attention_gather_fwd, _attention_gather_bwd)


def _attention(q, k, v):
    return _attention_gather(q, k, v, ())[0]


RET_UNROLL = 4


def _bf(x):
    return x.astype(BF16)


def _dot(a, b, dims=(((1,), (0,)), ((), ()))):
    return lax.dot_general(_bf(a), _bf(b), dims, preferred_element_type=F32)


def _sum_all(x):
    return jnp.sum(jnp.sum(x, axis=1, keepdims=True), axis=0, keepdims=True)


def _ret_consts(lgf_ref, lgb_ref):
    c = RET_CHUNK
    lgf = lgf_ref[0][:, :1]
    lgb = lgb_ref[0][:, :1]
    diff = (lax.broadcasted_iota(jnp.int32, (c, c), 0) - lax.broadcasted_iota(jnp.int32, (c, c), 1)).astype(F32)
    mf = diff >= 0
    dmat = jnp.where(mf, jnp.exp(lgf * jnp.where(mf, diff, 0.0)), jnp.exp(lgb * jnp.where(mf, 0.0, -diff)))
    col = lax.broadcasted_iota(jnp.int32, (c, 1), 0).astype(F32)
    return dict(diff=diff, mf=mf, dmat=dmat, col=col,
                xif=jnp.exp(lgf * (col + 1.0)), zf=jnp.exp(lgf * (c - 1.0 - col)),
                xib=jnp.exp(lgb * (c - col)), zb=jnp.exp(lgb * col),
                gf=jnp.exp(lgf * c), gb=jnp.exp(lgb * c))


def _ret_rows(n):
    return pl.ds(pl.multiple_of(n * RET_CHUNK, RET_CHUNK), RET_CHUNK)


def _ret_fwd_call(q, k, v, lgf, lgb, s0f, s0b):
    h, n_tok, dk = q.shape
    dv = v.shape[-1]
    nc = n_tok // RET_CHUNK

    def body(q_ref, k_ref, v_ref, lgf_ref, lgb_ref, s0f_ref, s0b_ref, y_ref, sff_ref, sbf_ref, sb_scr):
        cs = _ret_consts(lgf_ref, lgb_ref)

        sbf_ref[0] = s0b_ref[0]

        @pl.loop(0, nc, unroll=RET_UNROLL)
        def _(t):
            n = nc - 1 - t
            sb = sbf_ref[0]
            sb_scr[n] = sb
            sbf_ref[0] = cs["gb"] * sb + _dot(k_ref[0, _ret_rows(n), :] * cs["zb"], v_ref[0, _ret_rows(n), :], _TN)

        sff_ref[0] = s0f_ref[0]

        @pl.loop(0, nc, unroll=RET_UNROLL)
        def _(n):
            sf = sff_ref[0]
            qc, kc, vc = q_ref[0, _ret_rows(n), :], k_ref[0, _ret_rows(n), :], v_ref[0, _ret_rows(n), :]
            p = _dot(qc, kc, _NT) * cs["dmat"]
            y_ref[0, _ret_rows(n), :] = _dot(p, vc) + _dot(qc * cs["xif"], sf) + _dot(qc * cs["xib"], sb_scr[n])
            sff_ref[0] = cs["gf"] * sf + _dot(kc * cs["zf"], vc, _TN)

    tok = lambda d: pl.BlockSpec((1, n_tok, d), lambda hh: (hh, 0, 0), pipeline_mode=pl.Buffered(1))
    lg = pl.BlockSpec((1, 1, LANES), lambda hh: (hh, 0, 0))
    st = pl.BlockSpec((1, dk, dv), lambda hh: (hh, 0, 0))
    return pl.pallas_call(
        body, name="ret_fwd_%d" % n_tok, grid=(h,),
        in_specs=[tok(dk), tok(dk), tok(dv), lg, lg, st, st], out_specs=[tok(dv), st, st],
        out_shape=[jax.ShapeDtypeStruct((h, n_tok, dv), F32)] + [jax.ShapeDtypeStruct((h, dk, dv), F32)] * 2,
        scratch_shapes=[pltpu.VMEM((nc, dk, dv), F32)],
        compiler_params=_params(("parallel",)),
    )(q, k, v, lgf, lgb, s0f, s0b)


def _ret_bwd_call(q, k, v, lgf, lgb, s0f, s0b, dy, dsff, dsbf):
    h, n_tok, dk = q.shape
    dv = v.shape[-1]
    nc = n_tok // RET_CHUNK
    c = float(RET_CHUNK)

    def body(q_ref, k_ref, v_ref, lgf_ref, lgb_ref, s0f_ref, s0b_ref, dy_ref, dsff_ref, dsbf_ref,
             dq_ref, dk_ref, dv_ref, dlgf_ref, dlgb_ref, ds0f_ref, ds0b_ref, sb_scr, gf_scr, st_a, st_b):
        cs = _ret_consts(lgf_ref, lgb_ref)

        st_a[...] = s0b_ref[0]
        st_b[...] = dsff_ref[0]

        @pl.loop(0, nc, unroll=RET_UNROLL)
        def _(t):
            n = nc - 1 - t
            sb, gf_next = st_a[...], st_b[...]
            sb_scr[n] = sb
            gf_scr[n] = gf_next
            qc, kc, vc, dyc = (r[0, _ret_rows(n), :] for r in (q_ref, k_ref, v_ref, dy_ref))
            st_a[...] = cs["gb"] * sb + _dot(kc * cs["zb"], vc, _TN)
            st_b[...] = _dot(qc * cs["xif"], dyc, _TN) + cs["gf"] * gf_next

        ds0f_ref[0] = st_b[...]

        st_a[...] = s0f_ref[0]
        st_b[...] = dsbf_ref[0]
        dlgf_ref[...] = jnp.zeros_like(dlgf_ref)
        dlgb_ref[...] = jnp.zeros_like(dlgb_ref)

        @pl.loop(0, nc, unroll=RET_UNROLL)
        def _(n):
            sf, gb_prev = st_a[...], st_b[...]
            sb, gf_next = sb_scr[n], gf_scr[n]
            qc, kc, vc, dyc = (r[0, _ret_rows(n), :] for r in (q_ref, k_ref, v_ref, dy_ref))
            a = _dot(qc, kc, _NT)
            dp = _dot(dyc, vc, _NT)
            da = _bf(dp * cs["dmat"])
            dqf = _dot(dyc, sf, _NT)
            dqb = _dot(dyc, sb, _NT)
            dkf = _dot(vc, gf_next, _NT)
            dkb = _dot(vc, gb_prev, _NT)
            dq_ref[0, _ret_rows(n), :] = _dot(da, kc) + dqf * cs["xif"] + dqb * cs["xib"]
            dk_ref[0, _ret_rows(n), :] = _dot(da, qc, _TN) + dkf * cs["zf"] + dkb * cs["zb"]
            dv_ref[0, _ret_rows(n), :] = (_dot(a * cs["dmat"], dyc, _TN) + _dot(kc * cs["zf"], gf_next)
                                         + _dot(kc * cs["zb"], gb_prev))
            w = dp * a * cs["dmat"] * cs["diff"]
            row = lambda x: jnp.sum(x, axis=1, keepdims=True)
            dlgf_ref[0] += (_sum_all(jnp.where(cs["mf"], w, 0.0))
                            + _sum_all((cs["col"] + 1.0) * cs["xif"] * row(dqf * qc) + (c - 1.0 - cs["col"]) * cs["zf"] * row(dkf * kc))
                            + c * cs["gf"] * _sum_all(gf_next * sf))
            dlgb_ref[0] += (_sum_all((c - cs["col"]) * cs["xib"] * row(dqb * qc) + cs["col"] * cs["zb"] * row(dkb * kc))
                            + c * cs["gb"] * _sum_all(gb_prev * sb) - _sum_all(jnp.where(cs["mf"], 0.0, w)))
            st_a[...] = cs["gf"] * sf + _dot(kc * cs["zf"], vc, _TN)
            st_b[...] = _dot(qc * cs["xib"], dyc, _TN) + cs["gb"] * gb_prev

        ds0b_ref[0] = st_b[...]

    tok = lambda d: pl.BlockSpec((1, n_tok, d), lambda hh: (hh, 0, 0), pipeline_mode=pl.Buffered(1))
    lg = pl.BlockSpec((1, 1, LANES), lambda hh: (hh, 0, 0))
    st = pl.BlockSpec((1, dk, dv), lambda hh: (hh, 0, 0))
    return pl.pallas_call(
        body, name="ret_bwd_%d" % n_tok, grid=(h,),
        in_specs=[tok(dk), tok(dk), tok(dv), lg, lg, st, st, tok(dv), st, st],
        out_specs=[tok(dk), tok(dk), tok(dv), lg, lg, st, st],
        out_shape=[jax.ShapeDtypeStruct((h, n_tok, dk), F32)] * 2 + [jax.ShapeDtypeStruct((h, n_tok, dv), F32)]
        + [jax.ShapeDtypeStruct((h, 1, LANES), F32)] * 2 + [jax.ShapeDtypeStruct((h, dk, dv), F32)] * 2,
        scratch_shapes=[pltpu.VMEM((nc, dk, dv), F32), pltpu.VMEM((nc, dk, dv), F32), pltpu.VMEM((dk, dv), F32), pltpu.VMEM((dk, dv), F32)],
        compiler_params=_params(("parallel",)),
    )(q, k, v, lgf, lgb, s0f, s0b, dy, dsff, dsbf)


def _lane_bcast(lg):
    return jnp.broadcast_to(lg[:, None, None], (lg.shape[0], 1, LANES))


@jax.custom_vjp
def _retention(q, k, v, lgf, lgb, s0f, s0b):
    return tuple(_ret_fwd_call(q, k, v, _lane_bcast(lgf), _lane_bcast(lgb), s0f, s0b))


def _retention_fwd(q, k, v, lgf, lgb, s0f, s0b):
    return _retention(q, k, v, lgf, lgb, s0f, s0b), (q, k, v, lgf, lgb, s0f, s0b)


def _retention_bwd(res, cts):
    q, k, v, lgf, lgb, s0f, s0b = res
    dy, dsff, dsbf = cts
    dq, dk, dv, dlgf, dlgb, ds0f, ds0b = _ret_bwd_call(q, k, v, _lane_bcast(lgf), _lane_bcast(lgb), s0f, s0b, dy, dsff, dsbf)
    return dq, dk, dv, dlgf[:, 0, 0], dlgb[:, 0, 0], ds0f, ds0b


_retention.defvjp(_retention_fwd, _retention_bwd)


def _gn_specs(y):
    h, n, dv = y.shape
    r = _tile(n, 512, SUBLANES)
    return (h, n, dv, r, pl.BlockSpec((1, r, dv), lambda i, hh: (hh, i, 0)), pl.BlockSpec((r, dv), lambda i, hh: (i, hh)))


def _gn_norm(yv):
    mu = jnp.mean(yv, axis=-1, keepdims=True)
    yc = yv - mu
    rstd = lax.rsqrt(jnp.mean(yc * yc, axis=-1, keepdims=True) + GN_EPS)
    return yc * rstd, rstd


def _gn_gate_fwd_call(y, gate):
    h, n, dv, r, yspec, gspec = _gn_specs(y)

    def body(y_ref, g_ref, o_ref):
        gv = g_ref[...]
        o_ref[...] = gv * jax.nn.sigmoid(gv) * _gn_norm(y_ref[0])[0]

    return pl.pallas_call(
        body, name="gn_gate", grid=(n // r, h), in_specs=[yspec, gspec], out_specs=gspec,
        out_shape=jax.ShapeDtypeStruct((n, h * dv), F32), compiler_params=_params(("parallel", "parallel")),
    )(y, gate)


def _gn_gate_bwd_call(y, gate, dout):
    h, n, dv, r, yspec, gspec = _gn_specs(y)

    def body(y_ref, g_ref, do_ref, dy_ref, dg_ref):
        gv = g_ref[...]
        dov = do_ref[...]
        yn, rstd = _gn_norm(y_ref[0])
        sg = jax.nn.sigmoid(gv)
        dg_ref[...] = dov * yn * (sg * (1.0 + gv * (1.0 - sg)))
        dyn = dov * (gv * sg)
        dy_ref[0] = rstd * (dyn - jnp.mean(dyn, axis=-1, keepdims=True) - yn * jnp.mean(dyn * yn, axis=-1, keepdims=True))

    return pl.pallas_call(
        body, name="gn_gate_b", grid=(n // r, h), in_specs=[yspec, gspec, gspec], out_specs=[yspec, gspec],
        out_shape=[jax.ShapeDtypeStruct((h, n, dv), F32), jax.ShapeDtypeStruct((n, h * dv), F32)],
        compiler_params=_params(("parallel", "parallel")),
    )(y, gate, dout)


@jax.custom_vjp
def _gn_gate(y, gate):
    return _gn_gate_fwd_call(y, gate)


_gn_gate.defvjp(lambda y, gate: (_gn_gate_fwd_call(y, gate), (y, gate)),
                lambda res, dout: tuple(_gn_gate_bwd_call(res[0], res[1], dout)))


def _rope_tables(pos, dim, base):
    inv = base ** (-jnp.arange(0, dim, 2, dtype=F32) / dim)
    ang = pos.astype(F32)[:, None] * inv[None, :]
    return jnp.cos(ang)[:, None, :], jnp.sin(ang)[:, None, :]


def _rotate(x, cos, sin):
    x1, x2 = jnp.split(x, 2, axis=-1)
    return jnp.concatenate([x1 * cos - x2 * sin, x2 * cos + x1 * sin], axis=-1)


def _axial_rope(x, row_tab, col_tab):
    xr, xc = jnp.split(x, 2, axis=-1)
    return jnp.concatenate([_rotate(xr, *row_tab), _rotate(xc, *col_tab)], axis=-1)


def _heads(t):
    return jnp.swapaxes(t, 0, 1)


def _local_loss(x, mods_lat, mods_ctx, small, big, late, ctx, target):
    n_lat, d = x.shape
    n_ctx = ctx.shape[0]
    both = lambda i: jnp.stack([mods_ctx[i], mods_lat[i]])[:, None, :]
    lat = lambda i: mods_lat[i][None, None, :]

    xs = jnp.concatenate([ctx, x], axis=0)
    x1 = _make_ffn(n_ctx, "ffn1")(xs, small["norm1_g"], both(0), both(1), both(2),
                                  big["ffn1_wg"], big["ffn1_wu"], big["ffn1_wo"])
    proj = _make_normmod_linear(n_ctx, "mix_in")(x1, small["norm2_g"], both(3), both(4), big["mix_in"])
    offs = np.cumsum((0,) + MIX_SPLITS)
    part = lambda i, rows: proj[rows, offs[i]:offs[i + 1]]
    lat_rows, ctx_rows = slice(n_ctx, None), slice(0, n_ctx)

    zq = jnp.zeros((1, 1, MLA_Q_RANK), F32)
    zkv = jnp.zeros((1, 1, MLA_KV_RANK), F32)
    q = _make_normmod_linear(0, "mla_q")(part(4, lat_rows), small["mla_q_norm_g"], zq, zq, big["w_uq"])
    kv = _make_normmod_linear(0, "mla_kv")(part(5, slice(None)), small["mla_kv_norm_g"], zkv, zkv, big["w_ukv"])

    lgf = jax.nn.log_sigmoid(small["ret_decay_fwd"][0])
    lgb = jax.nn.log_sigmoid(small["ret_decay_bwd"][0])
    ret_tab = _rope_tables(jnp.arange(n_lat), RET_DK, ROPE_BASE)
    hd = lambda t, dd: t.reshape(t.shape[0], RET_HEADS, dd)
    s_zero = jnp.zeros((RET_HEADS, RET_DK, RET_DV), F32)
    _, s_f, s_b = _retention(_heads(hd(part(0, ctx_rows), RET_DK)), _heads(hd(part(1, ctx_rows), RET_DK) * (RET_DK ** -0.5)),
                             _heads(hd(part(2, ctx_rows), RET_DV)), lgf, lgb, s_zero, s_zero)
    rq = _rotate(hd(part(0, lat_rows), RET_DK), *ret_tab)
    rk = _rotate(hd(part(1, lat_rows), RET_DK) * (RET_DK ** -0.5), *ret_tab)
    y_lat, _, _ = _retention(_heads(rq), _heads(rk), _heads(hd(part(2, lat_rows), RET_DV)), lgf, lgb, s_f, s_b)
    ret_out = _gn_gate(y_lat, part(3, lat_rows))

    pos = jnp.arange(n_lat)
    row_tab = _rope_tables(pos // GRID_W, MLA_ROPE // 2, ROPE_BASE)
    col_tab = _rope_tables(pos % GRID_W, MLA_ROPE // 2, ROPE_BASE)
    q = q.reshape(n_lat, MLA_HEADS, MLA_NOPE + MLA_ROPE)
    q_all = jnp.concatenate([q[..., :MLA_NOPE], _axial_rope(q[..., MLA_NOPE:], row_tab, col_tab)], axis=-1)
    kv = kv.reshape(n_ctx + n_lat, MLA_HEADS, MLA_NOPE + MLA_V)
    kr_lat = _axial_rope(part(6, lat_rows)[:, None, :], row_tab, col_tab)
    kr = jnp.concatenate([kr_lat, part(6, ctx_rows)[:, None, :]], axis=0)
    kv_lat_first = jnp.concatenate([kv[n_ctx:], kv[:n_ctx]], axis=0)
    k_all = jnp.concatenate([kv_lat_first[..., :MLA_NOPE], jnp.broadcast_to(kr, (n_ctx + n_lat, MLA_HEADS, MLA_ROPE))], axis=-1)
    mla, (w_in2, w_out2) = _attention_gather(_heads(q_all), _heads(k_all), _heads(kv_lat_first[..., MLA_NOPE:]), tuple(late))
    mla_out = _heads(mla).reshape(n_lat, MLA_HEADS * MLA_V)

    x2 = _make_linear_gated_res("mix_out")(x1[n_ctx:], jnp.concatenate([ret_out, mla_out], axis=-1), big["mix_out"], lat(5))
    x3 = _make_ffn(0, "ffn2")(x2, small["norm3_g"], lat(6), lat(7), lat(8), _slots_side_by_side(w_in2, (0, 1)),
                              _slots_side_by_side(w_in2, (2, 3)), w_out2.reshape(-1, w_out2.shape[-1]))
    return _final_loss(x3, small["final_norm_g"][None, :], target)


HBM_SPEC = pl.BlockSpec(memory_space=pl.ANY)
VMEM_SPEC = pl.BlockSpec(memory_space=pltpu.VMEM)
ALL_PEERS = (1, 2, 3, 4, 5, 6, 7)
CHIP_PEERS = (4, 2, 6)


def _me():
    return lax.axis_index("x"), lax.axis_index("y"), lax.axis_index("c")


def _flip(pos, mask):
    x, y, c = pos
    return (1 - x if mask & 4 else x, 1 - y if mask & 2 else y, 1 - c if mask & 1 else c)


def _allgather_small(block, masks, chips_only, name):
    r, c = block.shape
    n_slots = 4 if chips_only else 8

    def body(x_ref, out_ref, send_sems, recv_sems, local_sem):
        pos = _me()
        slot = 2 * pos[0] + pos[1] if chips_only else 4 * pos[0] + 2 * pos[1] + pos[2]
        local = pltpu.make_async_copy(x_ref, out_ref.at[slot], local_sem)
        local.start()
        copies = [pltpu.make_async_remote_copy(src_ref=x_ref, dst_ref=out_ref.at[slot], send_sem=send_sems.at[j], recv_sem=recv_sems.at[j],
                                               device_id=_flip(pos, mask), device_id_type=MESH) for j, mask in enumerate(masks)]
        for cp in copies:
            cp.start()
        for cp in copies:
            cp.wait()
        local.wait()

    return pl.pallas_call(
        body, name=name, in_specs=[VMEM_SPEC], out_specs=VMEM_SPEC,
        out_shape=jax.ShapeDtypeStruct((n_slots, r, c), block.dtype),
        scratch_shapes=[pltpu.SemaphoreType.DMA((len(masks),)), pltpu.SemaphoreType.DMA((len(masks),)), pltpu.SemaphoreType.DMA],
        compiler_params=pltpu.CompilerParams(vmem_limit_bytes=VMEM_LIMIT_BYTES),
    )(block)


def _gather_weights(shards):
    n = len(shards)

    def body(*refs):
        ins, outs, sems = refs[:n], refs[n:2 * n], refs[2 * n:]
        _gather_start(ins, outs, sems)
        _gather_finish(ins, outs, sems)

    stacked = pl.pallas_call(
        body, name="gather_weights", in_specs=[HBM_SPEC] * n, out_specs=[HBM_SPEC] * n,
        out_shape=_gather_out_shapes(shards), scratch_shapes=_gather_sems(n),
    )(*shards)
    return _fill_own_slot(stacked, shards)


def _gather_out_shapes(shards):
    return [jax.ShapeDtypeStruct((4,) + s.shape, s.dtype) for s in shards]


def _gather_sems(n):
    return [pltpu.SemaphoreType.DMA((3 * n,)) for _ in range(4)]


def _fill_own_slot(stacked, shards):
    if not shards:
        return []
    chip = 2 * lax.axis_index("x") + lax.axis_index("y")
    return [lax.dynamic_update_slice_in_dim(st, sh[None], chip, axis=0) for st, sh in zip(stacked, shards)]


def _gather_send(ins, outs, sems, w, j, pos):
    x, y, c = pos
    half = ins[w].shape[0] // 2
    mine = pl.ds(c * half, half)
    return pltpu.make_async_remote_copy(src_ref=ins[w].at[mine], dst_ref=outs[w].at[2 * x + y, mine], send_sem=sems[0].at[3 * w + j],
                                        recv_sem=sems[1].at[3 * w + j], device_id=_flip(pos, CHIP_PEERS[j]), device_id_type=MESH)


def _gather_pass(ins, outs, sems, w, j, pos, to_me):
    px, py, _ = _flip(pos, CHIP_PEERS[j])
    half = ins[w].shape[0] // 2
    slab = outs[w].at[2 * px + py, pl.ds(((1 - pos[2]) if to_me else pos[2]) * half, half)]
    return pltpu.make_async_remote_copy(src_ref=slab, dst_ref=slab, send_sem=sems[2].at[3 * w + j], recv_sem=sems[3].at[3 * w + j],
                                        device_id=_flip(pos, 1), device_id_type=MESH)


def _gather_start(ins, outs, sems):
    pos = _me()
    for w in range(len(ins)):
        for j in range(3):
            _gather_send(ins, outs, sems, w, j, pos).start()


def _gather_finish(ins, outs, sems):
    pos = _me()
    pairs = [(w, j) for w in range(len(ins)) for j in range(3)]
    for w, j in pairs:
        _gather_send(ins, outs, sems, w, j, pos).wait_recv()
        _gather_pass(ins, outs, sems, w, j, pos, False).start()
    for w, j in pairs:
        _gather_pass(ins, outs, sems, w, j, pos, True).wait_recv()
    for w, j in pairs:
        _gather_send(ins, outs, sems, w, j, pos).wait_send()
        _gather_pass(ins, outs, sems, w, j, pos, False).wait_send()


def _pair_swap_halves(grads, tag):
    n = len(grads)

    def body(*refs):
        ins, outs = refs[:n], refs[n:2 * n]
        send_sems, recv_sems = refs[2 * n:]
        pos = _me()
        copies = []
        for w in range(n):
            half = grads[w].shape[1] // 2
            cp = pltpu.make_async_remote_copy(src_ref=ins[w].at[:, pl.ds((1 - pos[2]) * half, half), :], dst_ref=outs[w], send_sem=send_sems.at[w],
                                              recv_sem=recv_sems.at[w], device_id=_flip(pos, 1), device_id_type=MESH)
            cp.start()
            copies.append(cp)
        for cp in copies:
            cp.wait()

    return pl.pallas_call(
        body, name="pair_swap_halves_" + tag, in_specs=[HBM_SPEC] * n, out_specs=[HBM_SPEC] * n,
        out_shape=[jax.ShapeDtypeStruct((4, g.shape[1] // 2, g.shape[2]), g.dtype) for g in grads],
        scratch_shapes=[pltpu.SemaphoreType.DMA((n,)), pltpu.SemaphoreType.DMA((n,))],
    )(*grads)


def _chip_scatter(parts):
    n = len(parts)

    def body(*refs):
        ins, outs, sems = refs[:n], refs[n:2 * n], refs[2 * n:]
        _scatter_start(ins, outs, sems)
        _scatter_finish(ins, outs, sems)

    return pl.pallas_call(
        body, name="chip_scatter", in_specs=[HBM_SPEC] * n, out_specs=[HBM_SPEC] * n,
        out_shape=[jax.ShapeDtypeStruct(p.shape, p.dtype) for p in parts], scratch_shapes=_scatter_sems(n),
    )(*parts)


def _scatter_sems(n):
    return [pltpu.SemaphoreType.DMA((3 * n,)), pltpu.SemaphoreType.DMA((3 * n,)), pltpu.SemaphoreType.DMA((n,))]


def _scatter_copies(ins, outs, sems):
    pos = _me()
    me = 2 * pos[0] + pos[1]
    local = [pltpu.make_async_copy(ins[w].at[me], outs[w].at[me], sems[2].at[w]) for w in range(len(ins))]
    remote = []
    for w in range(len(ins)):
        for j, mask in enumerate(CHIP_PEERS):
            px, py, _ = _flip(pos, mask)
            remote.append(pltpu.make_async_remote_copy(src_ref=ins[w].at[2 * px + py], dst_ref=outs[w].at[me], send_sem=sems[0].at[3 * w + j],
                                                       recv_sem=sems[1].at[3 * w + j], device_id=_flip(pos, mask), device_id_type=MESH))
    return local, remote


def _scatter_start(ins, outs, sems):
    local, remote = _scatter_copies(ins, outs, sems)
    for cp in local + remote:
        cp.start()


def _scatter_finish(ins, outs, sems):
    local, remote = _scatter_copies(ins, outs, sems)
    for cp in remote + local:
        cp.wait()


def _pair_swap_reduced(halves, tag):
    n = len(halves)

    def body(*refs):
        ins, outs = refs[:n], refs[n:2 * n]
        send_sems, recv_sems = refs[2 * n:]
        pos = _me()
        copies = []
        for w in range(n):
            cp = pltpu.make_async_remote_copy(src_ref=ins[w], dst_ref=outs[w], send_sem=send_sems.at[w], recv_sem=recv_sems.at[w],
                                              device_id=_flip(pos, 1), device_id_type=MESH)
            cp.start()
            copies.append(cp)
        for cp in copies:
            cp.wait()

    dma = lambda k: pltpu.SemaphoreType.DMA((k,))
    return pl.pallas_call(
        body, name="pair_swap_reduced_" + tag, in_specs=[HBM_SPEC] * n, out_specs=[HBM_SPEC] * n,
        out_shape=[jax.ShapeDtypeStruct(h.shape, h.dtype) for h in halves],
        scratch_shapes=[dma(n), dma(n)],
    )(*halves)


def _add_pair(mine, theirs, name):
    s, h, c = mine.shape
    r = _tile(h, max(16, (1 << 19) // c), 16)

    def body(a_ref, b_ref, o_ref):
        o_ref[...] = (a_ref[...].astype(F32) + b_ref[...].astype(F32)).astype(BF16)

    blk = pl.BlockSpec((1, r, c), lambda i, j: (i, j, 0))
    return pl.pallas_call(
        body, name=name, grid=(s, h // r), in_specs=[blk, blk], out_specs=blk,
        out_shape=jax.ShapeDtypeStruct(mine.shape, BF16), compiler_params=_params(("parallel", "parallel")),
    )(mine, theirs)


def _sum_slots(parts, name):
    s, h, c = parts.shape
    r = _tile(h, max(16, (1 << 18) // c), 16)

    def body(p_ref, o_ref):
        acc = p_ref[0].astype(F32)
        for k in range(1, s):
            acc = acc + p_ref[k].astype(F32)
        o_ref[...] = acc

    return pl.pallas_call(
        body, name=name, grid=(h // r,), in_specs=[pl.BlockSpec((s, r, c), lambda i: (0, i, 0))],
        out_specs=pl.BlockSpec((r, c), lambda i: (i, 0)),
        out_shape=jax.ShapeDtypeStruct((h, c), F32), compiler_params=_params(("parallel",)),
    )(parts)


def _reduce_scatter_grads(stacked, tag, scatter=_chip_scatter):
    c = lax.axis_index("c")
    theirs = _pair_swap_halves(stacked, tag)
    parts = []
    for w, (g, t) in enumerate(zip(stacked, theirs)):
        half = g.shape[1] // 2
        mine = lax.dynamic_slice_in_dim(g, c * half, half, axis=1)
        parts.append(_add_pair(mine, t, "rs_add_pair_%s_%d" % (tag, w)))
    landed = scatter(parts)
    halves = [_sum_slots(p, "rs_sum_slots_%s_%d" % (tag, w)) for w, p in enumerate(landed)]
    return list(zip(halves, _pair_swap_reduced(halves, tag)))


def _adamw_math(w, g, m, v):
    m = ADAM_B1 * m + (1.0 - ADAM_B1) * g
    v = ADAM_B2 * v + (1.0 - ADAM_B2) * (g * g)
    m_hat = m / (1.0 - ADAM_B1 ** ADAM_STEP)
    v_hat = v / (1.0 - ADAM_B2 ** ADAM_STEP)
    return -ADAM_LR * (m_hat / (jnp.sqrt(v_hat) + ADAM_EPS) + ADAM_WD * w), m, v


def _adamw(w, g, m, v, name):
    rows, cols = w.shape
    r = _tile(rows, max(SUBLANES, (1 << 18) // cols), SUBLANES)

    def body(w_ref, g_ref, m_ref, v_ref, d_ref, mo_ref, vo_ref):
        d_ref[...], mo_ref[...], vo_ref[...] = _adamw_math(w_ref[...], g_ref[...], m_ref[...], v_ref[...])

    blk = pl.BlockSpec((r, cols), lambda i: (i, 0))
    return pl.pallas_call(
        body, name=name, grid=(rows // r,), in_specs=[blk] * 4, out_specs=[blk] * 3,
        out_shape=[jax.ShapeDtypeStruct(w.shape, F32)] * 3, compiler_params=_params(("parallel",)),
    )(w, g, m, v)


def _adamw_halves(w, g_mine, g_theirs, m, v, core, name):
    rows, cols = w.shape
    half = rows // 2
    r = _tile(half, max(SUBLANES, (1 << 18) // cols), SUBLANES)
    nbh = half // r

    def body(core_ref, w_ref, gm_ref, gt_ref, m_ref, v_ref, g_ref, d_ref, mo_ref, vo_ref):
        is_mine = (pl.program_id(0) // nbh) == core_ref[0]

        @pl.when(is_mine)
        def _():
            g_ref[...] = gm_ref[...]

        @pl.when(jnp.logical_not(is_mine))
        def _():
            g_ref[...] = gt_ref[...]

        g = g_ref[...]
        d_ref[...], mo_ref[...], vo_ref[...] = _adamw_math(w_ref[...], g, m_ref[...], v_ref[...])

    full = pl.BlockSpec((r, cols), lambda i, core_ref: (i, 0))
    part = pl.BlockSpec((r, cols), lambda i, core_ref: (i % nbh, 0))
    return pl.pallas_call(
        body, name=name,
        grid_spec=pltpu.PrefetchScalarGridSpec(num_scalar_prefetch=1, grid=(rows // r,), in_specs=[full, part, part, full, full],
                                               out_specs=[full] * 4),
        out_shape=[jax.ShapeDtypeStruct(w.shape, F32)] * 4, compiler_params=_params(("parallel",)),
    )(core, w, g_mine, g_theirs, m, v)


def _adamw_reduced(parts, w, m, v, name):
    def body(p_ref, w_ref, m_ref, v_ref, g_ref, d_ref, mo_ref, vo_ref):
        g = p_ref[0]
        for k in range(1, parts.shape[0]):
            g = g + p_ref[k]
        g_ref[...] = g
        d_ref[...], mo_ref[...], vo_ref[...] = _adamw_math(w_ref[...], g, m_ref[...], v_ref[...])

    return pl.pallas_call(
        body, name=name, in_specs=[VMEM_SPEC] * 4, out_specs=[VMEM_SPEC] * 4,
        out_shape=[jax.ShapeDtypeStruct(w.shape, F32)] * 4,
        compiler_params=pltpu.CompilerParams(vmem_limit_bytes=VMEM_LIMIT_BYTES),
    )(parts, w, m, v)


WEIGHTS = ("c_ctx", "ada_w", "ada_b", "norm1_g", "ffn1_w_in", "ffn1_w_out", "norm2_g", "mix_w_in", "ret_decay_fwd", "ret_decay_bwd",
           "mla_q_norm_g", "mla_w_uq", "mla_kv_norm_g", "mla_w_ukv", "mix_w_out", "norm3_g", "ffn2_w_in", "ffn2_w_out", "final_norm_g")
SMALL = ("c_ctx", "ada_b", "norm1_g", "norm2_g", "ret_decay_fwd", "ret_decay_bwd", "mla_q_norm_g", "mla_kv_norm_g", "norm3_g", "final_norm_g")
BIG = (("ffn1_w_in", 1), ("ffn1_w_out", 0), ("mix_w_in", 1), ("mla_w_uq", 1), ("mla_w_ukv", 1), ("mix_w_out", 0))
LATE = ("ffn2_w_in", "ffn2_w_out")


def _slots_side_by_side(stacked, slots):
    return jnp.concatenate([stacked[k] for k in slots], axis=1)


def _pack(vectors):
    flat = jnp.concatenate([v.reshape(-1) for v in vectors])
    return jnp.pad(flat, (0, -flat.shape[0] % (SUBLANES * LANES))).reshape(SUBLANES, -1)


def _rows8(a):
    return a.reshape(a.shape[0] * SUBLANES, a.shape[1] // SUBLANES)


def _unpack(packed, like):
    packed = packed.reshape(-1)
    out, off = [], 0
    for ref in like:
        out.append(packed[off:off + ref.size].reshape(ref.shape))
        off += ref.size
    return out


def kernel(x, c, ctx, c_ctx, ada_w, ada_b, norm1_g, ffn1_w_in, ffn1_w_out, norm2_g, mix_w_in, ret_decay_fwd, ret_decay_bwd, mla_q_norm_g, mla_w_uq, mla_kv_norm_g, mla_w_ukv, mix_w_out, norm3_g, ffn2_w_in, ffn2_w_out, final_norm_g, loss_target, m_c_ctx, m_ada_w, m_ada_b, m_norm1_g, m_ffn1_w_in, m_ffn1_w_out, m_norm2_g, m_mix_w_in, m_ret_decay_fwd, m_ret_decay_bwd, m_mla_q_norm_g, m_mla_w_uq, m_mla_kv_norm_g, m_mla_w_ukv, m_mix_w_out, m_norm3_g, m_ffn2_w_in, m_ffn2_w_out, m_final_norm_g, v_c_ctx, v_ada_w, v_ada_b, v_norm1_g, v_ffn1_w_in, v_ffn1_w_out, v_norm2_g, v_mix_w_in, v_ret_decay_fwd, v_ret_decay_bwd, v_mla_q_norm_g, v_mla_w_uq, v_mla_kv_norm_g, v_mla_w_ukv, v_mix_w_out, v_norm3_g, v_ffn2_w_in, v_ffn2_w_out, v_final_norm_g):
    w = dict(c_ctx=c_ctx, ada_w=ada_w, ada_b=ada_b, norm1_g=norm1_g, ffn1_w_in=ffn1_w_in, ffn1_w_out=ffn1_w_out, norm2_g=norm2_g,
             mix_w_in=mix_w_in, ret_decay_fwd=ret_decay_fwd, ret_decay_bwd=ret_decay_bwd, mla_q_norm_g=mla_q_norm_g, mla_w_uq=mla_w_uq,
             mla_kv_norm_g=mla_kv_norm_g, mla_w_ukv=mla_w_ukv, mix_w_out=mix_w_out, norm3_g=norm3_g, ffn2_w_in=ffn2_w_in,
             ffn2_w_out=ffn2_w_out, final_norm_g=final_norm_g)
    mom_m = dict(zip(WEIGHTS, (m_c_ctx, m_ada_w, m_ada_b, m_norm1_g, m_ffn1_w_in, m_ffn1_w_out, m_norm2_g, m_mix_w_in, m_ret_decay_fwd,
                               m_ret_decay_bwd, m_mla_q_norm_g, m_mla_w_uq, m_mla_kv_norm_g, m_mla_w_ukv, m_mix_w_out, m_norm3_g,
                               m_ffn2_w_in, m_ffn2_w_out, m_final_norm_g)))
    mom_v = dict(zip(WEIGHTS, (v_c_ctx, v_ada_w, v_ada_b, v_norm1_g, v_ffn1_w_in, v_ffn1_w_out, v_norm2_g, v_mix_w_in, v_ret_decay_fwd,
                               v_ret_decay_bwd, v_mla_q_norm_g, v_mla_w_uq, v_mla_kv_norm_g, v_mla_w_ukv, v_mix_w_out, v_norm3_g,
                               v_ffn2_w_in, v_ffn2_w_out, v_final_norm_g)))
    xi, yi, ci = _me()
    chip = 2 * xi + yi
    example = 2 * chip + ci
    d = x.shape[-1]
    n_mod = ada_b.shape[-1] // d

    c_all = _allgather_small(_rows8(c), ALL_PEERS, False, "gather_c").reshape(8, d)
    cond = jnp.concatenate([c_all, jnp.broadcast_to(c_ctx[None, :], (8, d))], axis=0)
    cond_act = jax.nn.silu(cond)
    n_cols = ada_w.shape[-1]
    bias = lax.dynamic_slice_in_dim(ada_b, chip * n_cols, n_cols, axis=1)
    mods_cols = _matmul(cond_act, ada_w[0], "nn", F32, "ada_fwd", add=jnp.broadcast_to(bias, (16, n_cols)))
    mods = jnp.swapaxes(_allgather_small(mods_cols, CHIP_PEERS, True, "gather_mods"), 0, 1).reshape(16, 4 * n_cols)
    mods_lat = lax.dynamic_slice_in_dim(mods, example, 1, axis=0).reshape(n_mod, d)
    mods_ctx = mods[8].reshape(n_mod, d)

    stacked = _gather_weights([w[name][0].astype(BF16) for name, _ in BIG])
    st = dict(zip([name for name, _ in BIG], stacked))
    rows = lambda s: s.reshape(-1, s.shape[-1])
    cols = _slots_side_by_side
    big = dict(ffn1_wg=cols(st["ffn1_w_in"], (0, 1)), ffn1_wu=cols(st["ffn1_w_in"], (2, 3)), ffn1_wo=rows(st["ffn1_w_out"]),
               mix_in=jnp.pad(cols(st["mix_w_in"], (0, 1, 2, 3)), ((0, 0), (0, MIX_IN_PAD - MIX_IN))),
               w_uq=cols(st["mla_w_uq"], (0, 1, 2, 3)), w_ukv=cols(st["mla_w_ukv"], (0, 1, 2, 3)), mix_out=rows(st["mix_w_out"]))
    small = {k: w[k] for k in ("norm1_g", "norm2_g", "norm3_g", "final_norm_g", "mla_q_norm_g", "mla_kv_norm_g", "ret_decay_fwd", "ret_decay_bwd")}
    late = tuple(w[name][0] for name in LATE)

    loss_mine, (dx, dmods_lat, dmods_ctx, dsmall, dbig, dlate) = jax.value_and_grad(_local_loss, argnums=(0, 1, 2, 3, 4, 5))(
        x[0], mods_lat, mods_ctx, small, big, late, ctx[0], loss_target[0])

    dmods = _allgather_small(_rows8(jnp.stack([dmods_lat.reshape(-1), dmods_ctx.reshape(-1)])), ALL_PEERS, False, "gather_dmods")
    dmods = dmods.reshape(8, 2, n_mod * d)
    dmods_rows = jnp.concatenate([dmods[:, 0, :], dmods[:, 1, :]], axis=0)
    dmods_cols = lax.dynamic_slice_in_dim(dmods_rows, chip * n_cols, n_cols, axis=1)
    g_ada_w = _matmul(cond_act, dmods_cols, "tn", F32, "ada_dw")
    dcond_act = _matmul(dmods_cols, ada_w[0], "nt", F32, "ada_dcond")
    sig = jax.nn.sigmoid(c_ctx)
    dc_ctx = jnp.sum(dcond_act[8:], axis=0) * (sig * (1.0 + c_ctx * (1.0 - sig)))
    share = dict(dsmall)
    share["c_ctx"] = jnp.where(ci == 0, dc_ctx, jnp.zeros_like(dc_ctx))
    share["ada_b"] = (dmods_lat + dmods_ctx).reshape(1, -1)
    zero = jnp.zeros((1,), F32)
    parts = _allgather_small(_pack([share[k] for k in SMALL] + [loss_mine.reshape(1)]), ALL_PEERS, False, "gather_small_grads")
    packed = _adamw_reduced(parts, _pack([w[k] for k in SMALL] + [zero]), _pack([mom_m[k] for k in SMALL] + [zero]),
                            _pack([mom_v[k] for k in SMALL] + [zero]), "adamw_small")
    like = [w[k] for k in SMALL] + [zero]
    grads, deltas, new_m, new_v = ({k: a for k, a in zip(SMALL + ("loss",), _unpack(p, like))} for p in packed)
    loss = grads.pop("loss").reshape(())

    def col_slots(parts, n):
        return jnp.stack([p[:, k * (p.shape[1] // n):(k + 1) * (p.shape[1] // n)] for p in parts for k in range(n)])

    row_slots = lambda g: g.reshape(4, g.shape[0] // 4, g.shape[1])
    dstacked = [col_slots([dbig["ffn1_wg"], dbig["ffn1_wu"]], 2), row_slots(dbig["ffn1_wo"]), col_slots([dbig["mix_in"][:, :MIX_IN]], 4),
                col_slots([dbig["w_uq"]], 4), col_slots([dbig["w_ukv"]], 4), row_slots(dbig["mix_out"])]
    core = ci.astype(jnp.int32).reshape(1)
    for (name, _), (g_mine, g_theirs) in zip(BIG, _reduce_scatter_grads(dstacked, "main")):
        g, dl, mo, vo = _adamw_halves(w[name][0], g_mine, g_theirs, mom_m[name][0], mom_v[name][0], core, "adamw_" + name)
        grads[name], deltas[name], new_m[name], new_v[name] = g[None], dl[None], mo[None], vo[None]
    for name, g in (("ada_w", g_ada_w),) + tuple(zip(LATE, dlate)):
        dl, mo, vo = _adamw(w[name][0], g, mom_m[name][0], mom_v[name][0], "adamw_" + name)
        grads[name], deltas[name], new_m[name], new_v[name] = g[None], dl[None], mo[None], vo[None]

    return (loss, dx[None], *[grads[k] for k in WEIGHTS], *[deltas[k] for k in WEIGHTS], *[new_m[k] for k in WEIGHTS],
            *[new_v[k] for k in WEIGHTS])
```

```python
import functools

import jax
import jax.numpy as jnp
import numpy as np
from jax import lax
from jax.experimental import pallas as pl
from jax.experimental.pallas import tpu as pltpu

F32 = jnp.float32
BF16 = jnp.bfloat16
MESH = pl.DeviceIdType.MESH

VMEM_LIMIT_BYTES = 52 * 1024 * 1024
LANES = 128
SUBLANES = 8

D_FF_SPLIT = 2
RET_HEADS, RET_DK, RET_DV, RET_CHUNK = 8, 64, 128, 128
MLA_HEADS, MLA_Q_RANK, MLA_KV_RANK, MLA_NOPE, MLA_ROPE, MLA_V = 8, 512, 256, 128, 64, 128
GRID_W = 64
ROPE_BASE = 10000.0
RMS_EPS = 1e-6
GN_EPS = 1e-5
MIX_SPLITS = (RET_HEADS * RET_DK, RET_HEADS * RET_DK, RET_HEADS * RET_DV, RET_HEADS * RET_DV,
              MLA_Q_RANK, MLA_KV_RANK, MLA_ROPE)
MIX_IN = sum(MIX_SPLITS)
MIX_IN_PAD = 4096
ADAM_LR, ADAM_B1, ADAM_B2, ADAM_EPS, ADAM_WD, ADAM_STEP = 0.001, 0.9, 0.999, 1e-08, 0.01, 10


def _tile(n, pref, align):
    best = None
    t = align
    while t <= min(n, pref):
        if n % t == 0:
            best = t
        t += align
    return n if best is None else best


def _params(sem=None):
    return pltpu.CompilerParams(dimension_semantics=sem, vmem_limit_bytes=VMEM_LIMIT_BYTES)


def _matmul(a, b, mode, out_dtype, name, add=None, tm=1024, tn=1024, tk=2048):
    if mode == "nn":
        (m, k), (k2, n) = a.shape, b.shape
        dims = (((1,), (0,)), ((), ()))
    elif mode == "nt":
        (m, k), (n, k2) = a.shape, b.shape
        dims = (((1,), (1,)), ((), ()))
    else:
        (k, m), (k2, n) = a.shape, b.shape
        dims = (((0,), (0,)), ((), ()))
    assert k == k2, (a.shape, b.shape, mode)
    tm = _tile(m, tm, LANES if mode == "tn" else 16)
    tn = _tile(n, tn, LANES)
    tk = _tile(k, tk, LANES if mode != "tn" else 16)
    nk = k // tk
    a_spec = pl.BlockSpec((tk, tm), lambda i, j, kk: (kk, i)) if mode == "tn" else pl.BlockSpec((tm, tk), lambda i, j, kk: (i, kk))
    b_spec = pl.BlockSpec((tn, tk), lambda i, j, kk: (j, kk)) if mode == "nt" else pl.BlockSpec((tk, tn), lambda i, j, kk: (kk, j))
    o_spec = pl.BlockSpec((tm, tn), lambda i, j, kk: (i, j))
    has_add = add is not None

    def body(*refs):
        a_ref, b_ref = refs[0], refs[1]
        add_ref = refs[2] if has_add else None
        o_ref = refs[2 + has_add]
        p = lax.dot_general(a_ref[...].astype(BF16), b_ref[...].astype(BF16), dims, preferred_element_type=F32)
        if nk == 1:
            if has_add:
                p = p + add_ref[...].astype(F32)
            o_ref[...] = p.astype(out_dtype)
        else:
            acc = refs[3 + has_add]
            kk = pl.program_id(2)

            @pl.when(kk == 0)
            def _():
                acc[...] = p + add_ref[...].astype(F32) if has_add else p

            @pl.when(kk > 0)
            def _():
                acc[...] += p

            @pl.when(kk == nk - 1)
            def _():
                o_ref[...] = acc[...].astype(out_dtype)

    return pl.pallas_call(
        body, name=name, grid=(m // tm, n // tn, nk),
        in_specs=[a_spec, b_spec] + ([o_spec] if has_add else []),
        out_specs=o_spec,
        out_shape=jax.ShapeDtypeStruct((m, n), out_dtype),
        scratch_shapes=[pltpu.VMEM((tm, tn), F32)] if nk > 1 else [],
        compiler_params=_params(("parallel", "parallel", "arbitrary")),
    )(*((a, b, add) if has_add else (a, b)))


def _row_tile(t, n_ctx, d):
    pref = max(SUBLANES, min(256, (1 << 19) // d))
    r = _tile(int(np.gcd(t, n_ctx)) if n_ctx else t, pref, SUBLANES)
    return r, (n_ctx // r if n_ctx else 0)


def _seg_map(nct):
    if nct:
        return lambda i: (jnp.minimum(i // nct, 1), 0, 0)
    return lambda i: (0, 0, 0)


def _normmod_fwd(x, g, shift, scale, n_ctx, name):
    t, d = x.shape
    r, nct = _row_tile(t, n_ctx, d)

    def body(x_ref, g_ref, sh_ref, sc_ref, h_ref, ht_ref):
        xv = x_ref[...]
        rstd = lax.rsqrt(jnp.mean(xv * xv, axis=-1, keepdims=True) + RMS_EPS)
        n = xv * rstd * g_ref[...]
        h = n * (1.0 + sc_ref[0]) + sh_ref[0]
        h_ref[...] = h.astype(BF16)
        ht_ref[...] = h.T.astype(BF16)

    row = pl.BlockSpec((r, d), lambda i: (i, 0))
    seg = pl.BlockSpec((1, 1, d), _seg_map(nct))
    return pl.pallas_call(
        body, name=name, grid=(t // r,),
        in_specs=[row, pl.BlockSpec((1, d), lambda i: (0, 0)), seg, seg],
        out_specs=[row, pl.BlockSpec((d, r), lambda i: (0, i))],
        out_shape=[jax.ShapeDtypeStruct((t, d), BF16), jax.ShapeDtypeStruct((d, t), BF16)],
        compiler_params=_params(("parallel",)),
    )(x, g, shift, scale)


def _normmod_bwd(dh, x, g, shift, scale, dres, n_ctx, name):
    t, d = x.shape
    r, nct = _row_tile(t, n_ctx, d)
    has_res = dres is not None
    nseg = shift.shape[0]

    def body(*refs):
        dh_ref, x_ref, g_ref, sh_ref, sc_ref = refs[:5]
        dres_ref = refs[5] if has_res else None
        dx_ref, dg_ref, dsh_ref, dsc_ref = refs[5 + has_res:]
        i = pl.program_id(0)
        xv = x_ref[...]
        dhv = dh_ref[...].astype(F32)
        rstd = lax.rsqrt(jnp.mean(xv * xv, axis=-1, keepdims=True) + RMS_EPS)
        y = xv * rstd
        gv = g_ref[...]
        dn = dhv * (1.0 + sc_ref[0])
        dy = dn * gv
        dx = rstd * (dy - y * jnp.mean(dy * y, axis=-1, keepdims=True))
        if has_res:
            dx = dx + dres_ref[...]
        dx_ref[...] = dx

        @pl.when(i == 0)
        def _():
            dg_ref[...] = jnp.zeros_like(dg_ref)

        @pl.when(jnp.logical_or(i == 0, i == nct))
        def _():
            dsh_ref[...] = jnp.zeros_like(dsh_ref)
            dsc_ref[...] = jnp.zeros_like(dsc_ref)

        dg_ref[...] += jnp.sum(dn * y, axis=0, keepdims=True)
        dsh_ref[0] += jnp.sum(dhv, axis=0, keepdims=True)
        dsc_ref[0] += jnp.sum(dhv * (y * gv), axis=0, keepdims=True)

    row = pl.BlockSpec((r, d), lambda i: (i, 0))
    seg = pl.BlockSpec((1, 1, d), _seg_map(nct))
    vec = pl.BlockSpec((1, d), lambda i: (0, 0))
    return pl.pallas_call(
        body, name=name, grid=(t // r,),
        in_specs=[row, row, vec, seg, seg] + ([row] if has_res else []),
        out_specs=[row, vec, seg, seg],
        out_shape=[jax.ShapeDtypeStruct((t, d), F32), jax.ShapeDtypeStruct((1, d), F32),
                   jax.ShapeDtypeStruct((nseg, 1, d), F32), jax.ShapeDtypeStruct((nseg, 1, d), F32)],
        compiler_params=_params(("arbitrary",)),
    )(*((dh, x, g, shift, scale, dres) if has_res else (dh, x, g, shift, scale)))


def _gated_res_fwd(x, y, gate, coef, n_ctx, name):
    t, d = x.shape
    r, nct = _row_tile(t, n_ctx, d)

    def body(x_ref, y_ref, gt_ref, o_ref):
        o_ref[...] = x_ref[...] + (coef * gt_ref[0]) * y_ref[...]

    row = pl.BlockSpec((r, d), lambda i: (i, 0))
    return pl.pallas_call(
        body, name=name, grid=(t // r,),
        in_specs=[row, row, pl.BlockSpec((1, 1, d), _seg_map(nct))],
        out_specs=row, out_shape=jax.ShapeDtypeStruct((t, d), F32),
        compiler_params=_params(("parallel",)),
    )(x, y, gate)


def _matmul_gated_res(a, w, x, gate, coef, name, tm):
    t, k = a.shape
    d = w.shape[1]
    tm = _tile(t, tm, 16)
    tn = _tile(d, 512, LANES)

    def body(a_ref, w_ref, x_ref, gt_ref, o_ref, y_ref):
        y = jnp.dot(a_ref[...], w_ref[...], preferred_element_type=F32)
        o_ref[...] = x_ref[...] + (coef * gt_ref[0]) * y
        y_ref[...] = y.astype(BF16)

    blk = pl.BlockSpec((tm, tn), lambda i, j: (i, j))
    return pl.pallas_call(
        body, name=name, grid=(t // tm, d // tn),
        in_specs=[pl.BlockSpec((tm, k), lambda i, j: (i, 0)), pl.BlockSpec((k, tn), lambda i, j: (0, j)), blk,
                  pl.BlockSpec((1, 1, tn), lambda i, j: (0, 0, j))],
        out_specs=[blk, blk], out_shape=[jax.ShapeDtypeStruct((t, d), F32), jax.ShapeDtypeStruct((t, d), BF16)],
        compiler_params=_params(("parallel", "parallel")),
    )(a, w, x, gate)


def _gated_res_bwd(dout, y, gate, coef, n_ctx, name):
    t, d = dout.shape
    r, nct = _row_tile(t, n_ctx, d)
    nseg = gate.shape[0]

    def body(do_ref, y_ref, gt_ref, dy_ref, dgt_ref):
        i = pl.program_id(0)
        dov = do_ref[...] * coef
        dy_ref[...] = (dov * gt_ref[0]).astype(BF16)

        @pl.when(jnp.logical_or(i == 0, i == nct))
        def _():
            dgt_ref[...] = jnp.zeros_like(dgt_ref)

        dgt_ref[0] += jnp.sum(dov * y_ref[...], axis=0, keepdims=True)

    row = pl.BlockSpec((r, d), lambda i: (i, 0))
    seg = pl.BlockSpec((1, 1, d), _seg_map(nct))
    return pl.pallas_call(
        body, name=name, grid=(t // r,),
        in_specs=[row, row, seg], out_specs=[row, seg],
        out_shape=[jax.ShapeDtypeStruct((t, d), BF16), jax.ShapeDtypeStruct((nseg, 1, d), F32)],
        compiler_params=_params(("arbitrary",)),
    )(dout, y, gate)


def _swiglu_matmul(h, wg, wu, name, tm, shards=()):
    t, k = h.shape
    f = wg.shape[1]
    tm = _tile(t, tm, LANES)
    tn = _tile(f, 512, LANES)
    n = len(shards)
    grid = (t // tm, f // tn)

    def body(h_ref, wg_ref, wu_ref, *rest):
        ins, (g_ref, u_ref, a_ref, at_ref), outs, sems = rest[:n], rest[n:n + 4], rest[n + 4:2 * n + 4], rest[2 * n + 4:]
        if n:
            first, last = _first_last_step(grid)
            pl.when(first)(lambda: _gather_start(ins, outs, sems))
        hv = h_ref[...]
        g = jnp.dot(hv, wg_ref[...], preferred_element_type=F32)
        u = jnp.dot(hv, wu_ref[...], preferred_element_type=F32)
        a = g * jax.nn.sigmoid(g) * u
        g_ref[...] = g.astype(BF16)
        u_ref[...] = u.astype(BF16)
        a_ref[...] = a.astype(BF16)
        at_ref[...] = a.T.astype(BF16)
        if n:
            pl.when(last)(lambda: _gather_finish(ins, outs, sems))

    w_spec = pl.BlockSpec((k, tn), lambda i, j: (0, j))
    o_spec = pl.BlockSpec((tm, tn), lambda i, j: (i, j))
    return pl.pallas_call(
        body, name=name, grid=grid,
        in_specs=[pl.BlockSpec((tm, k), lambda i, j: (i, 0)), w_spec, w_spec] + [HBM_SPEC] * n,
        out_specs=[o_spec, o_spec, o_spec, pl.BlockSpec((tn, tm), lambda i, j: (j, i))] + [HBM_SPEC] * n,
        out_shape=[jax.ShapeDtypeStruct((t, f), BF16)] * 3 + [jax.ShapeDtypeStruct((f, t), BF16)] + _gather_out_shapes(shards),
        scratch_shapes=_gather_sems(n) if n else [],
        compiler_params=_params(("arbitrary", "arbitrary") if n else ("parallel", "parallel")),
    )(h, wg, wu, *shards)


def _swiglu_bwd_matmul(dy, wo, gg, uu, name, tm, parts=()):
    t, d = dy.shape
    f = wo.shape[0]
    tm = _tile(t, tm, 16)
    tn = _tile(f, 512, LANES)
    n = len(parts)
    grid = (t // tm, f // tn)

    def body(dy_ref, wo_ref, g_ref, u_ref, *rest):
        ins, (dg_ref, du_ref), outs, sems = rest[:n], rest[n:n + 2], rest[n + 2:2 * n + 2], rest[2 * n + 2:]
        if n:
            first, last = _first_last_step(grid)
            pl.when(first)(lambda: _scatter_start(ins, outs, sems))
        da = lax.dot_general(dy_ref[...], wo_ref[...], _NT, preferred_element_type=F32)
        gv = g_ref[...].astype(F32)
        sg = jax.nn.sigmoid(gv)
        dg_ref[...] = (da * u_ref[...].astype(F32) * (sg * (1.0 + gv * (1.0 - sg)))).astype(BF16)
        du_ref[...] = (da * (gv * sg)).astype(BF16)
        if n:
            pl.when(last)(lambda: _scatter_finish(ins, outs, sems))

    blk = pl.BlockSpec((tm, tn), lambda i, j: (i, j))
    return pl.pallas_call(
        body, name=name, grid=grid,
        in_specs=[pl.BlockSpec((tm, d), lambda i, j: (i, 0)), pl.BlockSpec((tn, d), lambda i, j: (j, 0)), blk, blk] + [HBM_SPEC] * n,
        out_specs=[blk, blk] + [HBM_SPEC] * n,
        out_shape=[jax.ShapeDtypeStruct((t, f), BF16)] * 2 + [jax.ShapeDtypeStruct(p.shape, p.dtype) for p in parts],
        scratch_shapes=_scatter_sems(n) if n else [],
        compiler_params=_params(("arbitrary", "arbitrary") if n else ("parallel", "parallel")),
    )(dy, wo, gg, uu, *parts)


def _tok_tile(t):
    return 1024 if t % 1024 == 0 else 768 if t % 768 == 0 else _tile(t, 1024, 16)


FF_TILE = 1408
FULL_K = 1 << 30


def _make_ffn(n_ctx, tag, has_late):
    @jax.custom_vjp
    def ffn(x, g, shift, scale, gate, wg, wu, wo, late):
        return fwd(x, g, shift, scale, gate, wg, wu, wo, late)[0]

    def fwd(x, g, shift, scale, gate, wg, wu, wo, late):
        tt = _tok_tile(x.shape[0])
        h, ht = _normmod_fwd(x, g, shift, scale, n_ctx, tag + "_norm")
        blocks = [s.astype(BF16) for s in late]
        gg, uu, a, at, *stacked = _swiglu_matmul(h, wg, wu, tag + "_mm_gu", tt, blocks)
        stacked = _fill_own_slot(stacked, blocks)
        if has_late:
            wo = stacked[0].reshape(-1, stacked[0].shape[-1])
        if n_ctx:
            y = _matmul(a, wo, "nn", F32, tag + "_mm_o", tm=tt, tn=512, tk=FULL_K)
            out = _gated_res_fwd(x, y, gate, 0.5, n_ctx, tag + "_res")
        else:
            out, y = _matmul_gated_res(a, wo, x, gate, 0.5, tag + "_mm_o_res", tt)
        return (out, tuple(stacked[1:])), (x, g, shift, scale, gate, wg, wu, wo, ht, gg, uu, at, y)

    def bwd(res, cts):
        x, g, shift, scale, gate, wg, wu, wo, ht, gg, uu, at, y = res
        dout, dgathered = cts
        tt = _tok_tile(x.shape[0])
        dy, dgate = _gated_res_bwd(dout, y, gate, 0.5, n_ctx, tag + "_res_b")
        dwo = _matmul(at, dy, "nn", BF16, tag + "_dwo", tm=512, tn=512, tk=FULL_K)
        if has_late:
            own = {}

            def scatter(parts):
                own["dgg"], own["duu"], *landed = _swiglu_bwd_matmul(dy, wo, gg, uu, tag + "_da_act", tt, parts)
                return landed

            top = lax.axis_index("c") == 0
            pending = [dwo.reshape(4, dwo.shape[0] // 4, dwo.shape[1])] + list(dgathered)
            dlate = tuple(jnp.concatenate([jnp.where(top, mine, theirs), jnp.where(top, theirs, mine)], axis=0)
                          for mine, theirs in _reduce_scatter_grads(pending, tag, scatter))
            dgg, duu, dwo = own["dgg"], own["duu"], None
        else:
            dgg, duu = _swiglu_bwd_matmul(dy, wo, gg, uu, tag + "_da_act", tt)
            dlate = ()
        dwg = _matmul(ht, dgg, "nn", BF16, tag + "_dwg", tm=512, tn=512, tk=FULL_K)
        dwu = _matmul(ht, duu, "nn", BF16, tag + "_dwu", tm=512, tn=512, tk=FULL_K)
        dh = _matmul(dgg, wg, "nt", F32, tag + "_dh_g", tm=tt, tn=512, tk=FULL_K)
        dh = _matmul(duu, wu, "nt", F32, tag + "_dh_u", add=dh, tm=tt, tn=512, tk=FULL_K)
        dx, dg, dshift, dscale = _normmod_bwd(dh, x, g, shift, scale, dout, n_ctx, tag + "_norm_b")
        return dx, dg, dshift, dscale, dgate, dwg, dwu, dwo, dlate

    ffn.defvjp(fwd, bwd)
    return ffn


def _make_normmod_linear(n_ctx, tag):
    @jax.custom_vjp
    def op(x, g, shift, scale, w):
        return fwd(x, g, shift, scale, w)[0]

    def fwd(x, g, shift, scale, w):
        h, ht = _normmod_fwd(x, g, shift, scale, n_ctx, tag + "_norm")
        y = _matmul(h, w, "nn", F32, tag + "_mm", tm=_tok_tile(x.shape[0]))
        return y, (x, g, shift, scale, w, ht)

    def bwd(res, dy):
        x, g, shift, scale, w, ht = res
        tt = _tok_tile(x.shape[0])
        dyb = dy.astype(BF16)
        dw = _matmul(ht, dyb, "nn", BF16, tag + "_dw", tm=512, tn=512, tk=FULL_K)
        dh = _matmul(dyb, w, "nt", F32, tag + "_dh", tm=tt, tn=512, tk=FULL_K)
        dx, dg, dshift, dscale = _normmod_bwd(dh, x, g, shift, scale, None, n_ctx, tag + "_norm_b")
        return dx, dg, dshift, dscale, dw

    op.defvjp(fwd, bwd)
    return op


def _make_linear_gated_res(tag):
    @jax.custom_vjp
    def op(x, a, w, gate):
        return fwd(x, a, w, gate)[0]

    def fwd(x, a, w, gate):
        ab = a.astype(BF16)
        out, y = _matmul_gated_res(ab, w, x, gate, 1.0, tag + "_mm_res", _tok_tile(x.shape[0]))
        return out, (ab, w, gate, y)

    def bwd(res, dout):
        ab, w, gate, y = res
        tt = _tok_tile(dout.shape[0])
        dy, dgate = _gated_res_bwd(dout, y, gate, 1.0, 0, tag + "_res_b")
        dw = _matmul(ab.T, dy, "nn", BF16, tag + "_dw", tm=512, tn=512, tk=FULL_K)
        da = _matmul(dy, w, "nt", F32, tag + "_da", tm=tt)
        return dout, da, dw, dgate

    op.defvjp(fwd, bwd)
    return op


def _final_loss_call(x, g, target, name):
    t, d = x.shape
    r = _tile(t, 256, SUBLANES)

    def body(x_ref, g_ref, t_ref, loss_ref, dx_ref, dg_ref):
        i = pl.program_id(0)
        xv = x_ref[...]
        gv = g_ref[...]
        rstd = lax.rsqrt(jnp.mean(xv * xv, axis=-1, keepdims=True) + RMS_EPS)
        xh = xv * rstd
        e = xh * gv - t_ref[...]
        dy = e * (1.0 / d)
        dn = dy * gv
        dx_ref[...] = rstd * (dn - xh * jnp.mean(dn * xh, axis=-1, keepdims=True))

        @pl.when(i == 0)
        def _():
            loss_ref[...] = jnp.zeros_like(loss_ref)
            dg_ref[...] = jnp.zeros_like(dg_ref)

        loss_ref[...] += 0.5 * jnp.sum(jnp.mean(e * e, axis=-1, keepdims=True), axis=0, keepdims=True)
        dg_ref[...] += jnp.sum(dy * xh, axis=0, keepdims=True)

    row = pl.BlockSpec((r, d), lambda i: (i, 0))
    vec = pl.BlockSpec((1, d), lambda i: (0, 0))
    return pl.pallas_call(
        body, name=name, grid=(t // r,),
        in_specs=[row, vec, row], out_specs=[pl.BlockSpec((1, 1), lambda i: (0, 0)), row, vec],
        out_shape=[jax.ShapeDtypeStruct((1, 1), F32), jax.ShapeDtypeStruct((t, d), F32), jax.ShapeDtypeStruct((1, d), F32)],
        compiler_params=_params(("arbitrary",)),
    )(x, g, target)


@jax.custom_vjp
def _final_loss(x, g, target):
    return _final_loss_call(x, g, target, "final_loss")[0][0, 0]


def _final_loss_fwd(x, g, target):
    loss, dx, dg = _final_loss_call(x, g, target, "final_loss")
    return loss[0, 0], (dx, dg, target)


def _final_loss_bwd(res, dl):
    dx, dg, target = res
    return dx * dl, dg * dl, jnp.zeros_like(target)


_final_loss.defvjp(_final_loss_fwd, _final_loss_bwd)


_NT = (((1,), (1,)), ((), ()))
_TN = (((0,), (0,)), ((), ()))
ATTN_Q_TILE = 512
ATTN_K_CHUNK = 1408


def _first_last_step(grid):
    ids = [pl.program_id(a) for a in range(len(grid))]
    first = functools.reduce(jnp.logical_and, [i == 0 for i in ids])
    last = functools.reduce(jnp.logical_and, [i == g - 1 for i, g in zip(ids, grid)])
    return first, last


def _attn_fwd_call(q, k, v, scale, shards=()):
    h, nq, dq = q.shape
    nk, dv = v.shape[1], v.shape[2]
    tq = _tile(nq, ATTN_Q_TILE, 16)
    ck = _tile(nk, ATTN_K_CHUNK, LANES)
    nchunk = nk // ck
    n = len(shards)
    grid = (h, nq // tq)

    exp2_scale = scale * float(np.log2(np.e))

    def body(q_ref, k_ref, v_ref, *rest):
        ins, (o_ref, lse_ref), outs = rest[:n], rest[n:n + 2], rest[n + 2:2 * n + 2]
        m_scr, l_scr, acc_scr = rest[2 * n + 2:2 * n + 5]
        sems = rest[2 * n + 5:]
        if n:
            first, last = _first_last_step(grid)
            pl.when(first)(lambda: _gather_start(ins, outs, sems))
        qv = q_ref[0]
        m_scr[...] = jnp.full_like(m_scr, -jnp.inf)
        l_scr[...] = jnp.zeros_like(l_scr)
        acc_scr[...] = jnp.zeros_like(acc_scr)

        scores = lambda c: lax.dot_general(qv, k_ref[0, c * ck:(c + 1) * ck, :], _NT, preferred_element_type=F32)
        s_next = scores(0)
        for c in range(nchunk):
            s = s_next
            if c + 1 < nchunk:
                s_next = scores(c + 1)
            m_old = m_scr[...]
            m_new = jnp.maximum(m_old, jnp.max(s, axis=-1, keepdims=True))
            alpha = jnp.exp2((m_old - m_new) * exp2_scale)
            p = jnp.exp2((s - m_new) * exp2_scale)
            l_scr[...] = alpha * l_scr[...] + jnp.sum(p, axis=-1, keepdims=True)
            acc_scr[...] = alpha * acc_scr[...] + jnp.dot(p.astype(BF16), v_ref[0, c * ck:(c + 1) * ck, :], preferred_element_type=F32)
            m_scr[...] = m_new

        o_ref[0] = acc_scr[...] / l_scr[...]
        lse_ref[0] = m_scr[...] * scale + jnp.log(l_scr[...])
        if n:
            pl.when(last)(lambda: _gather_finish(ins, outs, sems))

    keys = lambda d: pl.BlockSpec((1, nk, d), lambda hh, i: (hh, 0, 0))
    return pl.pallas_call(
        body, name="attn_fwd", grid=grid,
        in_specs=[pl.BlockSpec((1, tq, dq), lambda hh, i: (hh, i, 0)), keys(dq), keys(dv)] + [HBM_SPEC] * n,
        out_specs=[pl.BlockSpec((1, tq, dv), lambda hh, i: (hh, i, 0)),
                   pl.BlockSpec((1, tq, 1), lambda hh, i: (hh, i, 0))] + [HBM_SPEC] * n,
        out_shape=[jax.ShapeDtypeStruct((h, nq, dv), F32), jax.ShapeDtypeStruct((h, nq, 1), F32)] + _gather_out_shapes(shards),
        scratch_shapes=[pltpu.VMEM((tq, 1), F32), pltpu.VMEM((tq, 1), F32), pltpu.VMEM((tq, dv), F32)] + (_gather_sems(n) if n else []),
        compiler_params=_params(("arbitrary", "arbitrary") if n else ("parallel", "arbitrary")),
    )(q, k, v, *shards)


def _attn_bwd_call(q, k, v, o, do, lse, scale, parts=()):
    h, nq, dq = q.shape
    nk, dv = v.shape[1], v.shape[2]
    tq = _tile(nq, ATTN_Q_TILE, 16)
    ck = _tile(nk, ATTN_K_CHUNK, LANES)
    nchunk = nk // ck
    n = len(parts)
    grid = (h, nq // tq)

    log2e = float(np.log2(np.e))
    exp2_scale = scale * log2e

    def body(q_ref, k_ref, v_ref, o_ref, do_ref, lse_ref, *rest):
        ins, (dq_ref, dk_ref, dv_ref), outs, sems = rest[:n], rest[n:n + 3], rest[n + 3:2 * n + 3], rest[2 * n + 3:]
        if n:
            first, last = _first_last_step(grid)
            pl.when(first)(lambda: _scatter_start(ins, outs, sems))
        i = pl.program_id(1)

        @pl.when(i == 0)
        def _():
            dk_ref[...] = jnp.zeros_like(dk_ref)
            dv_ref[...] = jnp.zeros_like(dv_ref)

        qv = q_ref[0]
        dov = do_ref[0]
        dob = dov.astype(BF16)
        delta = jnp.sum(dov * o_ref[0], axis=-1, keepdims=True)
        lse2 = lse_ref[0] * log2e

        def scores(c):
            rows = slice(c * ck, (c + 1) * ck)
            return (lax.dot_general(qv, k_ref[0, rows, :], _NT, preferred_element_type=F32),
                    lax.dot_general(dob, v_ref[0, rows, :], _NT, preferred_element_type=F32))

        nxt = scores(0)
        dq_acc = None
        for c in range(nchunk):
            rows = slice(c * ck, (c + 1) * ck)
            s, dp = nxt
            if c + 1 < nchunk:
                nxt = scores(c + 1)
            p = jnp.exp2(s * exp2_scale - lse2)
            ds = (p * (dp - delta) * scale).astype(BF16)
            dv_ref[0, rows, :] += lax.dot_general(p.astype(BF16), dob, _TN, preferred_element_type=F32)
            dk_ref[0, rows, :] += lax.dot_general(ds, qv, _TN, preferred_element_type=F32)
            part = jnp.dot(ds, k_ref[0, rows, :], preferred_element_type=F32)
            dq_acc = part if dq_acc is None else dq_acc + part
        dq_ref[0] = dq_acc
        if n:
            pl.when(last)(lambda: _scatter_finish(ins, outs, sems))

    qspec = lambda d: pl.BlockSpec((1, tq, d), lambda hh, i: (hh, i, 0))
    kspec = lambda d: pl.BlockSpec((1, nk, d), lambda hh, i: (hh, 0, 0), pipeline_mode=pl.Buffered(1))
    return pl.pallas_call(
        body, name="attn_bwd", grid=grid,
        in_specs=[qspec(dq), kspec(dq), kspec(dv), qspec(dv), qspec(dv), qspec(1)] + [HBM_SPEC] * n,
        out_specs=[qspec(dq), kspec(dq), kspec(dv)] + [HBM_SPEC] * n,
        out_shape=[jax.ShapeDtypeStruct((h, nq, dq), F32), jax.ShapeDtypeStruct((h, nk, dq), F32),
                   jax.ShapeDtypeStruct((h, nk, dv), F32)] + [jax.ShapeDtypeStruct(p.shape, p.dtype) for p in parts],
        scratch_shapes=_scatter_sems(n) if n else [],
        compiler_params=_params(("arbitrary", "arbitrary") if n else ("parallel", "arbitrary")),
    )(q, k, v, o, do, lse, *parts)


@jax.custom_vjp
def _attention_gather(q, k, v, shards):
    return _attention_gather_fwd(q, k, v, shards)[0]


def _attention_gather_fwd(q, k, v, shards):
    scale = q.shape[-1] ** -0.5
    qb, kb, vb = q.astype(BF16), k.astype(BF16), v.astype(BF16)
    blocks = [s.astype(BF16) for s in shards]
    o, lse, *stacked = _attn_fwd_call(qb, kb, vb, scale, blocks)
    return (o, tuple(_fill_own_slot(stacked, blocks))), (qb, kb, vb, o, lse)


def _attention_gather_bwd(res, cts):
    qb, kb, vb, o, lse = res
    do, dstacked = cts
    scale = qb.shape[-1] ** -0.5
    if not dstacked:
        return (*_attn_bwd_call(qb, kb, vb, o, do, lse, scale), ())
    own = {}

    def scatter(parts):
        own["dq"], own["dk"], own["dv"], *landed = _attn_bwd_call(qb, kb, vb, o, do, lse, scale, parts)
        return landed

    top = lax.axis_index("c") == 0
    grads = tuple(jnp.concatenate([jnp.where(top, mine, theirs), jnp.where(top, theirs, mine)], axis=0)
                  for mine, theirs in _reduce_scatter_grads(list(dstacked), "attn", scatter))
    return own["dq"], own["dk"], own["dv"], grads


_attention_gather.defvjp(_attention_gather_fwd, _attention_gather_bwd)


def _attention(q, k, v):
    return _attention_gather(q, k, v, ())[0]


RET_UNROLL = 4


def _bf(x):
    return x.astype(BF16)


def _dot(a, b, dims=(((1,), (0,)), ((), ()))):
    return lax.dot_general(_bf(a), _bf(b), dims, preferred_element_type=F32)


def _sum_all(x):
    return jnp.sum(jnp.sum(x, axis=1, keepdims=True), axis=0, keepdims=True)


def _ret_consts(lgf_ref, lgb_ref):
    c = RET_CHUNK
    lgf = lgf_ref[0][:, :1]
    lgb = lgb_ref[0][:, :1]
    diff = (lax.broadcasted_iota(jnp.int32, (c, c), 0) - lax.broadcasted_iota(jnp.int32, (c, c), 1)).astype(F32)
    mf = diff >= 0
    dmat = jnp.where(mf, jnp.exp(lgf * jnp.where(mf, diff, 0.0)), jnp.exp(lgb * jnp.where(mf, 0.0, -diff)))
    col = lax.broadcasted_iota(jnp.int32, (c, 1), 0).astype(F32)
    return dict(diff=diff, mf=mf, dmat=dmat, col=col,
                xif=jnp.exp(lgf * (col + 1.0)), zf=jnp.exp(lgf * (c - 1.0 - col)),
                xib=jnp.exp(lgb * (c - col)), zb=jnp.exp(lgb * col),
                gf=jnp.exp(lgf * c), gb=jnp.exp(lgb * c))


def _ret_rows(n):
    return pl.ds(pl.multiple_of(n * RET_CHUNK, RET_CHUNK), RET_CHUNK)


def _ret_fwd_call(q, k, v, lgf, lgb, s0f, s0b):
    h, n_tok, dk = q.shape
    dv = v.shape[-1]
    nc = n_tok // RET_CHUNK

    def body(q_ref, k_ref, v_ref, lgf_ref, lgb_ref, s0f_ref, s0b_ref, y_ref, sff_ref, sbf_ref, sb_scr):
        cs = _ret_consts(lgf_ref, lgb_ref)

        sbf_ref[0] = s0b_ref[0]

        @pl.loop(0, nc, unroll=RET_UNROLL)
        def _(t):
            n = nc - 1 - t
            sb = sbf_ref[0]
            sb_scr[n] = sb
            sbf_ref[0] = cs["gb"] * sb + _dot(k_ref[0, _ret_rows(n), :] * cs["zb"], v_ref[0, _ret_rows(n), :], _TN)

        sff_ref[0] = s0f_ref[0]

        @pl.loop(0, nc, unroll=RET_UNROLL)
        def _(n):
            sf = sff_ref[0]
            qc, kc, vc = q_ref[0, _ret_rows(n), :], k_ref[0, _ret_rows(n), :], v_ref[0, _ret_rows(n), :]
            p = _dot(qc, kc, _NT) * cs["dmat"]
            y_ref[0, _ret_rows(n), :] = _dot(p, vc) + _dot(qc * cs["xif"], sf) + _dot(qc * cs["xib"], sb_scr[n])
            sff_ref[0] = cs["gf"] * sf + _dot(kc * cs["zf"], vc, _TN)

    tok = lambda d: pl.BlockSpec((1, n_tok, d), lambda hh: (hh, 0, 0), pipeline_mode=pl.Buffered(1))
    lg = pl.BlockSpec((1, 1, LANES), lambda hh: (hh, 0, 0))
    st = pl.BlockSpec((1, dk, dv), lambda hh: (hh, 0, 0))
    return pl.pallas_call(
        body, name="ret_fwd_%d" % n_tok, grid=(h,),
        in_specs=[tok(dk), tok(dk), tok(dv), lg, lg, st, st], out_specs=[tok(dv), st, st],
        out_shape=[jax.ShapeDtypeStruct((h, n_tok, dv), F32)] + [jax.ShapeDtypeStruct((h, dk, dv), F32)] * 2,
        scratch_shapes=[pltpu.VMEM((nc, dk, dv), F32)],
        compiler_params=_params(("parallel",)),
    )(q, k, v, lgf, lgb, s0f, s0b)


def _ret_bwd_call(q, k, v, lgf, lgb, s0f, s0b, dy, dsff, dsbf):
    h, n_tok, dk = q.shape
    dv = v.shape[-1]
    nc = n_tok // RET_CHUNK
    c = float(RET_CHUNK)

    def body(q_ref, k_ref, v_ref, lgf_ref, lgb_ref, s0f_ref, s0b_ref, dy_ref, dsff_ref, dsbf_ref,
             dq_ref, dk_ref, dv_ref, dlgf_ref, dlgb_ref, ds0f_ref, ds0b_ref, sb_scr, gf_scr, st_a, st_b):
        cs = _ret_consts(lgf_ref, lgb_ref)

        st_a[...] = s0b_ref[0]
        st_b[...] = dsff_ref[0]

        @pl.loop(0, nc, unroll=RET_UNROLL)
        def _(t):
            n = nc - 1 - t
            sb, gf_next = st_a[...], st_b[...]
            sb_scr[n] = sb
            gf_scr[n] = gf_next
            qc, kc, vc, dyc = (r[0, _ret_rows(n), :] for r in (q_ref, k_ref, v_ref, dy_ref))
            st_a[...] = cs["gb"] * sb + _dot(kc * cs["zb"], vc, _TN)
            st_b[...] = _dot(qc * cs["xif"], dyc, _TN) + cs["gf"] * gf_next

        ds0f_ref[0] = st_b[...]

        st_a[...] = s0f_ref[0]
        st_b[...] = dsbf_ref[0]
        dlgf_ref[...] = jnp.zeros_like(dlgf_ref)
        dlgb_ref[...] = jnp.zeros_like(dlgb_ref)

        @pl.loop(0, nc, unroll=RET_UNROLL)
        def _(n):
            sf, gb_prev = st_a[...], st_b[...]
            sb, gf_next = sb_scr[n], gf_scr[n]
            qc, kc, vc, dyc = (r[0, _ret_rows(n), :] for r in (q_ref, k_ref, v_ref, dy_ref))
            a = _dot(qc, kc, _NT)
            dp = _dot(dyc, vc, _NT)
            da = _bf(dp * cs["dmat"])
            dqf = _dot(dyc, sf, _NT)
            dqb = _dot(dyc, sb, _NT)
            dkf = _dot(vc, gf_next, _NT)
            dkb = _dot(vc, gb_prev, _NT)
            dq_ref[0, _ret_rows(n), :] = _dot(da, kc) + dqf * cs["xif"] + dqb * cs["xib"]
            dk_ref[0, _ret_rows(n), :] = _dot(da, qc, _TN) + dkf * cs["zf"] + dkb * cs["zb"]
            dv_ref[0, _ret_rows(n), :] = (_dot(a * cs["dmat"], dyc, _TN) + _dot(kc * cs["zf"], gf_next)
                                         + _dot(kc * cs["zb"], gb_prev))
            w = dp * a * cs["dmat"] * cs["diff"]
            row = lambda x: jnp.sum(x, axis=1, keepdims=True)
            dlgf_ref[0] += (_sum_all(jnp.where(cs["mf"], w, 0.0))
                            + _sum_all((cs["col"] + 1.0) * cs["xif"] * row(dqf * qc) + (c - 1.0 - cs["col"]) * cs["zf"] * row(dkf * kc))
                            + c * cs["gf"] * _sum_all(gf_next * sf))
            dlgb_ref[0] += (_sum_all((c - cs["col"]) * cs["xib"] * row(dqb * qc) + cs["col"] * cs["zb"] * row(dkb * kc))
                            + c * cs["gb"] * _sum_all(gb_prev * sb) - _sum_all(jnp.where(cs["mf"], 0.0, w)))
            st_a[...] = cs["gf"] * sf + _dot(kc * cs["zf"], vc, _TN)
            st_b[...] = _dot(qc * cs["xib"], dyc, _TN) + cs["gb"] * gb_prev

        ds0b_ref[0] = st_b[...]

    tok = lambda d: pl.BlockSpec((1, n_tok, d), lambda hh: (hh, 0, 0), pipeline_mode=pl.Buffered(1))
    lg = pl.BlockSpec((1, 1, LANES), lambda hh: (hh, 0, 0))
    st = pl.BlockSpec((1, dk, dv), lambda hh: (hh, 0, 0))
    return pl.pallas_call(
        body, name="ret_bwd_%d" % n_tok, grid=(h,),
        in_specs=[tok(dk), tok(dk), tok(dv), lg, lg, st, st, tok(dv), st, st],
        out_specs=[tok(dk), tok(dk), tok(dv), lg, lg, st, st],
        out_shape=[jax.ShapeDtypeStruct((h, n_tok, dk), F32)] * 2 + [jax.ShapeDtypeStruct((h, n_tok, dv), F32)]
        + [jax.ShapeDtypeStruct((h, 1, LANES), F32)] * 2 + [jax.ShapeDtypeStruct((h, dk, dv), F32)] * 2,
        scratch_shapes=[pltpu.VMEM((nc, dk, dv), F32), pltpu.VMEM((nc, dk, dv), F32), pltpu.VMEM((dk, dv), F32), pltpu.VMEM((dk, dv), F32)],
        compiler_params=_params(("parallel",)),
    )(q, k, v, lgf, lgb, s0f, s0b, dy, dsff, dsbf)


def _lane_bcast(lg):
    return jnp.broadcast_to(lg[:, None, None], (lg.shape[0], 1, LANES))


@jax.custom_vjp
def _retention(q, k, v, lgf, lgb, s0f, s0b):
    return tuple(_ret_fwd_call(q, k, v, _lane_bcast(lgf), _lane_bcast(lgb), s0f, s0b))


def _retention_fwd(q, k, v, lgf, lgb, s0f, s0b):
    return _retention(q, k, v, lgf, lgb, s0f, s0b), (q, k, v, lgf, lgb, s0f, s0b)


def _retention_bwd(res, cts):
    q, k, v, lgf, lgb, s0f, s0b = res
    dy, dsff, dsbf = cts
    dq, dk, dv, dlgf, dlgb, ds0f, ds0b = _ret_bwd_call(q, k, v, _lane_bcast(lgf), _lane_bcast(lgb), s0f, s0b, dy, dsff, dsbf)
    return dq, dk, dv, dlgf[:, 0, 0], dlgb[:, 0, 0], ds0f, ds0b


_retention.defvjp(_retention_fwd, _retention_bwd)


def _gn_specs(y):
    h, n, dv = y.shape
    r = _tile(n, 512, SUBLANES)
    return (h, n, dv, r, pl.BlockSpec((1, r, dv), lambda i, hh: (hh, i, 0)), pl.BlockSpec((r, dv), lambda i, hh: (i, hh)))


def _gn_norm(yv):
    mu = jnp.mean(yv, axis=-1, keepdims=True)
    yc = yv - mu
    rstd = lax.rsqrt(jnp.mean(yc * yc, axis=-1, keepdims=True) + GN_EPS)
    return yc * rstd, rstd


def _gn_gate_fwd_call(y, gate):
    h, n, dv, r, yspec, gspec = _gn_specs(y)

    def body(y_ref, g_ref, o_ref):
        gv = g_ref[...]
        o_ref[...] = gv * jax.nn.sigmoid(gv) * _gn_norm(y_ref[0])[0]

    return pl.pallas_call(
        body, name="gn_gate", grid=(n // r, h), in_specs=[yspec, gspec], out_specs=gspec,
        out_shape=jax.ShapeDtypeStruct((n, h * dv), F32), compiler_params=_params(("parallel", "parallel")),
    )(y, gate)


def _gn_gate_bwd_call(y, gate, dout):
    h, n, dv, r, yspec, gspec = _gn_specs(y)

    def body(y_ref, g_ref, do_ref, dy_ref, dg_ref):
        gv = g_ref[...]
        dov = do_ref[...]
        yn, rstd = _gn_norm(y_ref[0])
        sg = jax.nn.sigmoid(gv)
        dg_ref[...] = dov * yn * (sg * (1.0 + gv * (1.0 - sg)))
        dyn = dov * (gv * sg)
        dy_ref[0] = rstd * (dyn - jnp.mean(dyn, axis=-1, keepdims=True) - yn * jnp.mean(dyn * yn, axis=-1, keepdims=True))

    return pl.pallas_call(
        body, name="gn_gate_b", grid=(n // r, h), in_specs=[yspec, gspec, gspec], out_specs=[yspec, gspec],
        out_shape=[jax.ShapeDtypeStruct((h, n, dv), F32), jax.ShapeDtypeStruct((n, h * dv), F32)],
        compiler_params=_params(("parallel", "parallel")),
    )(y, gate, dout)


@jax.custom_vjp
def _gn_gate(y, gate):
    return _gn_gate_fwd_call(y, gate)


_gn_gate.defvjp(lambda y, gate: (_gn_gate_fwd_call(y, gate), (y, gate)),
                lambda res, dout: tuple(_gn_gate_bwd_call(res[0], res[1], dout)))


def _rope_tables(pos, dim, base):
    inv = base ** (-jnp.arange(0, dim, 2, dtype=F32) / dim)
    ang = pos.astype(F32)[:, None] * inv[None, :]
    return jnp.cos(ang)[:, None, :], jnp.sin(ang)[:, None, :]


def _rotate(x, cos, sin):
    x1, x2 = jnp.split(x, 2, axis=-1)
    return jnp.concatenate([x1 * cos - x2 * sin, x2 * cos + x1 * sin], axis=-1)


def _axial_rope(x, row_tab, col_tab):
    xr, xc = jnp.split(x, 2, axis=-1)
    return jnp.concatenate([_rotate(xr, *row_tab), _rotate(xc, *col_tab)], axis=-1)


def _heads(t):
    return jnp.swapaxes(t, 0, 1)


def _local_loss(x, mods_lat, mods_ctx, small, big, early, late, ctx, target):
    n_lat, d = x.shape
    n_ctx = ctx.shape[0]
    both = lambda i: jnp.stack([mods_ctx[i], mods_lat[i]])[:, None, :]
    lat = lambda i: mods_lat[i][None, None, :]

    xs = jnp.concatenate([ctx, x], axis=0)
    x1, (st_mix_in, st_uq, st_ukv, st_mix_out) = _make_ffn(n_ctx, "ffn1", True)(
        xs, small["norm1_g"], both(0), both(1), both(2), big["ffn1_wg"], big["ffn1_wu"], None, tuple(early))
    every = (0, 1, 2, 3)
    big = dict(mix_in=jnp.pad(_slots_side_by_side(st_mix_in, every), ((0, 0), (0, MIX_IN_PAD - MIX_IN))),
               w_uq=_slots_side_by_side(st_uq, every), w_ukv=_slots_side_by_side(st_ukv, every),
               mix_out=st_mix_out.reshape(-1, st_mix_out.shape[-1]))
    proj = _make_normmod_linear(n_ctx, "mix_in")(x1, small["norm2_g"], both(3), both(4), big["mix_in"])
    offs = np.cumsum((0,) + MIX_SPLITS)
    part = lambda i, rows: proj[rows, offs[i]:offs[i + 1]]
    lat_rows, ctx_rows = slice(n_ctx, None), slice(0, n_ctx)

    zq = jnp.zeros((1, 1, MLA_Q_RANK), F32)
    zkv = jnp.zeros((1, 1, MLA_KV_RANK), F32)
    q = _make_normmod_linear(0, "mla_q")(part(4, lat_rows), small["mla_q_norm_g"], zq, zq, big["w_uq"])
    kv = _make_normmod_linear(0, "mla_kv")(part(5, slice(None)), small["mla_kv_norm_g"], zkv, zkv, big["w_ukv"])

    lgf = jax.nn.log_sigmoid(small["ret_decay_fwd"][0])
    lgb = jax.nn.log_sigmoid(small["ret_decay_bwd"][0])
    ret_tab = _rope_tables(jnp.arange(n_lat), RET_DK, ROPE_BASE)
    hd = lambda t, dd: t.reshape(t.shape[0], RET_HEADS, dd)
    s_zero = jnp.zeros((RET_HEADS, RET_DK, RET_DV), F32)
    _, s_f, s_b = _retention(_heads(hd(part(0, ctx_rows), RET_DK)), _heads(hd(part(1, ctx_rows), RET_DK) * (RET_DK ** -0.5)),
                             _heads(hd(part(2, ctx_rows), RET_DV)), lgf, lgb, s_zero, s_zero)
    rq = _rotate(hd(part(0, lat_rows), RET_DK), *ret_tab)
    rk = _rotate(hd(part(1, lat_rows), RET_DK) * (RET_DK ** -0.5), *ret_tab)
    y_lat, _, _ = _retention(_heads(rq), _heads(rk), _heads(hd(part(2, lat_rows), RET_DV)), lgf, lgb, s_f, s_b)
    ret_out = _gn_gate(y_lat, part(3, lat_rows))

    pos = jnp.arange(n_lat)
    row_tab = _rope_tables(pos // GRID_W, MLA_ROPE // 2, ROPE_BASE)
    col_tab = _rope_tables(pos % GRID_W, MLA_ROPE // 2, ROPE_BASE)
    q = q.reshape(n_lat, MLA_HEADS, MLA_NOPE + MLA_ROPE)
    q_all = jnp.concatenate([q[..., :MLA_NOPE], _axial_rope(q[..., MLA_NOPE:], row_tab, col_tab)], axis=-1)
    kv = kv.reshape(n_ctx + n_lat, MLA_HEADS, MLA_NOPE + MLA_V)
    kr_lat = _axial_rope(part(6, lat_rows)[:, None, :], row_tab, col_tab)
    kr = jnp.concatenate([kr_lat, part(6, ctx_rows)[:, None, :]], axis=0)
    kv_lat_first = jnp.concatenate([kv[n_ctx:], kv[:n_ctx]], axis=0)
    k_all = jnp.concatenate([kv_lat_first[..., :MLA_NOPE], jnp.broadcast_to(kr, (n_ctx + n_lat, MLA_HEADS, MLA_ROPE))], axis=-1)
    mla, (w_in2, w_out2) = _attention_gather(_heads(q_all), _heads(k_all), _heads(kv_lat_first[..., MLA_NOPE:]), tuple(late))
    mla_out = _heads(mla).reshape(n_lat, MLA_HEADS * MLA_V)

    x2 = _make_linear_gated_res("mix_out")(x1[n_ctx:], jnp.concatenate([ret_out, mla_out], axis=-1), big["mix_out"], lat(5))
    x3, _ = _make_ffn(0, "ffn2", False)(x2, small["norm3_g"], lat(6), lat(7), lat(8), _slots_side_by_side(w_in2, (0, 1)),
                                        _slots_side_by_side(w_in2, (2, 3)), w_out2.reshape(-1, w_out2.shape[-1]), ())
    return _final_loss(x3, small["final_norm_g"][None, :], target)


HBM_SPEC = pl.BlockSpec(memory_space=pl.ANY)
VMEM_SPEC = pl.BlockSpec(memory_space=pltpu.VMEM)
ALL_PEERS = (1, 2, 3, 4, 5, 6, 7)
CHIP_PEERS = (4, 2, 6)


def _me():
    return lax.axis_index("x"), lax.axis_index("y"), lax.axis_index("c")


def _flip(pos, mask):
    x, y, c = pos
    return (1 - x if mask & 4 else x, 1 - y if mask & 2 else y, 1 - c if mask & 1 else c)


def _allgather_small(block, masks, chips_only, name):
    r, c = block.shape
    n_slots = 4 if chips_only else 8

    def body(x_ref, out_ref, send_sems, recv_sems, local_sem):
        pos = _me()
        slot = 2 * pos[0] + pos[1] if chips_only else 4 * pos[0] + 2 * pos[1] + pos[2]
        local = pltpu.make_async_copy(x_ref, out_ref.at[slot], local_sem)
        local.start()
        copies = [pltpu.make_async_remote_copy(src_ref=x_ref, dst_ref=out_ref.at[slot], send_sem=send_sems.at[j], recv_sem=recv_sems.at[j],
                                               device_id=_flip(pos, mask), device_id_type=MESH) for j, mask in enumerate(masks)]
        for cp in copies:
            cp.start()
        for cp in copies:
            cp.wait()
        local.wait()

    return pl.pallas_call(
        body, name=name, in_specs=[VMEM_SPEC], out_specs=VMEM_SPEC,
        out_shape=jax.ShapeDtypeStruct((n_slots, r, c), block.dtype),
        scratch_shapes=[pltpu.SemaphoreType.DMA((len(masks),)), pltpu.SemaphoreType.DMA((len(masks),)), pltpu.SemaphoreType.DMA],
        compiler_params=pltpu.CompilerParams(vmem_limit_bytes=VMEM_LIMIT_BYTES),
    )(block)


def _gather_weights(shards):
    n = len(shards)

    def body(*refs):
        ins, outs, sems = refs[:n], refs[n:2 * n], refs[2 * n:]
        _gather_start(ins, outs, sems)
        _gather_finish(ins, outs, sems)

    stacked = pl.pallas_call(
        body, name="gather_weights", in_specs=[HBM_SPEC] * n, out_specs=[HBM_SPEC] * n,
        out_shape=_gather_out_shapes(shards), scratch_shapes=_gather_sems(n),
    )(*shards)
    return _fill_own_slot(stacked, shards)


def _gather_out_shapes(shards):
    return [jax.ShapeDtypeStruct((4,) + s.shape, s.dtype) for s in shards]


def _gather_sems(n):
    return [pltpu.SemaphoreType.DMA((3 * n,)) for _ in range(4)]


def _fill_own_slot(stacked, shards):
    if not shards:
        return []
    chip = 2 * lax.axis_index("x") + lax.axis_index("y")
    return [lax.dynamic_update_slice_in_dim(st, sh[None], chip, axis=0) for st, sh in zip(stacked, shards)]


def _gather_send(ins, outs, sems, w, j, pos):
    x, y, c = pos
    half = ins[w].shape[0] // 2
    mine = pl.ds(c * half, half)
    return pltpu.make_async_remote_copy(src_ref=ins[w].at[mine], dst_ref=outs[w].at[2 * x + y, mine], send_sem=sems[0].at[3 * w + j],
                                        recv_sem=sems[1].at[3 * w + j], device_id=_flip(pos, CHIP_PEERS[j]), device_id_type=MESH)


def _gather_pass(ins, outs, sems, w, j, pos, to_me):
    px, py, _ = _flip(pos, CHIP_PEERS[j])
    half = ins[w].shape[0] // 2
    slab = outs[w].at[2 * px + py, pl.ds(((1 - pos[2]) if to_me else pos[2]) * half, half)]
    return pltpu.make_async_remote_copy(src_ref=slab, dst_ref=slab, send_sem=sems[2].at[3 * w + j], recv_sem=sems[3].at[3 * w + j],
                                        device_id=_flip(pos, 1), device_id_type=MESH)


def _gather_start(ins, outs, sems):
    pos = _me()
    for w in range(len(ins)):
        for j in range(3):
            _gather_send(ins, outs, sems, w, j, pos).start()


def _gather_finish(ins, outs, sems):
    pos = _me()
    pairs = [(w, j) for w in range(len(ins)) for j in range(3)]
    for w, j in pairs:
        _gather_send(ins, outs, sems, w, j, pos).wait_recv()
        _gather_pass(ins, outs, sems, w, j, pos, False).start()
    for w, j in pairs:
        _gather_pass(ins, outs, sems, w, j, pos, True).wait_recv()
    for w, j in pairs:
        _gather_send(ins, outs, sems, w, j, pos).wait_send()
        _gather_pass(ins, outs, sems, w, j, pos, False).wait_send()


def _pair_swap_halves(grads, tag):
    n = len(grads)

    def body(*refs):
        ins, outs = refs[:n], refs[n:2 * n]
        send_sems, recv_sems = refs[2 * n:]
        pos = _me()
        copies = []
        for w in range(n):
            half = grads[w].shape[1] // 2
            cp = pltpu.make_async_remote_copy(src_ref=ins[w].at[:, pl.ds((1 - pos[2]) * half, half), :], dst_ref=outs[w], send_sem=send_sems.at[w],
                                              recv_sem=recv_sems.at[w], device_id=_flip(pos, 1), device_id_type=MESH)
            cp.start()
            copies.append(cp)
        for cp in copies:
            cp.wait()

    return pl.pallas_call(
        body, name="pair_swap_halves_" + tag, in_specs=[HBM_SPEC] * n, out_specs=[HBM_SPEC] * n,
        out_shape=[jax.ShapeDtypeStruct((4, g.shape[1] // 2, g.shape[2]), g.dtype) for g in grads],
        scratch_shapes=[pltpu.SemaphoreType.DMA((n,)), pltpu.SemaphoreType.DMA((n,))],
    )(*grads)


def _chip_scatter(parts):
    n = len(parts)

    def body(*refs):
        ins, outs, sems = refs[:n], refs[n:2 * n], refs[2 * n:]
        _scatter_start(ins, outs, sems)
        _scatter_finish(ins, outs, sems)

    return pl.pallas_call(
        body, name="chip_scatter", in_specs=[HBM_SPEC] * n, out_specs=[HBM_SPEC] * n,
        out_shape=[jax.ShapeDtypeStruct(p.shape, p.dtype) for p in parts], scratch_shapes=_scatter_sems(n),
    )(*parts)


def _scatter_sems(n):
    return [pltpu.SemaphoreType.DMA((3 * n,)), pltpu.SemaphoreType.DMA((3 * n,)), pltpu.SemaphoreType.DMA((n,))]


def _scatter_copies(ins, outs, sems):
    pos = _me()
    me = 2 * pos[0] + pos[1]
    local = [pltpu.make_async_copy(ins[w].at[me], outs[w].at[me], sems[2].at[w]) for w in range(len(ins))]
    remote = []
    for w in range(len(ins)):
        for j, mask in enumerate(CHIP_PEERS):
            px, py, _ = _flip(pos, mask)
            remote.append(pltpu.make_async_remote_copy(src_ref=ins[w].at[2 * px + py], dst_ref=outs[w].at[me], send_sem=sems[0].at[3 * w + j],
                                                       recv_sem=sems[1].at[3 * w + j], device_id=_flip(pos, mask), device_id_type=MESH))
    return local, remote


def _scatter_start(ins, outs, sems):
    local, remote = _scatter_copies(ins, outs, sems)
    for cp in local + remote:
        cp.start()


def _scatter_finish(ins, outs, sems):
    local, remote = _scatter_copies(ins, outs, sems)
    for cp in remote + local:
        cp.wait()


def _pair_swap_reduced(halves, tag):
    n = len(halves)

    def body(*refs):
        ins, outs = refs[:n], refs[n:2 * n]
        send_sems, recv_sems = refs[2 * n:]
        pos = _me()
        copies = []
        for w in range(n):
            cp = pltpu.make_async_remote_copy(src_ref=ins[w], dst_ref=outs[w], send_sem=send_sems.at[w], recv_sem=recv_sems.at[w],
                                              device_id=_flip(pos, 1), device_id_type=MESH)
            cp.start()
            copies.append(cp)
        for cp in copies:
            cp.wait()

    dma = lambda k: pltpu.SemaphoreType.DMA((k,))
    return pl.pallas_call(
        body, name="pair_swap_reduced_" + tag, in_specs=[HBM_SPEC] * n, out_specs=[HBM_SPEC] * n,
        out_shape=[jax.ShapeDtypeStruct(h.shape, h.dtype) for h in halves],
        scratch_shapes=[dma(n), dma(n)],
    )(*halves)


def _add_pair(mine, theirs, name):
    s, h, c = mine.shape
    r = _tile(h, max(16, (1 << 19) // c), 16)

    def body(a_ref, b_ref, o_ref):
        o_ref[...] = (a_ref[...].astype(F32) + b_ref[...].astype(F32)).astype(BF16)

    blk = pl.BlockSpec((1, r, c), lambda i, j: (i, j, 0))
    return pl.pallas_call(
        body, name=name, grid=(s, h // r), in_specs=[blk, blk], out_specs=blk,
        out_shape=jax.ShapeDtypeStruct(mine.shape, BF16), compiler_params=_params(("parallel", "parallel")),
    )(mine, theirs)


def _sum_slots(parts, name):
    s, h, c = parts.shape
    r = _tile(h, max(16, (1 << 18) // c), 16)

    def body(p_ref, o_ref):
        acc = p_ref[0].astype(F32)
        for k in range(1, s):
            acc = acc + p_ref[k].astype(F32)
        o_ref[...] = acc

    return pl.pallas_call(
        body, name=name, grid=(h // r,), in_specs=[pl.BlockSpec((s, r, c), lambda i: (0, i, 0))],
        out_specs=pl.BlockSpec((r, c), lambda i: (i, 0)),
        out_shape=jax.ShapeDtypeStruct((h, c), F32), compiler_params=_params(("parallel",)),
    )(parts)


def _reduce_scatter_grads(stacked, tag, scatter=_chip_scatter):
    c = lax.axis_index("c")
    theirs = _pair_swap_halves(stacked, tag)
    parts = []
    for w, (g, t) in enumerate(zip(stacked, theirs)):
        half = g.shape[1] // 2
        mine = lax.dynamic_slice_in_dim(g, c * half, half, axis=1)
        parts.append(_add_pair(mine, t, "rs_add_pair_%s_%d" % (tag, w)))
    landed = scatter(parts)
    halves = [_sum_slots(p, "rs_sum_slots_%s_%d" % (tag, w)) for w, p in enumerate(landed)]
    return list(zip(halves, _pair_swap_reduced(halves, tag)))


def _adamw_math(w, g, m, v):
    m = ADAM_B1 * m + (1.0 - ADAM_B1) * g
    v = ADAM_B2 * v + (1.0 - ADAM_B2) * (g * g)
    m_hat = m / (1.0 - ADAM_B1 ** ADAM_STEP)
    v_hat = v / (1.0 - ADAM_B2 ** ADAM_STEP)
    return -ADAM_LR * (m_hat / (jnp.sqrt(v_hat) + ADAM_EPS) + ADAM_WD * w), m, v


def _adamw(w, g, m, v, name):
    rows, cols = w.shape
    r = _tile(rows, max(SUBLANES, (1 << 18) // cols), SUBLANES)

    def body(w_ref, g_ref, m_ref, v_ref, d_ref, mo_ref, vo_ref):
        d_ref[...], mo_ref[...], vo_ref[...] = _adamw_math(w_ref[...], g_ref[...], m_ref[...], v_ref[...])

    blk = pl.BlockSpec((r, cols), lambda i: (i, 0))
    return pl.pallas_call(
        body, name=name, grid=(rows // r,), in_specs=[blk] * 4, out_specs=[blk] * 3,
        out_shape=[jax.ShapeDtypeStruct(w.shape, F32)] * 3, compiler_params=_params(("parallel",)),
    )(w, g, m, v)


def _adamw_halves(w, g_mine, g_theirs, m, v, core, name):
    rows, cols = w.shape
    half = rows // 2
    r = _tile(half, max(SUBLANES, (1 << 18) // cols), SUBLANES)
    nbh = half // r

    def body(core_ref, w_ref, gm_ref, gt_ref, m_ref, v_ref, g_ref, d_ref, mo_ref, vo_ref):
        is_mine = (pl.program_id(0) // nbh) == core_ref[0]

        @pl.when(is_mine)
        def _():
            g_ref[...] = gm_ref[...]

        @pl.when(jnp.logical_not(is_mine))
        def _():
            g_ref[...] = gt_ref[...]

        g = g_ref[...]
        d_ref[...], mo_ref[...], vo_ref[...] = _adamw_math(w_ref[...], g, m_ref[...], v_ref[...])

    full = pl.BlockSpec((r, cols), lambda i, core_ref: (i, 0))
    part = pl.BlockSpec((r, cols), lambda i, core_ref: (i % nbh, 0))
    return pl.pallas_call(
        body, name=name,
        grid_spec=pltpu.PrefetchScalarGridSpec(num_scalar_prefetch=1, grid=(rows // r,), in_specs=[full, part, part, full, full],
                                               out_specs=[full] * 4),
        out_shape=[jax.ShapeDtypeStruct(w.shape, F32)] * 4, compiler_params=_params(("parallel",)),
    )(core, w, g_mine, g_theirs, m, v)


def _adamw_reduced(parts, w, m, v, name):
    def body(p_ref, w_ref, m_ref, v_ref, g_ref, d_ref, mo_ref, vo_ref):
        g = p_ref[0]
        for k in range(1, parts.shape[0]):
            g = g + p_ref[k]
        g_ref[...] = g
        d_ref[...], mo_ref[...], vo_ref[...] = _adamw_math(w_ref[...], g, m_ref[...], v_ref[...])

    return pl.pallas_call(
        body, name=name, in_specs=[VMEM_SPEC] * 4, out_specs=[VMEM_SPEC] * 4,
        out_shape=[jax.ShapeDtypeStruct(w.shape, F32)] * 4,
        compiler_params=pltpu.CompilerParams(vmem_limit_bytes=VMEM_LIMIT_BYTES),
    )(parts, w, m, v)


WEIGHTS = ("c_ctx", "ada_w", "ada_b", "norm1_g", "ffn1_w_in", "ffn1_w_out", "norm2_g", "mix_w_in", "ret_decay_fwd", "ret_decay_bwd",
           "mla_q_norm_g", "mla_w_uq", "mla_kv_norm_g", "mla_w_ukv", "mix_w_out", "norm3_g", "ffn2_w_in", "ffn2_w_out", "final_norm_g")
SMALL = ("c_ctx", "ada_b", "norm1_g", "norm2_g", "ret_decay_fwd", "ret_decay_bwd", "mla_q_norm_g", "mla_kv_norm_g", "norm3_g", "final_norm_g")
FIRST = "ffn1_w_in"
EARLY = ("ffn1_w_out", "mix_w_in", "mla_w_uq", "mla_w_ukv", "mix_w_out")
LATE = ("ffn2_w_in", "ffn2_w_out")


def _slots_side_by_side(stacked, slots):
    return jnp.concatenate([stacked[k] for k in slots], axis=1)


def _pack(vectors):
    flat = jnp.concatenate([v.reshape(-1) for v in vectors])
    return jnp.pad(flat, (0, -flat.shape[0] % (SUBLANES * LANES))).reshape(SUBLANES, -1)


def _rows8(a):
    return a.reshape(a.shape[0] * SUBLANES, a.shape[1] // SUBLANES)


def _unpack(packed, like):
    packed = packed.reshape(-1)
    out, off = [], 0
    for ref in like:
        out.append(packed[off:off + ref.size].reshape(ref.shape))
        off += ref.size
    return out


def kernel(x, c, ctx, c_ctx, ada_w, ada_b, norm1_g, ffn1_w_in, ffn1_w_out, norm2_g, mix_w_in, ret_decay_fwd, ret_decay_bwd, mla_q_norm_g, mla_w_uq, mla_kv_norm_g, mla_w_ukv, mix_w_out, norm3_g, ffn2_w_in, ffn2_w_out, final_norm_g, loss_target, m_c_ctx, m_ada_w, m_ada_b, m_norm1_g, m_ffn1_w_in, m_ffn1_w_out, m_norm2_g, m_mix_w_in, m_ret_decay_fwd, m_ret_decay_bwd, m_mla_q_norm_g, m_mla_w_uq, m_mla_kv_norm_g, m_mla_w_ukv, m_mix_w_out, m_norm3_g, m_ffn2_w_in, m_ffn2_w_out, m_final_norm_g, v_c_ctx, v_ada_w, v_ada_b, v_norm1_g, v_ffn1_w_in, v_ffn1_w_out, v_norm2_g, v_mix_w_in, v_ret_decay_fwd, v_ret_decay_bwd, v_mla_q_norm_g, v_mla_w_uq, v_mla_kv_norm_g, v_mla_w_ukv, v_mix_w_out, v_norm3_g, v_ffn2_w_in, v_ffn2_w_out, v_final_norm_g):
    w = dict(c_ctx=c_ctx, ada_w=ada_w, ada_b=ada_b, norm1_g=norm1_g, ffn1_w_in=ffn1_w_in, ffn1_w_out=ffn1_w_out, norm2_g=norm2_g,
             mix_w_in=mix_w_in, ret_decay_fwd=ret_decay_fwd, ret_decay_bwd=ret_decay_bwd, mla_q_norm_g=mla_q_norm_g, mla_w_uq=mla_w_uq,
             mla_kv_norm_g=mla_kv_norm_g, mla_w_ukv=mla_w_ukv, mix_w_out=mix_w_out, norm3_g=norm3_g, ffn2_w_in=ffn2_w_in,
             ffn2_w_out=ffn2_w_out, final_norm_g=final_norm_g)
    mom_m = dict(zip(WEIGHTS, (m_c_ctx, m_ada_w, m_ada_b, m_norm1_g, m_ffn1_w_in, m_ffn1_w_out, m_norm2_g, m_mix_w_in, m_ret_decay_fwd,
                               m_ret_decay_bwd, m_mla_q_norm_g, m_mla_w_uq, m_mla_kv_norm_g, m_mla_w_ukv, m_mix_w_out, m_norm3_g,
                               m_ffn2_w_in, m_ffn2_w_out, m_final_norm_g)))
    mom_v = dict(zip(WEIGHTS, (v_c_ctx, v_ada_w, v_ada_b, v_norm1_g, v_ffn1_w_in, v_ffn1_w_out, v_norm2_g, v_mix_w_in, v_ret_decay_fwd,
                               v_ret_decay_bwd, v_mla_q_norm_g, v_mla_w_uq, v_mla_kv_norm_g, v_mla_w_ukv, v_mix_w_out, v_norm3_g,
                               v_ffn2_w_in, v_ffn2_w_out, v_final_norm_g)))
    xi, yi, ci = _me()
    chip = 2 * xi + yi
    example = 2 * chip + ci
    d = x.shape[-1]
    n_mod = ada_b.shape[-1] // d

    c_all = _allgather_small(_rows8(c), ALL_PEERS, False, "gather_c").reshape(8, d)
    cond = jnp.concatenate([c_all, jnp.broadcast_to(c_ctx[None, :], (8, d))], axis=0)
    cond_act = jax.nn.silu(cond)
    n_cols = ada_w.shape[-1]
    bias = lax.dynamic_slice_in_dim(ada_b, chip * n_cols, n_cols, axis=1)
    mods_cols = _matmul(cond_act, ada_w[0], "nn", F32, "ada_fwd", add=jnp.broadcast_to(bias, (16, n_cols)))
    mods = jnp.swapaxes(_allgather_small(mods_cols, CHIP_PEERS, True, "gather_mods"), 0, 1).reshape(16, 4 * n_cols)
    mods_lat = lax.dynamic_slice_in_dim(mods, example, 1, axis=0).reshape(n_mod, d)
    mods_ctx = mods[8].reshape(n_mod, d)

    (st_in1,) = _gather_weights([w[FIRST][0].astype(BF16)])
    big = dict(ffn1_wg=_slots_side_by_side(st_in1, (0, 1)), ffn1_wu=_slots_side_by_side(st_in1, (2, 3)))
    small = {k: w[k] for k in ("norm1_g", "norm2_g", "norm3_g", "final_norm_g", "mla_q_norm_g", "mla_kv_norm_g", "ret_decay_fwd", "ret_decay_bwd")}
    early = tuple(w[name][0] for name in EARLY)
    late = tuple(w[name][0] for name in LATE)

    loss_mine, (dx, dmods_lat, dmods_ctx, dsmall, dbig, dearly, dlate) = jax.value_and_grad(_local_loss, argnums=(0, 1, 2, 3, 4, 5, 6))(
        x[0], mods_lat, mods_ctx, small, big, early, late, ctx[0], loss_target[0])

    dmods = _allgather_small(_rows8(jnp.stack([dmods_lat.reshape(-1), dmods_ctx.reshape(-1)])), ALL_PEERS, False, "gather_dmods")
    dmods = dmods.reshape(8, 2, n_mod * d)
    dmods_rows = jnp.concatenate([dmods[:, 0, :], dmods[:, 1, :]], axis=0)
    dmods_cols = lax.dynamic_slice_in_dim(dmods_rows, chip * n_cols, n_cols, axis=1)
    g_ada_w = _matmul(cond_act, dmods_cols, "tn", F32, "ada_dw")
    dcond_act = _matmul(dmods_cols, ada_w[0], "nt", F32, "ada_dcond")
    sig = jax.nn.sigmoid(c_ctx)
    dc_ctx = jnp.sum(dcond_act[8:], axis=0) * (sig * (1.0 + c_ctx * (1.0 - sig)))
    share = dict(dsmall)
    share["c_ctx"] = jnp.where(ci == 0, dc_ctx, jnp.zeros_like(dc_ctx))
    share["ada_b"] = (dmods_lat + dmods_ctx).reshape(1, -1)
    zero = jnp.zeros((1,), F32)
    parts = _allgather_small(_pack([share[k] for k in SMALL] + [loss_mine.reshape(1)]), ALL_PEERS, False, "gather_small_grads")
    packed = _adamw_reduced(parts, _pack([w[k] for k in SMALL] + [zero]), _pack([mom_m[k] for k in SMALL] + [zero]),
                            _pack([mom_v[k] for k in SMALL] + [zero]), "adamw_small")
    like = [w[k] for k in SMALL] + [zero]
    grads, deltas, new_m, new_v = ({k: a for k, a in zip(SMALL + ("loss",), _unpack(p, like))} for p in packed)
    loss = grads.pop("loss").reshape(())

    def col_slots(parts, n):
        return jnp.stack([p[:, k * (p.shape[1] // n):(k + 1) * (p.shape[1] // n)] for p in parts for k in range(n)])

    core = ci.astype(jnp.int32).reshape(1)
    ((g_mine, g_theirs),) = _reduce_scatter_grads([col_slots([dbig["ffn1_wg"], dbig["ffn1_wu"]], 2)], "main")
    g, dl, mo, vo = _adamw_halves(w[FIRST][0], g_mine, g_theirs, mom_m[FIRST][0], mom_v[FIRST][0], core, "adamw_" + FIRST)
    grads[FIRST], deltas[FIRST], new_m[FIRST], new_v[FIRST] = g[None], dl[None], mo[None], vo[None]
    for name, g in (("ada_w", g_ada_w),) + tuple(zip(EARLY, dearly)) + tuple(zip(LATE, dlate)):
        dl, mo, vo = _adamw(w[name][0], g, mom_m[name][0], mom_v[name][0], "adamw_" + name)
        grads[name], deltas[name], new_m[name], new_v[name] = g[None], dl[None], mo[None], vo[None]

    return (loss, dx[None], *[grads[k] for k in WEIGHTS], *[deltas[k] for k in WEIGHTS], *[new_m[k] for k in WEIGHTS],
            *[new_v[k] for k in WEIGHTS])
```

```python
import functools

import jax
import jax.numpy as jnp
import numpy as np
from jax import lax
from jax.experimental import pallas as pl
from jax.experimental.pallas import tpu as pltpu

F32 = jnp.float32
BF16 = jnp.bfloat16
MESH = pl.DeviceIdType.MESH

VMEM_LIMIT_BYTES = 52 * 1024 * 1024
LANES = 128
SUBLANES = 8

D_FF_SPLIT = 2
RET_HEADS, RET_DK, RET_DV, RET_CHUNK = 8, 64, 128, 128
MLA_HEADS, MLA_Q_RANK, MLA_KV_RANK, MLA_NOPE, MLA_ROPE, MLA_V = 8, 512, 256, 128, 64, 128
GRID_W = 64
ROPE_BASE = 10000.0
RMS_EPS = 1e-6
GN_EPS = 1e-5
MIX_SPLITS = (RET_HEADS * RET_DK, RET_HEADS * RET_DK, RET_HEADS * RET_DV, RET_HEADS * RET_DV,
              MLA_Q_RANK, MLA_KV_RANK, MLA_ROPE)
MIX_IN = sum(MIX_SPLITS)
MIX_IN_PAD = 4096
ADAM_LR, ADAM_B1, ADAM_B2, ADAM_EPS, ADAM_WD, ADAM_STEP = 0.001, 0.9, 0.999, 1e-08, 0.01, 10


def _tile(n, pref, align):
    best = None
    t = align
    while t <= min(n, pref):
        if n % t == 0:
            best = t
        t += align
    return n if best is None else best


def _params(sem=None):
    return pltpu.CompilerParams(dimension_semantics=sem, vmem_limit_bytes=VMEM_LIMIT_BYTES)


def _matmul(a, b, mode, out_dtype, name, add=None, tm=1024, tn=1024, tk=2048):
    if mode == "nn":
        (m, k), (k2, n) = a.shape, b.shape
        dims = (((1,), (0,)), ((), ()))
    elif mode == "nt":
        (m, k), (n, k2) = a.shape, b.shape
        dims = (((1,), (1,)), ((), ()))
    else:
        (k, m), (k2, n) = a.shape, b.shape
        dims = (((0,), (0,)), ((), ()))
    assert k == k2, (a.shape, b.shape, mode)
    tm = _tile(m, tm, LANES if mode == "tn" else 16)
    tn = _tile(n, tn, LANES)
    tk = _tile(k, tk, LANES if mode != "tn" else 16)
    nk = k // tk
    a_spec = pl.BlockSpec((tk, tm), lambda i, j, kk: (kk, i)) if mode == "tn" else pl.BlockSpec((tm, tk), lambda i, j, kk: (i, kk))
    b_spec = pl.BlockSpec((tn, tk), lambda i, j, kk: (j, kk)) if mode == "nt" else pl.BlockSpec((tk, tn), lambda i, j, kk: (kk, j))
    o_spec = pl.BlockSpec((tm, tn), lambda i, j, kk: (i, j))
    has_add = add is not None

    def body(*refs):
        a_ref, b_ref = refs[0], refs[1]
        add_ref = refs[2] if has_add else None
        o_ref = refs[2 + has_add]
        p = lax.dot_general(a_ref[...].astype(BF16), b_ref[...].astype(BF16), dims, preferred_element_type=F32)
        if nk == 1:
            if has_add:
                p = p + add_ref[...].astype(F32)
            o_ref[...] = p.astype(out_dtype)
        else:
            acc = refs[3 + has_add]
            kk = pl.program_id(2)

            @pl.when(kk == 0)
            def _():
                acc[...] = p + add_ref[...].astype(F32) if has_add else p

            @pl.when(kk > 0)
            def _():
                acc[...] += p

            @pl.when(kk == nk - 1)
            def _():
                o_ref[...] = acc[...].astype(out_dtype)

    return pl.pallas_call(
        body, name=name, grid=(m // tm, n // tn, nk),
        in_specs=[a_spec, b_spec] + ([o_spec] if has_add else []),
        out_specs=o_spec,
        out_shape=jax.ShapeDtypeStruct((m, n), out_dtype),
        scratch_shapes=[pltpu.VMEM((tm, tn), F32)] if nk > 1 else [],
        compiler_params=_params(("parallel", "parallel", "arbitrary")),
    )(*((a, b, add) if has_add else (a, b)))


def _row_tile(t, n_ctx, d):
    pref = max(SUBLANES, min(256, (1 << 19) // d))
    r = _tile(int(np.gcd(t, n_ctx)) if n_ctx else t, pref, SUBLANES)
    return r, (n_ctx // r if n_ctx else 0)


def _seg_map(nct):
    if nct:
        return lambda i: (jnp.minimum(i // nct, 1), 0, 0)
    return lambda i: (0, 0, 0)


def _normmod_fwd(x, g, shift, scale, n_ctx, name):
    t, d = x.shape
    r, nct = _row_tile(t, n_ctx, d)

    def body(x_ref, g_ref, sh_ref, sc_ref, h_ref, ht_ref):
        xv = x_ref[...]
        rstd = lax.rsqrt(jnp.mean(xv * xv, axis=-1, keepdims=True) + RMS_EPS)
        n = xv * rstd * g_ref[...]
        h = n * (1.0 + sc_ref[0]) + sh_ref[0]
        h_ref[...] = h.astype(BF16)
        ht_ref[...] = h.T.astype(BF16)

    row = pl.BlockSpec((r, d), lambda i: (i, 0))
    seg = pl.BlockSpec((1, 1, d), _seg_map(nct))
    return pl.pallas_call(
        body, name=name, grid=(t // r,),
        in_specs=[row, pl.BlockSpec((1, d), lambda i: (0, 0)), seg, seg],
        out_specs=[row, pl.BlockSpec((d, r), lambda i: (0, i))],
        out_shape=[jax.ShapeDtypeStruct((t, d), BF16), jax.ShapeDtypeStruct((d, t), BF16)],
        compiler_params=_params(("parallel",)),
    )(x, g, shift, scale)


def _normmod_bwd(dh, x, g, shift, scale, dres, n_ctx, name):
    t, d = x.shape
    r, nct = _row_tile(t, n_ctx, d)
    has_res = dres is not None
    nseg = shift.shape[0]

    def body(*refs):
        dh_ref, x_ref, g_ref, sh_ref, sc_ref = refs[:5]
        dres_ref = refs[5] if has_res else None
        dx_ref, dg_ref, dsh_ref, dsc_ref = refs[5 + has_res:]
        i = pl.program_id(0)
        xv = x_ref[...]
        dhv = dh_ref[...].astype(F32)
        rstd = lax.rsqrt(jnp.mean(xv * xv, axis=-1, keepdims=True) + RMS_EPS)
        y = xv * rstd
        gv = g_ref[...]
        dn = dhv * (1.0 + sc_ref[0])
        dy = dn * gv
        dx = rstd * (dy - y * jnp.mean(dy * y, axis=-1, keepdims=True))
        if has_res:
            dx = dx + dres_ref[...]
        dx_ref[...] = dx

        @pl.when(i == 0)
        def _():
            dg_ref[...] = jnp.zeros_like(dg_ref)

        @pl.when(jnp.logical_or(i == 0, i == nct))
        def _():
            dsh_ref[...] = jnp.zeros_like(dsh_ref)
            dsc_ref[...] = jnp.zeros_like(dsc_ref)

        dg_ref[...] += jnp.sum(dn * y, axis=0, keepdims=True)
        dsh_ref[0] += jnp.sum(dhv, axis=0, keepdims=True)
        dsc_ref[0] += jnp.sum(dhv * (y * gv), axis=0, keepdims=True)

    row = pl.BlockSpec((r, d), lambda i: (i, 0))
    seg = pl.BlockSpec((1, 1, d), _seg_map(nct))
    vec = pl.BlockSpec((1, d), lambda i: (0, 0))
    return pl.pallas_call(
        body, name=name, grid=(t // r,),
        in_specs=[row, row, vec, seg, seg] + ([row] if has_res else []),
        out_specs=[row, vec, seg, seg],
        out_shape=[jax.ShapeDtypeStruct((t, d), F32), jax.ShapeDtypeStruct((1, d), F32),
                   jax.ShapeDtypeStruct((nseg, 1, d), F32), jax.ShapeDtypeStruct((nseg, 1, d), F32)],
        compiler_params=_params(("arbitrary",)),
    )(*((dh, x, g, shift, scale, dres) if has_res else (dh, x, g, shift, scale)))


def _gated_res_fwd(x, y, gate, coef, n_ctx, name):
    t, d = x.shape
    r, nct = _row_tile(t, n_ctx, d)

    def body(x_ref, y_ref, gt_ref, o_ref):
        o_ref[...] = x_ref[...] + (coef * gt_ref[0]) * y_ref[...]

    row = pl.BlockSpec((r, d), lambda i: (i, 0))
    return pl.pallas_call(
        body, name=name, grid=(t // r,),
        in_specs=[row, row, pl.BlockSpec((1, 1, d), _seg_map(nct))],
        out_specs=row, out_shape=jax.ShapeDtypeStruct((t, d), F32),
        compiler_params=_params(("parallel",)),
    )(x, y, gate)


def _matmul_gated_res(a, w, x, gate, coef, name, tm):
    t, k = a.shape
    d = w.shape[1]
    tm = _tile(t, tm, 16)
    tn = _tile(d, 512, LANES)

    def body(a_ref, w_ref, x_ref, gt_ref, o_ref, y_ref):
        y = jnp.dot(a_ref[...], w_ref[...], preferred_element_type=F32)
        o_ref[...] = x_ref[...] + (coef * gt_ref[0]) * y
        y_ref[...] = y.astype(BF16)

    blk = pl.BlockSpec((tm, tn), lambda i, j: (i, j))
    return pl.pallas_call(
        body, name=name, grid=(t // tm, d // tn),
        in_specs=[pl.BlockSpec((tm, k), lambda i, j: (i, 0)), pl.BlockSpec((k, tn), lambda i, j: (0, j)), blk,
                  pl.BlockSpec((1, 1, tn), lambda i, j: (0, 0, j))],
        out_specs=[blk, blk], out_shape=[jax.ShapeDtypeStruct((t, d), F32), jax.ShapeDtypeStruct((t, d), BF16)],
        compiler_params=_params(("parallel", "parallel")),
    )(a, w, x, gate)


def _gated_res_bwd(dout, y, gate, coef, n_ctx, name):
    t, d = dout.shape
    r, nct = _row_tile(t, n_ctx, d)
    nseg = gate.shape[0]

    def body(do_ref, y_ref, gt_ref, dy_ref, dgt_ref):
        i = pl.program_id(0)
        dov = do_ref[...] * coef
        dy_ref[...] = (dov * gt_ref[0]).astype(BF16)

        @pl.when(jnp.logical_or(i == 0, i == nct))
        def _():
            dgt_ref[...] = jnp.zeros_like(dgt_ref)

        dgt_ref[0] += jnp.sum(dov * y_ref[...], axis=0, keepdims=True)

    row = pl.BlockSpec((r, d), lambda i: (i, 0))
    seg = pl.BlockSpec((1, 1, d), _seg_map(nct))
    return pl.pallas_call(
        body, name=name, grid=(t // r,),
        in_specs=[row, row, seg], out_specs=[row, seg],
        out_shape=[jax.ShapeDtypeStruct((t, d), BF16), jax.ShapeDtypeStruct((nseg, 1, d), F32)],
        compiler_params=_params(("arbitrary",)),
    )(dout, y, gate)


def _swiglu_matmul(h, wg, wu, name, tm, shards=()):
    t, k = h.shape
    f = wg.shape[1]
    tm = _tile(t, tm, LANES)
    tn = _tile(f, 512, LANES)
    n = len(shards)
    grid = (t // tm, f // tn)

    def body(h_ref, wg_ref, wu_ref, *rest):
        ins, (g_ref, u_ref, a_ref, at_ref), outs, sems = rest[:n], rest[n:n + 4], rest[n + 4:2 * n + 4], rest[2 * n + 4:]
        if n:
            first, last = _first_last_step(grid)
            pl.when(first)(lambda: _gather_start(ins, outs, sems))
        hv = h_ref[...]
        g = jnp.dot(hv, wg_ref[...], preferred_element_type=F32)
        u = jnp.dot(hv, wu_ref[...], preferred_element_type=F32)
        a = g * jax.nn.sigmoid(g) * u
        g_ref[...] = g.astype(BF16)
        u_ref[...] = u.astype(BF16)
        a_ref[...] = a.astype(BF16)
        at_ref[...] = a.T.astype(BF16)
        if n:
            pl.when(last)(lambda: _gather_finish(ins, outs, sems))

    w_spec = pl.BlockSpec((k, tn), lambda i, j: (0, j))
    o_spec = pl.BlockSpec((tm, tn), lambda i, j: (i, j))
    return pl.pallas_call(
        body, name=name, grid=grid,
        in_specs=[pl.BlockSpec((tm, k), lambda i, j: (i, 0)), w_spec, w_spec] + [HBM_SPEC] * n,
        out_specs=[o_spec, o_spec, o_spec, pl.BlockSpec((tn, tm), lambda i, j: (j, i))] + [HBM_SPEC] * n,
        out_shape=[jax.ShapeDtypeStruct((t, f), BF16)] * 3 + [jax.ShapeDtypeStruct((f, t), BF16)] + _gather_out_shapes(shards),
        scratch_shapes=_gather_sems(n) if n else [],
        compiler_params=_params(("arbitrary", "arbitrary") if n else ("parallel", "parallel")),
    )(h, wg, wu, *shards)


def _swiglu_bwd_matmul(dy, wo, gg, uu, name, tm, parts=()):
    t, d = dy.shape
    f = wo.shape[0]
    tm = _tile(t, tm, 16)
    tn = _tile(f, 512, LANES)
    n = len(parts)
    grid = (t // tm, f // tn)

    def body(dy_ref, wo_ref, g_ref, u_ref, *rest):
        ins, (dg_ref, du_ref), outs, sems = rest[:n], rest[n:n + 2], rest[n + 2:2 * n + 2], rest[2 * n + 2:]
        if n:
            first, last = _first_last_step(grid)
            pl.when(first)(lambda: _scatter_start(ins, outs, sems))
        da = lax.dot_general(dy_ref[...], wo_ref[...], _NT, preferred_element_type=F32)
        gv = g_ref[...].astype(F32)
        sg = jax.nn.sigmoid(gv)
        dg_ref[...] = (da * u_ref[...].astype(F32) * (sg * (1.0 + gv * (1.0 - sg)))).astype(BF16)
        du_ref[...] = (da * (gv * sg)).astype(BF16)
        if n:
            pl.when(last)(lambda: _scatter_finish(ins, outs, sems))

    blk = pl.BlockSpec((tm, tn), lambda i, j: (i, j))
    return pl.pallas_call(
        body, name=name, grid=grid,
        in_specs=[pl.BlockSpec((tm, d), lambda i, j: (i, 0)), pl.BlockSpec((tn, d), lambda i, j: (j, 0)), blk, blk] + [HBM_SPEC] * n,
        out_specs=[blk, blk] + [HBM_SPEC] * n,
        out_shape=[jax.ShapeDtypeStruct((t, f), BF16)] * 2 + [jax.ShapeDtypeStruct(p.shape, p.dtype) for p in parts],
        scratch_shapes=_scatter_sems(n) if n else [],
        compiler_params=_params(("arbitrary", "arbitrary") if n else ("parallel", "parallel")),
    )(dy, wo, gg, uu, *parts)


def _matmul_nt_hiding_scatter(a, b, name, tm, parts):
    m, k = a.shape
    n_out = b.shape[0]
    tm = _tile(m, tm, 16)
    tn = _tile(n_out, 512, LANES)
    n = len(parts)
    grid = (m // tm, n_out // tn)

    def body(a_ref, b_ref, *rest):
        ins, o_ref, outs, sems = rest[:n], rest[n], rest[n + 1:2 * n + 1], rest[2 * n + 1:]
        first, last = _first_last_step(grid)
        pl.when(first)(lambda: _scatter_start(ins, outs, sems))
        o_ref[...] = lax.dot_general(a_ref[...], b_ref[...], _NT, preferred_element_type=F32)
        pl.when(last)(lambda: _scatter_finish(ins, outs, sems))

    return pl.pallas_call(
        body, name=name, grid=grid,
        in_specs=[pl.BlockSpec((tm, k), lambda i, j: (i, 0)), pl.BlockSpec((tn, k), lambda i, j: (j, 0))] + [HBM_SPEC] * n,
        out_specs=[pl.BlockSpec((tm, tn), lambda i, j: (i, j))] + [HBM_SPEC] * n,
        out_shape=[jax.ShapeDtypeStruct((m, n_out), F32)] + [jax.ShapeDtypeStruct(p.shape, p.dtype) for p in parts],
        scratch_shapes=_scatter_sems(n), compiler_params=_params(("arbitrary", "arbitrary")),
    )(a, b, *parts)


def _tok_tile(t):
    return 1024 if t % 1024 == 0 else 768 if t % 768 == 0 else _tile(t, 1024, 16)


FF_TILE = 1408
FULL_K = 1 << 30


def _make_ffn(n_ctx, tag, has_late):
    @jax.custom_vjp
    def ffn(x, g, shift, scale, gate, wg, wu, w_in_block, wo, late):
        return fwd(x, g, shift, scale, gate, wg, wu, w_in_block, wo, late)[0]

    def fwd(x, g, shift, scale, gate, wg, wu, w_in_block, wo, late):
        tt = _tok_tile(x.shape[0])
        h, ht = _normmod_fwd(x, g, shift, scale, n_ctx, tag + "_norm")
        blocks = [s.astype(BF16) for s in late]
        gg, uu, a, at, *stacked = _swiglu_matmul(h, wg, wu, tag + "_mm_gu", tt, blocks)
        stacked = _fill_own_slot(stacked, blocks)
        if has_late:
            wo = stacked[0].reshape(-1, stacked[0].shape[-1])
        if n_ctx:
            y = _matmul(a, wo, "nn", F32, tag + "_mm_o", tm=tt, tn=512, tk=FULL_K)
            out = _gated_res_fwd(x, y, gate, 0.5, n_ctx, tag + "_res")
        else:
            out, y = _matmul_gated_res(a, wo, x, gate, 0.5, tag + "_mm_o_res", tt)
        return (out, tuple(stacked[1:])), (x, g, shift, scale, gate, wg, wu, wo, ht, gg, uu, at, y)

    def bwd(res, cts):
        x, g, shift, scale, gate, wg, wu, wo, ht, gg, uu, at, y = res
        dout, dgathered = cts
        tt = _tok_tile(x.shape[0])
        dy, dgate = _gated_res_bwd(dout, y, gate, 0.5, n_ctx, tag + "_res_b")
        dwo = _matmul(at, dy, "nn", BF16, tag + "_dwo", tm=512, tn=512, tk=FULL_K)
        if has_late:
            own = {}

            def scatter(parts):
                own["dgg"], own["duu"], *landed = _swiglu_bwd_matmul(dy, wo, gg, uu, tag + "_da_act", tt, parts)
                return landed

            top = lax.axis_index("c") == 0
            pending = [dwo.reshape(4, dwo.shape[0] // 4, dwo.shape[1])] + list(dgathered)
            dlate = tuple(jnp.concatenate([jnp.where(top, mine, theirs), jnp.where(top, theirs, mine)], axis=0)
                          for mine, theirs in _reduce_scatter_grads(pending, tag, scatter))
            dgg, duu, dwo = own["dgg"], own["duu"], None
        else:
            dgg, duu = _swiglu_bwd_matmul(dy, wo, gg, uu, tag + "_da_act", tt)
            dlate = ()
        dwg = _matmul(ht, dgg, "nn", BF16, tag + "_dwg", tm=512, tn=512, tk=FULL_K)
        dwu = _matmul(ht, duu, "nn", BF16, tag + "_dwu", tm=512, tn=512, tk=FULL_K)
        if has_late:
            def scatter_in(parts):
                own["dh"], *landed = _matmul_nt_hiding_scatter(dgg, wg, tag + "_dh_g", tt, parts)
                return landed

            half = dwg.shape[1] // 2
            slots = jnp.stack([dwg[:, :half], dwg[:, half:], dwu[:, :half], dwu[:, half:]])
            ((mine, theirs),) = _reduce_scatter_grads([slots], tag + "_in", scatter_in)
            dw_in_block = jnp.concatenate([jnp.where(top, mine, theirs), jnp.where(top, theirs, mine)], axis=0)
            dh, dwg, dwu = own["dh"], jnp.zeros_like(wg), jnp.zeros_like(wu)
        else:
            dh = _matmul(dgg, wg, "nt", F32, tag + "_dh_g", tm=tt, tn=512, tk=FULL_K)
            dw_in_block = None
        dh = _matmul(duu, wu, "nt", F32, tag + "_dh_u", add=dh, tm=tt, tn=512, tk=FULL_K)
        dx, dg, dshift, dscale = _normmod_bwd(dh, x, g, shift, scale, dout, n_ctx, tag + "_norm_b")
        return dx, dg, dshift, dscale, dgate, dwg, dwu, dw_in_block, dwo, dlate

    ffn.defvjp(fwd, bwd)
    return ffn


def _make_normmod_linear(n_ctx, tag):
    @jax.custom_vjp
    def op(x, g, shift, scale, w):
        return fwd(x, g, shift, scale, w)[0]

    def fwd(x, g, shift, scale, w):
        h, ht = _normmod_fwd(x, g, shift, scale, n_ctx, tag + "_norm")
        y = _matmul(h, w, "nn", F32, tag + "_mm", tm=_tok_tile(x.shape[0]))
        return y, (x, g, shift, scale, w, ht)

    def bwd(res, dy):
        x, g, shift, scale, w, ht = res
        tt = _tok_tile(x.shape[0])
        dyb = dy.astype(BF16)
        dw = _matmul(ht, dyb, "nn", BF16, tag + "_dw", tm=512, tn=512, tk=FULL_K)
        dh = _matmul(dyb, w, "nt", F32, tag + "_dh", tm=tt, tn=512, tk=FULL_K)
        dx, dg, dshift, dscale = _normmod_bwd(dh, x, g, shift, scale, None, n_ctx, tag + "_norm_b")
        return dx, dg, dshift, dscale, dw

    op.defvjp(fwd, bwd)
    return op


def _make_linear_gated_res(tag):
    @jax.custom_vjp
    def op(x, a, w, gate):
        return fwd(x, a, w, gate)[0]

    def fwd(x, a, w, gate):
        ab = a.astype(BF16)
        out, y = _matmul_gated_res(ab, w, x, gate, 1.0, tag + "_mm_res", _tok_tile(x.shape[0]))
        return out, (ab, w, gate, y)

    def bwd(res, dout):
        ab, w, gate, y = res
        tt = _tok_tile(dout.shape[0])
        dy, dgate = _gated_res_bwd(dout, y, gate, 1.0, 0, tag + "_res_b")
        dw = _matmul(ab.T, dy, "nn", BF16, tag + "_dw", tm=512, tn=512, tk=FULL_K)
        da = _matmul(dy, w, "nt", F32, tag + "_da", tm=tt)
        return dout, da, dw, dgate

    op.defvjp(fwd, bwd)
    return op


def _final_loss_call(x, g, target, name):
    t, d = x.shape
    r = _tile(t, 256, SUBLANES)

    def body(x_ref, g_ref, t_ref, loss_ref, dx_ref, dg_ref):
        i = pl.program_id(0)
        xv = x_ref[...]
        gv = g_ref[...]
        rstd = lax.rsqrt(jnp.mean(xv * xv, axis=-1, keepdims=True) + RMS_EPS)
        xh = xv * rstd
        e = xh * gv - t_ref[...]
        dy = e * (1.0 / d)
        dn = dy * gv
        dx_ref[...] = rstd * (dn - xh * jnp.mean(dn * xh, axis=-1, keepdims=True))

        @pl.when(i == 0)
        def _():
            loss_ref[...] = jnp.zeros_like(loss_ref)
            dg_ref[...] = jnp.zeros_like(dg_ref)

        loss_ref[...] += 0.5 * jnp.sum(jnp.mean(e * e, axis=-1, keepdims=True), axis=0, keepdims=True)
        dg_ref[...] += jnp.sum(dy * xh, axis=0, keepdims=True)

    row = pl.BlockSpec((r, d), lambda i: (i, 0))
    vec = pl.BlockSpec((1, d), lambda i: (0, 0))
    return pl.pallas_call(
        body, name=name, grid=(t // r,),
        in_specs=[row, vec, row], out_specs=[pl.BlockSpec((1, 1), lambda i: (0, 0)), row, vec],
        out_shape=[jax.ShapeDtypeStruct((1, 1), F32), jax.ShapeDtypeStruct((t, d), F32), jax.ShapeDtypeStruct((1, d), F32)],
        compiler_params=_params(("arbitrary",)),
    )(x, g, target)


@jax.custom_vjp
def _final_loss(x, g, target):
    return _final_loss_call(x, g, target, "final_loss")[0][0, 0]


def _final_loss_fwd(x, g, target):
    loss, dx, dg = _final_loss_call(x, g, target, "final_loss")
    return loss[0, 0], (dx, dg, target)


def _final_loss_bwd(res, dl):
    dx, dg, target = res
    return dx * dl, dg * dl, jnp.zeros_like(target)


_final_loss.defvjp(_final_loss_fwd, _final_loss_bwd)


_NT = (((1,), (1,)), ((), ()))
_TN = (((0,), (0,)), ((), ()))
ATTN_Q_TILE = 512
ATTN_K_CHUNK = 1408


def _first_last_step(grid):
    ids = [pl.program_id(a) for a in range(len(grid))]
    first = functools.reduce(jnp.logical_and, [i == 0 for i in ids])
    last = functools.reduce(jnp.logical_and, [i == g - 1 for i, g in zip(ids, grid)])
    return first, last


def _attn_fwd_call(q, k, v, scale, shards=()):
    h, nq, dq = q.shape
    nk, dv = v.shape[1], v.shape[2]
    tq = _tile(nq, ATTN_Q_TILE, 16)
    ck = _tile(nk, ATTN_K_CHUNK, LANES)
    nchunk = nk // ck
    n = len(shards)
    grid = (h, nq // tq)

    exp2_scale = scale * float(np.log2(np.e))

    def body(q_ref, k_ref, v_ref, *rest):
        ins, (o_ref, lse_ref), outs = rest[:n], rest[n:n + 2], rest[n + 2:2 * n + 2]
        m_scr, l_scr, acc_scr = rest[2 * n + 2:2 * n + 5]
        sems = rest[2 * n + 5:]
        if n:
            first, last = _first_last_step(grid)
            pl.when(first)(lambda: _gather_start(ins, outs, sems))
        qv = q_ref[0]
        m_scr[...] = jnp.full_like(m_scr, -jnp.inf)
        l_scr[...] = jnp.zeros_like(l_scr)
        acc_scr[...] = jnp.zeros_like(acc_scr)

        scores = lambda c: lax.dot_general(qv, k_ref[0, c * ck:(c + 1) * ck, :], _NT, preferred_element_type=F32)
        s_next = scores(0)
        for c in range(nchunk):
            s = s_next
            if c + 1 < nchunk:
                s_next = scores(c + 1)
            m_old = m_scr[...]
            m_new = jnp.maximum(m_old, jnp.max(s, axis=-1, keepdims=True))
            alpha = jnp.exp2((m_old - m_new) * exp2_scale)
            p = jnp.exp2((s - m_new) * exp2_scale)
            l_scr[...] = alpha * l_scr[...] + jnp.sum(p, axis=-1, keepdims=True)
            acc_scr[...] = alpha * acc_scr[...] + jnp.dot(p.astype(BF16), v_ref[0, c * ck:(c + 1) * ck, :], preferred_element_type=F32)
            m_scr[...] = m_new

        o_ref[0] = acc_scr[...] / l_scr[...]
        lse_ref[0] = m_scr[...] * scale + jnp.log(l_scr[...])
        if n:
            pl.when(last)(lambda: _gather_finish(ins, outs, sems))

    keys = lambda d: pl.BlockSpec((1, nk, d), lambda hh, i: (hh, 0, 0))
    return pl.pallas_call(
        body, name="attn_fwd", grid=grid,
        in_specs=[pl.BlockSpec((1, tq, dq), lambda hh, i: (hh, i, 0)), keys(dq), keys(dv)] + [HBM_SPEC] * n,
        out_specs=[pl.BlockSpec((1, tq, dv), lambda hh, i: (hh, i, 0)),
                   pl.BlockSpec((1, tq, 1), lambda hh, i: (hh, i, 0))] + [HBM_SPEC] * n,
        out_shape=[jax.ShapeDtypeStruct((h, nq, dv), F32), jax.ShapeDtypeStruct((h, nq, 1), F32)] + _gather_out_shapes(shards),
        scratch_shapes=[pltpu.VMEM((tq, 1), F32), pltpu.VMEM((tq, 1), F32), pltpu.VMEM((tq, dv), F32)] + (_gather_sems(n) if n else []),
        compiler_params=_params(("arbitrary", "arbitrary") if n else ("parallel", "arbitrary")),
    )(q, k, v, *shards)


def _attn_bwd_call(q, k, v, o, do, lse, scale, parts=()):
    h, nq, dq = q.shape
    nk, dv = v.shape[1], v.shape[2]
    tq = _tile(nq, ATTN_Q_TILE, 16)
    ck = _tile(nk, ATTN_K_CHUNK, LANES)
    nchunk = nk // ck
    n = len(parts)
    grid = (h, nq // tq)

    log2e = float(np.log2(np.e))
    exp2_scale = scale * log2e

    def body(q_ref, k_ref, v_ref, o_ref, do_ref, lse_ref, *rest):
        ins, (dq_ref, dk_ref, dv_ref), outs, sems = rest[:n], rest[n:n + 3], rest[n + 3:2 * n + 3], rest[2 * n + 3:]
        if n:
            first, last = _first_last_step(grid)
            pl.when(first)(lambda: _scatter_start(ins, outs, sems))
        i = pl.program_id(1)

        @pl.when(i == 0)
        def _():
            dk_ref[...] = jnp.zeros_like(dk_ref)
            dv_ref[...] = jnp.zeros_like(dv_ref)

        qv = q_ref[0]
        dov = do_ref[0]
        dob = dov.astype(BF16)
        delta = jnp.sum(dov * o_ref[0], axis=-1, keepdims=True)
        lse2 = lse_ref[0] * log2e

        def scores(c):
            rows = slice(c * ck, (c + 1) * ck)
            return (lax.dot_general(qv, k_ref[0, rows, :], _NT, preferred_element_type=F32),
                    lax.dot_general(dob, v_ref[0, rows, :], _NT, preferred_element_type=F32))

        nxt = scores(0)
        dq_acc = None
        for c in range(nchunk):
            rows = slice(c * ck, (c + 1) * ck)
            s, dp = nxt
            if c + 1 < nchunk:
                nxt = scores(c + 1)
            p = jnp.exp2(s * exp2_scale - lse2)
            ds = (p * (dp - delta) * scale).astype(BF16)
            dv_ref[0, rows, :] += lax.dot_general(p.astype(BF16), dob, _TN, preferred_element_type=F32)
            dk_ref[0, rows, :] += lax.dot_general(ds, qv, _TN, preferred_element_type=F32)
            part = jnp.dot(ds, k_ref[0, rows, :], preferred_element_type=F32)
            dq_acc = part if dq_acc is None else dq_acc + part
        dq_ref[0] = dq_acc
        if n:
            pl.when(last)(lambda: _scatter_finish(ins, outs, sems))

    qspec = lambda d: pl.BlockSpec((1, tq, d), lambda hh, i: (hh, i, 0))
    kspec = lambda d: pl.BlockSpec((1, nk, d), lambda hh, i: (hh, 0, 0), pipeline_mode=pl.Buffered(1))
    return pl.pallas_call(
        body, name="attn_bwd", grid=grid,
        in_specs=[qspec(dq), kspec(dq), kspec(dv), qspec(dv), qspec(dv), qspec(1)] + [HBM_SPEC] * n,
        out_specs=[qspec(dq), kspec(dq), kspec(dv)] + [HBM_SPEC] * n,
        out_shape=[jax.ShapeDtypeStruct((h, nq, dq), F32), jax.ShapeDtypeStruct((h, nk, dq), F32),
                   jax.ShapeDtypeStruct((h, nk, dv), F32)] + [jax.ShapeDtypeStruct(p.shape, p.dtype) for p in parts],
        scratch_shapes=_scatter_sems(n) if n else [],
        compiler_params=_params(("arbitrary", "arbitrary") if n else ("parallel", "arbitrary")),
    )(q, k, v, o, do, lse, *parts)


@jax.custom_vjp
def _attention_gather(q, k, v, shards):
    return _attention_gather_fwd(q, k, v, shards)[0]


def _attention_gather_fwd(q, k, v, shards):
    scale = q.shape[-1] ** -0.5
    qb, kb, vb = q.astype(BF16), k.astype(BF16), v.astype(BF16)
    blocks = [s.astype(BF16) for s in shards]
    o, lse, *stacked = _attn_fwd_call(qb, kb, vb, scale, blocks)
    return (o, tuple(_fill_own_slot(stacked, blocks))), (qb, kb, vb, o, lse)


def _attention_gather_bwd(res, cts):
    qb, kb, vb, o, lse = res
    do, dstacked = cts
    scale = qb.shape[-1] ** -0.5
    if not dstacked:
        return (*_attn_bwd_call(qb, kb, vb, o, do, lse, scale), ())
    own = {}

    def scatter(parts):
        own["dq"], own["dk"], own["dv"], *landed = _attn_bwd_call(qb, kb, vb, o, do, lse, scale, parts)
        return landed

    top = lax.axis_index("c") == 0
    grads = tuple(jnp.concatenate([jnp.where(top, mine, theirs), jnp.where(top, theirs, mine)], axis=0)
                  for mine, theirs in _reduce_scatter_grads(list(dstacked), "attn", scatter))
    return own["dq"], own["dk"], own["dv"], grads


_attention_gather.defvjp(_attention_gather_fwd, _attention_gather_bwd)


def _attention(q, k, v):
    return _attention_gather(q, k, v, ())[0]


RET_UNROLL = 4


def _bf(x):
    return x.astype(BF16)


def _dot(a, b, dims=(((1,), (0,)), ((), ()))):
    return lax.dot_general(_bf(a), _bf(b), dims, preferred_element_type=F32)


def _sum_all(x):
    return jnp.sum(jnp.sum(x, axis=1, keepdims=True), axis=0, keepdims=True)


def _ret_consts(lgf_ref, lgb_ref):
    c = RET_CHUNK
    lgf = lgf_ref[0][:, :1]
    lgb = lgb_ref[0][:, :1]
    diff = (lax.broadcasted_iota(jnp.int32, (c, c), 0) - lax.broadcasted_iota(jnp.int32, (c, c), 1)).astype(F32)
    mf = diff >= 0
    dmat = jnp.where(mf, jnp.exp(lgf * jnp.where(mf, diff, 0.0)), jnp.exp(lgb * jnp.where(mf, 0.0, -diff)))
    col = lax.broadcasted_iota(jnp.int32, (c, 1), 0).astype(F32)
    return dict(diff=diff, mf=mf, dmat=dmat, col=col,
                xif=jnp.exp(lgf * (col + 1.0)), zf=jnp.exp(lgf * (c - 1.0 - col)),
                xib=jnp.exp(lgb * (c - col)), zb=jnp.exp(lgb * col),
                gf=jnp.exp(lgf * c), gb=jnp.exp(lgb * c))


def _ret_rows(n):
    return pl.ds(pl.multiple_of(n * RET_CHUNK, RET_CHUNK), RET_CHUNK)


def _ret_fwd_call(q, k, v, lgf, lgb, s0f, s0b):
    h, n_tok, dk = q.shape
    dv = v.shape[-1]
    nc = n_tok // RET_CHUNK

    def body(q_ref, k_ref, v_ref, lgf_ref, lgb_ref, s0f_ref, s0b_ref, y_ref, sff_ref, sbf_ref, sb_scr):
        cs = _ret_consts(lgf_ref, lgb_ref)

        sbf_ref[0] = s0b_ref[0]

        @pl.loop(0, nc, unroll=RET_UNROLL)
        def _(t):
            n = nc - 1 - t
            sb = sbf_ref[0]
            sb_scr[n] = sb
            sbf_ref[0] = cs["gb"] * sb + _dot(k_ref[0, _ret_rows(n), :] * cs["zb"], v_ref[0, _ret_rows(n), :], _TN)

        sff_ref[0] = s0f_ref[0]

        @pl.loop(0, nc, unroll=RET_UNROLL)
        def _(n):
            sf = sff_ref[0]
            qc, kc, vc = q_ref[0, _ret_rows(n), :], k_ref[0, _ret_rows(n), :], v_ref[0, _ret_rows(n), :]
            p = _dot(qc, kc, _NT) * cs["dmat"]
            y_ref[0, _ret_rows(n), :] = _dot(p, vc) + _dot(qc * cs["xif"], sf) + _dot(qc * cs["xib"], sb_scr[n])
            sff_ref[0] = cs["gf"] * sf + _dot(kc * cs["zf"], vc, _TN)

    tok = lambda d: pl.BlockSpec((1, n_tok, d), lambda hh: (hh, 0, 0), pipeline_mode=pl.Buffered(1))
    lg = pl.BlockSpec((1, 1, LANES), lambda hh: (hh, 0, 0))
    st = pl.BlockSpec((1, dk, dv), lambda hh: (hh, 0, 0))
    return pl.pallas_call(
        body, name="ret_fwd_%d" % n_tok, grid=(h,),
        in_specs=[tok(dk), tok(dk), tok(dv), lg, lg, st, st], out_specs=[tok(dv), st, st],
        out_shape=[jax.ShapeDtypeStruct((h, n_tok, dv), F32)] + [jax.ShapeDtypeStruct((h, dk, dv), F32)] * 2,
        scratch_shapes=[pltpu.VMEM((nc, dk, dv), F32)],
        compiler_params=_params(("parallel",)),
    )(q, k, v, lgf, lgb, s0f, s0b)


def _ret_bwd_call(q, k, v, lgf, lgb, s0f, s0b, dy, dsff, dsbf):
    h, n_tok, dk = q.shape
    dv = v.shape[-1]
    nc = n_tok // RET_CHUNK
    c = float(RET_CHUNK)

    def body(q_ref, k_ref, v_ref, lgf_ref, lgb_ref, s0f_ref, s0b_ref, dy_ref, dsff_ref, dsbf_ref,
             dq_ref, dk_ref, dv_ref, dlgf_ref, dlgb_ref, ds0f_ref, ds0b_ref, sb_scr, gf_scr, st_a, st_b):
        cs = _ret_consts(lgf_ref, lgb_ref)

        st_a[...] = s0b_ref[0]
        st_b[...] = dsff_ref[0]

        @pl.loop(0, nc, unroll=RET_UNROLL)
        def _(t):
            n = nc - 1 - t
            sb, gf_next = st_a[...], st_b[...]
            sb_scr[n] = sb
            gf_scr[n] = gf_next
            qc, kc, vc, dyc = (r[0, _ret_rows(n), :] for r in (q_ref, k_ref, v_ref, dy_ref))
            st_a[...] = cs["gb"] * sb + _dot(kc * cs["zb"], vc, _TN)
            st_b[...] = _dot(qc * cs["xif"], dyc, _TN) + cs["gf"] * gf_next

        ds0f_ref[0] = st_b[...]

        st_a[...] = s0f_ref[0]
        st_b[...] = dsbf_ref[0]
        dlgf_ref[...] = jnp.zeros_like(dlgf_ref)
        dlgb_ref[...] = jnp.zeros_like(dlgb_ref)

        @pl.loop(0, nc, unroll=RET_UNROLL)
        def _(n):
            sf, gb_prev = st_a[...], st_b[...]
            sb, gf_next = sb_scr[n], gf_scr[n]
            qc, kc, vc, dyc = (r[0, _ret_rows(n), :] for r in (q_ref, k_ref, v_ref, dy_ref))
            a = _dot(qc, kc, _NT)
            dp = _dot(dyc, vc, _NT)
            da = _bf(dp * cs["dmat"])
            dqf = _dot(dyc, sf, _NT)
            dqb = _dot(dyc, sb, _NT)
            dkf = _dot(vc, gf_next, _NT)
            dkb = _dot(vc, gb_prev, _NT)
            dq_ref[0, _ret_rows(n), :] = _dot(da, kc) + dqf * cs["xif"] + dqb * cs["xib"]
            dk_ref[0, _ret_rows(n), :] = _dot(da, qc, _TN) + dkf * cs["zf"] + dkb * cs["zb"]
            dv_ref[0, _ret_rows(n), :] = (_dot(a * cs["dmat"], dyc, _TN) + _dot(kc * cs["zf"], gf_next)
                                         + _dot(kc * cs["zb"], gb_prev))
            w = dp * a * cs["dmat"] * cs["diff"]
            row = lambda x: jnp.sum(x, axis=1, keepdims=True)
            dlgf_ref[0] += (_sum_all(jnp.where(cs["mf"], w, 0.0))
                            + _sum_all((cs["col"] + 1.0) * cs["xif"] * row(dqf * qc) + (c - 1.0 - cs["col"]) * cs["zf"] * row(dkf * kc))
                            + c * cs["gf"] * _sum_all(gf_next * sf))
            dlgb_ref[0] += (_sum_all((c - cs["col"]) * cs["xib"] * row(dqb * qc) + cs["col"] * cs["zb"] * row(dkb * kc))
                            + c * cs["gb"] * _sum_all(gb_prev * sb) - _sum_all(jnp.where(cs["mf"], 0.0, w)))
            st_a[...] = cs["gf"] * sf + _dot(kc * cs["zf"], vc, _TN)
            st_b[...] = _dot(qc * cs["xib"], dyc, _TN) + cs["gb"] * gb_prev

        ds0b_ref[0] = st_b[...]

    tok = lambda d: pl.BlockSpec((1, n_tok, d), lambda hh: (hh, 0, 0), pipeline_mode=pl.Buffered(1))
    lg = pl.BlockSpec((1, 1, LANES), lambda hh: (hh, 0, 0))
    st = pl.BlockSpec((1, dk, dv), lambda hh: (hh, 0, 0))
    return pl.pallas_call(
        body, name="ret_bwd_%d" % n_tok, grid=(h,),
        in_specs=[tok(dk), tok(dk), tok(dv), lg, lg, st, st, tok(dv), st, st],
        out_specs=[tok(dk), tok(dk), tok(dv), lg, lg, st, st],
        out_shape=[jax.ShapeDtypeStruct((h, n_tok, dk), F32)] * 2 + [jax.ShapeDtypeStruct((h, n_tok, dv), F32)]
        + [jax.ShapeDtypeStruct((h, 1, LANES), F32)] * 2 + [jax.ShapeDtypeStruct((h, dk, dv), F32)] * 2,
        scratch_shapes=[pltpu.VMEM((nc, dk, dv), F32), pltpu.VMEM((nc, dk, dv), F32), pltpu.VMEM((dk, dv), F32), pltpu.VMEM((dk, dv), F32)],
        compiler_params=_params(("parallel",)),
    )(q, k, v, lgf, lgb, s0f, s0b, dy, dsff, dsbf)


def _lane_bcast(lg):
    return jnp.broadcast_to(lg[:, None, None], (lg.shape[0], 1, LANES))


@jax.custom_vjp
def _retention(q, k, v, lgf, lgb, s0f, s0b):
    return tuple(_ret_fwd_call(q, k, v, _lane_bcast(lgf), _lane_bcast(lgb), s0f, s0b))


def _retention_fwd(q, k, v, lgf, lgb, s0f, s0b):
    return _retention(q, k, v, lgf, lgb, s0f, s0b), (q, k, v, lgf, lgb, s0f, s0b)


def _retention_bwd(res, cts):
    q, k, v, lgf, lgb, s0f, s0b = res
    dy, dsff, dsbf = cts
    dq, dk, dv, dlgf, dlgb, ds0f, ds0b = _ret_bwd_call(q, k, v, _lane_bcast(lgf), _lane_bcast(lgb), s0f, s0b, dy, dsff, dsbf)
    return dq, dk, dv, dlgf[:, 0, 0], dlgb[:, 0, 0], ds0f, ds0b


_retention.defvjp(_retention_fwd, _retention_bwd)


def _gn_specs(y):
    h, n, dv = y.shape
    r = _tile(n, 512, SUBLANES)
    return (h, n, dv, r, pl.BlockSpec((1, r, dv), lambda i, hh: (hh, i, 0)), pl.BlockSpec((r, dv), lambda i, hh: (i, hh)))


def _gn_norm(yv):
    mu = jnp.mean(yv, axis=-1, keepdims=True)
    yc = yv - mu
    rstd = lax.rsqrt(jnp.mean(yc * yc, axis=-1, keepdims=True) + GN_EPS)
    return yc * rstd, rstd


def _gn_gate_fwd_call(y, gate):
    h, n, dv, r, yspec, gspec = _gn_specs(y)

    def body(y_ref, g_ref, o_ref):
        gv = g_ref[...]
        o_ref[...] = gv * jax.nn.sigmoid(gv) * _gn_norm(y_ref[0])[0]

    return pl.pallas_call(
        body, name="gn_gate", grid=(n // r, h), in_specs=[yspec, gspec], out_specs=gspec,
        out_shape=jax.ShapeDtypeStruct((n, h * dv), F32), compiler_params=_params(("parallel", "parallel")),
    )(y, gate)


def _gn_gate_bwd_call(y, gate, dout):
    h, n, dv, r, yspec, gspec = _gn_specs(y)

    def body(y_ref, g_ref, do_ref, dy_ref, dg_ref):
        gv = g_ref[...]
        dov = do_ref[...]
        yn, rstd = _gn_norm(y_ref[0])
        sg = jax.nn.sigmoid(gv)
        dg_ref[...] = dov * yn * (sg * (1.0 + gv * (1.0 - sg)))
        dyn = dov * (gv * sg)
        dy_ref[0] = rstd * (dyn - jnp.mean(dyn, axis=-1, keepdims=True) - yn * jnp.mean(dyn * yn, axis=-1, keepdims=True))

    return pl.pallas_call(
        body, name="gn_gate_b", grid=(n // r, h), in_specs=[yspec, gspec, gspec], out_specs=[yspec, gspec],
        out_shape=[jax.ShapeDtypeStruct((h, n, dv), F32), jax.ShapeDtypeStruct((n, h * dv), F32)],
        compiler_params=_params(("parallel", "parallel")),
    )(y, gate, dout)


@jax.custom_vjp
def _gn_gate(y, gate):
    return _gn_gate_fwd_call(y, gate)


_gn_gate.defvjp(lambda y, gate: (_gn_gate_fwd_call(y, gate), (y, gate)),
                lambda res, dout: tuple(_gn_gate_bwd_call(res[0], res[1], dout)))


def _rope_tables(pos, dim, base):
    inv = base ** (-jnp.arange(0, dim, 2, dtype=F32) / dim)
    ang = pos.astype(F32)[:, None] * inv[None, :]
    return jnp.cos(ang)[:, None, :], jnp.sin(ang)[:, None, :]


def _rotate(x, cos, sin):
    x1, x2 = jnp.split(x, 2, axis=-1)
    return jnp.concatenate([x1 * cos - x2 * sin, x2 * cos + x1 * sin], axis=-1)


def _axial_rope(x, row_tab, col_tab):
    xr, xc = jnp.split(x, 2, axis=-1)
    return jnp.concatenate([_rotate(xr, *row_tab), _rotate(xc, *col_tab)], axis=-1)


def _heads(t):
    return jnp.swapaxes(t, 0, 1)


def _local_loss(x, mods_lat, mods_ctx, small, first, early, late, big, ctx, target):
    n_lat, d = x.shape
    n_ctx = ctx.shape[0]
    both = lambda i: jnp.stack([mods_ctx[i], mods_lat[i]])[:, None, :]
    lat = lambda i: mods_lat[i][None, None, :]

    xs = jnp.concatenate([ctx, x], axis=0)
    x1, (st_mix_in, st_uq, st_ukv, st_mix_out) = _make_ffn(n_ctx, "ffn1", True)(
        xs, small["norm1_g"], both(0), both(1), both(2), big["ffn1_wg"], big["ffn1_wu"], first, None, tuple(early))
    every = (0, 1, 2, 3)
    big = dict(mix_in=jnp.pad(_slots_side_by_side(st_mix_in, every), ((0, 0), (0, MIX_IN_PAD - MIX_IN))),
               w_uq=_slots_side_by_side(st_uq, every), w_ukv=_slots_side_by_side(st_ukv, every),
               mix_out=st_mix_out.reshape(-1, st_mix_out.shape[-1]))
    proj = _make_normmod_linear(n_ctx, "mix_in")(x1, small["norm2_g"], both(3), both(4), big["mix_in"])
    offs = np.cumsum((0,) + MIX_SPLITS)
    part = lambda i, rows: proj[rows, offs[i]:offs[i + 1]]
    lat_rows, ctx_rows = slice(n_ctx, None), slice(0, n_ctx)

    zq = jnp.zeros((1, 1, MLA_Q_RANK), F32)
    zkv = jnp.zeros((1, 1, MLA_KV_RANK), F32)
    q = _make_normmod_linear(0, "mla_q")(part(4, lat_rows), small["mla_q_norm_g"], zq, zq, big["w_uq"])
    kv = _make_normmod_linear(0, "mla_kv")(part(5, slice(None)), small["mla_kv_norm_g"], zkv, zkv, big["w_ukv"])

    lgf = jax.nn.log_sigmoid(small["ret_decay_fwd"][0])
    lgb = jax.nn.log_sigmoid(small["ret_decay_bwd"][0])
    ret_tab = _rope_tables(jnp.arange(n_lat), RET_DK, ROPE_BASE)
    hd = lambda t, dd: t.reshape(t.shape[0], RET_HEADS, dd)
    s_zero = jnp.zeros((RET_HEADS, RET_DK, RET_DV), F32)
    _, s_f, s_b = _retention(_heads(hd(part(0, ctx_rows), RET_DK)), _heads(hd(part(1, ctx_rows), RET_DK) * (RET_DK ** -0.5)),
                             _heads(hd(part(2, ctx_rows), RET_DV)), lgf, lgb, s_zero, s_zero)
    rq = _rotate(hd(part(0, lat_rows), RET_DK), *ret_tab)
    rk = _rotate(hd(part(1, lat_rows), RET_DK) * (RET_DK ** -0.5), *ret_tab)
    y_lat, _, _ = _retention(_heads(rq), _heads(rk), _heads(hd(part(2, lat_rows), RET_DV)), lgf, lgb, s_f, s_b)
    ret_out = _gn_gate(y_lat, part(3, lat_rows))

    pos = jnp.arange(n_lat)
    row_tab = _rope_tables(pos // GRID_W, MLA_ROPE // 2, ROPE_BASE)
    col_tab = _rope_tables(pos % GRID_W, MLA_ROPE // 2, ROPE_BASE)
    q = q.reshape(n_lat, MLA_HEADS, MLA_NOPE + MLA_ROPE)
    q_all = jnp.concatenate([q[..., :MLA_NOPE], _axial_rope(q[..., MLA_NOPE:], row_tab, col_tab)], axis=-1)
    kv = kv.reshape(n_ctx + n_lat, MLA_HEADS, MLA_NOPE + MLA_V)
    kr_lat = _axial_rope(part(6, lat_rows)[:, None, :], row_tab, col_tab)
    kr = jnp.concatenate([kr_lat, part(6, ctx_rows)[:, None, :]], axis=0)
    kv_lat_first = jnp.concatenate([kv[n_ctx:], kv[:n_ctx]], axis=0)
    k_all = jnp.concatenate([kv_lat_first[..., :MLA_NOPE], jnp.broadcast_to(kr, (n_ctx + n_lat, MLA_HEADS, MLA_ROPE))], axis=-1)
    mla, (w_in2, w_out2) = _attention_gather(_heads(q_all), _heads(k_all), _heads(kv_lat_first[..., MLA_NOPE:]), tuple(late))
    mla_out = _heads(mla).reshape(n_lat, MLA_HEADS * MLA_V)

    x2 = _make_linear_gated_res("mix_out")(x1[n_ctx:], jnp.concatenate([ret_out, mla_out], axis=-1), big["mix_out"], lat(5))
    x3, _ = _make_ffn(0, "ffn2", False)(x2, small["norm3_g"], lat(6), lat(7), lat(8), _slots_side_by_side(w_in2, (0, 1)),
                                        _slots_side_by_side(w_in2, (2, 3)), None, w_out2.reshape(-1, w_out2.shape[-1]), ())
    return _final_loss(x3, small["final_norm_g"][None, :], target)


HBM_SPEC = pl.BlockSpec(memory_space=pl.ANY)
VMEM_SPEC = pl.BlockSpec(memory_space=pltpu.VMEM)
ALL_PEERS = (1, 2, 3, 4, 5, 6, 7)
CHIP_PEERS = (4, 2, 6)


def _me():
    return lax.axis_index("x"), lax.axis_index("y"), lax.axis_index("c")


def _flip(pos, mask):
    x, y, c = pos
    return (1 - x if mask & 4 else x, 1 - y if mask & 2 else y, 1 - c if mask & 1 else c)


def _allgather_small(block, masks, chips_only, name):
    r, c = block.shape
    n_slots = 4 if chips_only else 8

    def body(x_ref, out_ref, send_sems, recv_sems, local_sem):
        pos = _me()
        slot = 2 * pos[0] + pos[1] if chips_only else 4 * pos[0] + 2 * pos[1] + pos[2]
        local = pltpu.make_async_copy(x_ref, out_ref.at[slot], local_sem)
        local.start()
        copies = [pltpu.make_async_remote_copy(src_ref=x_ref, dst_ref=out_ref.at[slot], send_sem=send_sems.at[j], recv_sem=recv_sems.at[j],
                                               device_id=_flip(pos, mask), device_id_type=MESH) for j, mask in enumerate(masks)]
        for cp in copies:
            cp.start()
        for cp in copies:
            cp.wait()
        local.wait()

    return pl.pallas_call(
        body, name=name, in_specs=[VMEM_SPEC], out_specs=VMEM_SPEC,
        out_shape=jax.ShapeDtypeStruct((n_slots, r, c), block.dtype),
        scratch_shapes=[pltpu.SemaphoreType.DMA((len(masks),)), pltpu.SemaphoreType.DMA((len(masks),)), pltpu.SemaphoreType.DMA],
        compiler_params=pltpu.CompilerParams(vmem_limit_bytes=VMEM_LIMIT_BYTES),
    )(block)


def _gather_weights(shards):
    n = len(shards)

    def body(*refs):
        ins, outs, sems = refs[:n], refs[n:2 * n], refs[2 * n:]
        _gather_start(ins, outs, sems)
        _gather_finish(ins, outs, sems)

    stacked = pl.pallas_call(
        body, name="gather_weights", in_specs=[HBM_SPEC] * n, out_specs=[HBM_SPEC] * n,
        out_shape=_gather_out_shapes(shards), scratch_shapes=_gather_sems(n),
    )(*shards)
    return _fill_own_slot(stacked, shards)


def _gather_out_shapes(shards):
    return [jax.ShapeDtypeStruct((4,) + s.shape, s.dtype) for s in shards]


def _gather_sems(n):
    return [pltpu.SemaphoreType.DMA((3 * n,)) for _ in range(4)]


def _fill_own_slot(stacked, shards):
    if not shards:
        return []
    chip = 2 * lax.axis_index("x") + lax.axis_index("y")
    return [lax.dynamic_update_slice_in_dim(st, sh[None], chip, axis=0) for st, sh in zip(stacked, shards)]


def _gather_send(ins, outs, sems, w, j, pos):
    x, y, c = pos
    half = ins[w].shape[0] // 2
    mine = pl.ds(c * half, half)
    return pltpu.make_async_remote_copy(src_ref=ins[w].at[mine], dst_ref=outs[w].at[2 * x + y, mine], send_sem=sems[0].at[3 * w + j],
                                        recv_sem=sems[1].at[3 * w + j], device_id=_flip(pos, CHIP_PEERS[j]), device_id_type=MESH)


def _gather_pass(ins, outs, sems, w, j, pos, to_me):
    px, py, _ = _flip(pos, CHIP_PEERS[j])
    half = ins[w].shape[0] // 2
    slab = outs[w].at[2 * px + py, pl.ds(((1 - pos[2]) if to_me else pos[2]) * half, half)]
    return pltpu.make_async_remote_copy(src_ref=slab, dst_ref=slab, send_sem=sems[2].at[3 * w + j], recv_sem=sems[3].at[3 * w + j],
                                        device_id=_flip(pos, 1), device_id_type=MESH)


def _gather_start(ins, outs, sems):
    pos = _me()
    for w in range(len(ins)):
        for j in range(3):
            _gather_send(ins, outs, sems, w, j, pos).start()


def _gather_finish(ins, outs, sems):
    pos = _me()
    pairs = [(w, j) for w in range(len(ins)) for j in range(3)]
    for w, j in pairs:
        _gather_send(ins, outs, sems, w, j, pos).wait_recv()
        _gather_pass(ins, outs, sems, w, j, pos, False).start()
    for w, j in pairs:
        _gather_pass(ins, outs, sems, w, j, pos, True).wait_recv()
    for w, j in pairs:
        _gather_send(ins, outs, sems, w, j, pos).wait_send()
        _gather_pass(ins, outs, sems, w, j, pos, False).wait_send()


def _pair_swap_halves(grads, tag):
    n = len(grads)

    def body(*refs):
        ins, outs = refs[:n], refs[n:2 * n]
        send_sems, recv_sems = refs[2 * n:]
        pos = _me()
        copies = []
        for w in range(n):
            half = grads[w].shape[1] // 2
            cp = pltpu.make_async_remote_copy(src_ref=ins[w].at[:, pl.ds((1 - pos[2]) * half, half), :], dst_ref=outs[w], send_sem=send_sems.at[w],
                                              recv_sem=recv_sems.at[w], device_id=_flip(pos, 1), device_id_type=MESH)
            cp.start()
            copies.append(cp)
        for cp in copies:
            cp.wait()

    return pl.pallas_call(
        body, name="pair_swap_halves_" + tag, in_specs=[HBM_SPEC] * n, out_specs=[HBM_SPEC] * n,
        out_shape=[jax.ShapeDtypeStruct((4, g.shape[1] // 2, g.shape[2]), g.dtype) for g in grads],
        scratch_shapes=[pltpu.SemaphoreType.DMA((n,)), pltpu.SemaphoreType.DMA((n,))],
    )(*grads)


def _chip_scatter(parts):
    n = len(parts)

    def body(*refs):
        ins, outs, sems = refs[:n], refs[n:2 * n], refs[2 * n:]
        _scatter_start(ins, outs, sems)
        _scatter_finish(ins, outs, sems)

    return pl.pallas_call(
        body, name="chip_scatter", in_specs=[HBM_SPEC] * n, out_specs=[HBM_SPEC] * n,
        out_shape=[jax.ShapeDtypeStruct(p.shape, p.dtype) for p in parts], scratch_shapes=_scatter_sems(n),
    )(*parts)


def _scatter_sems(n):
    return [pltpu.SemaphoreType.DMA((3 * n,)), pltpu.SemaphoreType.DMA((3 * n,)), pltpu.SemaphoreType.DMA((n,))]


def _scatter_copies(ins, outs, sems):
    pos = _me()
    me = 2 * pos[0] + pos[1]
    local = [pltpu.make_async_copy(ins[w].at[me], outs[w].at[me], sems[2].at[w]) for w in range(len(ins))]
    remote = []
    for w in range(len(ins)):
        for j, mask in enumerate(CHIP_PEERS):
            px, py, _ = _flip(pos, mask)
            remote.append(pltpu.make_async_remote_copy(src_ref=ins[w].at[2 * px + py], dst_ref=outs[w].at[me], send_sem=sems[0].at[3 * w + j],
                                                       recv_sem=sems[1].at[3 * w + j], device_id=_flip(pos, mask), device_id_type=MESH))
    return local, remote


def _scatter_start(ins, outs, sems):
    local, remote = _scatter_copies(ins, outs, sems)
    for cp in local + remote:
        cp.start()


def _scatter_finish(ins, outs, sems):
    local, remote = _scatter_copies(ins, outs, sems)
    for cp in remote + local:
        cp.wait()


def _pair_swap_reduced(halves, tag):
    n = len(halves)

    def body(*refs):
        ins, outs = refs[:n], refs[n:2 * n]
        send_sems, recv_sems = refs[2 * n:]
        pos = _me()
        copies = []
        for w in range(n):
            cp = pltpu.make_async_remote_copy(src_ref=ins[w], dst_ref=outs[w], send_sem=send_sems.at[w], recv_sem=recv_sems.at[w],
                                              device_id=_flip(pos, 1), device_id_type=MESH)
            cp.start()
            copies.append(cp)
        for cp in copies:
            cp.wait()

    dma = lambda k: pltpu.SemaphoreType.DMA((k,))
    return pl.pallas_call(
        body, name="pair_swap_reduced_" + tag, in_specs=[HBM_SPEC] * n, out_specs=[HBM_SPEC] * n,
        out_shape=[jax.ShapeDtypeStruct(h.shape, h.dtype) for h in halves],
        scratch_shapes=[dma(n), dma(n)],
    )(*halves)


def _add_pair(mine, theirs, name):
    s, h, c = mine.shape
    r = _tile(h, max(16, (1 << 19) // c), 16)

    def body(a_ref, b_ref, o_ref):
        o_ref[...] = (a_ref[...].astype(F32) + b_ref[...].astype(F32)).astype(BF16)

    blk = pl.BlockSpec((1, r, c), lambda i, j: (i, j, 0))
    return pl.pallas_call(
        body, name=name, grid=(s, h // r), in_specs=[blk, blk], out_specs=blk,
        out_shape=jax.ShapeDtypeStruct(mine.shape, BF16), compiler_params=_params(("parallel", "parallel")),
    )(mine, theirs)


def _sum_slots(parts, name):
    s, h, c = parts.shape
    r = _tile(h, max(16, (1 << 18) // c), 16)

    def body(p_ref, o_ref):
        acc = p_ref[0].astype(F32)
        for k in range(1, s):
            acc = acc + p_ref[k].astype(F32)
        o_ref[...] = acc

    return pl.pallas_call(
        body, name=name, grid=(h // r,), in_specs=[pl.BlockSpec((s, r, c), lambda i: (0, i, 0))],
        out_specs=pl.BlockSpec((r, c), lambda i: (i, 0)),
        out_shape=jax.ShapeDtypeStruct((h, c), F32), compiler_params=_params(("parallel",)),
    )(parts)


def _reduce_scatter_grads(stacked, tag, scatter=_chip_scatter):
    c = lax.axis_index("c")
    theirs = _pair_swap_halves(stacked, tag)
    parts = []
    for w, (g, t) in enumerate(zip(stacked, theirs)):
        half = g.shape[1] // 2
        mine = lax.dynamic_slice_in_dim(g, c * half, half, axis=1)
        parts.append(_add_pair(mine, t, "rs_add_pair_%s_%d" % (tag, w)))
    landed = scatter(parts)
    halves = [_sum_slots(p, "rs_sum_slots_%s_%d" % (tag, w)) for w, p in enumerate(landed)]
    return list(zip(halves, _pair_swap_reduced(halves, tag)))


def _adamw_math(w, g, m, v):
    m = ADAM_B1 * m + (1.0 - ADAM_B1) * g
    v = ADAM_B2 * v + (1.0 - ADAM_B2) * (g * g)
    m_hat = m / (1.0 - ADAM_B1 ** ADAM_STEP)
    v_hat = v / (1.0 - ADAM_B2 ** ADAM_STEP)
    return -ADAM_LR * (m_hat / (jnp.sqrt(v_hat) + ADAM_EPS) + ADAM_WD * w), m, v


def _adamw(w, g, m, v, name):
    rows, cols = w.shape
    r = _tile(rows, max(SUBLANES, (1 << 18) // cols), SUBLANES)

    def body(w_ref, g_ref, m_ref, v_ref, d_ref, mo_ref, vo_ref):
        d_ref[...], mo_ref[...], vo_ref[...] = _adamw_math(w_ref[...], g_ref[...], m_ref[...], v_ref[...])

    blk = pl.BlockSpec((r, cols), lambda i: (i, 0))
    return pl.pallas_call(
        body, name=name, grid=(rows // r,), in_specs=[blk] * 4, out_specs=[blk] * 3,
        out_shape=[jax.ShapeDtypeStruct(w.shape, F32)] * 3, compiler_params=_params(("parallel",)),
    )(w, g, m, v)


def _adamw_halves(w, g_mine, g_theirs, m, v, core, name):
    rows, cols = w.shape
    half = rows // 2
    r = _tile(half, max(SUBLANES, (1 << 18) // cols), SUBLANES)
    nbh = half // r

    def body(core_ref, w_ref, gm_ref, gt_ref, m_ref, v_ref, g_ref, d_ref, mo_ref, vo_ref):
        is_mine = (pl.program_id(0) // nbh) == core_ref[0]

        @pl.when(is_mine)
        def _():
            g_ref[...] = gm_ref[...]

        @pl.when(jnp.logical_not(is_mine))
        def _():
            g_ref[...] = gt_ref[...]

        g = g_ref[...]
        d_ref[...], mo_ref[...], vo_ref[...] = _adamw_math(w_ref[...], g, m_ref[...], v_ref[...])

    full = pl.BlockSpec((r, cols), lambda i, core_ref: (i, 0))
    part = pl.BlockSpec((r, cols), lambda i, core_ref: (i % nbh, 0))
    return pl.pallas_call(
        body, name=name,
        grid_spec=pltpu.PrefetchScalarGridSpec(num_scalar_prefetch=1, grid=(rows // r,), in_specs=[full, part, part, full, full],
                                               out_specs=[full] * 4),
        out_shape=[jax.ShapeDtypeStruct(w.shape, F32)] * 4, compiler_params=_params(("parallel",)),
    )(core, w, g_mine, g_theirs, m, v)


def _adamw_reduced(parts, w, m, v, name):
    def body(p_ref, w_ref, m_ref, v_ref, g_ref, d_ref, mo_ref, vo_ref):
        g = p_ref[0]
        for k in range(1, parts.shape[0]):
            g = g + p_ref[k]
        g_ref[...] = g
        d_ref[...], mo_ref[...], vo_ref[...] = _adamw_math(w_ref[...], g, m_ref[...], v_ref[...])

    return pl.pallas_call(
        body, name=name, in_specs=[VMEM_SPEC] * 4, out_specs=[VMEM_SPEC] * 4,
        out_shape=[jax.ShapeDtypeStruct(w.shape, F32)] * 4,
        compiler_params=pltpu.CompilerParams(vmem_limit_bytes=VMEM_LIMIT_BYTES),
    )(parts, w, m, v)


WEIGHTS = ("c_ctx", "ada_w", "ada_b", "norm1_g", "ffn1_w_in", "ffn1_w_out", "norm2_g", "mix_w_in", "ret_decay_fwd", "ret_decay_bwd",
           "mla_q_norm_g", "mla_w_uq", "mla_kv_norm_g", "mla_w_ukv", "mix_w_out", "norm3_g", "ffn2_w_in", "ffn2_w_out", "final_norm_g")
SMALL = ("c_ctx", "ada_b", "norm1_g", "norm2_g", "ret_decay_fwd", "ret_decay_bwd", "mla_q_norm_g", "mla_kv_norm_g", "norm3_g", "final_norm_g")
FIRST = "ffn1_w_in"
EARLY = ("ffn1_w_out", "mix_w_in", "mla_w_uq", "mla_w_ukv", "mix_w_out")
LATE = ("ffn2_w_in", "ffn2_w_out")


def _slots_side_by_side(stacked, slots):
    return jnp.concatenate([stacked[k] for k in slots], axis=1)


def _pack(vectors):
    flat = jnp.concatenate([v.reshape(-1) for v in vectors])
    return jnp.pad(flat, (0, -flat.shape[0] % (SUBLANES * LANES))).reshape(SUBLANES, -1)


def _rows8(a):
    return a.reshape(a.shape[0] * SUBLANES, a.shape[1] // SUBLANES)


def _unpack(packed, like):
    packed = packed.reshape(-1)
    out, off = [], 0
    for ref in like:
        out.append(packed[off:off + ref.size].reshape(ref.shape))
        off += ref.size
    return out


def kernel(x, c, ctx, c_ctx, ada_w, ada_b, norm1_g, ffn1_w_in, ffn1_w_out, norm2_g, mix_w_in, ret_decay_fwd, ret_decay_bwd, mla_q_norm_g, mla_w_uq, mla_kv_norm_g, mla_w_ukv, mix_w_out, norm3_g, ffn2_w_in, ffn2_w_out, final_norm_g, loss_target, m_c_ctx, m_ada_w, m_ada_b, m_norm1_g, m_ffn1_w_in, m_ffn1_w_out, m_norm2_g, m_mix_w_in, m_ret_decay_fwd, m_ret_decay_bwd, m_mla_q_norm_g, m_mla_w_uq, m_mla_kv_norm_g, m_mla_w_ukv, m_mix_w_out, m_norm3_g, m_ffn2_w_in, m_ffn2_w_out, m_final_norm_g, v_c_ctx, v_ada_w, v_ada_b, v_norm1_g, v_ffn1_w_in, v_ffn1_w_out, v_norm2_g, v_mix_w_in, v_ret_decay_fwd, v_ret_decay_bwd, v_mla_q_norm_g, v_mla_w_uq, v_mla_kv_norm_g, v_mla_w_ukv, v_mix_w_out, v_norm3_g, v_ffn2_w_in, v_ffn2_w_out, v_final_norm_g):
    w = dict(c_ctx=c_ctx, ada_w=ada_w, ada_b=ada_b, norm1_g=norm1_g, ffn1_w_in=ffn1_w_in, ffn1_w_out=ffn1_w_out, norm2_g=norm2_g,
             mix_w_in=mix_w_in, ret_decay_fwd=ret_decay_fwd, ret_decay_bwd=ret_decay_bwd, mla_q_norm_g=mla_q_norm_g, mla_w_uq=mla_w_uq,
             mla_kv_norm_g=mla_kv_norm_g, mla_w_ukv=mla_w_ukv, mix_w_out=mix_w_out, norm3_g=norm3_g, ffn2_w_in=ffn2_w_in,
             ffn2_w_out=ffn2_w_out, final_norm_g=final_norm_g)
    mom_m = dict(zip(WEIGHTS, (m_c_ctx, m_ada_w, m_ada_b, m_norm1_g, m_ffn1_w_in, m_ffn1_w_out, m_norm2_g, m_mix_w_in, m_ret_decay_fwd,
                               m_ret_decay_bwd, m_mla_q_norm_g, m_mla_w_uq, m_mla_kv_norm_g, m_mla_w_ukv, m_mix_w_out, m_norm3_g,
                               m_ffn2_w_in, m_ffn2_w_out, m_final_norm_g)))
    mom_v = dict(zip(WEIGHTS, (v_c_ctx, v_ada_w, v_ada_b, v_norm1_g, v_ffn1_w_in, v_ffn1_w_out, v_norm2_g, v_mix_w_in, v_ret_decay_fwd,
                               v_ret_decay_bwd, v_mla_q_norm_g, v_mla_w_uq, v_mla_kv_norm_g, v_mla_w_ukv, v_mix_w_out, v_norm3_g,
                               v_ffn2_w_in, v_ffn2_w_out, v_final_norm_g)))
    xi, yi, ci = _me()
    chip = 2 * xi + yi
    example = 2 * chip + ci
    d = x.shape[-1]
    n_mod = ada_b.shape[-1] // d

    c_all = _allgather_small(_rows8(c), ALL_PEERS, False, "gather_c").reshape(8, d)
    cond = jnp.concatenate([c_all, jnp.broadcast_to(c_ctx[None, :], (8, d))], axis=0)
    cond_act = jax.nn.silu(cond)
    n_cols = ada_w.shape[-1]
    bias = lax.dynamic_slice_in_dim(ada_b, chip * n_cols, n_cols, axis=1)
    mods_cols = _matmul(cond_act, ada_w[0], "nn", F32, "ada_fwd", add=jnp.broadcast_to(bias, (16, n_cols)))
    mods = jnp.swapaxes(_allgather_small(mods_cols, CHIP_PEERS, True, "gather_mods"), 0, 1).reshape(16, 4 * n_cols)
    mods_lat = lax.dynamic_slice_in_dim(mods, example, 1, axis=0).reshape(n_mod, d)
    mods_ctx = mods[8].reshape(n_mod, d)

    (st_in1,) = _gather_weights([w[FIRST][0].astype(BF16)])
    big = dict(ffn1_wg=_slots_side_by_side(st_in1, (0, 1)), ffn1_wu=_slots_side_by_side(st_in1, (2, 3)))
    small = {k: w[k] for k in ("norm1_g", "norm2_g", "norm3_g", "final_norm_g", "mla_q_norm_g", "mla_kv_norm_g", "ret_decay_fwd", "ret_decay_bwd")}
    early = tuple(w[name][0] for name in EARLY)
    late = tuple(w[name][0] for name in LATE)

    loss_mine, (dx, dmods_lat, dmods_ctx, dsmall, dfirst, dearly, dlate) = jax.value_and_grad(_local_loss, argnums=(0, 1, 2, 3, 4, 5, 6))(
        x[0], mods_lat, mods_ctx, small, w[FIRST][0], early, late, big, ctx[0], loss_target[0])

    dmods = _allgather_small(_rows8(jnp.stack([dmods_lat.reshape(-1), dmods_ctx.reshape(-1)])), ALL_PEERS, False, "gather_dmods")
    dmods = dmods.reshape(8, 2, n_mod * d)
    dmods_rows = jnp.concatenate([dmods[:, 0, :], dmods[:, 1, :]], axis=0)
    dmods_cols = lax.dynamic_slice_in_dim(dmods_rows, chip * n_cols, n_cols, axis=1)
    g_ada_w = _matmul(cond_act, dmods_cols, "tn", F32, "ada_dw")
    dcond_act = _matmul(dmods_cols, ada_w[0], "nt", F32, "ada_dcond")
    sig = jax.nn.sigmoid(c_ctx)
    dc_ctx = jnp.sum(dcond_act[8:], axis=0) * (sig * (1.0 + c_ctx * (1.0 - sig)))
    share = dict(dsmall)
    share["c_ctx"] = jnp.where(ci == 0, dc_ctx, jnp.zeros_like(dc_ctx))
    share["ada_b"] = (dmods_lat + dmods_ctx).reshape(1, -1)
    zero = jnp.zeros((1,), F32)
    parts = _allgather_small(_pack([share[k] for k in SMALL] + [loss_mine.reshape(1)]), ALL_PEERS, False, "gather_small_grads")
    packed = _adamw_reduced(parts, _pack([w[k] for k in SMALL] + [zero]), _pack([mom_m[k] for k in SMALL] + [zero]),
                            _pack([mom_v[k] for k in SMALL] + [zero]), "adamw_small")
    like = [w[k] for k in SMALL] + [zero]
    grads, deltas, new_m, new_v = ({k: a for k, a in zip(SMALL + ("loss",), _unpack(p, like))} for p in packed)
    loss = grads.pop("loss").reshape(())

    for name, g in (("ada_w", g_ada_w), (FIRST, dfirst)) + tuple(zip(EARLY, dearly)) + tuple(zip(LATE, dlate)):
        dl, mo, vo = _adamw(w[name][0], g, mom_m[name][0], mom_v[name][0], "adamw_" + name)
        grads[name], deltas[name], new_m[name], new_v[name] = g[None], dl[None], mo[None], vo[None]

    return (loss, dx[None], *[grads[k] for k in WEIGHTS], *[deltas[k] for k in WEIGHTS], *[new_m[k] for k in WEIGHTS],
            *[new_v[k] for k in WEIGHTS])
```

```python
import functools

import jax
import jax.numpy as jnp
import numpy as np
from jax import lax
from jax.experimental import pallas as pl
from jax.experimental.pallas import tpu as pltpu

F32 = jnp.float32
BF16 = jnp.bfloat16
MESH = pl.DeviceIdType.MESH

VMEM_LIMIT_BYTES = 52 * 1024 * 1024
LANES = 128
SUBLANES = 8

D_FF_SPLIT = 2
RET_HEADS, RET_DK, RET_DV, RET_CHUNK = 8, 64, 128, 128
MLA_HEADS, MLA_Q_RANK, MLA_KV_RANK, MLA_NOPE, MLA_ROPE, MLA_V = 8, 512, 256, 128, 64, 128
GRID_W = 64
ROPE_BASE = 10000.0
RMS_EPS = 1e-6
GN_EPS = 1e-5
MIX_SPLITS = (RET_HEADS * RET_DK, RET_HEADS * RET_DK, RET_HEADS * RET_DV, RET_HEADS * RET_DV,
              MLA_Q_RANK, MLA_KV_RANK, MLA_ROPE)
MIX_IN = sum(MIX_SPLITS)
MIX_IN_PAD = 4096
ADAM_LR, ADAM_B1, ADAM_B2, ADAM_EPS, ADAM_WD, ADAM_STEP = 0.001, 0.9, 0.999, 1e-08, 0.01, 10


def _tile(n, pref, align):
    best = None
    t = align
    while t <= min(n, pref):
        if n % t == 0:
            best = t
        t += align
    return n if best is None else best


def _params(sem=None):
    return pltpu.CompilerParams(dimension_semantics=sem, vmem_limit_bytes=VMEM_LIMIT_BYTES)


def _matmul(a, b, mode, out_dtype, name, add=None, tm=1024, tn=1024, tk=2048):
    if mode == "nn":
        (m, k), (k2, n) = a.shape, b.shape
        dims = (((1,), (0,)), ((), ()))
    elif mode == "nt":
        (m, k), (n, k2) = a.shape, b.shape
        dims = (((1,), (1,)), ((), ()))
    else:
        (k, m), (k2, n) = a.shape, b.shape
        dims = (((0,), (0,)), ((), ()))
    assert k == k2, (a.shape, b.shape, mode)
    tm = _tile(m, tm, LANES if mode == "tn" else 16)
    tn = _tile(n, tn, LANES)
    tk = _tile(k, tk, LANES if mode != "tn" else 16)
    nk = k // tk
    a_spec = pl.BlockSpec((tk, tm), lambda i, j, kk: (kk, i)) if mode == "tn" else pl.BlockSpec((tm, tk), lambda i, j, kk: (i, kk))
    b_spec = pl.BlockSpec((tn, tk), lambda i, j, kk: (j, kk)) if mode == "nt" else pl.BlockSpec((tk, tn), lambda i, j, kk: (kk, j))
    o_spec = pl.BlockSpec((tm, tn), lambda i, j, kk: (i, j))
    has_add = add is not None

    def body(*refs):
        a_ref, b_ref = refs[0], refs[1]
        add_ref = refs[2] if has_add else None
        o_ref = refs[2 + has_add]
        p = lax.dot_general(a_ref[...].astype(BF16), b_ref[...].astype(BF16), dims, preferred_element_type=F32)
        if nk == 1:
            if has_add:
                p = p + add_ref[...].astype(F32)
            o_ref[...] = p.astype(out_dtype)
        else:
            acc = refs[3 + has_add]
            kk = pl.program_id(2)

            @pl.when(kk == 0)
            def _():
                acc[...] = p + add_ref[...].astype(F32) if has_add else p

            @pl.when(kk > 0)
            def _():
                acc[...] += p

            @pl.when(kk == nk - 1)
            def _():
                o_ref[...] = acc[...].astype(out_dtype)

    return pl.pallas_call(
        body, name=name, grid=(m // tm, n // tn, nk),
        in_specs=[a_spec, b_spec] + ([o_spec] if has_add else []),
        out_specs=o_spec,
        out_shape=jax.ShapeDtypeStruct((m, n), out_dtype),
        scratch_shapes=[pltpu.VMEM((tm, tn), F32)] if nk > 1 else [],
        compiler_params=_params(("parallel", "parallel", "arbitrary")),
    )(*((a, b, add) if has_add else (a, b)))


def _row_tile(t, n_ctx, d):
    pref = max(SUBLANES, min(256, (1 << 19) // d))
    r = _tile(int(np.gcd(t, n_ctx)) if n_ctx else t, pref, SUBLANES)
    return r, (n_ctx // r if n_ctx else 0)


def _seg_map(nct):
    if nct:
        return lambda i: (jnp.minimum(i // nct, 1), 0, 0)
    return lambda i: (0, 0, 0)


def _normmod_fwd(x, g, shift, scale, n_ctx, name):
    t, d = x.shape
    r, nct = _row_tile(t, n_ctx, d)

    def body(x_ref, g_ref, sh_ref, sc_ref, h_ref, ht_ref):
        xv = x_ref[...]
        rstd = lax.rsqrt(jnp.mean(xv * xv, axis=-1, keepdims=True) + RMS_EPS)
        n = xv * rstd * g_ref[...]
        h = n * (1.0 + sc_ref[0]) + sh_ref[0]
        h_ref[...] = h.astype(BF16)
        ht_ref[...] = h.T.astype(BF16)

    row = pl.BlockSpec((r, d), lambda i: (i, 0))
    seg = pl.BlockSpec((1, 1, d), _seg_map(nct))
    return pl.pallas_call(
        body, name=name, grid=(t // r,),
        in_specs=[row, pl.BlockSpec((1, d), lambda i: (0, 0)), seg, seg],
        out_specs=[row, pl.BlockSpec((d, r), lambda i: (0, i))],
        out_shape=[jax.ShapeDtypeStruct((t, d), BF16), jax.ShapeDtypeStruct((d, t), BF16)],
        compiler_params=_params(("parallel",)),
    )(x, g, shift, scale)


def _normmod_bwd(dh, x, g, shift, scale, dres, n_ctx, name):
    t, d = x.shape
    r, nct = _row_tile(t, n_ctx, d)
    has_res = dres is not None
    nseg = shift.shape[0]

    def body(*refs):
        dh_ref, x_ref, g_ref, sh_ref, sc_ref = refs[:5]
        dres_ref = refs[5] if has_res else None
        dx_ref, dg_ref, dsh_ref, dsc_ref = refs[5 + has_res:]
        i = pl.program_id(0)
        xv = x_ref[...]
        dhv = dh_ref[...].astype(F32)
        rstd = lax.rsqrt(jnp.mean(xv * xv, axis=-1, keepdims=True) + RMS_EPS)
        y = xv * rstd
        gv = g_ref[...]
        dn = dhv * (1.0 + sc_ref[0])
        dy = dn * gv
        dx = rstd * (dy - y * jnp.mean(dy * y, axis=-1, keepdims=True))
        if has_res:
            dx = dx + dres_ref[...]
        dx_ref[...] = dx

        @pl.when(i == 0)
        def _():
            dg_ref[...] = jnp.zeros_like(dg_ref)

        @pl.when(jnp.logical_or(i == 0, i == nct))
        def _():
            dsh_ref[...] = jnp.zeros_like(dsh_ref)
            dsc_ref[...] = jnp.zeros_like(dsc_ref)

        dg_ref[...] += jnp.sum(dn * y, axis=0, keepdims=True)
        dsh_ref[0] += jnp.sum(dhv, axis=0, keepdims=True)
        dsc_ref[0] += jnp.sum(dhv * (y * gv), axis=0, keepdims=True)

    row = pl.BlockSpec((r, d), lambda i: (i, 0))
    seg = pl.BlockSpec((1, 1, d), _seg_map(nct))
    vec = pl.BlockSpec((1, d), lambda i: (0, 0))
    return pl.pallas_call(
        body, name=name, grid=(t // r,),
        in_specs=[row, row, vec, seg, seg] + ([row] if has_res else []),
        out_specs=[row, vec, seg, seg],
        out_shape=[jax.ShapeDtypeStruct((t, d), F32), jax.ShapeDtypeStruct((1, d), F32),
                   jax.ShapeDtypeStruct((nseg, 1, d), F32), jax.ShapeDtypeStruct((nseg, 1, d), F32)],
        compiler_params=_params(("arbitrary",)),
    )(*((dh, x, g, shift, scale, dres) if has_res else (dh, x, g, shift, scale)))


def _gated_res_fwd(x, y, gate, coef, n_ctx, name):
    t, d = x.shape
    r, nct = _row_tile(t, n_ctx, d)

    def body(x_ref, y_ref, gt_ref, o_ref):
        o_ref[...] = x_ref[...] + (coef * gt_ref[0]) * y_ref[...]

    row = pl.BlockSpec((r, d), lambda i: (i, 0))
    return pl.pallas_call(
        body, name=name, grid=(t // r,),
        in_specs=[row, row, pl.BlockSpec((1, 1, d), _seg_map(nct))],
        out_specs=row, out_shape=jax.ShapeDtypeStruct((t, d), F32),
        compiler_params=_params(("parallel",)),
    )(x, y, gate)


def _matmul_gated_res(a, w, x, gate, coef, name, tm):
    t, k = a.shape
    d = w.shape[1]
    tm = _tile(t, tm, 16)
    tn = _tile(d, 512, LANES)

    def body(a_ref, w_ref, x_ref, gt_ref, o_ref, y_ref):
        y = jnp.dot(a_ref[...], w_ref[...], preferred_element_type=F32)
        o_ref[...] = x_ref[...] + (coef * gt_ref[0]) * y
        y_ref[...] = y.astype(BF16)

    blk = pl.BlockSpec((tm, tn), lambda i, j: (i, j))
    return pl.pallas_call(
        body, name=name, grid=(t // tm, d // tn),
        in_specs=[pl.BlockSpec((tm, k), lambda i, j: (i, 0)), pl.BlockSpec((k, tn), lambda i, j: (0, j)), blk,
                  pl.BlockSpec((1, 1, tn), lambda i, j: (0, 0, j))],
        out_specs=[blk, blk], out_shape=[jax.ShapeDtypeStruct((t, d), F32), jax.ShapeDtypeStruct((t, d), BF16)],
        compiler_params=_params(("parallel", "parallel")),
    )(a, w, x, gate)


def _gated_res_bwd(dout, y, gate, coef, n_ctx, name):
    t, d = dout.shape
    r, nct = _row_tile(t, n_ctx, d)
    nseg = gate.shape[0]

    def body(do_ref, y_ref, gt_ref, dy_ref, dgt_ref):
        i = pl.program_id(0)
        dov = do_ref[...] * coef
        dy_ref[...] = (dov * gt_ref[0]).astype(BF16)

        @pl.when(jnp.logical_or(i == 0, i == nct))
        def _():
            dgt_ref[...] = jnp.zeros_like(dgt_ref)

        dgt_ref[0] += jnp.sum(dov * y_ref[...], axis=0, keepdims=True)

    row = pl.BlockSpec((r, d), lambda i: (i, 0))
    seg = pl.BlockSpec((1, 1, d), _seg_map(nct))
    return pl.pallas_call(
        body, name=name, grid=(t // r,),
        in_specs=[row, row, seg], out_specs=[row, seg],
        out_shape=[jax.ShapeDtypeStruct((t, d), BF16), jax.ShapeDtypeStruct((nseg, 1, d), F32)],
        compiler_params=_params(("arbitrary",)),
    )(dout, y, gate)


def _swiglu_matmul(h, wg, wu, name, tm, shards=()):
    t, k = h.shape
    f = wg.shape[1]
    tm = _tile(t, tm, LANES)
    tn = _tile(f, 512, LANES)
    n = len(shards)
    grid = (t // tm, f // tn)

    def body(h_ref, wg_ref, wu_ref, *rest):
        ins, (g_ref, u_ref, a_ref, at_ref), outs, sems = rest[:n], rest[n:n + 4], rest[n + 4:2 * n + 4], rest[2 * n + 4:]
        if n:
            first, last = _first_last_step(grid)
            pl.when(first)(lambda: _gather_start(ins, outs, sems))
        hv = h_ref[...]
        g = jnp.dot(hv, wg_ref[...], preferred_element_type=F32)
        u = jnp.dot(hv, wu_ref[...], preferred_element_type=F32)
        a = g * jax.nn.sigmoid(g) * u
        g_ref[...] = g.astype(BF16)
        u_ref[...] = u.astype(BF16)
        a_ref[...] = a.astype(BF16)
        at_ref[...] = a.T.astype(BF16)
        if n:
            pl.when(last)(lambda: _gather_finish(ins, outs, sems))

    w_spec = pl.BlockSpec((k, tn), lambda i, j: (0, j))
    o_spec = pl.BlockSpec((tm, tn), lambda i, j: (i, j))
    return pl.pallas_call(
        body, name=name, grid=grid,
        in_specs=[pl.BlockSpec((tm, k), lambda i, j: (i, 0)), w_spec, w_spec] + [HBM_SPEC] * n,
        out_specs=[o_spec, o_spec, o_spec, pl.BlockSpec((tn, tm), lambda i, j: (j, i))] + [HBM_SPEC] * n,
        out_shape=[jax.ShapeDtypeStruct((t, f), BF16)] * 3 + [jax.ShapeDtypeStruct((f, t), BF16)] + _gather_out_shapes(shards),
        scratch_shapes=_gather_sems(n) if n else [],
        compiler_params=_params(("arbitrary", "arbitrary") if n else ("parallel", "parallel")),
    )(h, wg, wu, *shards)


def _swiglu_bwd_matmul(dy, wo, gg, uu, name, tm, parts=()):
    t, d = dy.shape
    f = wo.shape[0]
    tm = _tile(t, tm, 16)
    tn = _tile(f, 512, LANES)
    n = len(parts)
    grid = (t // tm, f // tn)

    def body(dy_ref, wo_ref, g_ref, u_ref, *rest):
        ins, (dg_ref, du_ref), outs, sems = rest[:n], rest[n:n + 2], rest[n + 2:2 * n + 2], rest[2 * n + 2:]
        if n:
            first, last = _first_last_step(grid)
            pl.when(first)(lambda: _scatter_start(ins, outs, sems))
        da = lax.dot_general(dy_ref[...], wo_ref[...], _NT, preferred_element_type=F32)
        gv = g_ref[...].astype(F32)
        sg = jax.nn.sigmoid(gv)
        dg_ref[...] = (da * u_ref[...].astype(F32) * (sg * (1.0 + gv * (1.0 - sg)))).astype(BF16)
        du_ref[...] = (da * (gv * sg)).astype(BF16)
        if n:
            pl.when(last)(lambda: _scatter_finish(ins, outs, sems))

    blk = pl.BlockSpec((tm, tn), lambda i, j: (i, j))
    return pl.pallas_call(
        body, name=name, grid=grid,
        in_specs=[pl.BlockSpec((tm, d), lambda i, j: (i, 0)), pl.BlockSpec((tn, d), lambda i, j: (j, 0)), blk, blk] + [HBM_SPEC] * n,
        out_specs=[blk, blk] + [HBM_SPEC] * n,
        out_shape=[jax.ShapeDtypeStruct((t, f), BF16)] * 2 + [jax.ShapeDtypeStruct(p.shape, p.dtype) for p in parts],
        scratch_shapes=_scatter_sems(n) if n else [],
        compiler_params=_params(("arbitrary", "arbitrary") if n else ("parallel", "parallel")),
    )(dy, wo, gg, uu, *parts)


def _matmul_nt_hiding_scatter(a, b, name, tm, parts):
    m, k = a.shape
    n_out = b.shape[0]
    tm = _tile(m, tm, 16)
    tn = _tile(n_out, 512, LANES)
    n = len(parts)
    grid = (m // tm, n_out // tn)

    def body(a_ref, b_ref, *rest):
        ins, o_ref, outs, sems = rest[:n], rest[n], rest[n + 1:2 * n + 1], rest[2 * n + 1:]
        first, last = _first_last_step(grid)
        pl.when(first)(lambda: _scatter_start(ins, outs, sems))
        o_ref[...] = lax.dot_general(a_ref[...], b_ref[...], _NT, preferred_element_type=F32)
        pl.when(last)(lambda: _scatter_finish(ins, outs, sems))

    return pl.pallas_call(
        body, name=name, grid=grid,
        in_specs=[pl.BlockSpec((tm, k), lambda i, j: (i, 0)), pl.BlockSpec((tn, k), lambda i, j: (j, 0))] + [HBM_SPEC] * n,
        out_specs=[pl.BlockSpec((tm, tn), lambda i, j: (i, j))] + [HBM_SPEC] * n,
        out_shape=[jax.ShapeDtypeStruct((m, n_out), F32)] + [jax.ShapeDtypeStruct(p.shape, p.dtype) for p in parts],
        scratch_shapes=_scatter_sems(n), compiler_params=_params(("arbitrary", "arbitrary")),
    )(a, b, *parts)


def _tok_tile(t):
    return 1024 if t % 1024 == 0 else 768 if t % 768 == 0 else _tile(t, 1024, 16)


FF_TILE = 1408
FULL_K = 1 << 30


def _make_ffn(n_ctx, tag, has_late):
    @jax.custom_vjp
    def ffn(x, g, shift, scale, gate, wg, wu, w_in_block, wo, late):
        return fwd(x, g, shift, scale, gate, wg, wu, w_in_block, wo, late)[0]

    def fwd(x, g, shift, scale, gate, wg, wu, w_in_block, wo, late):
        tt = _tok_tile(x.shape[0])
        h, ht = _normmod_fwd(x, g, shift, scale, n_ctx, tag + "_norm")
        blocks = [s.astype(BF16) for s in late]
        gg, uu, a, at, *stacked = _swiglu_matmul(h, wg, wu, tag + "_mm_gu", tt, blocks)
        stacked = _fill_own_slot(stacked, blocks)
        if has_late:
            wo = stacked[0].reshape(-1, stacked[0].shape[-1])
        if n_ctx:
            y = _matmul(a, wo, "nn", F32, tag + "_mm_o", tm=tt, tn=512, tk=FULL_K)
            out = _gated_res_fwd(x, y, gate, 0.5, n_ctx, tag + "_res")
        else:
            out, y = _matmul_gated_res(a, wo, x, gate, 0.5, tag + "_mm_o_res", tt)
        return (out, tuple(stacked[1:])), (x, g, shift, scale, gate, wg, wu, wo, ht, gg, uu, at, y)

    def bwd(res, cts):
        x, g, shift, scale, gate, wg, wu, wo, ht, gg, uu, at, y = res
        dout, dgathered = cts
        tt = _tok_tile(x.shape[0])
        dy, dgate = _gated_res_bwd(dout, y, gate, 0.5, n_ctx, tag + "_res_b")
        dwo = _matmul(at, dy, "nn", BF16, tag + "_dwo", tm=512, tn=512, tk=FULL_K)
        if has_late:
            own = {}

            def scatter(parts):
                own["dgg"], own["duu"], *landed = _swiglu_bwd_matmul(dy, wo, gg, uu, tag + "_da_act", tt, parts)
                return landed

            top = lax.axis_index("c") == 0
            pending = [dwo.reshape(4, dwo.shape[0] // 4, dwo.shape[1])] + list(dgathered)
            dlate = tuple(jnp.concatenate([jnp.where(top, mine, theirs), jnp.where(top, theirs, mine)], axis=0)
                          for mine, theirs in _reduce_scatter_grads(pending, tag, scatter))
            dgg, duu, dwo = own["dgg"], own["duu"], None
        else:
            dgg, duu = _swiglu_bwd_matmul(dy, wo, gg, uu, tag + "_da_act", tt)
            dlate = ()
        dwg = _matmul(ht, dgg, "nn", BF16, tag + "_dwg", tm=512, tn=512, tk=FULL_K)
        dwu = _matmul(ht, duu, "nn", BF16, tag + "_dwu", tm=512, tn=512, tk=FULL_K)
        if has_late:
            def scatter_in(parts):
                own["dh"], *landed = _matmul_nt_hiding_scatter(dgg, wg, tag + "_dh_g", tt, parts)
                return landed

            half = dwg.shape[1] // 2
            slots = jnp.stack([dwg[:, :half], dwg[:, half:], dwu[:, :half], dwu[:, half:]])
            ((mine, theirs),) = _reduce_scatter_grads([slots], tag + "_in", scatter_in)
            dw_in_block = jnp.concatenate([jnp.where(top, mine, theirs), jnp.where(top, theirs, mine)], axis=0)
            dh, dwg, dwu = own["dh"], jnp.zeros_like(wg), jnp.zeros_like(wu)
        else:
            dh = _matmul(dgg, wg, "nt", F32, tag + "_dh_g", tm=tt, tn=512, tk=FULL_K)
            dw_in_block = None
        dh = _matmul(duu, wu, "nt", F32, tag + "_dh_u", add=dh, tm=tt, tn=512, tk=FULL_K)
        dx, dg, dshift, dscale = _normmod_bwd(dh, x, g, shift, scale, dout, n_ctx, tag + "_norm_b")
        return dx, dg, dshift, dscale, dgate, dwg, dwu, dw_in_block, dwo, dlate

    ffn.defvjp(fwd, bwd)
    return ffn


def _make_normmod_linear(n_ctx, tag):
    @jax.custom_vjp
    def op(x, g, shift, scale, w):
        return fwd(x, g, shift, scale, w)[0]

    def fwd(x, g, shift, scale, w):
        h, ht = _normmod_fwd(x, g, shift, scale, n_ctx, tag + "_norm")
        y = _matmul(h, w, "nn", F32, tag + "_mm", tm=_tok_tile(x.shape[0]))
        return y, (x, g, shift, scale, w, ht)

    def bwd(res, dy):
        x, g, shift, scale, w, ht = res
        tt = _tok_tile(x.shape[0])
        dyb = dy.astype(BF16)
        dw = _matmul(ht, dyb, "nn", BF16, tag + "_dw", tm=512, tn=512, tk=FULL_K)
        dh = _matmul(dyb, w, "nt", F32, tag + "_dh", tm=tt, tn=512, tk=FULL_K)
        dx, dg, dshift, dscale = _normmod_bwd(dh, x, g, shift, scale, None, n_ctx, tag + "_norm_b")
        return dx, dg, dshift, dscale, dw

    op.defvjp(fwd, bwd)
    return op


def _make_linear_gated_res(tag):
    @jax.custom_vjp
    def op(x, a, w, gate):
        return fwd(x, a, w, gate)[0]

    def fwd(x, a, w, gate):
        ab = a.astype(BF16)
        out, y = _matmul_gated_res(ab, w, x, gate, 1.0, tag + "_mm_res", _tok_tile(x.shape[0]))
        return out, (ab, w, gate, y)

    def bwd(res, dout):
        ab, w, gate, y = res
        tt = _tok_tile(dout.shape[0])
        dy, dgate = _gated_res_bwd(dout, y, gate, 1.0, 0, tag + "_res_b")
        dw = _matmul(ab.T, dy, "nn", BF16, tag + "_dw", tm=512, tn=512, tk=FULL_K)
        da = _matmul(dy, w, "nt", F32, tag + "_da", tm=tt)
        return dout, da, dw, dgate

    op.defvjp(fwd, bwd)
    return op


def _final_loss_call(x, g, target, name):
    t, d = x.shape
    r = _tile(t, 256, SUBLANES)

    def body(x_ref, g_ref, t_ref, loss_ref, dx_ref, dg_ref):
        i = pl.program_id(0)
        xv = x_ref[...]
        gv = g_ref[...]
        rstd = lax.rsqrt(jnp.mean(xv * xv, axis=-1, keepdims=True) + RMS_EPS)
        xh = xv * rstd
        e = xh * gv - t_ref[...]
        dy = e * (1.0 / d)
        dn = dy * gv
        dx_ref[...] = rstd * (dn - xh * jnp.mean(dn * xh, axis=-1, keepdims=True))

        @pl.when(i == 0)
        def _():
            loss_ref[...] = jnp.zeros_like(loss_ref)
            dg_ref[...] = jnp.zeros_like(dg_ref)

        loss_ref[...] += 0.5 * jnp.sum(jnp.mean(e * e, axis=-1, keepdims=True), axis=0, keepdims=True)
        dg_ref[...] += jnp.sum(dy * xh, axis=0, keepdims=True)

    row = pl.BlockSpec((r, d), lambda i: (i, 0))
    vec = pl.BlockSpec((1, d), lambda i: (0, 0))
    return pl.pallas_call(
        body, name=name, grid=(t // r,),
        in_specs=[row, vec, row], out_specs=[pl.BlockSpec((1, 1), lambda i: (0, 0)), row, vec],
        out_shape=[jax.ShapeDtypeStruct((1, 1), F32), jax.ShapeDtypeStruct((t, d), F32), jax.ShapeDtypeStruct((1, d), F32)],
        compiler_params=_params(("arbitrary",)),
    )(x, g, target)


@jax.custom_vjp
def _final_loss(x, g, target):
    return _final_loss_call(x, g, target, "final_loss")[0][0, 0]


def _final_loss_fwd(x, g, target):
    loss, dx, dg = _final_loss_call(x, g, target, "final_loss")
    return loss[0, 0], (dx, dg, target)


def _final_loss_bwd(res, dl):
    dx, dg, target = res
    return dx * dl, dg * dl, jnp.zeros_like(target)


_final_loss.defvjp(_final_loss_fwd, _final_loss_bwd)


_NT = (((1,), (1,)), ((), ()))
_TN = (((0,), (0,)), ((), ()))
ATTN_Q_TILE = 512
ATTN_K_CHUNK = 1408


def _first_last_step(grid):
    ids = [pl.program_id(a) for a in range(len(grid))]
    first = functools.reduce(jnp.logical_and, [i == 0 for i in ids])
    last = functools.reduce(jnp.logical_and, [i == g - 1 for i, g in zip(ids, grid)])
    return first, last


def _attn_fwd_call(q, k, v, scale, shards=()):
    h, nq, dq = q.shape
    nk, dv = v.shape[1], v.shape[2]
    tq = _tile(nq, ATTN_Q_TILE, 16)
    ck = _tile(nk, ATTN_K_CHUNK, LANES)
    nchunk = nk // ck
    n = len(shards)
    grid = (h, nq // tq)

    exp2_scale = scale * float(np.log2(np.e))

    def body(q_ref, k_ref, v_ref, *rest):
        ins, (o_ref, lse_ref), outs = rest[:n], rest[n:n + 2], rest[n + 2:2 * n + 2]
        m_scr, l_scr, acc_scr = rest[2 * n + 2:2 * n + 5]
        sems = rest[2 * n + 5:]
        if n:
            first, last = _first_last_step(grid)
            pl.when(first)(lambda: _gather_start(ins, outs, sems))
        qv = q_ref[0]
        m_scr[...] = jnp.full_like(m_scr, -jnp.inf)
        l_scr[...] = jnp.zeros_like(l_scr)
        acc_scr[...] = jnp.zeros_like(acc_scr)

        scores = lambda c: lax.dot_general(qv, k_ref[0, c * ck:(c + 1) * ck, :], _NT, preferred_element_type=F32)
        s_next = scores(0)
        for c in range(nchunk):
            s = s_next
            if c + 1 < nchunk:
                s_next = scores(c + 1)
            m_old = m_scr[...]
            m_new = jnp.maximum(m_old, jnp.max(s, axis=-1, keepdims=True))
            alpha = jnp.exp2((m_old - m_new) * exp2_scale)
            p = jnp.exp2((s - m_new) * exp2_scale)
            l_scr[...] = alpha * l_scr[...] + jnp.sum(p, axis=-1, keepdims=True)
            acc_scr[...] = alpha * acc_scr[...] + jnp.dot(p.astype(BF16), v_ref[0, c * ck:(c + 1) * ck, :], preferred_element_type=F32)
            m_scr[...] = m_new

        o_ref[0] = acc_scr[...] / l_scr[...]
        lse_ref[0] = m_scr[...] * scale + jnp.log(l_scr[...])
        if n:
            pl.when(last)(lambda: _gather_finish(ins, outs, sems))

    keys = lambda d: pl.BlockSpec((1, nk, d), lambda hh, i: (hh, 0, 0))
    return pl.pallas_call(
        body, name="attn_fwd", grid=grid,
        in_specs=[pl.BlockSpec((1, tq, dq), lambda hh, i: (hh, i, 0)), keys(dq), keys(dv)] + [HBM_SPEC] * n,
        out_specs=[pl.BlockSpec((1, tq, dv), lambda hh, i: (hh, i, 0)),
                   pl.BlockSpec((1, tq, 1), lambda hh, i: (hh, i, 0))] + [HBM_SPEC] * n,
        out_shape=[jax.ShapeDtypeStruct((h, nq, dv), F32), jax.ShapeDtypeStruct((h, nq, 1), F32)] + _gather_out_shapes(shards),
        scratch_shapes=[pltpu.VMEM((tq, 1), F32), pltpu.VMEM((tq, 1), F32), pltpu.VMEM((tq, dv), F32)] + (_gather_sems(n) if n else []),
        compiler_params=_params(("arbitrary", "arbitrary") if n else ("parallel", "arbitrary")),
    )(q, k, v, *shards)


def _attn_bwd_call(q, k, v, o, do, lse, scale, parts=()):
    h, nq, dq = q.shape
    nk, dv = v.shape[1], v.shape[2]
    tq = _tile(nq, ATTN_Q_TILE, 16)
    ck = _tile(nk, ATTN_K_CHUNK, LANES)
    nchunk = nk // ck
    n = len(parts)
    grid = (h, nq // tq)

    log2e = float(np.log2(np.e))
    exp2_scale = scale * log2e

    def body(q_ref, k_ref, v_ref, o_ref, do_ref, lse_ref, *rest):
        ins, (dq_ref, dk_ref, dv_ref), outs, sems = rest[:n], rest[n:n + 3], rest[n + 3:2 * n + 3], rest[2 * n + 3:]
        if n:
            first, last = _first_last_step(grid)
            pl.when(first)(lambda: _scatter_start(ins, outs, sems))
        i = pl.program_id(1)

        @pl.when(i == 0)
        def _():
            dk_ref[...] = jnp.zeros_like(dk_ref)
            dv_ref[...] = jnp.zeros_like(dv_ref)

        qv = q_ref[0]
        dov = do_ref[0]
        dob = dov.astype(BF16)
        delta = jnp.sum(dov * o_ref[0], axis=-1, keepdims=True)
        lse2 = lse_ref[0] * log2e

        def scores(c):
            rows = slice(c * ck, (c + 1) * ck)
            return (lax.dot_general(qv, k_ref[0, rows, :], _NT, preferred_element_type=F32),
                    lax.dot_general(dob, v_ref[0, rows, :], _NT, preferred_element_type=F32))

        nxt = scores(0)
        dq_acc = None
        for c in range(nchunk):
            rows = slice(c * ck, (c + 1) * ck)
            s, dp = nxt
            if c + 1 < nchunk:
                nxt = scores(c + 1)
            p = jnp.exp2(s * exp2_scale - lse2)
            ds = (p * (dp - delta) * scale).astype(BF16)
            dv_ref[0, rows, :] += lax.dot_general(p.astype(BF16), dob, _TN, preferred_element_type=F32)
            dk_ref[0, rows, :] += lax.dot_general(ds, qv, _TN, preferred_element_type=F32)
            part = jnp.dot(ds, k_ref[0, rows, :], preferred_element_type=F32)
            dq_acc = part if dq_acc is None else dq_acc + part
        dq_ref[0] = dq_acc
        if n:
            pl.when(last)(lambda: _scatter_finish(ins, outs, sems))

    qspec = lambda d: pl.BlockSpec((1, tq, d), lambda hh, i: (hh, i, 0))
    kspec = lambda d: pl.BlockSpec((1, nk, d), lambda hh, i: (hh, 0, 0), pipeline_mode=pl.Buffered(1))
    return pl.pallas_call(
        body, name="attn_bwd", grid=grid,
        in_specs=[qspec(dq), kspec(dq), kspec(dv), qspec(dv), qspec(dv), qspec(1)] + [HBM_SPEC] * n,
        out_specs=[qspec(dq), kspec(dq), kspec(dv)] + [HBM_SPEC] * n,
        out_shape=[jax.ShapeDtypeStruct((h, nq, dq), F32), jax.ShapeDtypeStruct((h, nk, dq), F32),
                   jax.ShapeDtypeStruct((h, nk, dv), F32)] + [jax.ShapeDtypeStruct(p.shape, p.dtype) for p in parts],
        scratch_shapes=_scatter_sems(n) if n else [],
        compiler_params=_params(("arbitrary", "arbitrary") if n else ("parallel", "arbitrary")),
    )(q, k, v, o, do, lse, *parts)


@jax.custom_vjp
def _attention_gather(q, k, v, shards):
    return _attention_gather_fwd(q, k, v, shards)[0]


def _attention_gather_fwd(q, k, v, shards):
    scale = q.shape[-1] ** -0.5
    qb, kb, vb = q.astype(BF16), k.astype(BF16), v.astype(BF16)
    blocks = [s.astype(BF16) for s in shards]
    o, lse, *stacked = _attn_fwd_call(qb, kb, vb, scale, blocks)
    return (o, tuple(_fill_own_slot(stacked, blocks))), (qb, kb, vb, o, lse)


def _attention_gather_bwd(res, cts):
    qb, kb, vb, o, lse = res
    do, dstacked = cts
    scale = qb.shape[-1] ** -0.5
    if not dstacked:
        return (*_attn_bwd_call(qb, kb, vb, o, do, lse, scale), ())
    own = {}

    def scatter(parts):
        own["dq"], own["dk"], own["dv"], *landed = _attn_bwd_call(qb, kb, vb, o, do, lse, scale, parts)
        return landed

    top = lax.axis_index("c") == 0
    grads = tuple(jnp.concatenate([jnp.where(top, mine, theirs), jnp.where(top, theirs, mine)], axis=0)
                  for mine, theirs in _reduce_scatter_grads(list(dstacked), "attn", scatter))
    return own["dq"], own["dk"], own["dv"], grads


_attention_gather.defvjp(_attention_gather_fwd, _attention_gather_bwd)


def _attention(q, k, v):
    return _attention_gather(q, k, v, ())[0]


RET_UNROLL = 4


def _bf(x):
    return x.astype(BF16)


def _dot(a, b, dims=(((1,), (0,)), ((), ()))):
    return lax.dot_general(_bf(a), _bf(b), dims, preferred_element_type=F32)


def _sum_all(x):
    return jnp.sum(jnp.sum(x, axis=1, keepdims=True), axis=0, keepdims=True)


def _ret_consts(lgf_ref, lgb_ref):
    c = RET_CHUNK
    lgf = lgf_ref[0][:, :1]
    lgb = lgb_ref[0][:, :1]
    diff = (lax.broadcasted_iota(jnp.int32, (c, c), 0) - lax.broadcasted_iota(jnp.int32, (c, c), 1)).astype(F32)
    mf = diff >= 0
    dmat = jnp.where(mf, jnp.exp(lgf * jnp.where(mf, diff, 0.0)), jnp.exp(lgb * jnp.where(mf, 0.0, -diff)))
    col = lax.broadcasted_iota(jnp.int32, (c, 1), 0).astype(F32)
    return dict(diff=diff, mf=mf, dmat=dmat, col=col,
                xif=jnp.exp(lgf * (col + 1.0)), zf=jnp.exp(lgf * (c - 1.0 - col)),
                xib=jnp.exp(lgb * (c - col)), zb=jnp.exp(lgb * col),
                gf=jnp.exp(lgf * c), gb=jnp.exp(lgb * c))


def _ret_rows(n):
    return pl.ds(pl.multiple_of(n * RET_CHUNK, RET_CHUNK), RET_CHUNK)


def _ret_fwd_call(q, k, v, lgf, lgb, s0f, s0b):
    h, n_tok, dk = q.shape
    dv = v.shape[-1]
    nc = n_tok // RET_CHUNK

    def body(q_ref, k_ref, v_ref, lgf_ref, lgb_ref, s0f_ref, s0b_ref, y_ref, sff_ref, sbf_ref, sb_scr):
        cs = _ret_consts(lgf_ref, lgb_ref)

        sbf_ref[0] = s0b_ref[0]

        @pl.loop(0, nc, unroll=RET_UNROLL)
        def _(t):
            n = nc - 1 - t
            sb = sbf_ref[0]
            sb_scr[n] = sb
            sbf_ref[0] = cs["gb"] * sb + _dot(k_ref[0, _ret_rows(n), :] * cs["zb"], v_ref[0, _ret_rows(n), :], _TN)

        sff_ref[0] = s0f_ref[0]

        @pl.loop(0, nc, unroll=RET_UNROLL)
        def _(n):
            sf = sff_ref[0]
            qc, kc, vc = q_ref[0, _ret_rows(n), :], k_ref[0, _ret_rows(n), :], v_ref[0, _ret_rows(n), :]
            p = _dot(qc, kc, _NT) * cs["dmat"]
            y_ref[0, _ret_rows(n), :] = _dot(p, vc) + _dot(qc * cs["xif"], sf) + _dot(qc * cs["xib"], sb_scr[n])
            sff_ref[0] = cs["gf"] * sf + _dot(kc * cs["zf"], vc, _TN)

    tok = lambda d: pl.BlockSpec((1, n_tok, d), lambda hh: (hh, 0, 0))
    lg = pl.BlockSpec((1, 1, LANES), lambda hh: (hh, 0, 0))
    st = pl.BlockSpec((1, dk, dv), lambda hh: (hh, 0, 0))
    return pl.pallas_call(
        body, name="ret_fwd_%d" % n_tok, grid=(h,),
        in_specs=[tok(dk), tok(dk), tok(dv), lg, lg, st, st], out_specs=[tok(dv), st, st],
        out_shape=[jax.ShapeDtypeStruct((h, n_tok, dv), F32)] + [jax.ShapeDtypeStruct((h, dk, dv), F32)] * 2,
        scratch_shapes=[pltpu.VMEM((nc, dk, dv), F32)],
        compiler_params=_params(("parallel",)),
    )(q, k, v, lgf, lgb, s0f, s0b)


def _ret_bwd_call(q, k, v, lgf, lgb, s0f, s0b, dy, dsff, dsbf):
    h, n_tok, dk = q.shape
    dv = v.shape[-1]
    nc = n_tok // RET_CHUNK
    c = float(RET_CHUNK)

    def body(q_ref, k_ref, v_ref, lgf_ref, lgb_ref, s0f_ref, s0b_ref, dy_ref, dsff_ref, dsbf_ref,
             dq_ref, dk_ref, dv_ref, dlgf_ref, dlgb_ref, ds0f_ref, ds0b_ref, sb_scr, gf_scr, st_a, st_b):
        cs = _ret_consts(lgf_ref, lgb_ref)

        st_a[...] = s0b_ref[0]
        st_b[...] = dsff_ref[0]

        @pl.loop(0, nc, unroll=RET_UNROLL)
        def _(t):
            n = nc - 1 - t
            sb, gf_next = st_a[...], st_b[...]
            sb_scr[n] = sb
            gf_scr[n] = gf_next
            qc, kc, vc, dyc = (r[0, _ret_rows(n), :] for r in (q_ref, k_ref, v_ref, dy_ref))
            st_a[...] = cs["gb"] * sb + _dot(kc * cs["zb"], vc, _TN)
            st_b[...] = _dot(qc * cs["xif"], dyc, _TN) + cs["gf"] * gf_next

        ds0f_ref[0] = st_b[...]

        st_a[...] = s0f_ref[0]
        st_b[...] = dsbf_ref[0]
        dlgf_ref[...] = jnp.zeros_like(dlgf_ref)
        dlgb_ref[...] = jnp.zeros_like(dlgb_ref)

        @pl.loop(0, nc, unroll=RET_UNROLL)
        def _(n):
            sf, gb_prev = st_a[...], st_b[...]
            sb, gf_next = sb_scr[n], gf_scr[n]
            qc, kc, vc, dyc = (r[0, _ret_rows(n), :] for r in (q_ref, k_ref, v_ref, dy_ref))
            a = _dot(qc, kc, _NT)
            dp = _dot(dyc, vc, _NT)
            da = _bf(dp * cs["dmat"])
            dqf = _dot(dyc, sf, _NT)
            dqb = _dot(dyc, sb, _NT)
            dkf = _dot(vc, gf_next, _NT)
            dkb = _dot(vc, gb_prev, _NT)
            dq_ref[0, _ret_rows(n), :] = _dot(da, kc) + dqf * cs["xif"] + dqb * cs["xib"]
            dk_ref[0, _ret_rows(n), :] = _dot(da, qc, _TN) + dkf * cs["zf"] + dkb * cs["zb"]
            dv_ref[0, _ret_rows(n), :] = (_dot(a * cs["dmat"], dyc, _TN) + _dot(kc * cs["zf"], gf_next)
                                         + _dot(kc * cs["zb"], gb_prev))
            w = dp * a * cs["dmat"] * cs["diff"]
            row = lambda x: jnp.sum(x, axis=1, keepdims=True)
            dlgf_ref[0] += (_sum_all(jnp.where(cs["mf"], w, 0.0))
                            + _sum_all((cs["col"] + 1.0) * cs["xif"] * row(dqf * qc) + (c - 1.0 - cs["col"]) * cs["zf"] * row(dkf * kc))
                            + c * cs["gf"] * _sum_all(gf_next * sf))
            dlgb_ref[0] += (_sum_all((c - cs["col"]) * cs["xib"] * row(dqb * qc) + cs["col"] * cs["zb"] * row(dkb * kc))
                            + c * cs["gb"] * _sum_all(gb_prev * sb) - _sum_all(jnp.where(cs["mf"], 0.0, w)))
            st_a[...] = cs["gf"] * sf + _dot(kc * cs["zf"], vc, _TN)
            st_b[...] = _dot(qc * cs["xib"], dyc, _TN) + cs["gb"] * gb_prev

        ds0b_ref[0] = st_b[...]

    tok = lambda d: pl.BlockSpec((1, n_tok, d), lambda hh: (hh, 0, 0), pipeline_mode=pl.Buffered(1))
    lg = pl.BlockSpec((1, 1, LANES), lambda hh: (hh, 0, 0))
    st = pl.BlockSpec((1, dk, dv), lambda hh: (hh, 0, 0))
    return pl.pallas_call(
        body, name="ret_bwd_%d" % n_tok, grid=(h,),
        in_specs=[tok(dk), tok(dk), tok(dv), lg, lg, st, st, tok(dv), st, st],
        out_specs=[tok(dk), tok(dk), tok(dv), lg, lg, st, st],
        out_shape=[jax.ShapeDtypeStruct((h, n_tok, dk), F32)] * 2 + [jax.ShapeDtypeStruct((h, n_tok, dv), F32)]
        + [jax.ShapeDtypeStruct((h, 1, LANES), F32)] * 2 + [jax.ShapeDtypeStruct((h, dk, dv), F32)] * 2,
        scratch_shapes=[pltpu.VMEM((nc, dk, dv), F32), pltpu.VMEM((nc, dk, dv), F32), pltpu.VMEM((dk, dv), F32), pltpu.VMEM((dk, dv), F32)],
        compiler_params=_params(("parallel",)),
    )(q, k, v, lgf, lgb, s0f, s0b, dy, dsff, dsbf)


def _lane_bcast(lg):
    return jnp.broadcast_to(lg[:, None, None], (lg.shape[0], 1, LANES))


@jax.custom_vjp
def _retention(q, k, v, lgf, lgb, s0f, s0b):
    return tuple(_ret_fwd_call(q, k, v, _lane_bcast(lgf), _lane_bcast(lgb), s0f, s0b))


def _retention_fwd(q, k, v, lgf, lgb, s0f, s0b):
    return _retention(q, k, v, lgf, lgb, s0f, s0b), (q, k, v, lgf, lgb, s0f, s0b)


def _retention_bwd(res, cts):
    q, k, v, lgf, lgb, s0f, s0b = res
    dy, dsff, dsbf = cts
    dq, dk, dv, dlgf, dlgb, ds0f, ds0b = _ret_bwd_call(q, k, v, _lane_bcast(lgf), _lane_bcast(lgb), s0f, s0b, dy, dsff, dsbf)
    return dq, dk, dv, dlgf[:, 0, 0], dlgb[:, 0, 0], ds0f, ds0b


_retention.defvjp(_retention_fwd, _retention_bwd)


def _gn_specs(y):
    h, n, dv = y.shape
    r = _tile(n, 512, SUBLANES)
    return (h, n, dv, r, pl.BlockSpec((1, r, dv), lambda i, hh: (hh, i, 0)), pl.BlockSpec((r, dv), lambda i, hh: (i, hh)))


def _gn_norm(yv):
    mu = jnp.mean(yv, axis=-1, keepdims=True)
    yc = yv - mu
    rstd = lax.rsqrt(jnp.mean(yc * yc, axis=-1, keepdims=True) + GN_EPS)
    return yc * rstd, rstd


def _gn_gate_fwd_call(y, gate):
    h, n, dv, r, yspec, gspec = _gn_specs(y)

    def body(y_ref, g_ref, o_ref):
        gv = g_ref[...]
        o_ref[...] = gv * jax.nn.sigmoid(gv) * _gn_norm(y_ref[0])[0]

    return pl.pallas_call(
        body, name="gn_gate", grid=(n // r, h), in_specs=[yspec, gspec], out_specs=gspec,
        out_shape=jax.ShapeDtypeStruct((n, h * dv), F32), compiler_params=_params(("parallel", "parallel")),
    )(y, gate)


def _gn_gate_bwd_call(y, gate, dout):
    h, n, dv, r, yspec, gspec = _gn_specs(y)

    def body(y_ref, g_ref, do_ref, dy_ref, dg_ref):
        gv = g_ref[...]
        dov = do_ref[...]
        yn, rstd = _gn_norm(y_ref[0])
        sg = jax.nn.sigmoid(gv)
        dg_ref[...] = dov * yn * (sg * (1.0 + gv * (1.0 - sg)))
        dyn = dov * (gv * sg)
        dy_ref[0] = rstd * (dyn - jnp.mean(dyn, axis=-1, keepdims=True) - yn * jnp.mean(dyn * yn, axis=-1, keepdims=True))

    return pl.pallas_call(
        body, name="gn_gate_b", grid=(n // r, h), in_specs=[yspec, gspec, gspec], out_specs=[yspec, gspec],
        out_shape=[jax.ShapeDtypeStruct((h, n, dv), F32), jax.ShapeDtypeStruct((n, h * dv), F32)],
        compiler_params=_params(("parallel", "parallel")),
    )(y, gate, dout)


@jax.custom_vjp
def _gn_gate(y, gate):
    return _gn_gate_fwd_call(y, gate)


_gn_gate.defvjp(lambda y, gate: (_gn_gate_fwd_call(y, gate), (y, gate)),
                lambda res, dout: tuple(_gn_gate_bwd_call(res[0], res[1], dout)))


def _rope_tables(pos, dim, base):
    inv = base ** (-jnp.arange(0, dim, 2, dtype=F32) / dim)
    ang = pos.astype(F32)[:, None] * inv[None, :]
    return jnp.cos(ang)[:, None, :], jnp.sin(ang)[:, None, :]


def _rotate(x, cos, sin):
    x1, x2 = jnp.split(x, 2, axis=-1)
    return jnp.concatenate([x1 * cos - x2 * sin, x2 * cos + x1 * sin], axis=-1)


def _axial_rope(x, row_tab, col_tab):
    xr, xc = jnp.split(x, 2, axis=-1)
    return jnp.concatenate([_rotate(xr, *row_tab), _rotate(xc, *col_tab)], axis=-1)


def _heads(t):
    return jnp.swapaxes(t, 0, 1)


def _local_loss(x, mods_lat, mods_ctx, small, first, early, late, big, ctx, target):
    n_lat, d = x.shape
    n_ctx = ctx.shape[0]
    both = lambda i: jnp.stack([mods_ctx[i], mods_lat[i]])[:, None, :]
    lat = lambda i: mods_lat[i][None, None, :]

    xs = jnp.concatenate([ctx, x], axis=0)
    x1, (st_mix_in, st_uq, st_ukv, st_mix_out) = _make_ffn(n_ctx, "ffn1", True)(
        xs, small["norm1_g"], both(0), both(1), both(2), big["ffn1_wg"], big["ffn1_wu"], first, None, tuple(early))
    every = (0, 1, 2, 3)
    big = dict(mix_in=jnp.pad(_slots_side_by_side(st_mix_in, every), ((0, 0), (0, MIX_IN_PAD - MIX_IN))),
               w_uq=_slots_side_by_side(st_uq, every), w_ukv=_slots_side_by_side(st_ukv, every),
               mix_out=st_mix_out.reshape(-1, st_mix_out.shape[-1]))
    proj = _make_normmod_linear(n_ctx, "mix_in")(x1, small["norm2_g"], both(3), both(4), big["mix_in"])
    offs = np.cumsum((0,) + MIX_SPLITS)
    part = lambda i, rows: proj[rows, offs[i]:offs[i + 1]]
    lat_rows, ctx_rows = slice(n_ctx, None), slice(0, n_ctx)

    zq = jnp.zeros((1, 1, MLA_Q_RANK), F32)
    zkv = jnp.zeros((1, 1, MLA_KV_RANK), F32)
    q = _make_normmod_linear(0, "mla_q")(part(4, lat_rows), small["mla_q_norm_g"], zq, zq, big["w_uq"])
    kv = _make_normmod_linear(0, "mla_kv")(part(5, slice(None)), small["mla_kv_norm_g"], zkv, zkv, big["w_ukv"])

    lgf = jax.nn.log_sigmoid(small["ret_decay_fwd"][0])
    lgb = jax.nn.log_sigmoid(small["ret_decay_bwd"][0])
    ret_tab = _rope_tables(jnp.arange(n_lat), RET_DK, ROPE_BASE)
    hd = lambda t, dd: t.reshape(t.shape[0], RET_HEADS, dd)
    s_zero = jnp.zeros((RET_HEADS, RET_DK, RET_DV), F32)
    _, s_f, s_b = _retention(_heads(hd(part(0, ctx_rows), RET_DK)), _heads(hd(part(1, ctx_rows), RET_DK) * (RET_DK ** -0.5)),
                             _heads(hd(part(2, ctx_rows), RET_DV)), lgf, lgb, s_zero, s_zero)
    rq = _rotate(hd(part(0, lat_rows), RET_DK), *ret_tab)
    rk = _rotate(hd(part(1, lat_rows), RET_DK) * (RET_DK ** -0.5), *ret_tab)
    y_lat, _, _ = _retention(_heads(rq), _heads(rk), _heads(hd(part(2, lat_rows), RET_DV)), lgf, lgb, s_f, s_b)
    ret_out = _gn_gate(y_lat, part(3, lat_rows))

    pos = jnp.arange(n_lat)
    row_tab = _rope_tables(pos // GRID_W, MLA_ROPE // 2, ROPE_BASE)
    col_tab = _rope_tables(pos % GRID_W, MLA_ROPE // 2, ROPE_BASE)
    q = q.reshape(n_lat, MLA_HEADS, MLA_NOPE + MLA_ROPE)
    q_all = jnp.concatenate([q[..., :MLA_NOPE], _axial_rope(q[..., MLA_NOPE:], row_tab, col_tab)], axis=-1)
    kv = kv.reshape(n_ctx + n_lat, MLA_HEADS, MLA_NOPE + MLA_V)
    kr_lat = _axial_rope(part(6, lat_rows)[:, None, :], row_tab, col_tab)
    kr = jnp.concatenate([kr_lat, part(6, ctx_rows)[:, None, :]], axis=0)
    kv_lat_first = jnp.concatenate([kv[n_ctx:], kv[:n_ctx]], axis=0)
    k_all = jnp.concatenate([kv_lat_first[..., :MLA_NOPE], jnp.broadcast_to(kr, (n_ctx + n_lat, MLA_HEADS, MLA_ROPE))], axis=-1)
    mla, (w_in2, w_out2) = _attention_gather(_heads(q_all), _heads(k_all), _heads(kv_lat_first[..., MLA_NOPE:]), tuple(late))
    mla_out = _heads(mla).reshape(n_lat, MLA_HEADS * MLA_V)

    x2 = _make_linear_gated_res("mix_out")(x1[n_ctx:], jnp.concatenate([ret_out, mla_out], axis=-1), big["mix_out"], lat(5))
    x3, _ = _make_ffn(0, "ffn2", False)(x2, small["norm3_g"], lat(6), lat(7), lat(8), _slots_side_by_side(w_in2, (0, 1)),
                                        _slots_side_by_side(w_in2, (2, 3)), None, w_out2.reshape(-1, w_out2.shape[-1]), ())
    return _final_loss(x3, small["final_norm_g"][None, :], target)


HBM_SPEC = pl.BlockSpec(memory_space=pl.ANY)
VMEM_SPEC = pl.BlockSpec(memory_space=pltpu.VMEM)
ALL_PEERS = (1, 2, 3, 4, 5, 6, 7)
CHIP_PEERS = (4, 2, 6)


def _me():
    return lax.axis_index("x"), lax.axis_index("y"), lax.axis_index("c")


def _flip(pos, mask):
    x, y, c = pos
    return (1 - x if mask & 4 else x, 1 - y if mask & 2 else y, 1 - c if mask & 1 else c)


def _allgather_small(block, masks, chips_only, name):
    r, c = block.shape
    n_slots = 4 if chips_only else 8

    def body(x_ref, out_ref, send_sems, recv_sems, local_sem):
        pos = _me()
        slot = 2 * pos[0] + pos[1] if chips_only else 4 * pos[0] + 2 * pos[1] + pos[2]
        local = pltpu.make_async_copy(x_ref, out_ref.at[slot], local_sem)
        local.start()
        copies = [pltpu.make_async_remote_copy(src_ref=x_ref, dst_ref=out_ref.at[slot], send_sem=send_sems.at[j], recv_sem=recv_sems.at[j],
                                               device_id=_flip(pos, mask), device_id_type=MESH) for j, mask in enumerate(masks)]
        for cp in copies:
            cp.start()
        for cp in copies:
            cp.wait()
        local.wait()

    return pl.pallas_call(
        body, name=name, in_specs=[VMEM_SPEC], out_specs=VMEM_SPEC,
        out_shape=jax.ShapeDtypeStruct((n_slots, r, c), block.dtype),
        scratch_shapes=[pltpu.SemaphoreType.DMA((len(masks),)), pltpu.SemaphoreType.DMA((len(masks),)), pltpu.SemaphoreType.DMA],
        compiler_params=pltpu.CompilerParams(vmem_limit_bytes=VMEM_LIMIT_BYTES),
    )(block)


def _gather_weights(shards):
    n = len(shards)

    def body(*refs):
        ins, outs, sems = refs[:n], refs[n:2 * n], refs[2 * n:]
        _gather_start(ins, outs, sems)
        _gather_finish(ins, outs, sems)

    stacked = pl.pallas_call(
        body, name="gather_weights", in_specs=[HBM_SPEC] * n, out_specs=[HBM_SPEC] * n,
        out_shape=_gather_out_shapes(shards), scratch_shapes=_gather_sems(n),
    )(*shards)
    return _fill_own_slot(stacked, shards)


def _gather_out_shapes(shards):
    return [jax.ShapeDtypeStruct((4,) + s.shape, s.dtype) for s in shards]


def _gather_sems(n):
    return [pltpu.SemaphoreType.DMA((3 * n,)) for _ in range(4)]


def _fill_own_slot(stacked, shards):
    if not shards:
        return []
    chip = 2 * lax.axis_index("x") + lax.axis_index("y")
    return [lax.dynamic_update_slice_in_dim(st, sh[None], chip, axis=0) for st, sh in zip(stacked, shards)]


def _gather_send(ins, outs, sems, w, j, pos):
    x, y, c = pos
    half = ins[w].shape[0] // 2
    mine = pl.ds(c * half, half)
    return pltpu.make_async_remote_copy(src_ref=ins[w].at[mine], dst_ref=outs[w].at[2 * x + y, mine], send_sem=sems[0].at[3 * w + j],
                                        recv_sem=sems[1].at[3 * w + j], device_id=_flip(pos, CHIP_PEERS[j]), device_id_type=MESH)


def _gather_pass(ins, outs, sems, w, j, pos, to_me):
    px, py, _ = _flip(pos, CHIP_PEERS[j])
    half = ins[w].shape[0] // 2
    slab = outs[w].at[2 * px + py, pl.ds(((1 - pos[2]) if to_me else pos[2]) * half, half)]
    return pltpu.make_async_remote_copy(src_ref=slab, dst_ref=slab, send_sem=sems[2].at[3 * w + j], recv_sem=sems[3].at[3 * w + j],
                                        device_id=_flip(pos, 1), device_id_type=MESH)


def _gather_start(ins, outs, sems):
    pos = _me()
    for w in range(len(ins)):
        for j in range(3):
            _gather_send(ins, outs, sems, w, j, pos).start()


def _gather_finish(ins, outs, sems):
    pos = _me()
    pairs = [(w, j) for w in range(len(ins)) for j in range(3)]
    for w, j in pairs:
        _gather_send(ins, outs, sems, w, j, pos).wait_recv()
        _gather_pass(ins, outs, sems, w, j, pos, False).start()
    for w, j in pairs:
        _gather_pass(ins, outs, sems, w, j, pos, True).wait_recv()
    for w, j in pairs:
        _gather_send(ins, outs, sems, w, j, pos).wait_send()
        _gather_pass(ins, outs, sems, w, j, pos, False).wait_send()


def _pair_swap_halves(grads, tag):
    n = len(grads)

    def body(*refs):
        ins, outs = refs[:n], refs[n:2 * n]
        send_sems, recv_sems = refs[2 * n:]
        pos = _me()
        copies = []
        for w in range(n):
            half = grads[w].shape[1] // 2
            cp = pltpu.make_async_remote_copy(src_ref=ins[w].at[:, pl.ds((1 - pos[2]) * half, half), :], dst_ref=outs[w], send_sem=send_sems.at[w],
                                              recv_sem=recv_sems.at[w], device_id=_flip(pos, 1), device_id_type=MESH)
            cp.start()
            copies.append(cp)
        for cp in copies:
            cp.wait()

    return pl.pallas_call(
        body, name="pair_swap_halves_" + tag, in_specs=[HBM_SPEC] * n, out_specs=[HBM_SPEC] * n,
        out_shape=[jax.ShapeDtypeStruct((4, g.shape[1] // 2, g.shape[2]), g.dtype) for g in grads],
        scratch_shapes=[pltpu.SemaphoreType.DMA((n,)), pltpu.SemaphoreType.DMA((n,))],
    )(*grads)


def _chip_scatter(parts):
    n = len(parts)

    def body(*refs):
        ins, outs, sems = refs[:n], refs[n:2 * n], refs[2 * n:]
        _scatter_start(ins, outs, sems)
        _scatter_finish(ins, outs, sems)

    return pl.pallas_call(
        body, name="chip_scatter", in_specs=[HBM_SPEC] * n, out_specs=[HBM_SPEC] * n,
        out_shape=[jax.ShapeDtypeStruct(p.shape, p.dtype) for p in parts], scratch_shapes=_scatter_sems(n),
    )(*parts)


def _scatter_sems(n):
    return [pltpu.SemaphoreType.DMA((3 * n,)), pltpu.SemaphoreType.DMA((3 * n,)), pltpu.SemaphoreType.DMA((n,))]


def _scatter_copies(ins, outs, sems):
    pos = _me()
    me = 2 * pos[0] + pos[1]
    local = [pltpu.make_async_copy(ins[w].at[me], outs[w].at[me], sems[2].at[w]) for w in range(len(ins))]
    remote = []
    for w in range(len(ins)):
        for j, mask in enumerate(CHIP_PEERS):
            px, py, _ = _flip(pos, mask)
            remote.append(pltpu.make_async_remote_copy(src_ref=ins[w].at[2 * px + py], dst_ref=outs[w].at[me], send_sem=sems[0].at[3 * w + j],
                                                       recv_sem=sems[1].at[3 * w + j], device_id=_flip(pos, mask), device_id_type=MESH))
    return local, remote


def _scatter_start(ins, outs, sems):
    local, remote = _scatter_copies(ins, outs, sems)
    for cp in local + remote:
        cp.start()


def _scatter_finish(ins, outs, sems):
    local, remote = _scatter_copies(ins, outs, sems)
    for cp in remote + local:
        cp.wait()


def _pair_swap_reduced(halves, tag):
    n = len(halves)

    def body(*refs):
        ins, outs = refs[:n], refs[n:2 * n]
        send_sems, recv_sems = refs[2 * n:]
        pos = _me()
        copies = []
        for w in range(n):
            cp = pltpu.make_async_remote_copy(src_ref=ins[w], dst_ref=outs[w], send_sem=send_sems.at[w], recv_sem=recv_sems.at[w],
                                              device_id=_flip(pos, 1), device_id_type=MESH)
            cp.start()
            copies.append(cp)
        for cp in copies:
            cp.wait()

    dma = lambda k: pltpu.SemaphoreType.DMA((k,))
    return pl.pallas_call(
        body, name="pair_swap_reduced_" + tag, in_specs=[HBM_SPEC] * n, out_specs=[HBM_SPEC] * n,
        out_shape=[jax.ShapeDtypeStruct(h.shape, h.dtype) for h in halves],
        scratch_shapes=[dma(n), dma(n)],
    )(*halves)


def _add_pair(mine, theirs, name):
    s, h, c = mine.shape
    r = _tile(h, max(16, (1 << 19) // c), 16)

    def body(a_ref, b_ref, o_ref):
        o_ref[...] = (a_ref[...].astype(F32) + b_ref[...].astype(F32)).astype(BF16)

    blk = pl.BlockSpec((1, r, c), lambda i, j: (i, j, 0))
    return pl.pallas_call(
        body, name=name, grid=(s, h // r), in_specs=[blk, blk], out_specs=blk,
        out_shape=jax.ShapeDtypeStruct(mine.shape, BF16), compiler_params=_params(("parallel", "parallel")),
    )(mine, theirs)


def _sum_slots(parts, name):
    s, h, c = parts.shape
    r = _tile(h, max(16, (1 << 18) // c), 16)

    def body(p_ref, o_ref):
        acc = p_ref[0].astype(F32)
        for k in range(1, s):
            acc = acc + p_ref[k].astype(F32)
        o_ref[...] = acc

    return pl.pallas_call(
        body, name=name, grid=(h // r,), in_specs=[pl.BlockSpec((s, r, c), lambda i: (0, i, 0))],
        out_specs=pl.BlockSpec((r, c), lambda i: (i, 0)),
        out_shape=jax.ShapeDtypeStruct((h, c), F32), compiler_params=_params(("parallel",)),
    )(parts)


def _reduce_scatter_grads(stacked, tag, scatter=_chip_scatter):
    c = lax.axis_index("c")
    theirs = _pair_swap_halves(stacked, tag)
    parts = []
    for w, (g, t) in enumerate(zip(stacked, theirs)):
        half = g.shape[1] // 2
        mine = lax.dynamic_slice_in_dim(g, c * half, half, axis=1)
        parts.append(_add_pair(mine, t, "rs_add_pair_%s_%d" % (tag, w)))
    landed = scatter(parts)
    halves = [_sum_slots(p, "rs_sum_slots_%s_%d" % (tag, w)) for w, p in enumerate(landed)]
    return list(zip(halves, _pair_swap_reduced(halves, tag)))


def _adamw_math(w, g, m, v):
    m = ADAM_B1 * m + (1.0 - ADAM_B1) * g
    v = ADAM_B2 * v + (1.0 - ADAM_B2) * (g * g)
    m_hat = m / (1.0 - ADAM_B1 ** ADAM_STEP)
    v_hat = v / (1.0 - ADAM_B2 ** ADAM_STEP)
    return -ADAM_LR * (m_hat / (jnp.sqrt(v_hat) + ADAM_EPS) + ADAM_WD * w), m, v


def _adamw(w, g, m, v, name):
    rows, cols = w.shape
    r = _tile(rows, max(SUBLANES, (1 << 18) // cols), SUBLANES)

    def body(w_ref, g_ref, m_ref, v_ref, d_ref, mo_ref, vo_ref):
        d_ref[...], mo_ref[...], vo_ref[...] = _adamw_math(w_ref[...], g_ref[...], m_ref[...], v_ref[...])

    blk = pl.BlockSpec((r, cols), lambda i: (i, 0))
    return pl.pallas_call(
        body, name=name, grid=(rows // r,), in_specs=[blk] * 4, out_specs=[blk] * 3,
        out_shape=[jax.ShapeDtypeStruct(w.shape, F32)] * 3, compiler_params=_params(("parallel",)),
    )(w, g, m, v)


def _adamw_halves(w, g_mine, g_theirs, m, v, core, name):
    rows, cols = w.shape
    half = rows // 2
    r = _tile(half, max(SUBLANES, (1 << 18) // cols), SUBLANES)
    nbh = half // r

    def body(core_ref, w_ref, gm_ref, gt_ref, m_ref, v_ref, g_ref, d_ref, mo_ref, vo_ref):
        is_mine = (pl.program_id(0) // nbh) == core_ref[0]

        @pl.when(is_mine)
        def _():
            g_ref[...] = gm_ref[...]

        @pl.when(jnp.logical_not(is_mine))
        def _():
            g_ref[...] = gt_ref[...]

        g = g_ref[...]
        d_ref[...], mo_ref[...], vo_ref[...] = _adamw_math(w_ref[...], g, m_ref[...], v_ref[...])

    full = pl.BlockSpec((r, cols), lambda i, core_ref: (i, 0))
    part = pl.BlockSpec((r, cols), lambda i, core_ref: (i % nbh, 0))
    return pl.pallas_call(
        body, name=name,
        grid_spec=pltpu.PrefetchScalarGridSpec(num_scalar_prefetch=1, grid=(rows // r,), in_specs=[full, part, part, full, full],
                                               out_specs=[full] * 4),
        out_shape=[jax.ShapeDtypeStruct(w.shape, F32)] * 4, compiler_params=_params(("parallel",)),
    )(core, w, g_mine, g_theirs, m, v)


def _adamw_reduced(parts, w, m, v, name):
    def body(p_ref, w_ref, m_ref, v_ref, g_ref, d_ref, mo_ref, vo_ref):
        g = p_ref[0]
        for k in range(1, parts.shape[0]):
            g = g + p_ref[k]
        g_ref[...] = g
        d_ref[...], mo_ref[...], vo_ref[...] = _adamw_math(w_ref[...], g, m_ref[...], v_ref[...])

    return pl.pallas_call(
        body, name=name, in_specs=[VMEM_SPEC] * 4, out_specs=[VMEM_SPEC] * 4,
        out_shape=[jax.ShapeDtypeStruct(w.shape, F32)] * 4,
        compiler_params=pltpu.CompilerParams(vmem_limit_bytes=VMEM_LIMIT_BYTES),
    )(parts, w, m, v)


WEIGHTS = ("c_ctx", "ada_w", "ada_b", "norm1_g", "ffn1_w_in", "ffn1_w_out", "norm2_g", "mix_w_in", "ret_decay_fwd", "ret_decay_bwd",
           "mla_q_norm_g", "mla_w_uq", "mla_kv_norm_g", "mla_w_ukv", "mix_w_out", "norm3_g", "ffn2_w_in", "ffn2_w_out", "final_norm_g")
SMALL = ("c_ctx", "ada_b", "norm1_g", "norm2_g", "ret_decay_fwd", "ret_decay_bwd", "mla_q_norm_g", "mla_kv_norm_g", "norm3_g", "final_norm_g")
FIRST = "ffn1_w_in"
EARLY = ("ffn1_w_out", "mix_w_in", "mla_w_uq", "mla_w_ukv", "mix_w_out")
LATE = ("ffn2_w_in", "ffn2_w_out")


def _slots_side_by_side(stacked, slots):
    return jnp.concatenate([stacked[k] for k in slots], axis=1)


def _pack(vectors):
    flat = jnp.concatenate([v.reshape(-1) for v in vectors])
    return jnp.pad(flat, (0, -flat.shape[0] % (SUBLANES * LANES))).reshape(SUBLANES, -1)


def _rows8(a):
    return a.reshape(a.shape[0] * SUBLANES, a.shape[1] // SUBLANES)


def _unpack(packed, like):
    packed = packed.reshape(-1)
    out, off = [], 0
    for ref in like:
        out.append(packed[off:off + ref.size].reshape(ref.shape))
        off += ref.size
    return out


def kernel(x, c, ctx, c_ctx, ada_w, ada_b, norm1_g, ffn1_w_in, ffn1_w_out, norm2_g, mix_w_in, ret_decay_fwd, ret_decay_bwd, mla_q_norm_g, mla_w_uq, mla_kv_norm_g, mla_w_ukv, mix_w_out, norm3_g, ffn2_w_in, ffn2_w_out, final_norm_g, loss_target, m_c_ctx, m_ada_w, m_ada_b, m_norm1_g, m_ffn1_w_in, m_ffn1_w_out, m_norm2_g, m_mix_w_in, m_ret_decay_fwd, m_ret_decay_bwd, m_mla_q_norm_g, m_mla_w_uq, m_mla_kv_norm_g, m_mla_w_ukv, m_mix_w_out, m_norm3_g, m_ffn2_w_in, m_ffn2_w_out, m_final_norm_g, v_c_ctx, v_ada_w, v_ada_b, v_norm1_g, v_ffn1_w_in, v_ffn1_w_out, v_norm2_g, v_mix_w_in, v_ret_decay_fwd, v_ret_decay_bwd, v_mla_q_norm_g, v_mla_w_uq, v_mla_kv_norm_g, v_mla_w_ukv, v_mix_w_out, v_norm3_g, v_ffn2_w_in, v_ffn2_w_out, v_final_norm_g):
    w = dict(c_ctx=c_ctx, ada_w=ada_w, ada_b=ada_b, norm1_g=norm1_g, ffn1_w_in=ffn1_w_in, ffn1_w_out=ffn1_w_out, norm2_g=norm2_g,
             mix_w_in=mix_w_in, ret_decay_fwd=ret_decay_fwd, ret_decay_bwd=ret_decay_bwd, mla_q_norm_g=mla_q_norm_g, mla_w_uq=mla_w_uq,
             mla_kv_norm_g=mla_kv_norm_g, mla_w_ukv=mla_w_ukv, mix_w_out=mix_w_out, norm3_g=norm3_g, ffn2_w_in=ffn2_w_in,
             ffn2_w_out=ffn2_w_out, final_norm_g=final_norm_g)
    mom_m = dict(zip(WEIGHTS, (m_c_ctx, m_ada_w, m_ada_b, m_norm1_g, m_ffn1_w_in, m_ffn1_w_out, m_norm2_g, m_mix_w_in, m_ret_decay_fwd,
                               m_ret_decay_bwd, m_mla_q_norm_g, m_mla_w_uq, m_mla_kv_norm_g, m_mla_w_ukv, m_mix_w_out, m_norm3_g,
                               m_ffn2_w_in, m_ffn2_w_out, m_final_norm_g)))
    mom_v = dict(zip(WEIGHTS, (v_c_ctx, v_ada_w, v_ada_b, v_norm1_g, v_ffn1_w_in, v_ffn1_w_out, v_norm2_g, v_mix_w_in, v_ret_decay_fwd,
                               v_ret_decay_bwd, v_mla_q_norm_g, v_mla_w_uq, v_mla_kv_norm_g, v_mla_w_ukv, v_mix_w_out, v_norm3_g,
                               v_ffn2_w_in, v_ffn2_w_out, v_final_norm_g)))
    xi, yi, ci = _me()
    chip = 2 * xi + yi
    example = 2 * chip + ci
    d = x.shape[-1]
    n_mod = ada_b.shape[-1] // d

    c_all = _allgather_small(_rows8(c), ALL_PEERS, False, "gather_c").reshape(8, d)
    cond = jnp.concatenate([c_all, jnp.broadcast_to(c_ctx[None, :], (8, d))], axis=0)
    cond_act = jax.nn.silu(cond)
    n_cols = ada_w.shape[-1]
    bias = lax.dynamic_slice_in_dim(ada_b, chip * n_cols, n_cols, axis=1)
    mods_cols = _matmul(cond_act, ada_w[0], "nn", F32, "ada_fwd", add=jnp.broadcast_to(bias, (16, n_cols)))
    mods = jnp.swapaxes(_allgather_small(mods_cols, CHIP_PEERS, True, "gather_mods"), 0, 1).reshape(16, 4 * n_cols)
    mods_lat = lax.dynamic_slice_in_dim(mods, example, 1, axis=0).reshape(n_mod, d)
    mods_ctx = mods[8].reshape(n_mod, d)

    (st_in1,) = _gather_weights([w[FIRST][0].astype(BF16)])
    big = dict(ffn1_wg=_slots_side_by_side(st_in1, (0, 1)), ffn1_wu=_slots_side_by_side(st_in1, (2, 3)))
    small = {k: w[k] for k in ("norm1_g", "norm2_g", "norm3_g", "final_norm_g", "mla_q_norm_g", "mla_kv_norm_g", "ret_decay_fwd", "ret_decay_bwd")}
    early = tuple(w[name][0] for name in EARLY)
    late = tuple(w[name][0] for name in LATE)

    loss_mine, (dx, dmods_lat, dmods_ctx, dsmall, dfirst, dearly, dlate) = jax.value_and_grad(_local_loss, argnums=(0, 1, 2, 3, 4, 5, 6))(
        x[0], mods_lat, mods_ctx, small, w[FIRST][0], early, late, big, ctx[0], loss_target[0])

    dmods = _allgather_small(_rows8(jnp.stack([dmods_lat.reshape(-1), dmods_ctx.reshape(-1)])), ALL_PEERS, False, "gather_dmods")
    dmods = dmods.reshape(8, 2, n_mod * d)
    dmods_rows = jnp.concatenate([dmods[:, 0, :], dmods[:, 1, :]], axis=0)
    dmods_cols = lax.dynamic_slice_in_dim(dmods_rows, chip * n_cols, n_cols, axis=1)
    g_ada_w = _matmul(cond_act, dmods_cols, "tn", F32, "ada_dw")
    dcond_act = _matmul(dmods_cols, ada_w[0], "nt", F32, "ada_dcond")
    sig = jax.nn.sigmoid(c_ctx)
    dc_ctx = jnp.sum(dcond_act[8:], axis=0) * (sig * (1.0 + c_ctx * (1.0 - sig)))
    share = dict(dsmall)
    share["c_ctx"] = jnp.where(ci == 0, dc_ctx, jnp.zeros_like(dc_ctx))
    share["ada_b"] = (dmods_lat + dmods_ctx).reshape(1, -1)
    zero = jnp.zeros((1,), F32)
    parts = _allgather_small(_pack([share[k] for k in SMALL] + [loss_mine.reshape(1)]), ALL_PEERS, False, "gather_small_grads")
    packed = _adamw_reduced(parts, _pack([w[k] for k in SMALL] + [zero]), _pack([mom_m[k] for k in SMALL] + [zero]),
                            _pack([mom_v[k] for k in SMALL] + [zero]), "adamw_small")
    like = [w[k] for k in SMALL] + [zero]
    grads, deltas, new_m, new_v = ({k: a for k, a in zip(SMALL + ("loss",), _unpack(p, like))} for p in packed)
    loss = grads.pop("loss").reshape(())

    for name, g in (("ada_w", g_ada_w), (FIRST, dfirst)) + tuple(zip(EARLY, dearly)) + tuple(zip(LATE, dlate)):
        dl, mo, vo = _adamw(w[name][0], g, mom_m[name][0], mom_v[name][0], "adamw_" + name)
        grads[name], deltas[name], new_m[name], new_v[name] = g[None], dl[None], mo[None], vo[None]

    return (loss, dx[None], *[grads[k] for k in WEIGHTS], *[deltas[k] for k in WEIGHTS], *[new_m[k] for k in WEIGHTS],
            *[new_v[k] for k in WEIGHTS])
```
